```python
import math
import jax
import jax.numpy as jnp
from jax import lax
import numpy as np

D_MODEL = 1024
BATCH = 8
SEQ = 2048
DEPTH = 2

GRID_W = 64
CTX_LEN = 256
EPS = 1e-6
N_DIR = 2
CONV_K = 3

SSD_HEADS = 6
SSD_HEAD_DIM = 64
SSD_GROUPS = 2
SSD_HPG = SSD_HEADS // SSD_GROUPS
SSD_STATE = 64
SSD_CHUNK = 128
SSD_WIDTH = SSD_HEADS * SSD_HEAD_DIM

GDN_HEADS = 4
GDN_HEAD_DIM = 64
GDN_CHUNK = 64
GDN_WIDTH = GDN_HEADS * GDN_HEAD_DIM

ATTN_Q_HEADS = 6
ATTN_KV_HEADS = 2
ATTN_GROUP = ATTN_Q_HEADS // ATTN_KV_HEADS
ATTN_HEAD_DIM = 64
ATTN_WIDTH = ATTN_Q_HEADS * ATTN_HEAD_DIM
ATTN_KV_WIDTH = ATTN_KV_HEADS * ATTN_HEAD_DIM
ATTN_BLOCK = 128
ROPE_THETA = 10000.0

D_MIX = SSD_WIDTH + GDN_WIDTH + ATTN_WIDTH

CONV_SPLITS = (SSD_WIDTH, SSD_GROUPS * SSD_STATE, SSD_GROUPS * SSD_STATE, GDN_WIDTH, GDN_WIDTH, GDN_WIDTH)
CONV_DIM = sum(CONV_SPLITS)
PROJ_SPLITS = (CONV_DIM, SSD_WIDTH, N_DIR * SSD_HEADS, GDN_WIDTH, N_DIR * GDN_HEADS, N_DIR * GDN_HEADS,
               ATTN_WIDTH, ATTN_KV_WIDTH, ATTN_KV_WIDTH, ATTN_WIDTH)
PROJ_DIM = sum(PROJ_SPLITS)

kernel_name = "hybrid_ssd_gdn_attn_dit_block"


def _split(x, sizes):
    idx = np.cumsum(sizes)[:-1].tolist()
    return jnp.split(x, idx, axis=-1)


def _rms(x, w):
    xf = x.astype(jnp.float32)
    y = xf * lax.rsqrt(jnp.mean(xf * xf, axis=-1, keepdims=True) + EPS)
    return (y * w.astype(jnp.float32)).astype(x.dtype)


def _l2norm(x):
    return x * lax.rsqrt(jnp.sum(x * x, axis=-1, keepdims=True) + EPS)


def _dwconv_centred(x, w, b):
    C = x.shape[-1]
    pad = CONV_K // 2
    y = lax.conv_general_dilated(x, w.astype(x.dtype)[:, None, :], window_strides=(1,),
                                 padding=[(pad, pad)], dimension_numbers=('NWC', 'WIO', 'NWC'),
                                 feature_group_count=C)
    return y + b.astype(x.dtype)


def _project(h, w_in, conv_w, conv_b):
    proj = h @ w_in
    (conv_in, ssd_z, ssd_dt, gdn_z, gdn_a, gdn_b, attn_q, attn_k, attn_v, attn_z) = _split(proj, PROJ_SPLITS)
    conv_out = jax.nn.silu(_dwconv_centred(conv_in, conv_w, conv_b))
    ssd_x, ssd_B, ssd_C, gdn_q, gdn_k, gdn_v = _split(conv_out, CONV_SPLITS)
    return dict(ssd_x=ssd_x, ssd_B=ssd_B, ssd_C=ssd_C, ssd_z=ssd_z, ssd_dt=ssd_dt,
                gdn_q=gdn_q, gdn_k=gdn_k, gdn_v=gdn_v, gdn_z=gdn_z, gdn_a=gdn_a, gdn_b=gdn_b,
                attn_q=attn_q, attn_k=attn_k, attn_v=attn_v, attn_z=attn_z)


def _ssd_scan(x, dt, Bm, Cm, A, s0):
    Bsz, L, G, Hg, P = x.shape
    N = Bm.shape[-1]
    Q = SSD_CHUNK
    nc = L // Q
    x = x.reshape(Bsz, nc, Q, G, Hg, P)
    dt = dt.reshape(Bsz, nc, Q, G, Hg)
    Bm = Bm.reshape(Bsz, nc, Q, G, N)
    Cm = Cm.reshape(Bsz, nc, Q, G, N)
    cum = jnp.cumsum(dt * A, axis=2)
    causal = jnp.tril(jnp.ones((Q, Q), bool))[:, :, None, None]
    diff = cum[:, :, :, None] - cum[:, :, None, :]
    Lmat = jnp.exp(jnp.where(causal, diff, -jnp.inf))
    scores = jnp.einsum('bcign,bcjgn->bcijg', Cm, Bm)
    W = scores[..., None] * Lmat * dt[:, :, None]
    y_diag = jnp.einsum('bcijgh,bcjghp->bcighp', W, x)
    decay_to_end = jnp.exp(cum[:, :, -1:] - cum)
    chunk_states = jnp.einsum('bcjgn,bcjgh,bcjghp->bcghpn', Bm, decay_to_end * dt, x)
    chunk_decay = jnp.exp(cum[:, :, -1])

    def step(s, inp):
        st, dec = inp
        return s * dec[..., None, None] + st, s

    s_final, s_in = lax.scan(step, s0, (jnp.moveaxis(chunk_states, 1, 0), jnp.moveaxis(chunk_decay, 1, 0)))
    s_in = jnp.moveaxis(s_in, 0, 1)
    y_off = jnp.einsum('bcign,bcghpn,bcigh->bcighp', Cm, s_in, jnp.exp(cum))
    return (y_diag + y_off).reshape(Bsz, L, G, Hg, P), s_final


def _ssd_bidir(p, A_log, dt_bias, init):
    f32 = jnp.float32
    Bsz, L, _ = p['ssd_x'].shape
    xs = p['ssd_x'].astype(f32).reshape(Bsz, L, SSD_GROUPS, SSD_HPG, SSD_HEAD_DIM)
    Bm = p['ssd_B'].astype(f32).reshape(Bsz, L, SSD_GROUPS, SSD_STATE)
    Cm = p['ssd_C'].astype(f32).reshape(Bsz, L, SSD_GROUPS, SSD_STATE)
    dt_raw = p['ssd_dt'].astype(f32).reshape(Bsz, L, N_DIR, SSD_GROUPS, SSD_HPG)
    dt = jax.nn.softplus(dt_raw + dt_bias.astype(f32).reshape(N_DIR, SSD_GROUPS, SSD_HPG))
    A = -jnp.exp(A_log.astype(f32)).reshape(N_DIR, SSD_GROUPS, SSD_HPG)
    if init is None:
        init = jnp.zeros((N_DIR, Bsz, SSD_GROUPS, SSD_HPG, SSD_HEAD_DIM, SSD_STATE), f32)
    ys, finals = [], []
    for d in range(N_DIR):
        args = (xs, dt[:, :, d], Bm, Cm)
        if d == 1:
            args = tuple(jnp.flip(a, axis=1) for a in args)
        y, s = _ssd_scan(*args, A[d], init[d])
        ys.append(jnp.flip(y, axis=1) if d == 1 else y)
        finals.append(s)
    return ys[0] + ys[1], jnp.stack(finals)


def _ssd_out(y, p, D_skip, norm_w):
    Bsz, L = y.shape[:2]
    xs = p['ssd_x'].astype(jnp.float32).reshape(Bsz, L, SSD_GROUPS, SSD_HPG, SSD_HEAD_DIM)
    y = y + D_skip.astype(jnp.float32).reshape(SSD_GROUPS, SSD_HPG)[..., None] * xs
    y = y.reshape(Bsz, L, SSD_WIDTH).astype(p['ssd_z'].dtype)
    return _rms(y * jax.nn.silu(p['ssd_z']), norm_w)


def _gdn_scan(q, k, v, g, beta, s0):
    Bsz, L, H, Dk = q.shape
    Dv = v.shape[-1]
    C = GDN_CHUNK
    nc = L // C

    def chunks(a):
        return jnp.moveaxis(a.reshape((Bsz, nc, C, H) + a.shape[3:]), 3, 1)

    q = chunks(q * (Dk ** -0.5))
    k = chunks(k)
    v = chunks(v)
    g = chunks(g)
    beta = chunks(beta)
    gc = jnp.cumsum(g, axis=-1)
    lower = jnp.tril(jnp.ones((C, C), bool))
    strict = jnp.tril(jnp.ones((C, C), bool), -1)
    decay = jnp.exp(jnp.where(lower, gc[..., :, None] - gc[..., None, :], -jnp.inf))
    k_beta = k * beta[..., None]
    v_beta = v * beta[..., None]
    tri = jnp.where(strict, jnp.einsum('bhnid,bhnjd->bhnij', k_beta, k) * decay, 0.0) + jnp.eye(C, dtype=q.dtype)
    u = lax.linalg.triangular_solve(tri, v_beta, left_side=True, lower=True, unit_diagonal=True)
    w = lax.linalg.triangular_solve(tri, k_beta * jnp.exp(gc)[..., None], left_side=True, lower=True,
                                    unit_diagonal=True)
    intra = jnp.where(lower, jnp.einsum('bhnid,bhnjd->bhnij', q, k) * decay, 0.0)
    q_dec = q * jnp.exp(gc)[..., None]
    k_dec = k * jnp.exp(gc[..., -1:] - gc)[..., None]
    chunk_decay = jnp.exp(gc[..., -1])

    def step(S, inp):
        qd, kd, u_i, w_i, a_i, d_i = inp
        v_new = u_i - jnp.einsum('bhcd,bhde->bhce', w_i, S)
        o = jnp.einsum('bhcd,bhde->bhce', qd, S) + jnp.einsum('bhij,bhje->bhie', a_i, v_new)
        S = S * d_i[..., None, None] + jnp.einsum('bhcd,bhce->bhde', kd, v_new)
        return S, o

    xs = tuple(jnp.moveaxis(a, 2, 0) for a in (q_dec, k_dec, u, w, intra, chunk_decay))
    s_final, o = lax.scan(step, s0, xs)
    o = jnp.moveaxis(jnp.moveaxis(o, 0, 2), 1, 3).reshape(Bsz, L, H, Dv)
    return o, s_final


def _gdn_bidir(p, A_log, dt_bias, init):
    f32 = jnp.float32
    Bsz, L, _ = p['gdn_q'].shape
    q = _l2norm(p['gdn_q'].astype(f32).reshape(Bsz, L, GDN_HEADS, GDN_HEAD_DIM))
    k = _l2norm(p['gdn_k'].astype(f32).reshape(Bsz, L, GDN_HEADS, GDN_HEAD_DIM))
    v = p['gdn_v'].astype(f32).reshape(Bsz, L, GDN_HEADS, GDN_HEAD_DIM)
    a = p['gdn_a'].astype(f32).reshape(Bsz, L, N_DIR, GDN_HEADS)
    b = p['gdn_b'].astype(f32).reshape(Bsz, L, N_DIR, GDN_HEADS)
    g = -jnp.exp(A_log.astype(f32)) * jax.nn.softplus(a + dt_bias.astype(f32))
    beta = jax.nn.sigmoid(b)
    if init is None:
        init = jnp.zeros((N_DIR, Bsz, GDN_HEADS, GDN_HEAD_DIM, GDN_HEAD_DIM), f32)
    os, finals = [], []
    for d in range(N_DIR):
        args = (q, k, v, g[:, :, d], beta[:, :, d])
        if d == 1:
            args = tuple(jnp.flip(t, axis=1) for t in args)
        o, s = _gdn_scan(*args, init[d])
        os.append(jnp.flip(o, axis=1) if d == 1 else o)
        finals.append(s)
    return os[0] + os[1], jnp.stack(finals)


def _gdn_out(o, p, norm_w):
    Bsz, L = o.shape[:2]
    o = _rms(o, norm_w).reshape(Bsz, L, GDN_WIDTH).astype(p['gdn_z'].dtype)
    return o * jax.nn.silu(p['gdn_z'])


def _rope_tables(pos_row, pos_col):
    n_freq = ATTN_HEAD_DIM // 4
    freqs = ROPE_THETA ** (-jnp.arange(n_freq, dtype=jnp.float32) / n_freq)
    ang_r = pos_row.astype(jnp.float32)[:, None] * freqs
    ang_c = pos_col.astype(jnp.float32)[:, None] * freqs
    return (jnp.cos(ang_r), jnp.sin(ang_r), jnp.cos(ang_c), jnp.sin(ang_c))


def _rope_axis(x, cos, sin):
    F = x.shape[-1] // 2
    x1, x2 = x[..., :F], x[..., F:]
    cos = cos.astype(x.dtype)[:, None]
    sin = sin.astype(x.dtype)[:, None]
    return jnp.concatenate([x1 * cos - x2 * sin, x1 * sin + x2 * cos], axis=-1)


def _rope2d(x, rope):
    cr, sr, cc, sc = rope
    half = x.shape[-1] // 2
    return jnp.concatenate([_rope_axis(x[..., :half], cr, sr), _rope_axis(x[..., half:], cc, sc)], axis=-1)


def _qkv(p, q_norm_w, k_norm_w, rope):
    Bsz, L, _ = p['attn_q'].shape
    q = _rms(p['attn_q'].reshape(Bsz, L, ATTN_Q_HEADS, ATTN_HEAD_DIM), q_norm_w)
    k = _rms(p['attn_k'].reshape(Bsz, L, ATTN_KV_HEADS, ATTN_HEAD_DIM), k_norm_w)
    v = p['attn_v'].reshape(Bsz, L, ATTN_KV_HEADS, ATTN_HEAD_DIM)
    if rope is not None:
        q = _rope2d(q, rope)
        k = _rope2d(k, rope)
    return q.reshape(Bsz, L, ATTN_KV_HEADS, ATTN_GROUP, ATTN_HEAD_DIM), k, v


def _attend(q, k, v):
    Bsz, Lq = q.shape[:2]
    nb = Lq // ATTN_BLOCK
    qb = jnp.moveaxis(q.reshape((Bsz, nb, ATTN_BLOCK) + q.shape[2:]), 1, 0)
    scale = ATTN_HEAD_DIM ** -0.5

    def block(qi):
        s = jnp.einsum('bqhgd,bkhd->bhgqk', qi, k).astype(jnp.float32) * scale
        pr = jax.nn.softmax(s, axis=-1).astype(v.dtype)
        return jnp.einsum('bhgqk,bkhd->bqhgd', pr, v)

    o = lax.map(block, qb)
    return jnp.moveaxis(o, 0, 1).reshape(Bsz, Lq, ATTN_WIDTH)


def _layer(x, xc, c, c_ctx, rope, norm_w, w_mod, b_mod, w_in, conv_w, conv_b, ssd_A_log, ssd_dt_bias,
           ssd_D, ssd_norm_w, gdn_A_log, gdn_dt_bias, gdn_norm_w, q_norm_w, k_norm_w, w_out, update_ctx):
    shift, scale, gate = jnp.split(jax.nn.silu(c) @ w_mod + b_mod, 3, axis=-1)
    shift_c, scale_c, gate_c = jnp.split(jax.nn.silu(c_ctx) @ w_mod + b_mod, 3, axis=-1)
    h = _rms(x, norm_w) * (1.0 + scale[:, None]) + shift[:, None]
    hc = _rms(xc, norm_w) * (1.0 + scale_c) + shift_c
    pl = _project(h, w_in, conv_w, conv_b)
    pc = _project(hc, w_in, conv_w, conv_b)

    ssd_yc, ssd_state = _ssd_bidir(pc, ssd_A_log, ssd_dt_bias, None)
    ssd_yl, _ = _ssd_bidir(pl, ssd_A_log, ssd_dt_bias, ssd_state)
    gdn_oc, gdn_state = _gdn_bidir(pc, gdn_A_log, gdn_dt_bias, None)
    gdn_ol, _ = _gdn_bidir(pl, gdn_A_log, gdn_dt_bias, gdn_state)
    qc, kc, vc = _qkv(pc, q_norm_w, k_norm_w, None)
    ql, kl, vl = _qkv(pl, q_norm_w, k_norm_w, rope)
    attn_l = _attend(ql, jnp.concatenate([kc, kl], axis=1), jnp.concatenate([vc, vl], axis=1))

    mix_l = jnp.concatenate([_ssd_out(ssd_yl, pl, ssd_D, ssd_norm_w),
                             _gdn_out(gdn_ol, pl, gdn_norm_w),
                             attn_l * jax.nn.silu(pl['attn_z'])], axis=-1)
    x = x + gate[:, None] * (mix_l @ w_out)
    if update_ctx:
        attn_c = _attend(qc, kc, vc)
        mix_c = jnp.concatenate([_ssd_out(ssd_yc, pc, ssd_D, ssd_norm_w),
                                 _gdn_out(gdn_oc, pc, gdn_norm_w),
                                 attn_c * jax.nn.silu(pc['attn_z'])], axis=-1)
        xc = xc + gate_c * (mix_c @ w_out)
    return x, xc


def setup_inputs(seed: int = 0) -> dict:
    key = jax.random.key(seed)
    ks = jax.random.split(key, 24)
    f32 = jnp.float32

    def nrm(k, shape, s):
        return jax.random.normal(k, shape, f32) * s

    def dt_bias_init(k, shape):
        dt = jnp.exp(jax.random.uniform(k, shape, f32, math.log(1e-3), math.log(1e-1)))
        return dt + jnp.log(-jnp.expm1(-dt))

    return {
        'x': nrm(ks[0], (BATCH, SEQ, D_MODEL), 1.0),
        'c': nrm(ks[1], (BATCH, D_MODEL), 1.0),
        'ctx': nrm(ks[2], (BATCH, CTX_LEN, D_MODEL), 1.0),
        'c_ctx': nrm(ks[3], (D_MODEL,), 1.0),
        'norm_w': 1.0 + nrm(ks[4], (DEPTH, D_MODEL), 0.02),
        'w_mod': nrm(ks[5], (DEPTH, D_MODEL, 3 * D_MODEL), D_MODEL ** -0.5),
        'b_mod': nrm(ks[6], (DEPTH, 3 * D_MODEL), 0.01),
        'w_in': nrm(ks[7], (DEPTH, D_MODEL, PROJ_DIM), D_MODEL ** -0.5),
        'conv_w': nrm(ks[8], (DEPTH, CONV_K, CONV_DIM), CONV_K ** -0.5),
        'conv_b': nrm(ks[9], (DEPTH, CONV_DIM), 0.01),
        'ssd_A_log': jnp.log(jax.random.uniform(ks[10], (DEPTH, N_DIR, SSD_HEADS), f32, 1.0, 16.0)),
        'ssd_dt_bias': dt_bias_init(ks[11], (DEPTH, N_DIR, SSD_HEADS)),
        'ssd_D': 1.0 + nrm(ks[12], (DEPTH, SSD_HEADS), 0.1),
        'ssd_norm_w': 1.0 + nrm(ks[13], (DEPTH, SSD_WIDTH), 0.02),
        'gdn_A_log': jnp.log(jax.random.uniform(ks[14], (DEPTH, N_DIR, GDN_HEADS), f32, 1.0, 16.0)),
        'gdn_dt_bias': dt_bias_init(ks[15], (DEPTH, N_DIR, GDN_HEADS)),
        'gdn_norm_w': 1.0 + nrm(ks[16], (DEPTH, GDN_HEAD_DIM), 0.02),
        'q_norm_w': 1.0 + nrm(ks[17], (DEPTH, ATTN_HEAD_DIM), 0.02),
        'k_norm_w': 1.0 + nrm(ks[18], (DEPTH, ATTN_HEAD_DIM), 0.02),
        'w_out': nrm(ks[19], (DEPTH, D_MIX, D_MODEL), D_MIX ** -0.5),
    }


def reference(x, c, ctx, c_ctx, norm_w, w_mod, b_mod, w_in, conv_w, conv_b, ssd_A_log, ssd_dt_bias, ssd_D,
              ssd_norm_w, gdn_A_log, gdn_dt_bias, gdn_norm_w, q_norm_w, k_norm_w, w_out):
    L = x.shape[1]
    ROWS = L // GRID_W
    pos_row = jnp.repeat(jnp.arange(ROWS, dtype=jnp.int32), GRID_W)
    pos_col = jnp.tile(jnp.arange(GRID_W, dtype=jnp.int32), ROWS)
    rope = _rope_tables(pos_row, pos_col)
    h, hc = x, ctx
    for i in range(DEPTH):
        h, hc = _layer(h, hc, c, c_ctx, rope, norm_w[i], w_mod[i], b_mod[i], w_in[i], conv_w[i], conv_b[i],
                       ssd_A_log[i], ssd_dt_bias[i], ssd_D[i], ssd_norm_w[i], gdn_A_log[i], gdn_dt_bias[i],
                       gdn_norm_w[i], q_norm_w[i], k_norm_w[i], w_out[i], i < DEPTH - 1)
    return h
```

```python
import functools
import math

import jax
import jax.numpy as jnp
from jax import lax
from jax.experimental import pallas as pl
from jax.experimental.pallas import tpu as pltpu

F32 = jnp.float32
BF16 = jnp.bfloat16

D_MODEL = 1024
CTX = 256
GRID_W = 64
EPS = 1e-6
DEPTH = 2

HD = 64
SSD_HEADS = 6
SSD_W = SSD_HEADS * HD
SSD_CHUNK = 128
GDN_HEADS = 4
GDN_W = GDN_HEADS * HD
GDN_CHUNK = 64
ATT_QH = 6
ATT_KVH = 2
ATT_W = ATT_QH * HD
ROPE_THETA = 10000.0

CONV_DIM = 1408
REST_DIM = 1792
TM = 256
HALO = 8
NEG = -1e30
VMEM_LIMIT = 56 * 1024 * 1024

SM_DT = 0
SM_A = 12
SM_B = 20


def _mm(a, b):
    return jnp.dot(a.astype(BF16), b.astype(BF16), preferred_element_type=F32)


def _mm_nt(a, b):
    return lax.dot_general(a.astype(BF16), b.astype(BF16), (((1,), (1,)), ((), ())),
                           preferred_element_type=F32)


def _mm_tn(a, b):
    return lax.dot_general(a.astype(BF16), b.astype(BF16), (((0,), (0,)), ((), ())),
                           preferred_element_type=F32)


def _split3(a):
    h = a.astype(BF16)
    r = a - h.astype(F32)
    m = r.astype(BF16)
    l = (r - m.astype(F32)).astype(BF16)
    return h, m, l


def _mm_exact_rhs(a, b_bf16):
    h, m, l = _split3(a)
    d = functools.partial(jnp.dot, preferred_element_type=F32)
    return d(h, b_bf16) + (d(m, b_bf16) + d(l, b_bf16))


def _mm_exact_lhs(a_bf16, b):
    h, m, l = _split3(b)
    d = functools.partial(jnp.dot, preferred_element_type=F32)
    return d(a_bf16, h) + (d(a_bf16, m) + d(a_bf16, l))


def _mm3(a, b):
    ah = a.astype(BF16)
    al = (a - ah.astype(F32)).astype(BF16)
    bh = b.astype(BF16)
    bl = (b - bh.astype(F32)).astype(BF16)
    d = functools.partial(jnp.dot, preferred_element_type=F32)
    return d(ah, bh) + (d(ah, bl) + d(al, bh))


def _silu(x):
    return x * jax.nn.sigmoid(x)


def _softplus(x):
    return jnp.maximum(x, 0.0) + jnp.log1p(jnp.exp(-jnp.abs(x)))


def _iota2(shape, dim):
    return lax.broadcasted_iota(jnp.int32, shape, dim)


def _group_sum64(xx):
    r = jnp.right_shift(_iota2((128, 128), 0), 6)
    c = jnp.right_shift(_iota2((128, 128), 1), 6)
    g = (r == c).astype(BF16)
    outs = []
    for t in range(xx.shape[1] // 128):
        outs.append(_mm_exact_rhs(xx[:, t * 128:(t + 1) * 128], g))
    return outs[0] if len(outs) == 1 else jnp.concatenate(outs, axis=1)


def _mod_kernel(c_ref, w_ref, b_ref, o_ref):
    o_ref[...] = _mm(_silu(c_ref[...]), w_ref[...]) + b_ref[...]


def _modulation(c_all, w_mod, b_mod):
    n = w_mod.shape[1]
    bn = 768
    rows = c_all.shape[0]
    return pl.pallas_call(
        _mod_kernel,
        out_shape=jax.ShapeDtypeStruct((rows, n), F32),
        grid=(n // bn,),
        in_specs=[pl.BlockSpec((rows, D_MODEL), lambda j: (0, 0)),
                  pl.BlockSpec((D_MODEL, bn), lambda j: (0, j)),
                  pl.BlockSpec((1, bn), lambda j: (0, j))],
        out_specs=pl.BlockSpec((rows, bn), lambda j: (0, j)),
        compiler_params=pltpu.CompilerParams(dimension_semantics=("arbitrary",),
                                             vmem_limit_bytes=VMEM_LIMIT),
        name="modulation",
    )(c_all, w_mod, b_mod.reshape(1, n))


def _proj_kernel(nt, xm_ref, xp_ref, xn_ref, mod_ref, nw_ref, wc_ref, wr_ref, cw_ref, cb_ref,
                 qkw_ref, cos_ref, sin_ref,
                 ssd_ref, gdn_ref, zs_ref, q_ref, kv_ref, small_ref):
    i = pl.program_id(1)
    m = mod_ref[0]
    gain = nw_ref[...] * (1.0 + m[:, D_MODEL:2 * D_MODEL])
    shift = m[:, :D_MODEL]

    xe = jnp.concatenate([xm_ref[0], xp_ref[0], xn_ref[0]], axis=0)
    ms = jnp.mean(xe * xe, axis=-1, keepdims=True)
    he = (xe * lax.rsqrt(ms + EPS) * gain + shift).astype(BF16)

    p = jnp.dot(he, wc_ref[...], preferred_element_type=F32)
    r = jnp.dot(he[:TM], wr_ref[...], preferred_element_type=F32)

    pm = p[:TM]
    seg_first = i <= 1
    seg_last = jnp.logical_or(i == 0, i == nt - 1)
    prev_row = jnp.where(seg_first, 0.0, p[TM + HALO - 1:TM + HALO])
    next_row = jnp.where(seg_last, 0.0, p[TM + HALO:TM + HALO + 1])
    rows = _iota2((TM, 1), 0)
    pm1 = jnp.where(rows == 0, prev_row, pltpu.roll(pm, 1, axis=0))
    pp1 = jnp.where(rows == TM - 1, next_row, pltpu.roll(pm, TM - 1, axis=0))
    cw = cw_ref[...]
    conv = cw[0:1] * pm1 + cw[1:2] * pm + cw[2:3] * pp1 + cb_ref[...]
    co = _silu(conv)

    ssd_ref[0] = co[:, :640]
    for pair in range(2):
        base = 640 + pair * 384
        qk = co[:, base:base + 256]
        qk = qk * lax.rsqrt(_group_sum64(qk * qk) + EPS)
        gdn_ref[0, :, pair * 384:pair * 384 + 128] = qk[:, :128] * (HD ** -0.5)
        gdn_ref[0, :, pair * 384 + 128:pair * 384 + 256] = qk[:, 128:]
        gdn_ref[0, :, pair * 384 + 256:pair * 384 + 384] = co[:, base + 256:base + 384]

    z1 = r[:, :640]
    z2 = r[:, 1280:1664]
    zs_ref[0, :, :640] = _silu(z1)
    zs_ref[0, :, 640:] = _silu(z2)

    aqk = r[:, 640:1152]
    aqk = aqk * lax.rsqrt(_group_sum64(aqk * aqk) * (1.0 / HD) + EPS) * qkw_ref[...]
    lane = _iota2((1, 512), 1)
    swapped = jnp.where(jnp.bitwise_and(lane, 31) < 16,
                        pltpu.roll(aqk, 512 - 16, axis=1), pltpu.roll(aqk, 16, axis=1))
    cos = jnp.concatenate([cos_ref[...]] * 4, axis=1)
    sin = jnp.concatenate([sin_ref[...]] * 4, axis=1)
    aqk = aqk * cos + swapped * sin
    q_ref[0] = aqk[:, :384]
    kv_ref[0, :, :128] = aqk[:, 384:]
    kv_ref[0, :, 128:] = r[:, 1152:1280]
    small_ref[0] = r[:, 1664:1792]


def _in_projection(xcat, mod3, norm_w, w_conv, w_rest, conv_w, conv_b, qk_w, cos_t, sin_t):
    bsz, t, _ = xcat.shape
    nt = t // TM
    nblk8 = t // HALO
    per_tile = TM // HALO
    row = lambda b, i: (b, i, 0)
    full = lambda b, i: (0, 0)
    outs = [jax.ShapeDtypeStruct((bsz, t, w), F32) for w in (640, 768, 1024, 384, 256, 128)]
    return pl.pallas_call(
        functools.partial(_proj_kernel, nt),
        out_shape=outs,
        grid=(bsz, nt),
        in_specs=[
            pl.BlockSpec((1, TM, D_MODEL), row),
            pl.BlockSpec((1, HALO, D_MODEL), lambda b, i: (b, jnp.maximum(i * per_tile - 1, 0), 0)),
            pl.BlockSpec((1, HALO, D_MODEL), lambda b, i: (b, jnp.minimum((i + 1) * per_tile, nblk8 - 1), 0)),
            pl.BlockSpec((1, 1, 3 * D_MODEL), lambda b, i: (jnp.where(i == 0, bsz, b), 0, 0)),
            pl.BlockSpec((1, D_MODEL), full),
            pl.BlockSpec((D_MODEL, CONV_DIM), full),
            pl.BlockSpec((D_MODEL, REST_DIM), full),
            pl.BlockSpec((3, CONV_DIM), full),
            pl.BlockSpec((1, CONV_DIM), full),
            pl.BlockSpec((1, 512), full),
            pl.BlockSpec((TM, 128), lambda b, i: (i, 0)),
            pl.BlockSpec((TM, 128), lambda b, i: (i, 0)),
        ],
        out_specs=[pl.BlockSpec((1, TM, w), row) for w in (640, 768, 1024, 384, 256, 128)],
        compiler_params=pltpu.CompilerParams(dimension_semantics=("arbitrary", "arbitrary"),
                                             vmem_limit_bytes=VMEM_LIMIT),
        name="in_projection",
    )(xcat, xcat, xcat, mod3, norm_w, w_conv, w_rest, conv_w, conv_b, qk_w, cos_t, sin_t)


def _ssd_kernel(nc, ctx_chunks, xin_ref, small_ref, prm_ref, dskip_ref, y_ref,
                cum_s, cumt_s, dt_s, dtt_s, y_s, st_s, sin_s):
    q = SSD_CHUNK
    ri = _iota2((q, q), 0)
    ci = _iota2((q, q), 1)
    ltri = (ri >= ci).astype(BF16)
    utri = (ri <= ci).astype(BF16)
    lane = _iota2((1, 128), 1)
    fwd_lane = lane < SSD_HEADS

    a_row = -jnp.exp(prm_ref[0:1, :])
    bias_row = prm_ref[1:2, :]

    def prep(c, carry):
        base = pl.multiple_of(c * q, q)
        dt = _softplus(small_ref[0, pl.ds(base, q), :] + bias_row)
        dta = dt * a_row
        cum = jnp.where(fwd_lane, _mm_exact_lhs(ltri, dta), _mm_exact_lhs(utri, dta))
        cum_s[pl.ds(base, q), :] = cum
        dt_s[pl.ds(base, q), :] = dt
        cumt_s[pl.ds(base, q), :] = cum.T
        dtt_s[pl.ds(base, q), :] = dt.T
        return carry

    lax.fori_loop(0, nc, prep, 0)

    lower = ri >= ci
    upper = ri <= ci
    dsk = dskip_ref[...]

    def diag(c, carry):
        base = pl.multiple_of(c * q, q)
        xin = xin_ref[0, pl.ds(base, q), :]
        cum = cum_s[pl.ds(base, q), :]
        cumt = cumt_s[pl.ds(base, q), :]
        dtc = dt_s[pl.ds(base, q), :]
        dtt = dtt_s[pl.ds(base, q), :]
        for g in range(2):
            bg = xin[:, 384 + g * HD:384 + (g + 1) * HD]
            cg = xin[:, 512 + g * HD:512 + (g + 1) * HD]
            sc = _mm_nt(cg, bg)
            xds = [[], []]
            for hg in range(3):
                h = g * 3 + hg
                xh = xin[:, h * HD:(h + 1) * HD]
                w = None
                for d in range(2):
                    col = d * SSD_HEADS + h
                    ccol = cum[:, col:col + 1]
                    diff = ccol - cumt[col:col + 1, :]
                    lm = jnp.exp(jnp.where(lower if d == 0 else upper, diff, NEG))
                    wd = lm * dtt[col:col + 1, :]
                    w = wd if w is None else w + wd
                    end = cum[q - 1:q, col:col + 1] if d == 0 else cum[0:1, col:col + 1]
                    xds[d].append(xh * (jnp.exp(end - ccol) * dtc[:, col:col + 1]))
                y_s[pl.ds(base, q), h * HD:(h + 1) * HD] = _mm(w * sc, xh) + dsk[:, h * HD:(h + 1) * HD] * xh
            for d in range(2):
                xd = jnp.concatenate(xds[d], axis=1)
                st_s[c * 4 + d * 2 + g] = _mm_tn(bg, xd)
        return carry

    lax.fori_loop(0, nc, diag, 0)

    orders = (list(range(nc)),
              list(range(ctx_chunks - 1, -1, -1)) + list(range(nc - 1, ctx_chunks - 1, -1)))
    for d in range(2):
        for g in range(2):
            s = jnp.zeros((HD, 3 * HD), F32)
            for c in orders[d]:
                slot = c * 4 + d * 2 + g
                sin_s[slot] = s
                row = c * q + (q - 1 if d == 0 else 0)
                decs = []
                for hg in range(3):
                    col = d * SSD_HEADS + g * 3 + hg
                    decs.append(jnp.broadcast_to(jnp.exp(cum_s[row:row + 1, col:col + 1]), (1, HD)))
                s = s * jnp.concatenate(decs, axis=1) + st_s[slot]

    def off(c, carry):
        base = pl.multiple_of(c * q, q)
        xin = xin_ref[0, pl.ds(base, q), :]
        cum = cum_s[pl.ds(base, q), :]
        parts = []
        for g in range(2):
            cg = xin[:, 512 + g * HD:512 + (g + 1) * HD]
            acc = None
            for d in range(2):
                yo = _mm(cg, sin_s[c * 4 + d * 2 + g])
                es = []
                for hg in range(3):
                    col = d * SSD_HEADS + g * 3 + hg
                    es.append(jnp.broadcast_to(jnp.exp(cum[:, col:col + 1]), (q, HD)))
                yo = yo * jnp.concatenate(es, axis=1)
                acc = yo if acc is None else acc + yo
            parts.append(acc)
        y_ref[0, pl.ds(base, q), :] = y_s[pl.ds(base, q), :] + jnp.concatenate(parts, axis=1)
        return carry

    lax.fori_loop(0, nc, off, 0)


def _ssd(ssd_in, small, prm, dskip):
    bsz, t, _ = ssd_in.shape
    nc = t // SSD_CHUNK
    return pl.pallas_call(
        functools.partial(_ssd_kernel, nc, CTX // SSD_CHUNK),
        out_shape=jax.ShapeDtypeStruct((bsz, t, SSD_W), F32),
        grid=(bsz,),
        in_specs=[pl.BlockSpec((1, t, 640), lambda b: (b, 0, 0)),
                  pl.BlockSpec((1, t, 128), lambda b: (b, 0, 0)),
                  pl.BlockSpec((8, 128), lambda b: (0, 0)),
                  pl.BlockSpec((1, SSD_W), lambda b: (0, 0))],
        out_specs=pl.BlockSpec((1, t, SSD_W), lambda b: (b, 0, 0)),
        scratch_shapes=[pltpu.VMEM((t, 128), F32), pltpu.VMEM((t, 128), F32),
                        pltpu.VMEM((t, 128), F32), pltpu.VMEM((t, 128), F32),
                        pltpu.VMEM((t, SSD_W), F32),
                        pltpu.VMEM((nc * 4, HD, 3 * HD), F32),
                        pltpu.VMEM((nc * 4, HD, 3 * HD), F32)],
        compiler_params=pltpu.CompilerParams(dimension_semantics=("arbitrary",),
                                             vmem_limit_bytes=VMEM_LIMIT),
        name="ssd",
    )(ssd_in, small, prm, dskip)


def _unit_tri_inverse(a_strict):
    n = a_strict.shape[0]
    ri = _iota2((n, n), 0)
    ci = _iota2((n, n), 1)
    eye = (ri == ci).astype(F32)

    def blk(shift):
        return jnp.right_shift(ri, shift) == jnp.right_shift(ci, shift)

    nd = jnp.where(blk(3), -a_strict, 0.0)
    qk = nd
    p = eye + nd
    for lvl in range(2):
        qk = _mm3(qk, qk)
        p = p + _mm3(p, qk)
    inv = p
    for shift in (4, 5, 6):
        e = jnp.where(jnp.logical_and(blk(shift), jnp.logical_not(blk(shift - 1))), a_strict, 0.0)
        inv = inv - _mm3(_mm3(inv, e), inv)
    return inv


def _gdn_kernel(nch, ctx_chunks, gin_ref, small_ref, prm_ref, o_ref,
                gc_s, gct_s, beta_s, u_s, wq_s, ik_s, cd_s, of_s, ob_s):
    hp = pl.program_id(1)
    c64 = GDN_CHUNK
    nsc = nch // 2
    ri128 = _iota2((128, 128), 0)
    ci128 = _iota2((128, 128), 1)
    same = jnp.right_shift(ri128, 6) == jnp.right_shift(ci128, 6)
    lbd = jnp.logical_and(same, ri128 >= ci128).astype(BF16)
    ubd = jnp.logical_and(same, ri128 <= ci128).astype(BF16)
    lane = _iota2((1, 128), 1)
    fwd_lane = lane < SM_A + GDN_HEADS

    a_row = -jnp.exp(prm_ref[0:1, :])
    bias_row = prm_ref[1:2, :]

    def prep(sc, carry):
        base = pl.multiple_of(sc * 128, 128)
        sm = small_ref[0, pl.ds(base, 128), :]
        g = a_row * _softplus(sm + bias_row)
        cum = jnp.where(fwd_lane, _mm_exact_lhs(lbd, g), _mm_exact_lhs(ubd, g))
        gc_s[pl.ds(base, 128), :] = cum
        beta_s[pl.ds(base, 128), :] = jax.nn.sigmoid(sm)
        cumt = cum.T
        gct_s[pl.ds(pl.multiple_of(sc * 256, 256), 128), :] = cumt[:, :c64]
        gct_s[pl.ds(pl.multiple_of(sc * 256 + 128, 128), 128), :] = cumt[:, c64:]
        return carry

    lax.fori_loop(0, nsc, prep, 0)

    ri = _iota2((c64, c64), 0)
    ci = _iota2((c64, c64), 1)
    incl = (ri >= ci, ri <= ci)
    strict = (ri > ci, ri < ci)

    def local(c, carry):
        base = pl.multiple_of(c * c64, c64)
        gin = gin_ref[0, pl.ds(base, c64), :]
        gc = gc_s[pl.ds(base, c64), :]
        beta = beta_s[pl.ds(base, c64), :]
        tbase = pl.multiple_of(c * 128, 128)
        for d in range(2):
            us, wqs, iks, cds = [], [], [], []
            for j in range(2):
                qh = gin[:, j * HD:(j + 1) * HD]
                kh = gin[:, 128 + j * HD:128 + (j + 1) * HD]
                vh = gin[:, 256 + j * HD:256 + (j + 1) * HD]
                kk = _mm_nt(kh, kh)
                qk = _mm_nt(qh, kh)
                gcc = jnp.where(hp == 0, gc[:, SM_A + d * 4 + j:SM_A + d * 4 + j + 1],
                                gc[:, SM_A + d * 4 + 2 + j:SM_A + d * 4 + 2 + j + 1])
                bcol = jnp.where(hp == 0, beta[:, SM_B + d * 4 + j:SM_B + d * 4 + j + 1],
                                 beta[:, SM_B + d * 4 + 2 + j:SM_B + d * 4 + 2 + j + 1])
                gcr = gct_s[pl.ds(tbase + SM_A + d * 4 + 2 * hp + j, 1), :]
                dec = jnp.exp(jnp.where(incl[d], gcc - gcr, NEG))
                a = jnp.where(strict[d], kk * dec, 0.0) * bcol
                tinv = _unit_tri_inverse(a)
                eg = jnp.exp(gcc)
                end = gcc[c64 - 1:c64, :] if d == 0 else gcc[0:1, :]
                rhs = jnp.concatenate([vh * bcol, kh * (bcol * eg)], axis=1)
                uw = _mm3(tinv, rhs)
                us.append(uw[:, :HD])
                wqs.append(jnp.concatenate([uw[:, HD:], qh * eg], axis=0))
                kd = kh * jnp.exp(end - gcc)
                iks.append(jnp.concatenate([qk * dec, kd.T], axis=0))
                cds.append(jnp.broadcast_to(jnp.exp(end), (8, HD)))
            slot = c * 2 + d
            u_s[slot] = jnp.concatenate(us, axis=1)
            wq_s[slot] = jnp.concatenate(wqs, axis=1).astype(BF16)
            ik_s[slot] = jnp.concatenate(iks, axis=1).astype(BF16)
            cd_s[slot] = jnp.concatenate(cds, axis=1)
        return carry

    lax.fori_loop(0, nch, local, 0)

    def step(t, states):
        cf = t
        cb = jnp.where(t < ctx_chunks, ctx_chunks - 1 - t, nch - 1 + ctx_chunks - t)
        new_states = []
        for d, c in ((0, cf), (1, cb)):
            slot = c * 2 + d
            u = u_s[slot]
            wq = wq_s[slot]
            ik = ik_s[slot]
            cd = cd_s[slot]
            outs = []
            for j in range(2):
                s = states[d * 2 + j]
                sl = slice(j * HD, (j + 1) * HD)
                r1 = jnp.dot(wq[:, sl], s.astype(BF16), preferred_element_type=F32)
                vn = u[:, sl] - r1[:c64]
                r2 = jnp.dot(ik[:, sl], vn.astype(BF16), preferred_element_type=F32)
                outs.append(r1[c64:] + r2[:c64])
                new_states.append(s * cd[0:1, sl] + r2[c64:])
            o = jnp.concatenate(outs, axis=1)
            rows = pl.ds(pl.multiple_of(c * c64, c64), c64)
            if d == 0:
                of_s[rows, :] = o
            else:
                ob_s[rows, :] = o
        return tuple(new_states)

    zero = jnp.zeros((HD, HD), F32)
    lax.fori_loop(0, nch, step, (zero, zero, zero, zero))
    o_ref[0] = of_s[...] + ob_s[...]


def _gdn(gdn_in, small, prm):
    bsz, t, _ = gdn_in.shape
    nch = t // GDN_CHUNK
    return pl.pallas_call(
        functools.partial(_gdn_kernel, nch, CTX // GDN_CHUNK),
        out_shape=jax.ShapeDtypeStruct((bsz, t, GDN_W), F32),
        grid=(bsz, 2),
        in_specs=[pl.BlockSpec((1, t, 384), lambda b, p: (b, 0, p)),
                  pl.BlockSpec((1, t, 128), lambda b, p: (b, 0, 0)),
                  pl.BlockSpec((8, 128), lambda b, p: (0, 0))],
        out_specs=pl.BlockSpec((1, t, 128), lambda b, p: (b, 0, p)),
        scratch_shapes=[pltpu.VMEM((t, 128), F32),
                        pltpu.VMEM((nch * 128, GDN_CHUNK), F32),
                        pltpu.VMEM((t, 128), F32),
                        pltpu.VMEM((nch * 2, GDN_CHUNK, 128), F32),
                        pltpu.VMEM((nch * 2, 128, 128), BF16),
                        pltpu.VMEM((nch * 2, 128, 128), BF16),
                        pltpu.VMEM((nch * 2, 8, 128), F32),
                        pltpu.VMEM((t, 128), F32),
                        pltpu.VMEM((t, 128), F32)],
        compiler_params=pltpu.CompilerParams(dimension_semantics=("arbitrary", "arbitrary"),
                                             vmem_limit_bytes=VMEM_LIMIT),
        name="gdn",
    )(gdn_in, small, prm)


def _attn_kernel(first_tile, q_ref, kv_ref, o_ref):
    i = pl.program_id(1) + first_tile
    tq = q_ref.shape[1]
    scale = HD ** -0.5

    def run(nkeys):
        q = q_ref[0]
        outs = []
        for g in range(ATT_KVH):
            k = kv_ref[0, :nkeys, g * HD:(g + 1) * HD]
            v = kv_ref[0, :nkeys, 128 + g * HD:128 + (g + 1) * HD]
            qs = jnp.concatenate([q[:, (3 * g + j) * HD:(3 * g + j + 1) * HD] for j in range(3)], axis=0)
            s = _mm_nt(qs, k) * scale
            m = jnp.max(s, axis=-1, keepdims=True)
            p = jnp.exp(s - m)
            l = jnp.sum(p, axis=-1, keepdims=True)
            o = _mm(p, v) / l
            outs.extend([o[j * tq:(j + 1) * tq] for j in range(3)])
        o_ref[0] = jnp.concatenate(outs, axis=1)

    @pl.when(i == 0)
    def _():
        run(CTX)

    @pl.when(i > 0)
    def _():
        run(kv_ref.shape[1])


def _attention(q, kv, first_tile):
    bsz, t, _ = q.shape
    nt = t // TM - first_tile
    return pl.pallas_call(
        functools.partial(_attn_kernel, first_tile),
        out_shape=jax.ShapeDtypeStruct((bsz, t, ATT_W), F32),
        grid=(bsz, nt),
        in_specs=[pl.BlockSpec((1, TM, ATT_W), lambda b, i: (b, i + first_tile, 0)),
                  pl.BlockSpec((1, t, 256), lambda b, i: (b, 0, 0))],
        out_specs=pl.BlockSpec((1, TM, ATT_W), lambda b, i: (b, i + first_tile, 0)),
        compiler_params=pltpu.CompilerParams(dimension_semantics=("arbitrary", "arbitrary"),
                                             vmem_limit_bytes=VMEM_LIMIT),
        name="attention",
    )(q, kv)


def _out_kernel(x_ref, y_ref, o_ref, a_ref, zs_ref, mod_ref, sw_ref, gw_ref, w_ref, out_ref):
    zs = zs_ref[0]
    t = y_ref[0] * zs[:, :SSD_W]
    ssd = t * lax.rsqrt(jnp.mean(t * t, axis=-1, keepdims=True) + EPS) * sw_ref[...]
    o = o_ref[0]
    gdn = o * lax.rsqrt(_group_sum64(o * o) * (1.0 / HD) + EPS) * gw_ref[...] * zs[:, SSD_W:SSD_W + GDN_W]
    att = a_ref[0] * zs[:, SSD_W + GDN_W:]
    mix = jnp.concatenate([ssd, gdn, att], axis=1).astype(BF16)
    gate = mod_ref[0][:, 2 * D_MODEL:]
    out_ref[0] = x_ref[0] + gate * jnp.dot(mix, w_ref[...], preferred_element_type=F32)


def _out_projection(xcat, y, o, a, zs, mod3, ssd_nw, gdn_nw, w_out, first_tile):
    bsz, t, _ = xcat.shape
    nt = t // TM - first_tile
    row = lambda b, i: (b, i + first_tile, 0)
    full = lambda b, i: (0, 0)
    return pl.pallas_call(
        _out_kernel,
        out_shape=jax.ShapeDtypeStruct((bsz, nt * TM, D_MODEL), F32),
        grid=(bsz, nt),
        in_specs=[pl.BlockSpec((1, TM, D_MODEL), row),
                  pl.BlockSpec((1, TM, SSD_W), row),
                  pl.BlockSpec((1, TM, GDN_W), row),
                  pl.BlockSpec((1, TM, ATT_W), row),
                  pl.BlockSpec((1, TM, D_MODEL), row),
                  pl.BlockSpec((1, 1, 3 * D_MODEL),
                               lambda b, i: (jnp.where(i + first_tile == 0, bsz, b), 0, 0)),
                  pl.BlockSpec((1, SSD_W), full),
                  pl.BlockSpec((1, GDN_W), full),
                  pl.BlockSpec((D_MODEL, D_MODEL), full)],
        out_specs=pl.BlockSpec((1, TM, D_MODEL), lambda b, i: (b, i, 0)),
        compiler_params=pltpu.CompilerParams(dimension_semantics=("arbitrary", "arbitrary"),
                                             vmem_limit_bytes=VMEM_LIMIT),
        name="out_projection",
    )(xcat, y, o, a, zs, mod3, ssd_nw, gdn_nw, w_out)


def _conv_perm():
    idx = list(range(640))
    for pair in range(2):
        for part in range(3):
            start = 640 + part * 256 + pair * 128
            idx.extend(range(start, start + 128))
    return idx


def _rest_perm():
    seg = lambda a, n: list(range(a, a + n))
    idx = (seg(1408, 384) + seg(1804, 256) + seg(2076, 384) + seg(2460, 128) + seg(2588, 128)
           + seg(2716, 384) + seg(1792, 12) + seg(2060, 8) + seg(2068, 8))
    return idx


def _rope_tables(t):
    pos = jnp.arange(t - CTX, dtype=jnp.int32)
    n_freq = HD // 4
    freqs = ROPE_THETA ** (-jnp.arange(n_freq, dtype=F32) / n_freq)
    ang_r = (pos // GRID_W).astype(F32)[:, None] * freqs
    ang_c = (pos % GRID_W).astype(F32)[:, None] * freqs
    cos = jnp.concatenate([jnp.cos(ang_r)] * 2 + [jnp.cos(ang_c)] * 2, axis=1)
    sin = jnp.concatenate([-jnp.sin(ang_r), jnp.sin(ang_r), -jnp.sin(ang_c), jnp.sin(ang_c)], axis=1)
    cos = jnp.concatenate([jnp.ones((CTX, HD), F32), cos], axis=0)
    sin = jnp.concatenate([jnp.zeros((CTX, HD), F32), sin], axis=0)
    return jnp.tile(cos, (1, 2)), jnp.tile(sin, (1, 2))


def _lane_row(values, offset):
    row = jnp.zeros((128,), F32)
    return lax.dynamic_update_slice(row, values.reshape(-1).astype(F32), (offset,))


def kernel(x, c, ctx, c_ctx, norm_w, w_mod, b_mod, w_in, conv_w, conv_b, ssd_A_log, ssd_dt_bias, ssd_D,
           ssd_norm_w, gdn_A_log, gdn_dt_bias, gdn_norm_w, q_norm_w, k_norm_w, w_out):
    bsz = x.shape[0]
    t = CTX + x.shape[1]
    xcat = jnp.concatenate([ctx, x], axis=1)
    c_all = jnp.concatenate([c, c_ctx[None, :], jnp.zeros((7, D_MODEL), F32)], axis=0)
    cos_t, sin_t = _rope_tables(t)
    cperm = jnp.asarray(_conv_perm(), jnp.int32)
    rperm = jnp.asarray(_rest_perm(), jnp.int32)

    for layer in range(DEPTH):
        last = layer == DEPTH - 1
        first_tile = 1 if last else 0
        mod3 = _modulation(c_all, w_mod[layer], b_mod[layer]).reshape(bsz + 8, 1, 3 * D_MODEL)

        w_conv = w_in[layer][:, :CONV_DIM][:, cperm].astype(BF16)
        w_rest = jnp.pad(w_in[layer][:, rperm], ((0, 0), (0, REST_DIM - rperm.shape[0]))).astype(BF16)
        cw = conv_w[layer][:, cperm]
        cb = conv_b[layer][cperm].reshape(1, CONV_DIM)
        qk_w = jnp.concatenate([jnp.tile(q_norm_w[layer], ATT_QH), jnp.tile(k_norm_w[layer], ATT_KVH)]).reshape(1, 512)

        ssd_in, gdn_in, zs, q, kv, small = _in_projection(
            xcat, mod3, norm_w[layer].reshape(1, D_MODEL), w_conv, w_rest, cw, cb, qk_w, cos_t, sin_t)

        ssd_prm = jnp.zeros((8, 128), F32)
        ssd_prm = ssd_prm.at[0].set(_lane_row(ssd_A_log[layer], SM_DT))
        ssd_prm = ssd_prm.at[1].set(_lane_row(ssd_dt_bias[layer], SM_DT))
        dskip = jnp.repeat(ssd_D[layer], HD).reshape(1, SSD_W)
        y = _ssd(ssd_in, small, ssd_prm, dskip)

        gdn_prm = jnp.zeros((8, 128), F32)
        gdn_prm = gdn_prm.at[0].set(_lane_row(gdn_A_log[layer], SM_A))
        gdn_prm = gdn_prm.at[1].set(_lane_row(gdn_dt_bias[layer], SM_A))
        o = _gdn(gdn_in, small, gdn_prm)

        a = _attention(q, kv, first_tile)

        xcat = _out_projection(xcat, y, o, a, zs, mod3, ssd_norm_w[layer].reshape(1, SSD_W),
                               jnp.tile(gdn_norm_w[layer], GDN_HEADS).reshape(1, GDN_W),
                               w_out[layer].astype(BF16), first_tile)
    return xcat
```

```python
import functools
import math

import jax
import jax.numpy as jnp
from jax import lax
from jax.experimental import pallas as pl
from jax.experimental.pallas import tpu as pltpu

F32 = jnp.float32
BF16 = jnp.bfloat16

D_MODEL = 1024
CTX = 256
GRID_W = 64
EPS = 1e-6
DEPTH = 2

HD = 64
SSD_HEADS = 6
SSD_W = SSD_HEADS * HD
SSD_CHUNK = 128
GDN_HEADS = 4
GDN_W = GDN_HEADS * HD
GDN_CHUNK = 64
ATT_QH = 6
ATT_KVH = 2
ATT_W = ATT_QH * HD
ROPE_THETA = 10000.0

CONV_DIM = 1408
REST_DIM = 1792
TM = 256
HALO = 8
NEG = -1e30
VMEM_LIMIT = 56 * 1024 * 1024

SM_DT = 0
SM_A = 12
SM_B = 20


def _mm(a, b):
    return jnp.dot(a.astype(BF16), b.astype(BF16), preferred_element_type=F32)


def _mm_nt(a, b):
    return lax.dot_general(a.astype(BF16), b.astype(BF16), (((1,), (1,)), ((), ())),
                           preferred_element_type=F32)


def _mm_tn(a, b):
    return lax.dot_general(a.astype(BF16), b.astype(BF16), (((0,), (0,)), ((), ())),
                           preferred_element_type=F32)


def _split3(a):
    h = a.astype(BF16)
    r = a - h.astype(F32)
    m = r.astype(BF16)
    l = (r - m.astype(F32)).astype(BF16)
    return h, m, l


def _mm_exact_rhs(a, b_bf16):
    h, m, l = _split3(a)
    d = functools.partial(jnp.dot, preferred_element_type=F32)
    return d(h, b_bf16) + (d(m, b_bf16) + d(l, b_bf16))


def _mm_exact_lhs(a_bf16, b):
    h, m, l = _split3(b)
    d = functools.partial(jnp.dot, preferred_element_type=F32)
    return d(a_bf16, h) + (d(a_bf16, m) + d(a_bf16, l))


def _mm3(a, b):
    ah = a.astype(BF16)
    al = (a - ah.astype(F32)).astype(BF16)
    bh = b.astype(BF16)
    bl = (b - bh.astype(F32)).astype(BF16)
    d = functools.partial(jnp.dot, preferred_element_type=F32)
    return d(ah, bh) + (d(ah, bl) + d(al, bh))


def _silu(x):
    return x * jax.nn.sigmoid(x)


def _softplus(x):
    return jnp.maximum(x, 0.0) + jnp.log1p(jnp.exp(-jnp.abs(x)))


def _iota2(shape, dim):
    return lax.broadcasted_iota(jnp.int32, shape, dim)


def _group_sum64(xx):
    r = jnp.right_shift(_iota2((128, 128), 0), 6)
    c = jnp.right_shift(_iota2((128, 128), 1), 6)
    g = (r == c).astype(BF16)
    outs = []
    for t in range(xx.shape[1] // 128):
        outs.append(_mm_exact_rhs(xx[:, t * 128:(t + 1) * 128], g))
    return outs[0] if len(outs) == 1 else jnp.concatenate(outs, axis=1)


def _mod_kernel(c_ref, w_ref, b_ref, o_ref):
    o_ref[...] = _mm(_silu(c_ref[...]), w_ref[...]) + b_ref[...]


def _modulation(c_all, w_mod, b_mod):
    n = w_mod.shape[1]
    bn = 768
    rows = c_all.shape[0]
    return pl.pallas_call(
        _mod_kernel,
        out_shape=jax.ShapeDtypeStruct((rows, n), F32),
        grid=(n // bn,),
        in_specs=[pl.BlockSpec((rows, D_MODEL), lambda j: (0, 0)),
                  pl.BlockSpec((D_MODEL, bn), lambda j: (0, j)),
                  pl.BlockSpec((1, bn), lambda j: (0, j))],
        out_specs=pl.BlockSpec((rows, bn), lambda j: (0, j)),
        compiler_params=pltpu.CompilerParams(dimension_semantics=("arbitrary",),
                                             vmem_limit_bytes=VMEM_LIMIT),
        name="modulation",
    )(c_all, w_mod, b_mod.reshape(1, n))


def _proj_kernel(nt, xm_ref, xp_ref, xn_ref, mod_ref, nw_ref, wc_ref, wr_ref, cw_ref, cb_ref,
                 qkw_ref, cos_ref, sin_ref,
                 ssd_ref, gdn_ref, zs_ref, q_ref, kv_ref, small_ref):
    i = pl.program_id(1)
    m = mod_ref[0]
    gain = nw_ref[...] * (1.0 + m[:, D_MODEL:2 * D_MODEL])
    shift = m[:, :D_MODEL]

    xe = jnp.concatenate([xm_ref[0], xp_ref[0], xn_ref[0]], axis=0)
    ms = jnp.mean(xe * xe, axis=-1, keepdims=True)
    he = (xe * lax.rsqrt(ms + EPS) * gain + shift).astype(BF16)

    p = jnp.dot(he, wc_ref[...], preferred_element_type=F32)
    r = jnp.dot(he[:TM], wr_ref[...], preferred_element_type=F32)

    pm = p[:TM]
    seg_first = i <= 1
    seg_last = jnp.logical_or(i == 0, i == nt - 1)
    prev_row = jnp.where(seg_first, 0.0, p[TM + HALO - 1:TM + HALO])
    next_row = jnp.where(seg_last, 0.0, p[TM + HALO:TM + HALO + 1])
    rows = _iota2((TM, 1), 0)
    pm1 = jnp.where(rows == 0, prev_row, pltpu.roll(pm, 1, axis=0))
    pp1 = jnp.where(rows == TM - 1, next_row, pltpu.roll(pm, TM - 1, axis=0))
    cw = cw_ref[...]
    conv = cw[0:1] * pm1 + cw[1:2] * pm + cw[2:3] * pp1 + cb_ref[...]
    co = _silu(conv)

    ssd_ref[0] = co[:, :640]
    for pair in range(2):
        base = 640 + pair * 384
        qk = co[:, base:base + 256]
        qk = qk * lax.rsqrt(_group_sum64(qk * qk) + EPS)
        gdn_ref[0, :, pair * 384:pair * 384 + 128] = qk[:, :128] * (HD ** -0.5)
        gdn_ref[0, :, pair * 384 + 128:pair * 384 + 256] = qk[:, 128:]
        gdn_ref[0, :, pair * 384 + 256:pair * 384 + 384] = co[:, base + 256:base + 384]

    z1 = r[:, :640]
    z2 = r[:, 1280:1664]
    zs_ref[0, :, :640] = _silu(z1)
    zs_ref[0, :, 640:] = _silu(z2)

    aqk = r[:, 640:1152]
    aqk = aqk * lax.rsqrt(_group_sum64(aqk * aqk) * (1.0 / HD) + EPS) * qkw_ref[...]
    lane = _iota2((1, 512), 1)
    swapped = jnp.where(jnp.bitwise_and(lane, 31) < 16,
                        pltpu.roll(aqk, 512 - 16, axis=1), pltpu.roll(aqk, 16, axis=1))
    cos = jnp.concatenate([cos_ref[...]] * 4, axis=1)
    sin = jnp.concatenate([sin_ref[...]] * 4, axis=1)
    aqk = aqk * cos + swapped * sin
    q_ref[0] = aqk[:, :384]
    kv_ref[0, :, :128] = aqk[:, 384:]
    kv_ref[0, :, 128:] = r[:, 1152:1280]
    small_ref[0] = r[:, 1664:1792]


def _in_projection(xcat, mod3, norm_w, w_conv, w_rest, conv_w, conv_b, qk_w, cos_t, sin_t):
    bsz, t, _ = xcat.shape
    nt = t // TM
    nblk8 = t // HALO
    per_tile = TM // HALO
    row = lambda b, i: (b, i, 0)
    full = lambda b, i: (0, 0)
    outs = [jax.ShapeDtypeStruct((bsz, t, w), F32) for w in (640, 768, 1024, 384, 256, 128)]
    return pl.pallas_call(
        functools.partial(_proj_kernel, nt),
        out_shape=outs,
        grid=(bsz, nt),
        in_specs=[
            pl.BlockSpec((1, TM, D_MODEL), row),
            pl.BlockSpec((1, HALO, D_MODEL), lambda b, i: (b, jnp.maximum(i * per_tile - 1, 0), 0)),
            pl.BlockSpec((1, HALO, D_MODEL), lambda b, i: (b, jnp.minimum((i + 1) * per_tile, nblk8 - 1), 0)),
            pl.BlockSpec((1, 1, 3 * D_MODEL), lambda b, i: (jnp.where(i == 0, bsz, b), 0, 0)),
            pl.BlockSpec((1, D_MODEL), full),
            pl.BlockSpec((D_MODEL, CONV_DIM), full),
            pl.BlockSpec((D_MODEL, REST_DIM), full),
            pl.BlockSpec((3, CONV_DIM), full),
            pl.BlockSpec((1, CONV_DIM), full),
            pl.BlockSpec((1, 512), full),
            pl.BlockSpec((TM, 128), lambda b, i: (i, 0)),
            pl.BlockSpec((TM, 128), lambda b, i: (i, 0)),
        ],
        out_specs=[pl.BlockSpec((1, TM, w), row) for w in (640, 768, 1024, 384, 256, 128)],
        compiler_params=pltpu.CompilerParams(dimension_semantics=("arbitrary", "arbitrary"),
                                             vmem_limit_bytes=VMEM_LIMIT),
        name="in_projection",
    )(xcat, xcat, xcat, mod3, norm_w, w_conv, w_rest, conv_w, conv_b, qk_w, cos_t, sin_t)


def _ssd_kernel(nc, ctx_chunks, xin_ref, small_ref, prm_ref, dskip_ref, y_ref,
                cum_s, cumt_s, dt_s, dtt_s, y_s, st_s, sin_s):
    q = SSD_CHUNK
    ri = _iota2((q, q), 0)
    ci = _iota2((q, q), 1)
    ltri = (ri >= ci).astype(BF16)
    utri = (ri <= ci).astype(BF16)
    lane = _iota2((1, 128), 1)
    fwd_lane = lane < SSD_HEADS

    a_row = -jnp.exp(prm_ref[0:1, :])
    bias_row = prm_ref[1:2, :]

    def prep(c, carry):
        base = pl.multiple_of(c * q, q)
        dt = _softplus(small_ref[0, pl.ds(base, q), :] + bias_row)
        dta = dt * a_row
        cum = jnp.where(fwd_lane, _mm_exact_lhs(ltri, dta), _mm_exact_lhs(utri, dta))
        cum_s[pl.ds(base, q), :] = cum
        dt_s[pl.ds(base, q), :] = dt
        cumt_s[pl.ds(base, q), :] = cum.T
        dtt_s[pl.ds(base, q), :] = dt.T
        return carry

    lax.fori_loop(0, nc, prep, 0)

    lower = ri >= ci
    upper = ri <= ci
    dsk = dskip_ref[...]

    def diag(c, carry):
        base = pl.multiple_of(c * q, q)
        xin = xin_ref[0, pl.ds(base, q), :]
        cum = cum_s[pl.ds(base, q), :]
        cumt = cumt_s[pl.ds(base, q), :]
        dtc = dt_s[pl.ds(base, q), :]
        dtt = dtt_s[pl.ds(base, q), :]
        for g in range(2):
            bg = xin[:, 384 + g * HD:384 + (g + 1) * HD]
            cg = xin[:, 512 + g * HD:512 + (g + 1) * HD]
            sc = _mm_nt(cg, bg)
            xds = [[], []]
            for hg in range(3):
                h = g * 3 + hg
                xh = xin[:, h * HD:(h + 1) * HD]
                w = None
                for d in range(2):
                    col = d * SSD_HEADS + h
                    ccol = cum[:, col:col + 1]
                    diff = ccol - cumt[col:col + 1, :]
                    lm = jnp.exp(jnp.where(lower if d == 0 else upper, diff, NEG))
                    wd = lm * dtt[col:col + 1, :]
                    w = wd if w is None else w + wd
                    end = cum[q - 1:q, col:col + 1] if d == 0 else cum[0:1, col:col + 1]
                    xds[d].append(xh * (jnp.exp(end - ccol) * dtc[:, col:col + 1]))
                y_s[pl.ds(base, q), h * HD:(h + 1) * HD] = _mm(w * sc, xh) + dsk[:, h * HD:(h + 1) * HD] * xh
            for d in range(2):
                xd = jnp.concatenate(xds[d], axis=1)
                st_s[c * 4 + d * 2 + g] = _mm_tn(bg, xd)
        return carry

    lax.fori_loop(0, nc, diag, 0)

    orders = (list(range(nc)),
              list(range(ctx_chunks - 1, -1, -1)) + list(range(nc - 1, ctx_chunks - 1, -1)))
    for d in range(2):
        for g in range(2):
            s = jnp.zeros((HD, 3 * HD), F32)
            for c in orders[d]:
                slot = c * 4 + d * 2 + g
                sin_s[slot] = s
                row = c * q + (q - 1 if d == 0 else 0)
                decs = []
                for hg in range(3):
                    col = d * SSD_HEADS + g * 3 + hg
                    decs.append(jnp.broadcast_to(jnp.exp(cum_s[row:row + 1, col:col + 1]), (1, HD)))
                s = s * jnp.concatenate(decs, axis=1) + st_s[slot]

    def off(c, carry):
        base = pl.multiple_of(c * q, q)
        xin = xin_ref[0, pl.ds(base, q), :]
        cum = cum_s[pl.ds(base, q), :]
        parts = []
        for g in range(2):
            cg = xin[:, 512 + g * HD:512 + (g + 1) * HD]
            acc = None
            for d in range(2):
                yo = _mm(cg, sin_s[c * 4 + d * 2 + g])
                es = []
                for hg in range(3):
                    col = d * SSD_HEADS + g * 3 + hg
                    es.append(jnp.broadcast_to(jnp.exp(cum[:, col:col + 1]), (q, HD)))
                yo = yo * jnp.concatenate(es, axis=1)
                acc = yo if acc is None else acc + yo
            parts.append(acc)
        y_ref[0, pl.ds(base, q), :] = y_s[pl.ds(base, q), :] + jnp.concatenate(parts, axis=1)
        return carry

    lax.fori_loop(0, nc, off, 0)


def _ssd(ssd_in, small, prm, dskip):
    bsz, t, _ = ssd_in.shape
    nc = t // SSD_CHUNK
    return pl.pallas_call(
        functools.partial(_ssd_kernel, nc, CTX // SSD_CHUNK),
        out_shape=jax.ShapeDtypeStruct((bsz, t, SSD_W), F32),
        grid=(bsz,),
        in_specs=[pl.BlockSpec((1, t, 640), lambda b: (b, 0, 0)),
                  pl.BlockSpec((1, t, 128), lambda b: (b, 0, 0)),
                  pl.BlockSpec((8, 128), lambda b: (0, 0)),
                  pl.BlockSpec((1, SSD_W), lambda b: (0, 0))],
        out_specs=pl.BlockSpec((1, t, SSD_W), lambda b: (b, 0, 0)),
        scratch_shapes=[pltpu.VMEM((t, 128), F32), pltpu.VMEM((t, 128), F32),
                        pltpu.VMEM((t, 128), F32), pltpu.VMEM((t, 128), F32),
                        pltpu.VMEM((t, SSD_W), F32),
                        pltpu.VMEM((nc * 4, HD, 3 * HD), F32),
                        pltpu.VMEM((nc * 4, HD, 3 * HD), F32)],
        compiler_params=pltpu.CompilerParams(dimension_semantics=("arbitrary",),
                                             vmem_limit_bytes=VMEM_LIMIT),
        name="ssd",
    )(ssd_in, small, prm, dskip)


def _bmm(a, b):
    return lax.dot_general(a.astype(BF16), b.astype(BF16), (((2,), (1,)), ((0,), (0,))),
                           preferred_element_type=F32)


def _bmm_nt(a, b):
    return lax.dot_general(a.astype(BF16), b.astype(BF16), (((2,), (2,)), ((0,), (0,))),
                           preferred_element_type=F32)


def _unit_tri_inverse(a_strict):
    n = a_strict.shape[-1]
    ri = _iota2((1, n, n), 1)
    ci = _iota2((1, n, n), 2)
    eye = (ri == ci).astype(F32)

    def blk(shift):
        return jnp.right_shift(ri, shift) == jnp.right_shift(ci, shift)

    nd = jnp.where(blk(3), -a_strict, 0.0)
    qk = nd
    p = eye + nd
    for lvl in range(2):
        qk = _bmm(qk, qk)
        p = p + _bmm(p, qk)
    inv = p
    for shift in (4, 5, 6):
        e = jnp.where(jnp.logical_and(blk(shift), jnp.logical_not(blk(shift - 1))), a_strict, 0.0)
        inv = inv - _bmm(_bmm(inv, e), inv)
    return inv


def _gdn_kernel(nch, ctx_chunks, gin_ref, small_ref, prm_ref, o_ref,
                gc_s, gct_s, beta_s, mq_s, n_s, cd_s, of_s, ob_s):
    c64 = GDN_CHUNK
    nsc = nch // 2
    ri128 = _iota2((128, 128), 0)
    ci128 = _iota2((128, 128), 1)
    same = jnp.right_shift(ri128, 6) == jnp.right_shift(ci128, 6)
    lbd = jnp.logical_and(same, ri128 >= ci128).astype(BF16)
    ubd = jnp.logical_and(same, ri128 <= ci128).astype(BF16)
    lane = _iota2((1, 128), 1)
    fwd_lane = lane < SM_A + GDN_HEADS

    a_row = -jnp.exp(prm_ref[0:1, :])
    bias_row = prm_ref[1:2, :]

    def prep(sc, carry):
        base = pl.multiple_of(sc * 128, 128)
        sm = small_ref[0, pl.ds(base, 128), :]
        g = a_row * _softplus(sm + bias_row)
        cum = jnp.where(fwd_lane, _mm_exact_lhs(lbd, g), _mm_exact_lhs(ubd, g))
        gc_s[pl.ds(base, 128), :] = cum
        beta_s[pl.ds(base, 128), :] = jax.nn.sigmoid(sm)
        cumt = cum.T
        gct_s[pl.ds(pl.multiple_of(sc * 256, 256), 128), :] = cumt[:, :c64]
        gct_s[pl.ds(pl.multiple_of(sc * 256 + 128, 128), 128), :] = cumt[:, c64:]
        return carry

    lax.fori_loop(0, nsc, prep, 0)

    ri = _iota2((c64, c64), 0)
    ci = _iota2((c64, c64), 1)
    incl = (ri >= ci, ri <= ci)
    strict = (ri > ci, ri < ci)

    def local(c, carry):
        base = pl.multiple_of(c * c64, c64)
        gin = gin_ref[0, pl.ds(base, c64), :]
        gc = gc_s[pl.ds(base, c64), :]
        beta = beta_s[pl.ds(base, c64), :]
        tbase = pl.multiple_of(c * 128, 128)
        qs, ks, vs = [], [], []
        for h in range(GDN_HEADS):
            off = (h // 2) * 384 + (h % 2) * HD
            qs.append(gin[:, off:off + HD])
            ks.append(gin[:, off + 128:off + 128 + HD])
            vs.append(gin[:, off + 256:off + 256 + HD])
        kst = jnp.stack(ks)
        kk = _bmm_nt(kst, kst)
        qk = _bmm_nt(jnp.stack(qs), kst)
        a_l, rhs_l, intra_l, kdt_l, qd_l, cd_l = [], [], [], [], [], []
        for d in range(2):
            for h in range(GDN_HEADS):
                colg = SM_A + d * GDN_HEADS + h
                colb = SM_B + d * GDN_HEADS + h
                gcc = gc[:, colg:colg + 1]
                bcol = beta[:, colb:colb + 1]
                gcr = gct_s[pl.ds(tbase + colg, 1), :]
                dec = jnp.exp(jnp.where(incl[d], gcc - gcr, NEG))
                a_l.append(jnp.where(strict[d], kk[h] * dec, 0.0) * bcol)
                eg = jnp.exp(gcc)
                end = gcc[c64 - 1:c64, :] if d == 0 else gcc[0:1, :]
                rhs_l.append(jnp.concatenate([vs[h] * bcol, ks[h] * (bcol * eg)], axis=1))
                intra_l.append(qk[h] * dec)
                kdt_l.append((ks[h] * jnp.exp(end - gcc)).T)
                qd_l.append(qs[h] * eg)
                cd_l.append(jnp.broadcast_to(jnp.exp(end), (8, HD)))
        tinv = _unit_tri_inverse(jnp.stack(a_l))
        uw = _bmm(tinv, jnp.stack(rhs_l))
        x1 = _bmm(jnp.stack(kdt_l), uw)
        x2 = _bmm(jnp.stack(intra_l), uw)
        for d in range(2):
            slot = c * 2 + d
            ids = [d * GDN_HEADS + h for h in range(GDN_HEADS)]
            mq = [jnp.concatenate([-x1[n][:, HD:], qd_l[n] - x2[n][:, HD:]], axis=0) for n in ids]
            mq_s[slot] = jnp.concatenate(mq, axis=1).astype(BF16)
            n_s[slot] = jnp.concatenate([x1[n][:, :HD] for n in ids], axis=1)
            cd_s[slot] = jnp.concatenate([cd_l[n] for n in ids], axis=1)
            oc = jnp.concatenate([x2[n][:, :HD] for n in ids], axis=1)
            if d == 0:
                of_s[pl.ds(base, c64), :] = oc
            else:
                ob_s[pl.ds(base, c64), :] = oc
        return carry

    lax.fori_loop(0, nch, local, 0)

    def step(t, states):
        cf = t
        cb = jnp.where(t < ctx_chunks, ctx_chunks - 1 - t, nch - 1 + ctx_chunks - t)
        new_states = []
        for d, c in ((0, cf), (1, cb)):
            slot = c * 2 + d
            mq = mq_s[slot]
            nn = n_s[slot]
            cd = cd_s[slot]
            outs = []
            for h in range(GDN_HEADS):
                s = states[d * GDN_HEADS + h]
                sl = slice(h * HD, (h + 1) * HD)
                r = jnp.dot(mq[:, sl], s.astype(BF16), preferred_element_type=F32)
                new_states.append(s * cd[0:1, sl] + r[:c64] + nn[:, sl])
                outs.append(r[c64:])
            rows = pl.ds(pl.multiple_of(c * c64, c64), c64)
            if d == 0:
                of_s[rows, :] = of_s[rows, :] + jnp.concatenate(outs, axis=1)
            else:
                ob_s[rows, :] = ob_s[rows, :] + jnp.concatenate(outs, axis=1)
        return tuple(new_states)

    zero = jnp.zeros((HD, HD), F32)
    lax.fori_loop(0, nch, step, (zero,) * (2 * GDN_HEADS))
    o_ref[0] = of_s[...] + ob_s[...]


def _gdn(gdn_in, small, prm):
    bsz, t, _ = gdn_in.shape
    nch = t // GDN_CHUNK
    return pl.pallas_call(
        functools.partial(_gdn_kernel, nch, CTX // GDN_CHUNK),
        out_shape=jax.ShapeDtypeStruct((bsz, t, GDN_W), F32),
        grid=(bsz,),
        in_specs=[pl.BlockSpec((1, t, 768), lambda b: (b, 0, 0)),
                  pl.BlockSpec((1, t, 128), lambda b: (b, 0, 0)),
                  pl.BlockSpec((8, 128), lambda b: (0, 0))],
        out_specs=pl.BlockSpec((1, t, GDN_W), lambda b: (b, 0, 0)),
        scratch_shapes=[pltpu.VMEM((t, 128), F32),
                        pltpu.VMEM((nch * 128, GDN_CHUNK), F32),
                        pltpu.VMEM((t, 128), F32),
                        pltpu.VMEM((nch * 2, 128, GDN_W), BF16),
                        pltpu.VMEM((nch * 2, GDN_CHUNK, GDN_W), F32),
                        pltpu.VMEM((nch * 2, 8, GDN_W), F32),
                        pltpu.VMEM((t, GDN_W), F32),
                        pltpu.VMEM((t, GDN_W), F32)],
        compiler_params=pltpu.CompilerParams(dimension_semantics=("arbitrary",),
                                             vmem_limit_bytes=VMEM_LIMIT),
        name="gdn",
    )(gdn_in, small, prm)


def _attn_kernel(first_tile, q_ref, kv_ref, o_ref):
    i = pl.program_id(1) + first_tile
    tq = q_ref.shape[1]
    scale = HD ** -0.5

    def run(nkeys):
        q = q_ref[0]
        outs = []
        for g in range(ATT_KVH):
            k = kv_ref[0, :nkeys, g * HD:(g + 1) * HD]
            v = kv_ref[0, :nkeys, 128 + g * HD:128 + (g + 1) * HD]
            qs = jnp.concatenate([q[:, (3 * g + j) * HD:(3 * g + j + 1) * HD] for j in range(3)], axis=0)
            s = _mm_nt(qs, k) * scale
            m = jnp.max(s, axis=-1, keepdims=True)
            p = jnp.exp(s - m)
            l = jnp.sum(p, axis=-1, keepdims=True)
            o = _mm(p, v) / l
            outs.extend([o[j * tq:(j + 1) * tq] for j in range(3)])
        o_ref[0] = jnp.concatenate(outs, axis=1)

    @pl.when(i == 0)
    def _():
        run(CTX)

    @pl.when(i > 0)
    def _():
        run(kv_ref.shape[1])


def _attention(q, kv, first_tile):
    bsz, t, _ = q.shape
    nt = t // TM - first_tile
    return pl.pallas_call(
        functools.partial(_attn_kernel, first_tile),
        out_shape=jax.ShapeDtypeStruct((bsz, nt * TM, ATT_W), F32),
        grid=(bsz, nt),
        in_specs=[pl.BlockSpec((1, TM, ATT_W), lambda b, i: (b, i + first_tile, 0)),
                  pl.BlockSpec((1, t, 256), lambda b, i: (b, 0, 0))],
        out_specs=pl.BlockSpec((1, TM, ATT_W), lambda b, i: (b, i, 0)),
        compiler_params=pltpu.CompilerParams(dimension_semantics=("arbitrary", "arbitrary"),
                                             vmem_limit_bytes=VMEM_LIMIT),
        name="attention",
    )(q, kv)


def _out_kernel(x_ref, y_ref, o_ref, a_ref, zs_ref, mod_ref, sw_ref, gw_ref, w_ref, out_ref):
    zs = zs_ref[0]
    t = y_ref[0] * zs[:, :SSD_W]
    ssd = t * lax.rsqrt(jnp.mean(t * t, axis=-1, keepdims=True) + EPS) * sw_ref[...]
    o = o_ref[0]
    gdn = o * lax.rsqrt(_group_sum64(o * o) * (1.0 / HD) + EPS) * gw_ref[...] * zs[:, SSD_W:SSD_W + GDN_W]
    att = a_ref[0] * zs[:, SSD_W + GDN_W:]
    mix = jnp.concatenate([ssd, gdn, att], axis=1).astype(BF16)
    gate = mod_ref[0][:, 2 * D_MODEL:]
    out_ref[0] = x_ref[0] + gate * jnp.dot(mix, w_ref[...], preferred_element_type=F32)


def _out_projection(xcat, y, o, a, zs, mod3, ssd_nw, gdn_nw, w_out, first_tile):
    bsz, t, _ = xcat.shape
    nt = t // TM - first_tile
    row = lambda b, i: (b, i + first_tile, 0)
    full = lambda b, i: (0, 0)
    return pl.pallas_call(
        _out_kernel,
        out_shape=jax.ShapeDtypeStruct((bsz, nt * TM, D_MODEL), F32),
        grid=(bsz, nt),
        in_specs=[pl.BlockSpec((1, TM, D_MODEL), row),
                  pl.BlockSpec((1, TM, SSD_W), row),
                  pl.BlockSpec((1, TM, GDN_W), row),
                  pl.BlockSpec((1, TM, ATT_W), lambda b, i: (b, i, 0)),
                  pl.BlockSpec((1, TM, D_MODEL), row),
                  pl.BlockSpec((1, 1, 3 * D_MODEL),
                               lambda b, i: (jnp.where(i + first_tile == 0, bsz, b), 0, 0)),
                  pl.BlockSpec((1, SSD_W), full),
                  pl.BlockSpec((1, GDN_W), full),
                  pl.BlockSpec((D_MODEL, D_MODEL), full)],
        out_specs=pl.BlockSpec((1, TM, D_MODEL), lambda b, i: (b, i, 0)),
        compiler_params=pltpu.CompilerParams(dimension_semantics=("arbitrary", "arbitrary"),
                                             vmem_limit_bytes=VMEM_LIMIT),
        name="out_projection",
    )(xcat, y, o, a, zs, mod3, ssd_nw, gdn_nw, w_out)


def _conv_perm():
    idx = list(range(640))
    for pair in range(2):
        for part in range(3):
            start = 640 + part * 256 + pair * 128
            idx.extend(range(start, start + 128))
    return idx


def _rest_perm():
    seg = lambda a, n: list(range(a, a + n))
    idx = (seg(1408, 384) + seg(1804, 256) + seg(2076, 384) + seg(2460, 128) + seg(2588, 128)
           + seg(2716, 384) + seg(1792, 12) + seg(2060, 8) + seg(2068, 8))
    return idx


def _rope_tables(t):
    pos = jnp.arange(t - CTX, dtype=jnp.int32)
    n_freq = HD // 4
    freqs = ROPE_THETA ** (-jnp.arange(n_freq, dtype=F32) / n_freq)
    ang_r = (pos // GRID_W).astype(F32)[:, None] * freqs
    ang_c = (pos % GRID_W).astype(F32)[:, None] * freqs
    cos = jnp.concatenate([jnp.cos(ang_r)] * 2 + [jnp.cos(ang_c)] * 2, axis=1)
    sin = jnp.concatenate([-jnp.sin(ang_r), jnp.sin(ang_r), -jnp.sin(ang_c), jnp.sin(ang_c)], axis=1)
    cos = jnp.concatenate([jnp.ones((CTX, HD), F32), cos], axis=0)
    sin = jnp.concatenate([jnp.zeros((CTX, HD), F32), sin], axis=0)
    return jnp.tile(cos, (1, 2)), jnp.tile(sin, (1, 2))


def _lane_row(values, offset):
    row = jnp.zeros((128,), F32)
    return lax.dynamic_update_slice(row, values.reshape(-1).astype(F32), (offset,))


def kernel(x, c, ctx, c_ctx, norm_w, w_mod, b_mod, w_in, conv_w, conv_b, ssd_A_log, ssd_dt_bias, ssd_D,
           ssd_norm_w, gdn_A_log, gdn_dt_bias, gdn_norm_w, q_norm_w, k_norm_w, w_out):
    bsz = x.shape[0]
    t = CTX + x.shape[1]
    xcat = jnp.concatenate([ctx, x], axis=1)
    c_all = jnp.concatenate([c, c_ctx[None, :], jnp.zeros((7, D_MODEL), F32)], axis=0)
    cos_t, sin_t = _rope_tables(t)
    cperm = jnp.asarray(_conv_perm(), jnp.int32)
    rperm = jnp.asarray(_rest_perm(), jnp.int32)

    for layer in range(DEPTH):
        last = layer == DEPTH - 1
        first_tile = 1 if last else 0
        mod3 = _modulation(c_all, w_mod[layer], b_mod[layer]).reshape(bsz + 8, 1, 3 * D_MODEL)

        w_conv = w_in[layer][:, :CONV_DIM][:, cperm].astype(BF16)
        w_rest = jnp.pad(w_in[layer][:, rperm], ((0, 0), (0, REST_DIM - rperm.shape[0]))).astype(BF16)
        cw = conv_w[layer][:, cperm]
        cb = conv_b[layer][cperm].reshape(1, CONV_DIM)
        qk_w = jnp.concatenate([jnp.tile(q_norm_w[layer], ATT_QH), jnp.tile(k_norm_w[layer], ATT_KVH)]).reshape(1, 512)

        ssd_in, gdn_in, zs, q, kv, small = _in_projection(
            xcat, mod3, norm_w[layer].reshape(1, D_MODEL), w_conv, w_rest, cw, cb, qk_w, cos_t, sin_t)

        ssd_prm = jnp.zeros((8, 128), F32)
        ssd_prm = ssd_prm.at[0].set(_lane_row(ssd_A_log[layer], SM_DT))
        ssd_prm = ssd_prm.at[1].set(_lane_row(ssd_dt_bias[layer], SM_DT))
        dskip = jnp.repeat(ssd_D[layer], HD).reshape(1, SSD_W)
        y = _ssd(ssd_in, small, ssd_prm, dskip)

        gdn_prm = jnp.zeros((8, 128), F32)
        gdn_prm = gdn_prm.at[0].set(_lane_row(gdn_A_log[layer], SM_A))
        gdn_prm = gdn_prm.at[1].set(_lane_row(gdn_dt_bias[layer], SM_A))
        o = _gdn(gdn_in, small, gdn_prm)

        a = _attention(q, kv, first_tile)

        xcat = _out_projection(xcat, y, o, a, zs, mod3, ssd_norm_w[layer].reshape(1, SSD_W),
                               jnp.tile(gdn_norm_w[layer], GDN_HEADS).reshape(1, GDN_W),
                               w_out[layer].astype(BF16), first_tile)
    return xcat
```

```python
import functools
import math

import jax
import jax.numpy as jnp
import numpy as np
from jax import lax
from jax.experimental import pallas as pl
from jax.experimental.pallas import tpu as pltpu

F32 = jnp.float32
BF16 = jnp.bfloat16

D_MODEL = 1024
CTX = 256
GRID_W = 64
EPS = 1e-6
DEPTH = 2

HD = 64
SSD_HEADS = 6
SSD_W = SSD_HEADS * HD
SSD_CHUNK = 128
GDN_HEADS = 4
GDN_W = GDN_HEADS * HD
GDN_CHUNK = 64
ATT_QH = 6
ATT_KVH = 2
ATT_W = ATT_QH * HD
ROPE_THETA = 10000.0
Q_PRESCALE = (HD ** -0.5) * math.log2(math.e)

CONV_DIM = 1408
REST_DIM = 1792
TM = 256
HALO = 8
NEG = -1e30
VMEM_LIMIT = 56 * 1024 * 1024

SM_DT = 0
SM_A = 12
SM_B = 20


def _mm(a, b):
    return jnp.dot(a.astype(BF16), b.astype(BF16), preferred_element_type=F32)


def _mm_nt(a, b):
    return lax.dot_general(a.astype(BF16), b.astype(BF16), (((1,), (1,)), ((), ())),
                           preferred_element_type=F32)


def _mm_tn(a, b):
    return lax.dot_general(a.astype(BF16), b.astype(BF16), (((0,), (0,)), ((), ())),
                           preferred_element_type=F32)


def _split3(a):
    h = a.astype(BF16)
    r = a - h.astype(F32)
    m = r.astype(BF16)
    l = (r - m.astype(F32)).astype(BF16)
    return h, m, l


def _mm_exact_rhs(a, b_bf16):
    h, m, l = _split3(a)
    d = functools.partial(jnp.dot, preferred_element_type=F32)
    return d(h, b_bf16) + (d(m, b_bf16) + d(l, b_bf16))


def _mm_exact_lhs(a_bf16, b):
    h, m, l = _split3(b)
    d = functools.partial(jnp.dot, preferred_element_type=F32)
    return d(a_bf16, h) + (d(a_bf16, m) + d(a_bf16, l))


def _mm3(a, b):
    ah = a.astype(BF16)
    al = (a - ah.astype(F32)).astype(BF16)
    bh = b.astype(BF16)
    bl = (b - bh.astype(F32)).astype(BF16)
    d = functools.partial(jnp.dot, preferred_element_type=F32)
    return d(ah, bh) + (d(ah, bl) + d(al, bh))


def _silu(x):
    return x * jax.nn.sigmoid(x)


def _softplus(x):
    return jnp.maximum(x, 0.0) + jnp.log1p(jnp.exp(-jnp.abs(x)))


def _iota2(shape, dim):
    return lax.broadcasted_iota(jnp.int32, shape, dim)


def _group_sum64(xx):
    r = jnp.right_shift(_iota2((128, 128), 0), 6)
    c = jnp.right_shift(_iota2((128, 128), 1), 6)
    g = (r == c).astype(BF16)
    outs = []
    for t in range(xx.shape[1] // 128):
        outs.append(_mm_exact_rhs(xx[:, t * 128:(t + 1) * 128], g))
    return outs[0] if len(outs) == 1 else jnp.concatenate(outs, axis=1)


def _mod_kernel(c_ref, w_ref, b_ref, o_ref):
    o_ref[...] = _mm(_silu(c_ref[...]), w_ref[...]) + b_ref[...]


def _modulation(c_all, w_mod, b_mod):
    n = w_mod.shape[1]
    bn = 768
    rows = c_all.shape[0]
    return pl.pallas_call(
        _mod_kernel,
        out_shape=jax.ShapeDtypeStruct((rows, n), F32),
        grid=(n // bn,),
        in_specs=[pl.BlockSpec((rows, D_MODEL), lambda j: (0, 0)),
                  pl.BlockSpec((D_MODEL, bn), lambda j: (0, j)),
                  pl.BlockSpec((1, bn), lambda j: (0, j))],
        out_specs=pl.BlockSpec((rows, bn), lambda j: (0, j)),
        compiler_params=pltpu.CompilerParams(dimension_semantics=("arbitrary",),
                                             vmem_limit_bytes=VMEM_LIMIT),
        name="modulation",
    )(c_all, w_mod, b_mod.reshape(1, n))


def _proj_kernel(nt, xm_ref, xp_ref, xn_ref, mod_ref, nw_ref, wc_ref, wr_ref, cw_ref, cb_ref,
                 qkw_ref, cos_ref, sin_ref,
                 ssd_ref, gdn_ref, zs_ref, q_ref, kv_ref, small_ref):
    i = pl.program_id(1)
    m = mod_ref[0]
    gain = nw_ref[...] * (1.0 + m[:, D_MODEL:2 * D_MODEL])
    shift = m[:, :D_MODEL]

    xe = jnp.concatenate([xm_ref[0], xp_ref[0], xn_ref[0]], axis=0)
    ms = jnp.mean(xe * xe, axis=-1, keepdims=True)
    he = (xe * lax.rsqrt(ms + EPS) * gain + shift).astype(BF16)

    p = jnp.dot(he, wc_ref[...], preferred_element_type=F32)
    r = jnp.dot(he[:TM], wr_ref[...], preferred_element_type=F32)

    pm = p[:TM]
    seg_first = i <= 1
    seg_last = jnp.logical_or(i == 0, i == nt - 1)
    prev_row = jnp.where(seg_first, 0.0, p[TM + HALO - 1:TM + HALO])
    next_row = jnp.where(seg_last, 0.0, p[TM + HALO:TM + HALO + 1])
    rows = _iota2((TM, 1), 0)
    pm1 = jnp.where(rows == 0, prev_row, pltpu.roll(pm, 1, axis=0))
    pp1 = jnp.where(rows == TM - 1, next_row, pltpu.roll(pm, TM - 1, axis=0))
    cw = cw_ref[...]
    conv = cw[0:1] * pm1 + cw[1:2] * pm + cw[2:3] * pp1 + cb_ref[...]
    co = _silu(conv)

    ssd_ref[0] = co[:, :640]
    for pair in range(2):
        base = 640 + pair * 384
        qk = co[:, base:base + 256]
        qk = qk * lax.rsqrt(_group_sum64(qk * qk) + EPS)
        gdn_ref[0, :, pair * 384:pair * 384 + 128] = qk[:, :128] * (HD ** -0.5)
        gdn_ref[0, :, pair * 384 + 128:pair * 384 + 256] = qk[:, 128:]
        gdn_ref[0, :, pair * 384 + 256:pair * 384 + 384] = co[:, base + 256:base + 384]

    z1 = r[:, :640]
    z2 = r[:, 1280:1664]
    zs_ref[0, :, :640] = _silu(z1)
    zs_ref[0, :, 640:] = _silu(z2)

    aqk = r[:, 640:1152]
    aqk = aqk * lax.rsqrt(_group_sum64(aqk * aqk) * (1.0 / HD) + EPS) * qkw_ref[...]
    lane = _iota2((1, 512), 1)
    swapped = jnp.where(jnp.bitwise_and(lane, 31) < 16,
                        pltpu.roll(aqk, 512 - 16, axis=1), pltpu.roll(aqk, 16, axis=1))
    cos = jnp.concatenate([cos_ref[...]] * 4, axis=1)
    sin = jnp.concatenate([sin_ref[...]] * 4, axis=1)
    aqk = aqk * cos + swapped * sin
    q_ref[0] = (aqk[:, :384] * Q_PRESCALE).astype(BF16)
    ones = jnp.ones((TM, HD), F32)
    v = r[:, 1152:1280]
    kv_ref[0] = jnp.concatenate([aqk[:, 384:], v[:, :HD], ones, v[:, HD:], ones], axis=1).astype(BF16)
    small_ref[0] = r[:, 1664:1792]


def _in_projection(xcat, mod3, norm_w, w_conv, w_rest, conv_w, conv_b, qk_w, cos_t, sin_t):
    bsz, t, _ = xcat.shape
    nt = t // TM
    nblk8 = t // HALO
    per_tile = TM // HALO
    row = lambda b, i: (b, i, 0)
    full = lambda b, i: (0, 0)
    outs = [jax.ShapeDtypeStruct((bsz, t, w), dt)
            for w, dt in ((640, F32), (768, F32), (1024, F32), (384, BF16), (384, BF16), (128, F32))]
    return pl.pallas_call(
        functools.partial(_proj_kernel, nt),
        out_shape=outs,
        grid=(bsz, nt),
        in_specs=[
            pl.BlockSpec((1, TM, D_MODEL), row),
            pl.BlockSpec((1, HALO, D_MODEL), lambda b, i: (b, jnp.maximum(i * per_tile - 1, 0), 0)),
            pl.BlockSpec((1, HALO, D_MODEL), lambda b, i: (b, jnp.minimum((i + 1) * per_tile, nblk8 - 1), 0)),
            pl.BlockSpec((1, 1, 3 * D_MODEL), lambda b, i: (jnp.where(i == 0, bsz, b), 0, 0)),
            pl.BlockSpec((1, D_MODEL), full),
            pl.BlockSpec((D_MODEL, CONV_DIM), full),
            pl.BlockSpec((D_MODEL, REST_DIM), full),
            pl.BlockSpec((3, CONV_DIM), full),
            pl.BlockSpec((1, CONV_DIM), full),
            pl.BlockSpec((1, 512), full),
            pl.BlockSpec((TM, 128), lambda b, i: (i, 0)),
            pl.BlockSpec((TM, 128), lambda b, i: (i, 0)),
        ],
        out_specs=[pl.BlockSpec((1, TM, w), row) for w in (640, 768, 1024, 384, 384, 128)],
        compiler_params=pltpu.CompilerParams(dimension_semantics=("arbitrary", "arbitrary"),
                                             vmem_limit_bytes=VMEM_LIMIT),
        name="in_projection",
    )(xcat, xcat, xcat, mod3, norm_w, w_conv, w_rest, conv_w, conv_b, qk_w, cos_t, sin_t)


def _ssd_kernel(nc, ctx_chunks, xin_ref, small_ref, prm_ref, dskip_ref, y_ref,
                cum_s, cumt_s, dt_s, dtt_s, y_s, st_s, sin_s):
    q = SSD_CHUNK
    ri = _iota2((q, q), 0)
    ci = _iota2((q, q), 1)
    ltri = (ri >= ci).astype(BF16)
    utri = (ri <= ci).astype(BF16)
    lane = _iota2((1, 128), 1)
    fwd_lane = lane < SSD_HEADS

    a_row = -jnp.exp(prm_ref[0:1, :])
    bias_row = prm_ref[1:2, :]

    def prep(c, carry):
        base = pl.multiple_of(c * q, q)
        dt = _softplus(small_ref[0, pl.ds(base, q), :] + bias_row)
        dta = dt * a_row
        cum = jnp.where(fwd_lane, _mm_exact_lhs(ltri, dta), _mm_exact_lhs(utri, dta))
        cum_s[pl.ds(base, q), :] = cum
        dt_s[pl.ds(base, q), :] = dt
        cumt_s[pl.ds(base, q), :] = cum.T
        dtt_s[pl.ds(base, q), :] = dt.T
        return carry

    lax.fori_loop(0, nc, prep, 0)

    lower = ri >= ci
    upper = ri <= ci
    dsk = dskip_ref[...]

    def diag(c, carry):
        base = pl.multiple_of(c * q, q)
        xin = xin_ref[0, pl.ds(base, q), :]
        cum = cum_s[pl.ds(base, q), :]
        cumt = cumt_s[pl.ds(base, q), :]
        dtc = dt_s[pl.ds(base, q), :]
        dtt = dtt_s[pl.ds(base, q), :]
        for g in range(2):
            bg = xin[:, 384 + g * HD:384 + (g + 1) * HD]
            cg = xin[:, 512 + g * HD:512 + (g + 1) * HD]
            sc = _mm_nt(cg, bg)
            xds = [[], []]
            for hg in range(3):
                h = g * 3 + hg
                xh = xin[:, h * HD:(h + 1) * HD]
                w = None
                for d in range(2):
                    col = d * SSD_HEADS + h
                    ccol = cum[:, col:col + 1]
                    diff = ccol - cumt[col:col + 1, :]
                    lm = jnp.exp(jnp.where(lower if d == 0 else upper, diff, NEG))
                    wd = lm * dtt[col:col + 1, :]
                    w = wd if w is None else w + wd
                    end = cum[q - 1:q, col:col + 1] if d == 0 else cum[0:1, col:col + 1]
                    xds[d].append(xh * (jnp.exp(end - ccol) * dtc[:, col:col + 1]))
                y_s[pl.ds(base, q), h * HD:(h + 1) * HD] = _mm(w * sc, xh) + dsk[:, h * HD:(h + 1) * HD] * xh
            for d in range(2):
                xd = jnp.concatenate(xds[d], axis=1)
                st_s[c * 4 + d * 2 + g] = _mm_tn(bg, xd)
        return carry

    lax.fori_loop(0, nc, diag, 0)

    orders = (list(range(nc)),
              list(range(ctx_chunks - 1, -1, -1)) + list(range(nc - 1, ctx_chunks - 1, -1)))
    for d in range(2):
        for g in range(2):
            s = jnp.zeros((HD, 3 * HD), F32)
            for c in orders[d]:
                slot = c * 4 + d * 2 + g
                sin_s[slot] = s
                row = c * q + (q - 1 if d == 0 else 0)
                decs = []
                for hg in range(3):
                    col = d * SSD_HEADS + g * 3 + hg
                    decs.append(jnp.broadcast_to(jnp.exp(cum_s[row:row + 1, col:col + 1]), (1, HD)))
                s = s * jnp.concatenate(decs, axis=1) + st_s[slot]

    def off(c, carry):
        base = pl.multiple_of(c * q, q)
        xin = xin_ref[0, pl.ds(base, q), :]
        cum = cum_s[pl.ds(base, q), :]
        parts = []
        for g in range(2):
            cg = xin[:, 512 + g * HD:512 + (g + 1) * HD]
            acc = None
            for d in range(2):
                yo = _mm(cg, sin_s[c * 4 + d * 2 + g])
                es = []
                for hg in range(3):
                    col = d * SSD_HEADS + g * 3 + hg
                    es.append(jnp.broadcast_to(jnp.exp(cum[:, col:col + 1]), (q, HD)))
                yo = yo * jnp.concatenate(es, axis=1)
                acc = yo if acc is None else acc + yo
            parts.append(acc)
        y_ref[0, pl.ds(base, q), :] = y_s[pl.ds(base, q), :] + jnp.concatenate(parts, axis=1)
        return carry

    lax.fori_loop(0, nc, off, 0)


def _ssd(ssd_in, small, prm, dskip):
    bsz, t, _ = ssd_in.shape
    nc = t // SSD_CHUNK
    return pl.pallas_call(
        functools.partial(_ssd_kernel, nc, CTX // SSD_CHUNK),
        out_shape=jax.ShapeDtypeStruct((bsz, t, SSD_W), F32),
        grid=(bsz,),
        in_specs=[pl.BlockSpec((1, t, 640), lambda b: (b, 0, 0)),
                  pl.BlockSpec((1, t, 128), lambda b: (b, 0, 0)),
                  pl.BlockSpec((8, 128), lambda b: (0, 0)),
                  pl.BlockSpec((1, SSD_W), lambda b: (0, 0))],
        out_specs=pl.BlockSpec((1, t, SSD_W), lambda b: (b, 0, 0)),
        scratch_shapes=[pltpu.VMEM((t, 128), F32), pltpu.VMEM((t, 128), F32),
                        pltpu.VMEM((t, 128), F32), pltpu.VMEM((t, 128), F32),
                        pltpu.VMEM((t, SSD_W), F32),
                        pltpu.VMEM((nc * 4, HD, 3 * HD), F32),
                        pltpu.VMEM((nc * 4, HD, 3 * HD), F32)],
        compiler_params=pltpu.CompilerParams(dimension_semantics=("arbitrary",),
                                             vmem_limit_bytes=VMEM_LIMIT),
        name="ssd",
    )(ssd_in, small, prm, dskip)


def _bmm(a, b):
    return lax.dot_general(a.astype(BF16), b.astype(BF16), (((2,), (1,)), ((0,), (0,))),
                           preferred_element_type=F32)


def _bmm_nt(a, b):
    return lax.dot_general(a.astype(BF16), b.astype(BF16), (((2,), (2,)), ((0,), (0,))),
                           preferred_element_type=F32)


def _unit_tri_inverse(a_strict):
    n = a_strict.shape[-1]
    ri = _iota2((1, n, n), 1)
    ci = _iota2((1, n, n), 2)
    eye = (ri == ci).astype(F32)

    def blk(shift):
        return jnp.right_shift(ri, shift) == jnp.right_shift(ci, shift)

    nd = jnp.where(blk(3), -a_strict, 0.0)
    qk = nd
    p = eye + nd
    for lvl in range(2):
        qk = _bmm(qk, qk)
        p = p + _bmm(p, qk)
    inv = p
    for shift in (4, 5, 6):
        e = jnp.where(jnp.logical_and(blk(shift), jnp.logical_not(blk(shift - 1))), a_strict, 0.0)
        inv = inv - _bmm(_bmm(inv, e), inv)
    return inv


def _gdn_kernel(nch, ctx_chunks, gin_ref, small_ref, prm_ref, o_ref,
                gc_s, gct_s, beta_s, mq_s, n_s, cd_s, of_s, ob_s):
    c64 = GDN_CHUNK
    nsc = nch // 2
    ri128 = _iota2((128, 128), 0)
    ci128 = _iota2((128, 128), 1)
    same = jnp.right_shift(ri128, 6) == jnp.right_shift(ci128, 6)
    lbd = jnp.logical_and(same, ri128 >= ci128).astype(BF16)
    ubd = jnp.logical_and(same, ri128 <= ci128).astype(BF16)
    lane = _iota2((1, 128), 1)
    fwd_lane = lane < SM_A + GDN_HEADS

    a_row = -jnp.exp(prm_ref[0:1, :])
    bias_row = prm_ref[1:2, :]

    def prep(sc, carry):
        base = pl.multiple_of(sc * 128, 128)
        sm = small_ref[0, pl.ds(base, 128), :]
        g = a_row * _softplus(sm + bias_row)
        cum = jnp.where(fwd_lane, _mm_exact_lhs(lbd, g), _mm_exact_lhs(ubd, g))
        gc_s[pl.ds(base, 128), :] = cum
        beta_s[pl.ds(base, 128), :] = jax.nn.sigmoid(sm)
        cumt = cum.T
        gct_s[pl.ds(pl.multiple_of(sc * 256, 256), 128), :] = cumt[:, :c64]
        gct_s[pl.ds(pl.multiple_of(sc * 256 + 128, 128), 128), :] = cumt[:, c64:]
        return carry

    lax.fori_loop(0, nsc, prep, 0)

    ri = _iota2((c64, c64), 0)
    ci = _iota2((c64, c64), 1)
    incl = (ri >= ci, ri <= ci)
    strict = (ri > ci, ri < ci)

    def local(c, carry):
        base = pl.multiple_of(c * c64, c64)
        gin = gin_ref[0, pl.ds(base, c64), :]
        gc = gc_s[pl.ds(base, c64), :]
        beta = beta_s[pl.ds(base, c64), :]
        tbase = pl.multiple_of(c * 128, 128)
        qs, ks, vs = [], [], []
        for h in range(GDN_HEADS):
            off = (h // 2) * 384 + (h % 2) * HD
            qs.append(gin[:, off:off + HD])
            ks.append(gin[:, off + 128:off + 128 + HD])
            vs.append(gin[:, off + 256:off + 256 + HD])
        kst = jnp.stack(ks)
        kk = _bmm_nt(kst, kst)
        qk = _bmm_nt(jnp.stack(qs), kst)
        a_l, rhs_l, intra_l, kdt_l, qd_l, cd_l = [], [], [], [], [], []
        for d in range(2):
            for h in range(GDN_HEADS):
                colg = SM_A + d * GDN_HEADS + h
                colb = SM_B + d * GDN_HEADS + h
                gcc = gc[:, colg:colg + 1]
                bcol = beta[:, colb:colb + 1]
                gcr = gct_s[pl.ds(tbase + colg, 1), :]
                dec = jnp.exp(jnp.where(incl[d], gcc - gcr, NEG))
                a_l.append(jnp.where(strict[d], kk[h] * dec, 0.0) * bcol)
                eg = jnp.exp(gcc)
                end = gcc[c64 - 1:c64, :] if d == 0 else gcc[0:1, :]
                rhs_l.append(jnp.concatenate([vs[h] * bcol, ks[h] * (bcol * eg)], axis=1))
                intra_l.append(qk[h] * dec)
                kdt_l.append((ks[h] * jnp.exp(end - gcc)).T)
                qd_l.append(qs[h] * eg)
                cd_l.append(jnp.broadcast_to(jnp.exp(end), (8, HD)))
        tinv = _unit_tri_inverse(jnp.stack(a_l))
        uw = _bmm(tinv, jnp.stack(rhs_l))
        x1 = _bmm(jnp.stack(kdt_l), uw)
        x2 = _bmm(jnp.stack(intra_l), uw)
        for d in range(2):
            slot = c * 2 + d
            ids = [d * GDN_HEADS + h for h in range(GDN_HEADS)]
            mq = [jnp.concatenate([-x1[n][:, HD:], qd_l[n] - x2[n][:, HD:]], axis=0) for n in ids]
            mq_s[slot] = jnp.concatenate(mq, axis=1).astype(BF16)
            n_s[slot] = jnp.concatenate([x1[n][:, :HD] for n in ids], axis=1)
            cd_s[slot] = jnp.concatenate([cd_l[n] for n in ids], axis=1)
            oc = jnp.concatenate([x2[n][:, :HD] for n in ids], axis=1)
            if d == 0:
                of_s[pl.ds(base, c64), :] = oc
            else:
                ob_s[pl.ds(base, c64), :] = oc
        return carry

    lax.fori_loop(0, nch, local, 0)

    def step(t, states):
        cf = t
        cb = jnp.where(t < ctx_chunks, ctx_chunks - 1 - t, nch - 1 + ctx_chunks - t)
        new_states = []
        for d, c in ((0, cf), (1, cb)):
            slot = c * 2 + d
            mq = mq_s[slot]
            nn = n_s[slot]
            cd = cd_s[slot]
            outs = []
            for h in range(GDN_HEADS):
                s = states[d * GDN_HEADS + h]
                sl = slice(h * HD, (h + 1) * HD)
                r = jnp.dot(mq[:, sl], s.astype(BF16), preferred_element_type=F32)
                new_states.append(s * cd[0:1, sl] + r[:c64] + nn[:, sl])
                outs.append(r[c64:])
            rows = pl.ds(pl.multiple_of(c * c64, c64), c64)
            if d == 0:
                of_s[rows, :] = of_s[rows, :] + jnp.concatenate(outs, axis=1)
            else:
                ob_s[rows, :] = ob_s[rows, :] + jnp.concatenate(outs, axis=1)
        return tuple(new_states)

    zero = jnp.zeros((HD, HD), F32)
    lax.fori_loop(0, nch, step, (zero,) * (2 * GDN_HEADS))
    o_ref[0] = of_s[...] + ob_s[...]


def _gdn(gdn_in, small, prm):
    bsz, t, _ = gdn_in.shape
    nch = t // GDN_CHUNK
    return pl.pallas_call(
        functools.partial(_gdn_kernel, nch, CTX // GDN_CHUNK),
        out_shape=jax.ShapeDtypeStruct((bsz, t, GDN_W), F32),
        grid=(bsz,),
        in_specs=[pl.BlockSpec((1, t, 768), lambda b: (b, 0, 0)),
                  pl.BlockSpec((1, t, 128), lambda b: (b, 0, 0)),
                  pl.BlockSpec((8, 128), lambda b: (0, 0))],
        out_specs=pl.BlockSpec((1, t, GDN_W), lambda b: (b, 0, 0)),
        scratch_shapes=[pltpu.VMEM((t, 128), F32),
                        pltpu.VMEM((nch * 128, GDN_CHUNK), F32),
                        pltpu.VMEM((t, 128), F32),
                        pltpu.VMEM((nch * 2, 128, GDN_W), BF16),
                        pltpu.VMEM((nch * 2, GDN_CHUNK, GDN_W), F32),
                        pltpu.VMEM((nch * 2, 8, GDN_W), F32),
                        pltpu.VMEM((t, GDN_W), F32),
                        pltpu.VMEM((t, GDN_W), F32)],
        compiler_params=pltpu.CompilerParams(dimension_semantics=("arbitrary",),
                                             vmem_limit_bytes=VMEM_LIMIT),
        name="gdn",
    )(gdn_in, small, prm)


def _attn_kernel(first_tile, q_ref, kv_ref, o_ref):
    i = pl.program_id(1) + first_tile

    def run(nkeys):
        q = q_ref[0]
        outs = []
        for g in range(ATT_KVH):
            k = kv_ref[0, :nkeys, g * HD:(g + 1) * HD].astype(BF16)
            va = kv_ref[0, :nkeys, 128 + g * 128:256 + g * 128].astype(BF16)
            for j in range(ATT_QH // ATT_KVH):
                h = g * (ATT_QH // ATT_KVH) + j
                s = _mm_nt(q[:, h * HD:(h + 1) * HD], k)
                m = jnp.max(s, axis=-1, keepdims=True)
                p = jnp.exp2(s - m)
                acc = _mm(p, va)
                outs.append(acc[:, :HD] / acc[:, HD:])
        o_ref[0] = jnp.concatenate(outs, axis=1)

    @pl.when(i == 0)
    def _():
        run(CTX)

    @pl.when(i > 0)
    def _():
        run(kv_ref.shape[1])


def _attention(q, kv, first_tile):
    bsz, t, _ = q.shape
    nt = t // TM - first_tile
    return pl.pallas_call(
        functools.partial(_attn_kernel, first_tile),
        out_shape=jax.ShapeDtypeStruct((bsz, nt * TM, ATT_W), F32),
        grid=(bsz, nt),
        in_specs=[pl.BlockSpec((1, TM, ATT_W), lambda b, i: (b, i + first_tile, 0)),
                  pl.BlockSpec((1, t, 384), lambda b, i: (b, 0, 0))],
        out_specs=pl.BlockSpec((1, TM, ATT_W), lambda b, i: (b, i, 0)),
        compiler_params=pltpu.CompilerParams(dimension_semantics=("arbitrary", "arbitrary"),
                                             vmem_limit_bytes=VMEM_LIMIT),
        name="attention",
    )(q, kv)


def _out_kernel(x_ref, y_ref, o_ref, a_ref, zs_ref, mod_ref, sw_ref, gw_ref, w_ref, out_ref):
    zs = zs_ref[0]
    t = y_ref[0] * zs[:, :SSD_W]
    ssd = t * lax.rsqrt(jnp.mean(t * t, axis=-1, keepdims=True) + EPS) * sw_ref[...]
    o = o_ref[0]
    gdn = o * lax.rsqrt(_group_sum64(o * o) * (1.0 / HD) + EPS) * gw_ref[...] * zs[:, SSD_W:SSD_W + GDN_W]
    att = a_ref[0] * zs[:, SSD_W + GDN_W:]
    mix = jnp.concatenate([ssd, gdn, att], axis=1).astype(BF16)
    gate = mod_ref[0][:, 2 * D_MODEL:]
    out_ref[0] = x_ref[0] + gate * jnp.dot(mix, w_ref[...], preferred_element_type=F32)


def _out_projection(xcat, y, o, a, zs, mod3, ssd_nw, gdn_nw, w_out, first_tile):
    bsz, t, _ = xcat.shape
    nt = t // TM - first_tile
    row = lambda b, i: (b, i + first_tile, 0)
    full = lambda b, i: (0, 0)
    return pl.pallas_call(
        _out_kernel,
        out_shape=jax.ShapeDtypeStruct((bsz, nt * TM, D_MODEL), F32),
        grid=(bsz, nt),
        in_specs=[pl.BlockSpec((1, TM, D_MODEL), row),
                  pl.BlockSpec((1, TM, SSD_W), row),
                  pl.BlockSpec((1, TM, GDN_W), row),
                  pl.BlockSpec((1, TM, ATT_W), lambda b, i: (b, i, 0)),
                  pl.BlockSpec((1, TM, D_MODEL), row),
                  pl.BlockSpec((1, 1, 3 * D_MODEL),
                               lambda b, i: (jnp.where(i + first_tile == 0, bsz, b), 0, 0)),
                  pl.BlockSpec((1, SSD_W), full),
                  pl.BlockSpec((1, GDN_W), full),
                  pl.BlockSpec((D_MODEL, D_MODEL), full)],
        out_specs=pl.BlockSpec((1, TM, D_MODEL), lambda b, i: (b, i, 0)),
        compiler_params=pltpu.CompilerParams(dimension_semantics=("arbitrary", "arbitrary"),
                                             vmem_limit_bytes=VMEM_LIMIT),
        name="out_projection",
    )(xcat, y, o, a, zs, mod3, ssd_nw, gdn_nw, w_out)


def _conv_perm():
    idx = list(range(640))
    for pair in range(2):
        for part in range(3):
            start = 640 + part * 256 + pair * 128
            idx.extend(range(start, start + 128))
    return idx


def _rest_perm():
    seg = lambda a, n: list(range(a, a + n))
    idx = (seg(1408, 384) + seg(1804, 256) + seg(2076, 384) + seg(2460, 128) + seg(2588, 128)
           + seg(2716, 384) + seg(1792, 12) + seg(2060, 8) + seg(2068, 8))
    return idx


def _rope_tables(t):
    f32 = np.float32
    pos = np.arange(t - CTX)
    n_freq = HD // 4
    freqs = np.power(f32(ROPE_THETA), -np.arange(n_freq, dtype=f32) / f32(n_freq)).astype(f32)
    ang_r = ((pos // GRID_W).astype(f32)[:, None] * freqs).astype(f32)
    ang_c = ((pos % GRID_W).astype(f32)[:, None] * freqs).astype(f32)
    cos = np.concatenate([np.cos(ang_r)] * 2 + [np.cos(ang_c)] * 2, axis=1)
    sin = np.concatenate([-np.sin(ang_r), np.sin(ang_r), -np.sin(ang_c), np.sin(ang_c)], axis=1)
    cos = np.concatenate([np.ones((CTX, HD), f32), cos], axis=0).astype(f32)
    sin = np.concatenate([np.zeros((CTX, HD), f32), sin], axis=0).astype(f32)
    return jnp.asarray(np.tile(cos, (1, 2))), jnp.asarray(np.tile(sin, (1, 2)))


def _lane_row(values, offset):
    row = jnp.zeros((128,), F32)
    return lax.dynamic_update_slice(row, values.reshape(-1).astype(F32), (offset,))


def kernel(x, c, ctx, c_ctx, norm_w, w_mod, b_mod, w_in, conv_w, conv_b, ssd_A_log, ssd_dt_bias, ssd_D,
           ssd_norm_w, gdn_A_log, gdn_dt_bias, gdn_norm_w, q_norm_w, k_norm_w, w_out):
    bsz = x.shape[0]
    t = CTX + x.shape[1]
    xcat = jnp.concatenate([ctx, x], axis=1)
    c_all = jnp.concatenate([c, c_ctx[None, :], jnp.zeros((7, D_MODEL), F32)], axis=0)
    cos_t, sin_t = _rope_tables(t)
    cperm = jnp.asarray(_conv_perm(), jnp.int32)
    rperm = jnp.asarray(_rest_perm(), jnp.int32)

    for layer in range(DEPTH):
        last = layer == DEPTH - 1
        first_tile = 1 if last else 0
        mod3 = _modulation(c_all, w_mod[layer], b_mod[layer]).reshape(bsz + 8, 1, 3 * D_MODEL)

        w_conv = w_in[layer][:, :CONV_DIM][:, cperm].astype(BF16)
        w_rest = jnp.pad(w_in[layer][:, rperm], ((0, 0), (0, REST_DIM - rperm.shape[0]))).astype(BF16)
        cw = conv_w[layer][:, cperm]
        cb = conv_b[layer][cperm].reshape(1, CONV_DIM)
        qk_w = jnp.concatenate([jnp.tile(q_norm_w[layer], ATT_QH), jnp.tile(k_norm_w[layer], ATT_KVH)]).reshape(1, 512)

        ssd_in, gdn_in, zs, q, kv, small = _in_projection(
            xcat, mod3, norm_w[layer].reshape(1, D_MODEL), w_conv, w_rest, cw, cb, qk_w, cos_t, sin_t)

        ssd_prm = jnp.zeros((8, 128), F32)
        ssd_prm = ssd_prm.at[0].set(_lane_row(ssd_A_log[layer], SM_DT))
        ssd_prm = ssd_prm.at[1].set(_lane_row(ssd_dt_bias[layer], SM_DT))
        dskip = jnp.repeat(ssd_D[layer], HD).reshape(1, SSD_W)
        y = _ssd(ssd_in, small, ssd_prm, dskip)

        gdn_prm = jnp.zeros((8, 128), F32)
        gdn_prm = gdn_prm.at[0].set(_lane_row(gdn_A_log[layer], SM_A))
        gdn_prm = gdn_prm.at[1].set(_lane_row(gdn_dt_bias[layer], SM_A))
        o = _gdn(gdn_in, small, gdn_prm)

        a = _attention(q, kv, first_tile)

        xcat = _out_projection(xcat, y, o, a, zs, mod3, ssd_norm_w[layer].reshape(1, SSD_W),
                               jnp.tile(gdn_norm_w[layer], GDN_HEADS).reshape(1, GDN_W),
                               w_out[layer].astype(BF16), first_tile)
    return xcat
```

```python
import functools
import math

import jax
import jax.numpy as jnp
import numpy as np
from jax import lax
from jax.experimental import pallas as pl
from jax.experimental.pallas import tpu as pltpu

F32 = jnp.float32
BF16 = jnp.bfloat16

D_MODEL = 1024
CTX = 256
GRID_W = 64
EPS = 1e-6
DEPTH = 2

HD = 64
SSD_HEADS = 6
SSD_W = SSD_HEADS * HD
SSD_CHUNK = 128
GDN_HEADS = 4
GDN_W = GDN_HEADS * HD
GDN_CHUNK = 64
ATT_QH = 6
ATT_KVH = 2
ATT_W = ATT_QH * HD
ROPE_THETA = 10000.0
Q_PRESCALE = (HD ** -0.5) * math.log2(math.e)

CONV_DIM = 1408
REST_DIM = 1792
TM = 256
HALO = 8
NEG = -1e30
VMEM_LIMIT = 56 * 1024 * 1024

SM_DT = 0
SM_A = 12
SM_B = 20


def _mm(a, b):
    return jnp.dot(a.astype(BF16), b.astype(BF16), preferred_element_type=F32)


def _mm_nt(a, b):
    return lax.dot_general(a.astype(BF16), b.astype(BF16), (((1,), (1,)), ((), ())),
                           preferred_element_type=F32)


def _mm_tn(a, b):
    return lax.dot_general(a.astype(BF16), b.astype(BF16), (((0,), (0,)), ((), ())),
                           preferred_element_type=F32)


def _split3(a):
    h = a.astype(BF16)
    r = a - h.astype(F32)
    m = r.astype(BF16)
    l = (r - m.astype(F32)).astype(BF16)
    return h, m, l


def _mm_exact_rhs(a, b_bf16):
    h, m, l = _split3(a)
    d = functools.partial(jnp.dot, preferred_element_type=F32)
    return d(h, b_bf16) + (d(m, b_bf16) + d(l, b_bf16))


def _mm_exact_lhs(a_bf16, b):
    h, m, l = _split3(b)
    d = functools.partial(jnp.dot, preferred_element_type=F32)
    return d(a_bf16, h) + (d(a_bf16, m) + d(a_bf16, l))


def _mm3(a, b):
    ah = a.astype(BF16)
    al = (a - ah.astype(F32)).astype(BF16)
    bh = b.astype(BF16)
    bl = (b - bh.astype(F32)).astype(BF16)
    d = functools.partial(jnp.dot, preferred_element_type=F32)
    return d(ah, bh) + (d(ah, bl) + d(al, bh))


def _silu(x):
    return x * jax.nn.sigmoid(x)


def _softplus(x):
    return jnp.maximum(x, 0.0) + jnp.log1p(jnp.exp(-jnp.abs(x)))


def _iota2(shape, dim):
    return lax.broadcasted_iota(jnp.int32, shape, dim)


def _group_sum64(xx):
    r = jnp.right_shift(_iota2((128, 128), 0), 6)
    c = jnp.right_shift(_iota2((128, 128), 1), 6)
    g = (r == c).astype(BF16)
    outs = []
    for t in range(xx.shape[1] // 128):
        outs.append(_mm_exact_rhs(xx[:, t * 128:(t + 1) * 128], g))
    return outs[0] if len(outs) == 1 else jnp.concatenate(outs, axis=1)


def _mod_kernel(c_ref, w_ref, b_ref, o_ref):
    o_ref[...] = _mm(_silu(c_ref[...]), w_ref[...]) + b_ref[...]


def _modulation(c_all, w_mod, b_mod):
    n = w_mod.shape[1]
    bn = 768
    rows = c_all.shape[0]
    return pl.pallas_call(
        _mod_kernel,
        out_shape=jax.ShapeDtypeStruct((rows, n), F32),
        grid=(n // bn,),
        in_specs=[pl.BlockSpec((rows, D_MODEL), lambda j: (0, 0)),
                  pl.BlockSpec((D_MODEL, bn), lambda j: (0, j)),
                  pl.BlockSpec((1, bn), lambda j: (0, j))],
        out_specs=pl.BlockSpec((rows, bn), lambda j: (0, j)),
        compiler_params=pltpu.CompilerParams(dimension_semantics=("arbitrary",),
                                             vmem_limit_bytes=VMEM_LIMIT),
        name="modulation",
    )(c_all, w_mod, b_mod.reshape(1, n))


def _proj_kernel(nt, xa_ref, xm_ref, xp_ref, xn_ref, mod_ref, nw_ref, wc_ref, wr_ref, cw_ref, cb_ref,
                 qkw_ref, cos_ref, sin_ref,
                 ssd_ref, gdn_ref, zs_ref, q_ref, kv_ref, small_ref):
    i = pl.program_id(1)
    m = mod_ref[0]
    gain = nw_ref[...] * (1.0 + m[:, D_MODEL:2 * D_MODEL])
    shift = m[:, :D_MODEL]

    xm = jnp.where(i == 0, xa_ref[0], xm_ref[0])
    xe = jnp.concatenate([xm, xp_ref[0], xn_ref[0]], axis=0)
    ms = jnp.mean(xe * xe, axis=-1, keepdims=True)
    he = (xe * lax.rsqrt(ms + EPS) * gain + shift).astype(BF16)

    p = jnp.dot(he, wc_ref[...], preferred_element_type=F32)
    r = jnp.dot(he[:TM], wr_ref[...], preferred_element_type=F32)

    pm = p[:TM]
    seg_first = i <= 1
    seg_last = jnp.logical_or(i == 0, i == nt - 1)
    prev_row = jnp.where(seg_first, 0.0, p[TM + HALO - 1:TM + HALO])
    next_row = jnp.where(seg_last, 0.0, p[TM + HALO:TM + HALO + 1])
    rows = _iota2((TM, 1), 0)
    pm1 = jnp.where(rows == 0, prev_row, pltpu.roll(pm, 1, axis=0))
    pp1 = jnp.where(rows == TM - 1, next_row, pltpu.roll(pm, TM - 1, axis=0))
    cw = cw_ref[...]
    conv = cw[0:1] * pm1 + cw[1:2] * pm + cw[2:3] * pp1 + cb_ref[...]
    co = _silu(conv)

    ssd_ref[0] = co[:, :640]
    for pair in range(2):
        base = 640 + pair * 384
        qk = co[:, base:base + 256]
        qk = qk * lax.rsqrt(_group_sum64(qk * qk) + EPS)
        gdn_ref[0, :, pair * 384:pair * 384 + 128] = qk[:, :128] * (HD ** -0.5)
        gdn_ref[0, :, pair * 384 + 128:pair * 384 + 256] = qk[:, 128:]
        gdn_ref[0, :, pair * 384 + 256:pair * 384 + 384] = co[:, base + 256:base + 384]

    z1 = r[:, :640]
    z2 = r[:, 1280:1664]
    zs_ref[0, :, :640] = _silu(z1).astype(BF16)
    zs_ref[0, :, 640:] = _silu(z2).astype(BF16)

    aqk = r[:, 640:1152]
    aqk = aqk * lax.rsqrt(_group_sum64(aqk * aqk) * (1.0 / HD) + EPS) * qkw_ref[...]
    lane = _iota2((1, 512), 1)
    swapped = jnp.where(jnp.bitwise_and(lane, 31) < 16,
                        pltpu.roll(aqk, 512 - 16, axis=1), pltpu.roll(aqk, 16, axis=1))
    cos = jnp.concatenate([cos_ref[...]] * 4, axis=1)
    sin = jnp.concatenate([sin_ref[...]] * 4, axis=1)
    aqk = aqk * cos + swapped * sin
    q_ref[0] = (aqk[:, :384] * Q_PRESCALE).astype(BF16)
    ones = jnp.ones((TM, HD), F32)
    v = r[:, 1152:1280]
    kv_ref[0] = jnp.concatenate([aqk[:, 384:], v[:, :HD], ones, v[:, HD:], ones], axis=1).astype(BF16)
    small_ref[0] = r[:, 1664:1792]


def _in_projection(xa, xb, off, mod3, norm_w, w_conv, w_rest, conv_w, conv_b, qk_w, cos_t, sin_t):
    bsz = xb.shape[0]
    nt = xb.shape[1] // TM + off
    t = nt * TM
    nblk8 = xb.shape[1] // HALO
    per_tile = TM // HALO
    row = lambda b, i: (b, i, 0)
    full = lambda b, i: (0, 0)
    outs = [jax.ShapeDtypeStruct((bsz, t, w), dt)
            for w, dt in ((640, F32), (768, F32), (1024, BF16), (384, BF16), (384, BF16), (128, F32))]
    return pl.pallas_call(
        functools.partial(_proj_kernel, nt),
        out_shape=outs,
        grid=(bsz, nt),
        in_specs=[
            pl.BlockSpec((1, TM, D_MODEL), lambda b, i: (b, 0, 0)),
            pl.BlockSpec((1, TM, D_MODEL), lambda b, i: (b, jnp.maximum(i - off, 0), 0)),
            pl.BlockSpec((1, HALO, D_MODEL), lambda b, i: (b, jnp.maximum((i - off) * per_tile - 1, 0), 0)),
            pl.BlockSpec((1, HALO, D_MODEL),
                         lambda b, i: (b, jnp.minimum((i - off + 1) * per_tile, nblk8 - 1), 0)),
            pl.BlockSpec((1, 1, 3 * D_MODEL), lambda b, i: (jnp.where(i == 0, bsz, b), 0, 0)),
            pl.BlockSpec((1, D_MODEL), full),
            pl.BlockSpec((D_MODEL, CONV_DIM), full),
            pl.BlockSpec((D_MODEL, REST_DIM), full),
            pl.BlockSpec((3, CONV_DIM), full),
            pl.BlockSpec((1, CONV_DIM), full),
            pl.BlockSpec((1, 512), full),
            pl.BlockSpec((TM, 128), lambda b, i: (i, 0)),
            pl.BlockSpec((TM, 128), lambda b, i: (i, 0)),
        ],
        out_specs=[pl.BlockSpec((1, TM, w), row) for w in (640, 768, 1024, 384, 384, 128)],
        compiler_params=pltpu.CompilerParams(dimension_semantics=("arbitrary", "arbitrary"),
                                             vmem_limit_bytes=VMEM_LIMIT),
        name="in_projection",
    )(xa, xb, xb, xb, mod3, norm_w, w_conv, w_rest, conv_w, conv_b, qk_w, cos_t, sin_t)


def _ssd_kernel(nc, ctx_chunks, xin_ref, small_ref, prm_ref, dskip_ref, y_ref,
                cum_s, cumt_s, dt_s, dtt_s, y_s, st_s, sin_s):
    q = SSD_CHUNK
    ri = _iota2((q, q), 0)
    ci = _iota2((q, q), 1)
    ltri = (ri >= ci).astype(BF16)
    utri = (ri <= ci).astype(BF16)
    lane = _iota2((1, 128), 1)
    fwd_lane = lane < SSD_HEADS

    a_row = -jnp.exp(prm_ref[0:1, :])
    bias_row = prm_ref[1:2, :]

    def prep(c, carry):
        base = pl.multiple_of(c * q, q)
        dt = _softplus(small_ref[0, pl.ds(base, q), :] + bias_row)
        dta = dt * a_row
        cum = jnp.where(fwd_lane, _mm_exact_lhs(ltri, dta), _mm_exact_lhs(utri, dta))
        cum_s[pl.ds(base, q), :] = cum
        dt_s[pl.ds(base, q), :] = dt
        cumt_s[pl.ds(base, q), :] = cum.T
        dtt_s[pl.ds(base, q), :] = dt.T
        return carry

    lax.fori_loop(0, nc, prep, 0)

    lower = ri >= ci
    upper = ri <= ci
    dsk = dskip_ref[...]

    def diag(c, carry):
        base = pl.multiple_of(c * q, q)
        xin = xin_ref[0, pl.ds(base, q), :]
        cum = cum_s[pl.ds(base, q), :]
        cumt = cumt_s[pl.ds(base, q), :]
        dtc = dt_s[pl.ds(base, q), :]
        dtt = dtt_s[pl.ds(base, q), :]
        for g in range(2):
            bg = xin[:, 384 + g * HD:384 + (g + 1) * HD]
            cg = xin[:, 512 + g * HD:512 + (g + 1) * HD]
            sc = _mm_nt(cg, bg)
            xds = [[], []]
            for hg in range(3):
                h = g * 3 + hg
                xh = xin[:, h * HD:(h + 1) * HD]
                w = None
                for d in range(2):
                    col = d * SSD_HEADS + h
                    ccol = cum[:, col:col + 1]
                    diff = ccol - cumt[col:col + 1, :]
                    lm = jnp.exp(jnp.where(lower if d == 0 else upper, diff, NEG))
                    wd = lm * dtt[col:col + 1, :]
                    w = wd if w is None else w + wd
                    end = cum[q - 1:q, col:col + 1] if d == 0 else cum[0:1, col:col + 1]
                    xds[d].append(xh * (jnp.exp(end - ccol) * dtc[:, col:col + 1]))
                y_s[pl.ds(base, q), h * HD:(h + 1) * HD] = _mm(w * sc, xh) + dsk[:, h * HD:(h + 1) * HD] * xh
            for d in range(2):
                xd = jnp.concatenate(xds[d], axis=1)
                st_s[c * 4 + d * 2 + g] = _mm_tn(bg, xd)
        return carry

    lax.fori_loop(0, nc, diag, 0)

    orders = (list(range(nc)),
              list(range(ctx_chunks - 1, -1, -1)) + list(range(nc - 1, ctx_chunks - 1, -1)))
    for d in range(2):
        for g in range(2):
            s = jnp.zeros((HD, 3 * HD), F32)
            for c in orders[d]:
                slot = c * 4 + d * 2 + g
                sin_s[slot] = s
                row = c * q + (q - 1 if d == 0 else 0)
                decs = []
                for hg in range(3):
                    col = d * SSD_HEADS + g * 3 + hg
                    decs.append(jnp.broadcast_to(jnp.exp(cum_s[row:row + 1, col:col + 1]), (1, HD)))
                s = s * jnp.concatenate(decs, axis=1) + st_s[slot]

    def off(c, carry):
        base = pl.multiple_of(c * q, q)
        xin = xin_ref[0, pl.ds(base, q), :]
        cum = cum_s[pl.ds(base, q), :]
        parts = []
        for g in range(2):
            cg = xin[:, 512 + g * HD:512 + (g + 1) * HD]
            acc = None
            for d in range(2):
                yo = _mm(cg, sin_s[c * 4 + d * 2 + g])
                es = []
                for hg in range(3):
                    col = d * SSD_HEADS + g * 3 + hg
                    es.append(jnp.broadcast_to(jnp.exp(cum[:, col:col + 1]), (q, HD)))
                yo = yo * jnp.concatenate(es, axis=1)
                acc = yo if acc is None else acc + yo
            parts.append(acc)
        y_ref[0, pl.ds(base, q), :] = y_s[pl.ds(base, q), :] + jnp.concatenate(parts, axis=1)
        return carry

    lax.fori_loop(0, nc, off, 0)


def _ssd(ssd_in, small, prm, dskip):
    bsz, t, _ = ssd_in.shape
    nc = t // SSD_CHUNK
    return pl.pallas_call(
        functools.partial(_ssd_kernel, nc, CTX // SSD_CHUNK),
        out_shape=jax.ShapeDtypeStruct((bsz, t, SSD_W), F32),
        grid=(bsz,),
        in_specs=[pl.BlockSpec((1, t, 640), lambda b: (b, 0, 0)),
                  pl.BlockSpec((1, t, 128), lambda b: (b, 0, 0)),
                  pl.BlockSpec((8, 128), lambda b: (0, 0)),
                  pl.BlockSpec((1, SSD_W), lambda b: (0, 0))],
        out_specs=pl.BlockSpec((1, t, SSD_W), lambda b: (b, 0, 0)),
        scratch_shapes=[pltpu.VMEM((t, 128), F32), pltpu.VMEM((t, 128), F32),
                        pltpu.VMEM((t, 128), F32), pltpu.VMEM((t, 128), F32),
                        pltpu.VMEM((t, SSD_W), F32),
                        pltpu.VMEM((nc * 4, HD, 3 * HD), F32),
                        pltpu.VMEM((nc * 4, HD, 3 * HD), F32)],
        compiler_params=pltpu.CompilerParams(dimension_semantics=("arbitrary",),
                                             vmem_limit_bytes=VMEM_LIMIT),
        name="ssd",
    )(ssd_in, small, prm, dskip)


def _bmm(a, b):
    return lax.dot_general(a.astype(BF16), b.astype(BF16), (((2,), (1,)), ((0,), (0,))),
                           preferred_element_type=F32)


def _bmm_nt(a, b):
    return lax.dot_general(a.astype(BF16), b.astype(BF16), (((2,), (2,)), ((0,), (0,))),
                           preferred_element_type=F32)


def _unit_tri_inverse(a_strict):
    n = a_strict.shape[-1]
    ri = _iota2((1, n, n), 1)
    ci = _iota2((1, n, n), 2)
    eye = (ri == ci).astype(F32)

    def blk(shift):
        return jnp.right_shift(ri, shift) == jnp.right_shift(ci, shift)

    nd = jnp.where(blk(3), -a_strict, 0.0)
    qk = nd
    p = eye + nd
    for lvl in range(2):
        qk = _bmm(qk, qk)
        p = p + _bmm(p, qk)
    inv = p
    for shift in (4, 5, 6):
        e = jnp.where(jnp.logical_and(blk(shift), jnp.logical_not(blk(shift - 1))), a_strict, 0.0)
        inv = inv - _bmm(_bmm(inv, e), inv)
    return inv


def _gdn_kernel(nch, ctx_chunks, gin_ref, small_ref, prm_ref, o_ref,
                gc_s, gct_s, mq_s, n_s, cd_s, of_s, ob_s):
    c64 = GDN_CHUNK
    nsc = nch // 2
    ri128 = _iota2((128, 128), 0)
    ci128 = _iota2((128, 128), 1)
    same = jnp.right_shift(ri128, 6) == jnp.right_shift(ci128, 6)
    lbd = jnp.logical_and(same, ri128 >= ci128).astype(BF16)
    ubd = jnp.logical_and(same, ri128 <= ci128).astype(BF16)
    lane = _iota2((1, 128), 1)
    fwd_lane = lane < SM_A + GDN_HEADS

    a_row = -jnp.exp(prm_ref[0:1, :])
    bias_row = prm_ref[1:2, :]

    beta_lane = jnp.logical_and(lane >= SM_B, lane < SM_B + 2 * GDN_HEADS)

    def prep(sc, carry):
        base = pl.multiple_of(sc * 128, 128)
        sm = small_ref[0, pl.ds(base, 128), :]
        g = a_row * _softplus(sm + bias_row)
        cum = jnp.where(fwd_lane, _mm_exact_lhs(lbd, g), _mm_exact_lhs(ubd, g))
        gc_s[pl.ds(base, 128), :] = cum
        slab_t = jnp.where(beta_lane, jax.nn.sigmoid(sm), cum).T
        gct_s[pl.ds(pl.multiple_of(sc * 256, 256), 128), :] = slab_t[:, :c64]
        gct_s[pl.ds(pl.multiple_of(sc * 256 + 128, 128), 128), :] = slab_t[:, c64:]
        return carry

    lax.fori_loop(0, nsc, prep, 0)

    ri = _iota2((c64, c64), 0)
    ci = _iota2((c64, c64), 1)
    incl = (ri >= ci, ri <= ci)
    strict = (ri > ci, ri < ci)

    def local(c, carry):
        base = pl.multiple_of(c * c64, c64)
        gin = gin_ref[0, pl.ds(base, c64), :]
        gc = gc_s[pl.ds(base, c64), :]
        tbase = pl.multiple_of(c * 128, 128)
        qs, ks, vs = [], [], []
        for h in range(GDN_HEADS):
            off = (h // 2) * 384 + (h % 2) * HD
            qs.append(gin[:, off:off + HD])
            ks.append(gin[:, off + 128:off + 128 + HD])
            vs.append(gin[:, off + 256:off + 256 + HD])
        kst = jnp.stack(ks)
        kk = _bmm_nt(kst, kst)
        qk = _bmm_nt(jnp.stack(qs), kst)
        kts = [k.T for k in ks]
        mb_l, keg_l, lhs_l, qd_l, cd_l = [], [], [], [], []
        for d in range(2):
            for h in range(GDN_HEADS):
                colg = SM_A + d * GDN_HEADS + h
                gmat = jnp.broadcast_to(gc[:, colg:colg + 1], (c64, c64))
                gcr = gct_s[pl.ds(tbase + colg, 1), :]
                br = gct_s[pl.ds(tbase + SM_B + d * GDN_HEADS + h, 1), :]
                dec = jnp.exp(jnp.where(incl[d], gmat - gcr, NEG))
                mb_l.append(jnp.where(strict[d], kk[h] * dec, 0.0) * br)
                eg = jnp.exp(gmat)
                endr = gcr[:, c64 - 1:c64] if d == 0 else gcr[:, 0:1]
                kdtb = kts[h] * (jnp.exp(endr - gcr) * br)
                lhs_l.append(jnp.concatenate([kdtb, qk[h] * dec * br], axis=0))
                keg_l.append(ks[h] * eg)
                qd_l.append(qs[h] * eg)
                cd_l.append(jnp.broadcast_to(jnp.exp(endr), (8, HD)))
        z = _unit_tri_inverse(jnp.stack(mb_l))
        zv = _bmm(z, jnp.stack(vs + vs))
        zk = _bmm(z, jnp.stack(keg_l))
        lhs = jnp.stack(lhs_l)
        a1 = _bmm(lhs, zv)
        a2 = _bmm(lhs, zk)
        for d in range(2):
            slot = c * 2 + d
            ids = [d * GDN_HEADS + h for h in range(GDN_HEADS)]
            mq = [jnp.concatenate([-a2[n][:c64], qd_l[n] - a2[n][c64:]], axis=0) for n in ids]
            mq_s[slot] = jnp.concatenate(mq, axis=1).astype(BF16)
            n_s[slot] = jnp.concatenate([a1[n][:c64] for n in ids], axis=1)
            cd_s[slot] = jnp.concatenate([cd_l[n] for n in ids], axis=1)
            oc = jnp.concatenate([a1[n][c64:] for n in ids], axis=1)
            if d == 0:
                of_s[pl.ds(base, c64), :] = oc
            else:
                ob_s[pl.ds(base, c64), :] = oc
        return carry

    lax.fori_loop(0, nch, local, 0, unroll=2)

    def step(t, states):
        cf = t
        cb = jnp.where(t < ctx_chunks, ctx_chunks - 1 - t, nch - 1 + ctx_chunks - t)
        new_states = []
        for d, c in ((0, cf), (1, cb)):
            slot = c * 2 + d
            mq = mq_s[slot]
            nn = n_s[slot]
            cd = cd_s[slot]
            outs = []
            for h in range(GDN_HEADS):
                s = states[d * GDN_HEADS + h]
                sl = slice(h * HD, (h + 1) * HD)
                r = jnp.dot(mq[:, sl], s.astype(BF16), preferred_element_type=F32)
                new_states.append(s * cd[0:1, sl] + r[:c64] + nn[:, sl])
                outs.append(r[c64:])
            rows = pl.ds(pl.multiple_of(c * c64, c64), c64)
            if d == 0:
                of_s[rows, :] = of_s[rows, :] + jnp.concatenate(outs, axis=1)
            else:
                ob_s[rows, :] = ob_s[rows, :] + jnp.concatenate(outs, axis=1)
        return tuple(new_states)

    zero = jnp.zeros((HD, HD), F32)
    lax.fori_loop(0, nch, step, (zero,) * (2 * GDN_HEADS))
    o_ref[0] = of_s[...] + ob_s[...]


def _gdn(gdn_in, small, prm):
    bsz, t, _ = gdn_in.shape
    nch = t // GDN_CHUNK
    return pl.pallas_call(
        functools.partial(_gdn_kernel, nch, CTX // GDN_CHUNK),
        out_shape=jax.ShapeDtypeStruct((bsz, t, GDN_W), F32),
        grid=(bsz,),
        in_specs=[pl.BlockSpec((1, t, 768), lambda b: (b, 0, 0)),
                  pl.BlockSpec((1, t, 128), lambda b: (b, 0, 0)),
                  pl.BlockSpec((8, 128), lambda b: (0, 0))],
        out_specs=pl.BlockSpec((1, t, GDN_W), lambda b: (b, 0, 0)),
        scratch_shapes=[pltpu.VMEM((t, 128), F32),
                        pltpu.VMEM((nch * 128, GDN_CHUNK), F32),
                        pltpu.VMEM((nch * 2, 128, GDN_W), BF16),
                        pltpu.VMEM((nch * 2, GDN_CHUNK, GDN_W), F32),
                        pltpu.VMEM((nch * 2, 8, GDN_W), F32),
                        pltpu.VMEM((t, GDN_W), F32),
                        pltpu.VMEM((t, GDN_W), F32)],
        compiler_params=pltpu.CompilerParams(dimension_semantics=("arbitrary",),
                                             vmem_limit_bytes=VMEM_LIMIT),
        name="gdn",
    )(gdn_in, small, prm)


def _attn_kernel(first_tile, q_ref, kv_ref, o_ref):
    i = pl.program_id(1) + first_tile

    def run(nkeys):
        q = q_ref[0]
        outs = []
        for g in range(ATT_KVH):
            k = kv_ref[0, :nkeys, g * HD:(g + 1) * HD].astype(BF16)
            va = kv_ref[0, :nkeys, 128 + g * 128:256 + g * 128].astype(BF16)
            for j in range(ATT_QH // ATT_KVH):
                h = g * (ATT_QH // ATT_KVH) + j
                s = _mm_nt(q[:, h * HD:(h + 1) * HD], k)
                m = jnp.max(s, axis=-1, keepdims=True)
                p = jnp.exp2(s - m)
                acc = _mm(p, va)
                outs.append(acc[:, :HD] / acc[:, HD:])
        o_ref[0] = jnp.concatenate(outs, axis=1).astype(BF16)

    @pl.when(i == 0)
    def _():
        run(CTX)

    @pl.when(i > 0)
    def _():
        run(kv_ref.shape[1])


def _attention(q, kv, first_tile):
    bsz, t, _ = q.shape
    nt = t // TM - first_tile
    return pl.pallas_call(
        functools.partial(_attn_kernel, first_tile),
        out_shape=jax.ShapeDtypeStruct((bsz, nt * TM, ATT_W), BF16),
        grid=(bsz, nt),
        in_specs=[pl.BlockSpec((1, TM, ATT_W), lambda b, i: (b, i + first_tile, 0)),
                  pl.BlockSpec((1, t, 384), lambda b, i: (b, 0, 0))],
        out_specs=pl.BlockSpec((1, TM, ATT_W), lambda b, i: (b, i, 0)),
        compiler_params=pltpu.CompilerParams(dimension_semantics=("arbitrary", "arbitrary"),
                                             vmem_limit_bytes=VMEM_LIMIT),
        name="attention",
    )(q, kv)


def _out_kernel(first_tile, xa_ref, x_ref, y_ref, o_ref, a_ref, zs_ref, mod_ref, sw_ref, gw_ref, w_ref, out_ref):
    zs = zs_ref[0].astype(F32)
    t = y_ref[0] * zs[:, :SSD_W]
    ssd = t * lax.rsqrt(jnp.mean(t * t, axis=-1, keepdims=True) + EPS) * sw_ref[...]
    o = o_ref[0]
    gdn = o * lax.rsqrt(_group_sum64(o * o) * (1.0 / HD) + EPS) * gw_ref[...] * zs[:, SSD_W:SSD_W + GDN_W]
    att = a_ref[0] * zs[:, SSD_W + GDN_W:]
    mix = jnp.concatenate([ssd, gdn, att], axis=1).astype(BF16)
    gate = mod_ref[0][:, 2 * D_MODEL:]
    x = jnp.where(pl.program_id(1) + first_tile == 0, xa_ref[0], x_ref[0])
    out_ref[0] = x + gate * jnp.dot(mix, w_ref[...], preferred_element_type=F32)


def _out_projection(xa, xb, off, y, o, a, zs, mod3, ssd_nw, gdn_nw, w_out, first_tile):
    bsz, t, _ = y.shape
    nt = t // TM - first_tile
    row = lambda b, i: (b, i + first_tile, 0)
    full = lambda b, i: (0, 0)
    return pl.pallas_call(
        functools.partial(_out_kernel, first_tile),
        out_shape=jax.ShapeDtypeStruct((bsz, nt * TM, D_MODEL), F32),
        grid=(bsz, nt),
        in_specs=[pl.BlockSpec((1, TM, D_MODEL), lambda b, i: (b, 0, 0)),
                  pl.BlockSpec((1, TM, D_MODEL), lambda b, i: (b, jnp.maximum(i + first_tile - off, 0), 0)),
                  pl.BlockSpec((1, TM, SSD_W), row),
                  pl.BlockSpec((1, TM, GDN_W), row),
                  pl.BlockSpec((1, TM, ATT_W), lambda b, i: (b, i, 0)),
                  pl.BlockSpec((1, TM, D_MODEL), row),
                  pl.BlockSpec((1, 1, 3 * D_MODEL),
                               lambda b, i: (jnp.where(i + first_tile == 0, bsz, b), 0, 0)),
                  pl.BlockSpec((1, SSD_W), full),
                  pl.BlockSpec((1, GDN_W), full),
                  pl.BlockSpec((D_MODEL, D_MODEL), full)],
        out_specs=pl.BlockSpec((1, TM, D_MODEL), lambda b, i: (b, i, 0)),
        compiler_params=pltpu.CompilerParams(dimension_semantics=("arbitrary", "arbitrary"),
                                             vmem_limit_bytes=VMEM_LIMIT),
        name="out_projection",
    )(xa, xb, y, o, a, zs, mod3, ssd_nw, gdn_nw, w_out)


_CONV_SEGS = [(0, 640)] + [(640 + part * 256 + pair * 128, 128) for pair in range(2) for part in range(3)]
_REST_SEGS = [(1408, 384), (1804, 256), (2076, 384), (2460, 128), (2588, 128), (2716, 384),
              (1792, 12), (2060, 8), (2068, 8)]


def _take_segments(w, segs, axis):
    return jnp.concatenate([lax.slice_in_dim(w, a, a + n, axis=axis) for a, n in segs], axis=axis)


def _prepare_params(w_in, conv_w, conv_b, ssd_A_log, ssd_dt_bias, gdn_A_log, gdn_dt_bias):
    depth = w_in.shape[0]
    w_conv = _take_segments(w_in, _CONV_SEGS, 2).astype(BF16)
    pad = jnp.zeros(w_in.shape[:2] + (REST_DIM - sum(n for _, n in _REST_SEGS),), w_in.dtype)
    w_rest = jnp.concatenate([_take_segments(w_in, _REST_SEGS, 2), pad], axis=2).astype(BF16)
    cw = _take_segments(conv_w, _CONV_SEGS, 2)
    cb = _take_segments(conv_b, _CONV_SEGS, 1)[:, None, :]

    def lanes(a, b):
        v = jnp.concatenate([a.reshape(depth, -1), b.reshape(depth, -1)], axis=1)
        return jnp.pad(v, ((0, 0), (0, 128 - v.shape[1])))

    prm = jnp.stack([lanes(ssd_A_log, gdn_A_log), lanes(ssd_dt_bias, gdn_dt_bias)], axis=1)
    prm = jnp.pad(prm, ((0, 0), (0, 6), (0, 0)))
    return w_conv, w_rest, cw, cb, prm


def _rope_tables(t):
    f32 = np.float32
    pos = np.arange(t - CTX)
    n_freq = HD // 4
    freqs = np.power(f32(ROPE_THETA), -np.arange(n_freq, dtype=f32) / f32(n_freq)).astype(f32)
    ang_r = ((pos // GRID_W).astype(f32)[:, None] * freqs).astype(f32)
    ang_c = ((pos % GRID_W).astype(f32)[:, None] * freqs).astype(f32)
    cos = np.concatenate([np.cos(ang_r)] * 2 + [np.cos(ang_c)] * 2, axis=1)
    sin = np.concatenate([-np.sin(ang_r), np.sin(ang_r), -np.sin(ang_c), np.sin(ang_c)], axis=1)
    cos = np.concatenate([np.ones((CTX, HD), f32), cos], axis=0).astype(f32)
    sin = np.concatenate([np.zeros((CTX, HD), f32), sin], axis=0).astype(f32)
    return jnp.asarray(np.tile(cos, (1, 2))), jnp.asarray(np.tile(sin, (1, 2)))


def kernel(x, c, ctx, c_ctx, norm_w, w_mod, b_mod, w_in, conv_w, conv_b, ssd_A_log, ssd_dt_bias, ssd_D,
           ssd_norm_w, gdn_A_log, gdn_dt_bias, gdn_norm_w, q_norm_w, k_norm_w, w_out):
    bsz = x.shape[0]
    t = CTX + x.shape[1]
    c_all = jnp.concatenate([c, c_ctx[None, :], jnp.zeros((7, D_MODEL), F32)], axis=0)
    cos_t, sin_t = _rope_tables(t)
    w_conv, w_rest, cw, cb, prm = _prepare_params(w_in, conv_w, conv_b, ssd_A_log, ssd_dt_bias,
                                                  gdn_A_log, gdn_dt_bias)
    qk_w = jnp.concatenate([jnp.tile(q_norm_w, (1, ATT_QH)), jnp.tile(k_norm_w, (1, ATT_KVH))], axis=1)
    dskip = jnp.repeat(ssd_D, HD, axis=1)
    gdn_nw = jnp.tile(gdn_norm_w, (1, GDN_HEADS))
    w_out16 = w_out.astype(BF16)

    xa, xb, off = ctx, x, 1
    for layer in range(DEPTH):
        first_tile = 1 if layer == DEPTH - 1 else 0
        mod3 = _modulation(c_all, w_mod[layer], b_mod[layer]).reshape(bsz + 8, 1, 3 * D_MODEL)
        ssd_in, gdn_in, zs, q, kv, small = _in_projection(
            xa, xb, off, mod3, norm_w[layer][None], w_conv[layer], w_rest[layer], cw[layer], cb[layer],
            qk_w[layer][None], cos_t, sin_t)
        y = _ssd(ssd_in, small, prm[layer], dskip[layer][None])
        o = _gdn(gdn_in, small, prm[layer])
        a = _attention(q, kv, first_tile)
        out = _out_projection(xa, xb, off, y, o, a, zs, mod3, ssd_norm_w[layer][None], gdn_nw[layer][None],
                              w_out16[layer], first_tile)
        xa, xb, off = out, out, 0
    return out
```

```python
import functools
import math

import jax
import jax.numpy as jnp
import numpy as np
from jax import lax
from jax.experimental import pallas as pl
from jax.experimental.pallas import tpu as pltpu

F32 = jnp.float32
BF16 = jnp.bfloat16

D_MODEL = 1024
CTX = 256
GRID_W = 64
EPS = 1e-6
DEPTH = 2

HD = 64
SSD_HEADS = 6
SSD_W = SSD_HEADS * HD
SSD_CHUNK = 128
GDN_HEADS = 4
GDN_W = GDN_HEADS * HD
GDN_CHUNK = 64
ATT_QH = 6
ATT_KVH = 2
ATT_W = ATT_QH * HD
ROPE_THETA = 10000.0
Q_PRESCALE = (HD ** -0.5) * math.log2(math.e)

CONV_DIM = 1408
REST_DIM = 1792
TM = 256
HALO = 8
NEG = -1e30
VMEM_LIMIT = 56 * 1024 * 1024

SM_DT = 0
SM_A = 12
SM_B = 20


def _mm(a, b):
    return jnp.dot(a.astype(BF16), b.astype(BF16), preferred_element_type=F32)


def _mm_nt(a, b):
    return lax.dot_general(a.astype(BF16), b.astype(BF16), (((1,), (1,)), ((), ())),
                           preferred_element_type=F32)


def _mm_tn(a, b):
    return lax.dot_general(a.astype(BF16), b.astype(BF16), (((0,), (0,)), ((), ())),
                           preferred_element_type=F32)


def _split3(a):
    h = a.astype(BF16)
    r = a - h.astype(F32)
    m = r.astype(BF16)
    l = (r - m.astype(F32)).astype(BF16)
    return h, m, l


def _mm_exact_rhs(a, b_bf16):
    h, m, l = _split3(a)
    d = functools.partial(jnp.dot, preferred_element_type=F32)
    return d(h, b_bf16) + (d(m, b_bf16) + d(l, b_bf16))


def _mm_exact_lhs(a_bf16, b):
    h, m, l = _split3(b)
    d = functools.partial(jnp.dot, preferred_element_type=F32)
    return d(a_bf16, h) + (d(a_bf16, m) + d(a_bf16, l))


def _silu(x):
    return x * jax.nn.sigmoid(x)


def _softplus(x):
    return jnp.maximum(x, 0.0) + jnp.log1p(jnp.exp(-jnp.abs(x)))


def _iota2(shape, dim):
    return lax.broadcasted_iota(jnp.int32, shape, dim)


def _group_sum64(xx):
    r = jnp.right_shift(_iota2((128, 128), 0), 6)
    c = jnp.right_shift(_iota2((128, 128), 1), 6)
    g = (r == c).astype(BF16)
    outs = []
    for t in range(xx.shape[1] // 128):
        outs.append(_mm_exact_rhs(xx[:, t * 128:(t + 1) * 128], g))
    return outs[0] if len(outs) == 1 else jnp.concatenate(outs, axis=1)


def _mod_kernel(c_ref, w_ref, b_ref, o_ref):
    o_ref[...] = _mm(_silu(c_ref[...]), w_ref[...]) + b_ref[...]


def _modulation(c_all, w_mod, b_mod):
    n = w_mod.shape[1]
    bn = 768
    rows = c_all.shape[0]
    return pl.pallas_call(
        _mod_kernel,
        out_shape=jax.ShapeDtypeStruct((rows, n), F32),
        grid=(n // bn,),
        in_specs=[pl.BlockSpec((rows, D_MODEL), lambda j: (0, 0)),
                  pl.BlockSpec((D_MODEL, bn), lambda j: (0, j)),
                  pl.BlockSpec((1, bn), lambda j: (0, j))],
        out_specs=pl.BlockSpec((rows, bn), lambda j: (0, j)),
        compiler_params=pltpu.CompilerParams(dimension_semantics=("arbitrary",),
                                             vmem_limit_bytes=VMEM_LIMIT),
        name="modulation",
    )(c_all, w_mod, b_mod.reshape(1, n))


def _proj_kernel(nt, xa_ref, xm_ref, xp_ref, xn_ref, mod_ref, nw_ref, wc_ref, wr_ref, cw_ref, cb_ref,
                 qkw_ref, cos_ref, sin_ref,
                 ssd_ref, gdn_ref, zs_ref, q_ref, kv_ref, small_ref):
    i = pl.program_id(1)
    m = mod_ref[0]
    gain = nw_ref[...] * (1.0 + m[:, D_MODEL:2 * D_MODEL])
    shift = m[:, :D_MODEL]

    xm = jnp.where(i == 0, xa_ref[0], xm_ref[0])
    xe = jnp.concatenate([xm, xp_ref[0], xn_ref[0]], axis=0)
    ms = jnp.mean(xe * xe, axis=-1, keepdims=True)
    he = (xe * lax.rsqrt(ms + EPS) * gain + shift).astype(BF16)

    p = jnp.dot(he, wc_ref[...], preferred_element_type=F32)
    r = jnp.dot(he[:TM], wr_ref[...], preferred_element_type=F32)

    pm = p[:TM]
    seg_first = i <= 1
    seg_last = jnp.logical_or(i == 0, i == nt - 1)
    prev_row = jnp.where(seg_first, 0.0, p[TM + HALO - 1:TM + HALO])
    next_row = jnp.where(seg_last, 0.0, p[TM + HALO:TM + HALO + 1])
    rows = _iota2((TM, 1), 0)
    pm1 = jnp.where(rows == 0, prev_row, pltpu.roll(pm, 1, axis=0))
    pp1 = jnp.where(rows == TM - 1, next_row, pltpu.roll(pm, TM - 1, axis=0))
    cw = cw_ref[...]
    conv = cw[0:1] * pm1 + cw[1:2] * pm + cw[2:3] * pp1 + cb_ref[...]
    co = _silu(conv)

    ssd_ref[0] = co[:, :640].astype(BF16)
    gqk = co[:, 640:1152]
    gqk = gqk * lax.rsqrt(_group_sum64(gqk * gqk) + EPS)
    gdn_ref[0, :, :256] = (gqk[:, :256] * (HD ** -0.5)).astype(BF16)
    gdn_ref[0, :, 256:512] = gqk[:, 256:].astype(BF16)
    gdn_ref[0, :, 512:] = co[:, 1152:1408].astype(BF16)

    z1 = r[:, :640]
    z2 = r[:, 1280:1664]
    zs_ref[0, :, :640] = _silu(z1).astype(BF16)
    zs_ref[0, :, 640:] = _silu(z2).astype(BF16)

    aqk = r[:, 640:1152]
    aqk = aqk * lax.rsqrt(_group_sum64(aqk * aqk) * (1.0 / HD) + EPS) * qkw_ref[...]
    lane = _iota2((1, 512), 1)
    swapped = jnp.where(jnp.bitwise_and(lane, 31) < 16,
                        pltpu.roll(aqk, 512 - 16, axis=1), pltpu.roll(aqk, 16, axis=1))
    cos = jnp.concatenate([cos_ref[...]] * 4, axis=1)
    sin = jnp.concatenate([sin_ref[...]] * 4, axis=1)
    aqk = aqk * cos + swapped * sin
    q_ref[0] = (aqk[:, :384] * Q_PRESCALE).astype(BF16)
    ones = jnp.ones((TM, HD), F32)
    v = r[:, 1152:1280]
    kv_ref[0] = jnp.concatenate([aqk[:, 384:], v[:, :HD], ones, v[:, HD:], ones], axis=1).astype(BF16)
    small_ref[0] = r[:, 1664:1792]


def _in_projection(xa, xb, off, mod3, norm_w, w_conv, w_rest, conv_w, conv_b, qk_w, cos_t, sin_t):
    bsz = xb.shape[0]
    nt = xb.shape[1] // TM + off
    t = nt * TM
    nblk8 = xb.shape[1] // HALO
    per_tile = TM // HALO
    row = lambda b, i: (b, i, 0)
    full = lambda b, i: (0, 0)
    outs = [jax.ShapeDtypeStruct((bsz, t, w), dt)
            for w, dt in ((640, BF16), (768, BF16), (1024, BF16), (384, BF16), (384, BF16), (128, F32))]
    return pl.pallas_call(
        functools.partial(_proj_kernel, nt),
        out_shape=outs,
        grid=(bsz, nt),
        in_specs=[
            pl.BlockSpec((1, TM, D_MODEL), lambda b, i: (b, 0, 0)),
            pl.BlockSpec((1, TM, D_MODEL), lambda b, i: (b, jnp.maximum(i - off, 0), 0)),
            pl.BlockSpec((1, HALO, D_MODEL), lambda b, i: (b, jnp.maximum((i - off) * per_tile - 1, 0), 0)),
            pl.BlockSpec((1, HALO, D_MODEL),
                         lambda b, i: (b, jnp.minimum((i - off + 1) * per_tile, nblk8 - 1), 0)),
            pl.BlockSpec((1, 1, 3 * D_MODEL), lambda b, i: (jnp.where(i == 0, bsz, b), 0, 0)),
            pl.BlockSpec((1, D_MODEL), full),
            pl.BlockSpec((D_MODEL, CONV_DIM), full),
            pl.BlockSpec((D_MODEL, REST_DIM), full),
            pl.BlockSpec((3, CONV_DIM), full),
            pl.BlockSpec((1, CONV_DIM), full),
            pl.BlockSpec((1, 512), full),
            pl.BlockSpec((TM, 128), lambda b, i: (i, 0)),
            pl.BlockSpec((TM, 128), lambda b, i: (i, 0)),
        ],
        out_specs=[pl.BlockSpec((1, TM, w), row) for w in (640, 768, 1024, 384, 384, 128)],
        compiler_params=pltpu.CompilerParams(dimension_semantics=("arbitrary", "arbitrary"),
                                             vmem_limit_bytes=VMEM_LIMIT),
        name="in_projection",
    )(xa, xb, xb, xb, mod3, norm_w, w_conv, w_rest, conv_w, conv_b, qk_w, cos_t, sin_t)


def _ssd_kernel(nc, ctx_chunks, xin_ref, small_ref, prm_ref, dskip_ref, y_ref,
                cum_s, cumt_s, dt_s, dtt_s, y_s, st_s, sin_s):
    q = SSD_CHUNK
    ri = _iota2((q, q), 0)
    ci = _iota2((q, q), 1)
    ltri = (ri >= ci).astype(BF16)
    utri = (ri <= ci).astype(BF16)
    lane = _iota2((1, 128), 1)
    fwd_lane = lane < SSD_HEADS

    a_row = -jnp.exp(prm_ref[0:1, :])
    bias_row = prm_ref[1:2, :]

    def prep(c, carry):
        base = pl.multiple_of(c * q, q)
        dt = _softplus(small_ref[0, pl.ds(base, q), :] + bias_row)
        dta = dt * a_row
        cum = jnp.where(fwd_lane, _mm_exact_lhs(ltri, dta), _mm_exact_lhs(utri, dta))
        cum_s[pl.ds(base, q), :] = cum
        dt_s[pl.ds(base, q), :] = dt
        cumt_s[pl.ds(base, q), :] = cum.T
        dtt_s[pl.ds(base, q), :] = dt.T
        return carry

    lax.fori_loop(0, nc, prep, 0)

    lower = ri >= ci
    upper = ri <= ci
    dsk = dskip_ref[...]

    def diag(c, carry):
        base = pl.multiple_of(c * q, q)
        xin = xin_ref[0, pl.ds(base, q), :]
        cum = cum_s[pl.ds(base, q), :]
        cumt = cumt_s[pl.ds(base, q), :]
        dtc = dt_s[pl.ds(base, q), :]
        dtt = dtt_s[pl.ds(base, q), :]
        for g in range(2):
            bg = xin[:, 384 + g * HD:384 + (g + 1) * HD]
            cg = xin[:, 512 + g * HD:512 + (g + 1) * HD]
            sc = _mm_nt(cg, bg)
            xds = [[], []]
            for hg in range(3):
                h = g * 3 + hg
                xh = xin[:, h * HD:(h + 1) * HD]
                w = None
                for d in range(2):
                    col = d * SSD_HEADS + h
                    ccol = cum[:, col:col + 1]
                    diff = ccol - cumt[col:col + 1, :]
                    lm = jnp.exp(jnp.where(lower if d == 0 else upper, diff, NEG))
                    wd = lm * dtt[col:col + 1, :]
                    w = wd if w is None else w + wd
                    end = cum[q - 1:q, col:col + 1] if d == 0 else cum[0:1, col:col + 1]
                    xds[d].append(xh * (jnp.exp(end - ccol) * dtc[:, col:col + 1]))
                y_s[pl.ds(base, q), h * HD:(h + 1) * HD] = _mm(w * sc, xh) + dsk[:, h * HD:(h + 1) * HD] * xh
            for d in range(2):
                xd = jnp.concatenate(xds[d], axis=1)
                st_s[c * 4 + d * 2 + g] = _mm_tn(bg, xd)
        return carry

    lax.fori_loop(0, nc, diag, 0)

    orders = (list(range(nc)),
              list(range(ctx_chunks - 1, -1, -1)) + list(range(nc - 1, ctx_chunks - 1, -1)))
    for d in range(2):
        for g in range(2):
            s = jnp.zeros((HD, 3 * HD), F32)
            for c in orders[d]:
                slot = c * 4 + d * 2 + g
                sin_s[slot] = s
                row = c * q + (q - 1 if d == 0 else 0)
                decs = []
                for hg in range(3):
                    col = d * SSD_HEADS + g * 3 + hg
                    decs.append(jnp.broadcast_to(jnp.exp(cum_s[row:row + 1, col:col + 1]), (1, HD)))
                s = s * jnp.concatenate(decs, axis=1) + st_s[slot]

    def off(c, carry):
        base = pl.multiple_of(c * q, q)
        xin = xin_ref[0, pl.ds(base, q), :]
        cum = cum_s[pl.ds(base, q), :]
        parts = []
        for g in range(2):
            cg = xin[:, 512 + g * HD:512 + (g + 1) * HD]
            acc = None
            for d in range(2):
                yo = _mm(cg, sin_s[c * 4 + d * 2 + g])
                es = []
                for hg in range(3):
                    col = d * SSD_HEADS + g * 3 + hg
                    es.append(jnp.broadcast_to(jnp.exp(cum[:, col:col + 1]), (q, HD)))
                yo = yo * jnp.concatenate(es, axis=1)
                acc = yo if acc is None else acc + yo
            parts.append(acc)
        y_ref[0, pl.ds(base, q), :] = (y_s[pl.ds(base, q), :] + jnp.concatenate(parts, axis=1)).astype(BF16)
        return carry

    lax.fori_loop(0, nc, off, 0)


def _ssd(ssd_in, small, prm, dskip):
    bsz, t, _ = ssd_in.shape
    nc = t // SSD_CHUNK
    return pl.pallas_call(
        functools.partial(_ssd_kernel, nc, CTX // SSD_CHUNK),
        out_shape=jax.ShapeDtypeStruct((bsz, t, SSD_W), BF16),
        grid=(bsz,),
        in_specs=[pl.BlockSpec((1, t, 640), lambda b: (b, 0, 0)),
                  pl.BlockSpec((1, t, 128), lambda b: (b, 0, 0)),
                  pl.BlockSpec((8, 128), lambda b: (0, 0)),
                  pl.BlockSpec((1, SSD_W), lambda b: (0, 0))],
        out_specs=pl.BlockSpec((1, t, SSD_W), lambda b: (b, 0, 0)),
        scratch_shapes=[pltpu.VMEM((t, 128), F32), pltpu.VMEM((t, 128), F32),
                        pltpu.VMEM((t, 128), F32), pltpu.VMEM((t, 128), F32),
                        pltpu.VMEM((t, SSD_W), F32),
                        pltpu.VMEM((nc * 4, HD, 3 * HD), F32),
                        pltpu.VMEM((nc * 4, HD, 3 * HD), F32)],
        compiler_params=pltpu.CompilerParams(dimension_semantics=("arbitrary",),
                                             vmem_limit_bytes=VMEM_LIMIT),
        name="ssd",
    )(ssd_in, small, prm, dskip)


def _bd4(x):
    xb = x.astype(BF16)
    blk = jnp.right_shift(_iota2((1, 4 * HD), 1), 6)
    zero = jnp.zeros_like(xb)
    return jnp.concatenate([jnp.where(blk == h, xb, zero) for h in range(GDN_HEADS)], axis=0)


def _bmm(a, b):
    return lax.dot_general(a.astype(BF16), b.astype(BF16), (((2,), (1,)), ((0,), (0,))),
                           preferred_element_type=F32)


def _bmm_nt(a, b):
    return lax.dot_general(a.astype(BF16), b.astype(BF16), (((2,), (2,)), ((0,), (0,))),
                           preferred_element_type=F32)


def _unit_tri_inverse(a_strict):
    n = a_strict.shape[-1]
    ri = _iota2((1, n, n), 1)
    ci = _iota2((1, n, n), 2)

    def blk(shift):
        return jnp.right_shift(ri, shift) == jnp.right_shift(ci, shift)

    nd = jnp.where(blk(3), -a_strict, 0.0)
    p0 = (ri == ci).astype(F32) + nd
    q1 = _bmm(nd, nd)
    t = _bmm(jnp.concatenate([p0, q1], axis=1), q1)
    p1 = p0 + t[:, :n]
    inv = p1 + _bmm(p1, t[:, n:])
    for shift in (4, 5, 6):
        e = jnp.where(jnp.logical_and(blk(shift), jnp.logical_not(blk(shift - 1))), a_strict, 0.0)
        inv = inv - _bmm(_bmm(inv, e), inv)
    return inv


def _gdn_kernel(nch, ctx_chunks, gin_ref, small_ref, prm_ref, o_ref,
                gc_s, rows_s, mq_s, n_s, cd_s, of_s, ob_s):
    c64 = GDN_CHUNK
    w4 = GDN_W
    nsc = nch // 2
    ri128 = _iota2((128, 128), 0)
    ci128 = _iota2((128, 128), 1)
    same = jnp.right_shift(ri128, 6) == jnp.right_shift(ci128, 6)
    lbd = jnp.logical_and(same, ri128 >= ci128).astype(BF16)
    ubd = jnp.logical_and(same, ri128 <= ci128).astype(BF16)
    lane = _iota2((1, 128), 1)
    fwd_lane = lane < SM_A + GDN_HEADS
    beta_lane = jnp.logical_and(lane >= SM_B, lane < SM_B + 2 * GDN_HEADS)

    a_row = -jnp.exp(prm_ref[0:1, :])
    bias_row = prm_ref[1:2, :]

    def prep(sc, carry):
        base = pl.multiple_of(sc * 128, 128)
        sm = small_ref[0, pl.ds(base, 128), :]
        g = a_row * _softplus(sm + bias_row)
        cum = jnp.where(fwd_lane, _mm_exact_lhs(lbd, g), _mm_exact_lhs(ubd, g))
        gc_s[pl.ds(base, 128), :] = cum
        slab_t = jnp.where(beta_lane, jax.nn.sigmoid(sm), cum).T
        for half in range(2):
            lo = half * c64
            table = []
            for first in (SM_A, SM_A + GDN_HEADS, SM_B, SM_B + GDN_HEADS):
                table.append(jnp.concatenate(
                    [slab_t[first + h:first + h + 1, lo:lo + c64] for h in range(GDN_HEADS)], axis=1))
            for d, edge in ((0, lo + c64 - 1), (1, lo)):
                first = SM_A + d * GDN_HEADS
                table.append(jnp.concatenate(
                    [jnp.broadcast_to(slab_t[first + h:first + h + 1, edge:edge + 1], (1, c64))
                     for h in range(GDN_HEADS)], axis=1))
            table.append(jnp.zeros((2, w4), F32))
            rows_s[pl.ds(pl.multiple_of(sc * 16 + half * 8, 8), 8), :] = jnp.concatenate(table, axis=0)
        return carry

    lax.fori_loop(0, nsc, prep, 0)

    ri = _iota2((c64, c64), 0)
    ci = _iota2((c64, c64), 1)
    incl = (ri >= ci, ri <= ci)
    strict = (ri > ci, ri < ci)

    def local(c, carry):
        base = pl.multiple_of(c * c64, c64)
        gin = gin_ref[0, pl.ds(base, c64), :]
        gc = gc_s[pl.ds(base, c64), :]
        rows = rows_s[pl.ds(pl.multiple_of(c * 8, 8), 8), :]
        qs = [gin[:, h * HD:(h + 1) * HD] for h in range(GDN_HEADS)]
        ks = [gin[:, w4 + h * HD:w4 + (h + 1) * HD] for h in range(GDN_HEADS)]
        vs = [gin[:, 2 * w4 + h * HD:2 * w4 + (h + 1) * HD] for h in range(GDN_HEADS)]
        kq = _bmm_nt(jnp.stack([jnp.concatenate([ks[h], qs[h]], axis=0) for h in range(GDN_HEADS)]),
                     jnp.stack(ks))
        kts = [k.astype(F32).T for k in ks]
        mb_l, rhs_l, lhs_l, qd_l = [], [], [], []
        for d in range(2):
            for h in range(GDN_HEADS):
                sl = slice(h * HD, (h + 1) * HD)
                colg = SM_A + d * GDN_HEADS + h
                gmat = jnp.broadcast_to(gc[:, colg:colg + 1], (c64, c64))
                gcr, br, endr = rows[d:d + 1, sl], rows[2 + d:3 + d, sl], rows[4 + d:5 + d, sl]
                dec = jnp.exp(jnp.where(incl[d], gmat - gcr, NEG))
                mb_l.append(jnp.where(strict[d], kq[h][:c64] * dec, 0.0) * br)
                eg = jnp.exp(gmat)
                kdtb = kts[h] * (jnp.exp(endr - gcr) * br)
                lhs_l.append(jnp.concatenate([kdtb, kq[h][c64:] * dec * br], axis=0))
                rhs_l.append(jnp.concatenate([vs[h], ks[h] * eg], axis=1))
                qd_l.append(qs[h] * eg)
        z = _unit_tri_inverse(jnp.stack(mb_l))
        zr = _bmm(z, jnp.stack(rhs_l))
        a12 = _bmm(jnp.stack(lhs_l), zr)
        for d in range(2):
            slot = c * 2 + d
            ids = [d * GDN_HEADS + h for h in range(GDN_HEADS)]
            mq = [jnp.concatenate([-a12[n][:c64, HD:], qd_l[n] - a12[n][c64:, HD:]], axis=0) for n in ids]
            mq_s[slot] = jnp.concatenate(mq, axis=1).astype(BF16)
            n_s[slot] = jnp.concatenate([a12[n][:c64, :HD] for n in ids], axis=1)
            cd_s[slot] = jnp.broadcast_to(jnp.exp(rows[4 + d:5 + d]), (8, w4))
            oc = jnp.concatenate([a12[n][c64:, :HD] for n in ids], axis=1)
            if d == 0:
                of_s[pl.ds(base, c64), :] = oc
            else:
                ob_s[pl.ds(base, c64), :] = oc
        return carry

    lax.fori_loop(0, nch, local, 0, unroll=2)

    def step(t, states):
        cf = t
        cb = jnp.where(t < ctx_chunks, ctx_chunks - 1 - t, nch - 1 + ctx_chunks - t)
        new_states = []
        for d, c in ((0, cf), (1, cb)):
            slot = c * 2 + d
            s = states[d]
            r = jnp.dot(mq_s[slot], _bd4(s), preferred_element_type=F32)
            new_states.append(s * cd_s[slot][0:1] + r[:c64] + n_s[slot])
            rows = pl.ds(pl.multiple_of(c * c64, c64), c64)
            if d == 0:
                of_s[rows, :] = of_s[rows, :] + r[c64:]
            else:
                ob_s[rows, :] = ob_s[rows, :] + r[c64:]
        return tuple(new_states)

    zero = jnp.zeros((c64, w4), F32)
    lax.fori_loop(0, nch, step, (zero, zero))
    o_ref[0] = (of_s[...] + ob_s[...]).astype(BF16)


def _gdn(gdn_in, small, prm):
    bsz, t, _ = gdn_in.shape
    nch = t // GDN_CHUNK
    return pl.pallas_call(
        functools.partial(_gdn_kernel, nch, CTX // GDN_CHUNK),
        out_shape=jax.ShapeDtypeStruct((bsz, t, GDN_W), BF16),
        grid=(bsz,),
        in_specs=[pl.BlockSpec((1, t, 768), lambda b: (b, 0, 0)),
                  pl.BlockSpec((1, t, 128), lambda b: (b, 0, 0)),
                  pl.BlockSpec((8, 128), lambda b: (0, 0))],
        out_specs=pl.BlockSpec((1, t, GDN_W), lambda b: (b, 0, 0)),
        scratch_shapes=[pltpu.VMEM((t, 128), F32),
                        pltpu.VMEM((nch * 8, GDN_W), F32),
                        pltpu.VMEM((nch * 2, 128, GDN_W), BF16),
                        pltpu.VMEM((nch * 2, GDN_CHUNK, GDN_W), F32),
                        pltpu.VMEM((nch * 2, 8, GDN_W), F32),
                        pltpu.VMEM((t, GDN_W), F32),
                        pltpu.VMEM((t, GDN_W), F32)],
        compiler_params=pltpu.CompilerParams(dimension_semantics=("arbitrary",),
                                             vmem_limit_bytes=VMEM_LIMIT),
        name="gdn",
    )(gdn_in, small, prm)


def _attn_kernel(first_tile, q_ref, kv_ref, o_ref):
    i = pl.program_id(1) + first_tile

    def run(nkeys):
        q = q_ref[0]
        outs = []
        for g in range(ATT_KVH):
            k = kv_ref[0, :nkeys, g * HD:(g + 1) * HD].astype(BF16)
            va = kv_ref[0, :nkeys, 128 + g * 128:256 + g * 128].astype(BF16)
            for j in range(ATT_QH // ATT_KVH):
                h = g * (ATT_QH // ATT_KVH) + j
                s = _mm_nt(q[:, h * HD:(h + 1) * HD], k)
                m = jnp.max(s, axis=-1, keepdims=True)
                p = jnp.exp2(s - m)
                acc = _mm(p, va)
                outs.append(acc[:, :HD] / acc[:, HD:])
        o_ref[0] = jnp.concatenate(outs, axis=1).astype(BF16)

    @pl.when(i == 0)
    def _():
        run(CTX)

    @pl.when(i > 0)
    def _():
        run(kv_ref.shape[1])


def _attention(q, kv, first_tile):
    bsz, t, _ = q.shape
    nt = t // TM - first_tile
    return pl.pallas_call(
        functools.partial(_attn_kernel, first_tile),
        out_shape=jax.ShapeDtypeStruct((bsz, nt * TM, ATT_W), BF16),
        grid=(bsz, nt),
        in_specs=[pl.BlockSpec((1, TM, ATT_W), lambda b, i: (b, i + first_tile, 0)),
                  pl.BlockSpec((1, t, 384), lambda b, i: (b, 0, 0))],
        out_specs=pl.BlockSpec((1, TM, ATT_W), lambda b, i: (b, i, 0)),
        compiler_params=pltpu.CompilerParams(dimension_semantics=("arbitrary", "arbitrary"),
                                             vmem_limit_bytes=VMEM_LIMIT),
        name="attention",
    )(q, kv)


def _out_kernel(first_tile, xa_ref, x_ref, y_ref, o_ref, a_ref, zs_ref, mod_ref, sw_ref, gw_ref, w_ref, out_ref):
    zs = zs_ref[0].astype(F32)
    t = y_ref[0].astype(F32) * zs[:, :SSD_W]
    ssd = t * lax.rsqrt(jnp.mean(t * t, axis=-1, keepdims=True) + EPS) * sw_ref[...]
    o = o_ref[0].astype(F32)
    gdn = o * lax.rsqrt(_group_sum64(o * o) * (1.0 / HD) + EPS) * gw_ref[...] * zs[:, SSD_W:SSD_W + GDN_W]
    att = a_ref[0] * zs[:, SSD_W + GDN_W:]
    mix = jnp.concatenate([ssd, gdn, att], axis=1).astype(BF16)
    gate = mod_ref[0][:, 2 * D_MODEL:]
    x = jnp.where(pl.program_id(1) + first_tile == 0, xa_ref[0], x_ref[0])
    out_ref[0] = x + gate * jnp.dot(mix, w_ref[...], preferred_element_type=F32)


def _out_projection(xa, xb, off, y, o, a, zs, mod3, ssd_nw, gdn_nw, w_out, first_tile):
    bsz, t, _ = y.shape
    nt = t // TM - first_tile
    row = lambda b, i: (b, i + first_tile, 0)
    full = lambda b, i: (0, 0)
    return pl.pallas_call(
        functools.partial(_out_kernel, first_tile),
        out_shape=jax.ShapeDtypeStruct((bsz, nt * TM, D_MODEL), F32),
        grid=(bsz, nt),
        in_specs=[pl.BlockSpec((1, TM, D_MODEL), lambda b, i: (b, 0, 0)),
                  pl.BlockSpec((1, TM, D_MODEL), lambda b, i: (b, jnp.maximum(i + first_tile - off, 0), 0)),
                  pl.BlockSpec((1, TM, SSD_W), row),
                  pl.BlockSpec((1, TM, GDN_W), row),
                  pl.BlockSpec((1, TM, ATT_W), lambda b, i: (b, i, 0)),
                  pl.BlockSpec((1, TM, D_MODEL), row),
                  pl.BlockSpec((1, 1, 3 * D_MODEL),
                               lambda b, i: (jnp.where(i + first_tile == 0, bsz, b), 0, 0)),
                  pl.BlockSpec((1, SSD_W), full),
                  pl.BlockSpec((1, GDN_W), full),
                  pl.BlockSpec((D_MODEL, D_MODEL), full)],
        out_specs=pl.BlockSpec((1, TM, D_MODEL), lambda b, i: (b, i, 0)),
        compiler_params=pltpu.CompilerParams(dimension_semantics=("arbitrary", "arbitrary"),
                                             vmem_limit_bytes=VMEM_LIMIT),
        name="out_projection",
    )(xa, xb, y, o, a, zs, mod3, ssd_nw, gdn_nw, w_out)


_CONV_SEGS = [(0, CONV_DIM)]
_REST_SEGS = [(1408, 384), (1804, 256), (2076, 384), (2460, 128), (2588, 128), (2716, 384),
              (1792, 12), (2060, 8), (2068, 8)]


def _take_segments(w, segs, axis):
    return jnp.concatenate([lax.slice_in_dim(w, a, a + n, axis=axis) for a, n in segs], axis=axis)


def _prepare_params(w_in, conv_w, conv_b, ssd_A_log, ssd_dt_bias, gdn_A_log, gdn_dt_bias):
    depth = w_in.shape[0]
    w_conv = _take_segments(w_in, _CONV_SEGS, 2).astype(BF16)
    pad = jnp.zeros(w_in.shape[:2] + (REST_DIM - sum(n for _, n in _REST_SEGS),), w_in.dtype)
    w_rest = jnp.concatenate([_take_segments(w_in, _REST_SEGS, 2), pad], axis=2).astype(BF16)
    cw = _take_segments(conv_w, _CONV_SEGS, 2)
    cb = _take_segments(conv_b, _CONV_SEGS, 1)[:, None, :]

    def lanes(a, b):
        v = jnp.concatenate([a.reshape(depth, -1), b.reshape(depth, -1)], axis=1)
        return jnp.pad(v, ((0, 0), (0, 128 - v.shape[1])))

    prm = jnp.stack([lanes(ssd_A_log, gdn_A_log), lanes(ssd_dt_bias, gdn_dt_bias)], axis=1)
    prm = jnp.pad(prm, ((0, 0), (0, 6), (0, 0)))
    return w_conv, w_rest, cw, cb, prm


def _rope_tables(t):
    f32 = np.float32
    pos = np.arange(t - CTX)
    n_freq = HD // 4
    freqs = np.power(f32(ROPE_THETA), -np.arange(n_freq, dtype=f32) / f32(n_freq)).astype(f32)
    ang_r = ((pos // GRID_W).astype(f32)[:, None] * freqs).astype(f32)
    ang_c = ((pos % GRID_W).astype(f32)[:, None] * freqs).astype(f32)
    cos = np.concatenate([np.cos(ang_r)] * 2 + [np.cos(ang_c)] * 2, axis=1)
    sin = np.concatenate([-np.sin(ang_r), np.sin(ang_r), -np.sin(ang_c), np.sin(ang_c)], axis=1)
    cos = np.concatenate([np.ones((CTX, HD), f32), cos], axis=0).astype(f32)
    sin = np.concatenate([np.zeros((CTX, HD), f32), sin], axis=0).astype(f32)
    return jnp.asarray(np.tile(cos, (1, 2))), jnp.asarray(np.tile(sin, (1, 2)))


def kernel(x, c, ctx, c_ctx, norm_w, w_mod, b_mod, w_in, conv_w, conv_b, ssd_A_log, ssd_dt_bias, ssd_D,
           ssd_norm_w, gdn_A_log, gdn_dt_bias, gdn_norm_w, q_norm_w, k_norm_w, w_out):
    bsz = x.shape[0]
    t = CTX + x.shape[1]
    c_all = jnp.concatenate([c, c_ctx[None, :], jnp.zeros((7, D_MODEL), F32)], axis=0)
    cos_t, sin_t = _rope_tables(t)
    w_conv, w_rest, cw, cb, prm = _prepare_params(w_in, conv_w, conv_b, ssd_A_log, ssd_dt_bias,
                                                  gdn_A_log, gdn_dt_bias)
    qk_w = jnp.concatenate([jnp.tile(q_norm_w, (1, ATT_QH)), jnp.tile(k_norm_w, (1, ATT_KVH))], axis=1)
    dskip = jnp.repeat(ssd_D, HD, axis=1)
    gdn_nw = jnp.tile(gdn_norm_w, (1, GDN_HEADS))
    w_out16 = w_out.astype(BF16)

    xa, xb, off = ctx, x, 1
    for layer in range(DEPTH):
        first_tile = 1 if layer == DEPTH - 1 else 0
        mod3 = _modulation(c_all, w_mod[layer], b_mod[layer]).reshape(bsz + 8, 1, 3 * D_MODEL)
        ssd_in, gdn_in, zs, q, kv, small = _in_projection(
            xa, xb, off, mod3, norm_w[layer][None], w_conv[layer], w_rest[layer], cw[layer], cb[layer],
            qk_w[layer][None], cos_t, sin_t)
        y = _ssd(ssd_in, small, prm[layer], dskip[layer][None])
        o = _gdn(gdn_in, small, prm[layer])
        a = _attention(q, kv, first_tile)
        out = _out_projection(xa, xb, off, y, o, a, zs, mod3, ssd_norm_w[layer][None], gdn_nw[layer][None],
                              w_out16[layer], first_tile)
        xa, xb, off = out, out, 0
    return out
```

```python
import functools
import math

import jax
import jax.numpy as jnp
import numpy as np
from jax import lax
from jax.experimental import pallas as pl
from jax.experimental.pallas import tpu as pltpu

F32 = jnp.float32
BF16 = jnp.bfloat16

D_MODEL = 1024
CTX = 256
GRID_W = 64
EPS = 1e-6
DEPTH = 2

HD = 64
SSD_HEADS = 6
SSD_W = SSD_HEADS * HD
SSD_CHUNK = 128
GDN_HEADS = 4
GDN_W = GDN_HEADS * HD
GDN_CHUNK = 64
ATT_QH = 6
ATT_KVH = 2
ATT_W = ATT_QH * HD
ROPE_THETA = 10000.0
Q_PRESCALE = (HD ** -0.5) * math.log2(math.e)

CONV_DIM = 1408
REST_DIM = 1792
TM = 256
HALO = 8
NEG = -1e30
VMEM_LIMIT = 56 * 1024 * 1024

SM_DT = 0
SM_A = 12
SM_B = 20


def _mm(a, b):
    return jnp.dot(a.astype(BF16), b.astype(BF16), preferred_element_type=F32)


def _mm_nt(a, b):
    return lax.dot_general(a.astype(BF16), b.astype(BF16), (((1,), (1,)), ((), ())),
                           preferred_element_type=F32)


def _mm_tn(a, b):
    return lax.dot_general(a.astype(BF16), b.astype(BF16), (((0,), (0,)), ((), ())),
                           preferred_element_type=F32)


def _split3(a):
    h = a.astype(BF16)
    r = a - h.astype(F32)
    m = r.astype(BF16)
    l = (r - m.astype(F32)).astype(BF16)
    return h, m, l


def _mm_exact_rhs(a, b_bf16):
    h, m, l = _split3(a)
    d = functools.partial(jnp.dot, preferred_element_type=F32)
    return d(h, b_bf16) + (d(m, b_bf16) + d(l, b_bf16))


def _mm_exact_lhs(a_bf16, b):
    h, m, l = _split3(b)
    d = functools.partial(jnp.dot, preferred_element_type=F32)
    return d(a_bf16, h) + (d(a_bf16, m) + d(a_bf16, l))


def _silu(x):
    return x * jax.nn.sigmoid(x)


def _softplus(x):
    return jnp.maximum(x, 0.0) + jnp.log1p(jnp.exp(-jnp.abs(x)))


def _iota2(shape, dim):
    return lax.broadcasted_iota(jnp.int32, shape, dim)


def _group_sum64(xx):
    r = jnp.right_shift(_iota2((128, 128), 0), 6)
    c = jnp.right_shift(_iota2((128, 128), 1), 6)
    g = (r == c).astype(BF16)
    outs = []
    for t in range(xx.shape[1] // 128):
        outs.append(_mm_exact_rhs(xx[:, t * 128:(t + 1) * 128], g))
    return outs[0] if len(outs) == 1 else jnp.concatenate(outs, axis=1)


def _mod_kernel(c_ref, w_ref, b_ref, o_ref):
    o_ref[...] = _mm(_silu(c_ref[...]), w_ref[...]) + b_ref[...]


def _modulation(c_all, w_mod, b_mod):
    n = w_mod.shape[1]
    bn = 768
    rows = c_all.shape[0]
    return pl.pallas_call(
        _mod_kernel,
        out_shape=jax.ShapeDtypeStruct((rows, n), F32),
        grid=(n // bn,),
        in_specs=[pl.BlockSpec((rows, D_MODEL), lambda j: (0, 0)),
                  pl.BlockSpec((D_MODEL, bn), lambda j: (0, j)),
                  pl.BlockSpec((1, bn), lambda j: (0, j))],
        out_specs=pl.BlockSpec((rows, bn), lambda j: (0, j)),
        compiler_params=pltpu.CompilerParams(dimension_semantics=("arbitrary",),
                                             vmem_limit_bytes=VMEM_LIMIT),
        name="modulation",
    )(c_all, w_mod, b_mod.reshape(1, n))


def _proj_kernel(nt, xa_ref, xm_ref, xp_ref, xn_ref, mod_ref, nw_ref, wc_ref, wr_ref, cw_ref, cb_ref,
                 qkw_ref, cos_ref, sin_ref,
                 ssd_ref, gdn_ref, zs_ref, q_ref, kv_ref, small_ref):
    i = pl.program_id(1)
    m = mod_ref[0]
    gain = nw_ref[...] * (1.0 + m[:, D_MODEL:2 * D_MODEL])
    shift = m[:, :D_MODEL]

    xm = jnp.where(i == 0, xa_ref[0], xm_ref[0])
    xe = jnp.concatenate([xm, xp_ref[0], xn_ref[0]], axis=0)
    ms = jnp.mean(xe * xe, axis=-1, keepdims=True)
    he = (xe * lax.rsqrt(ms + EPS) * gain + shift).astype(BF16)

    p = jnp.dot(he, wc_ref[...], preferred_element_type=F32)
    r = jnp.dot(he[:TM], wr_ref[...], preferred_element_type=F32)

    pm = p[:TM]
    seg_first = i <= 1
    seg_last = jnp.logical_or(i == 0, i == nt - 1)
    prev_row = jnp.where(seg_first, 0.0, p[TM + HALO - 1:TM + HALO])
    next_row = jnp.where(seg_last, 0.0, p[TM + HALO:TM + HALO + 1])
    rows = _iota2((TM, 1), 0)
    pm1 = jnp.where(rows == 0, prev_row, pltpu.roll(pm, 1, axis=0))
    pp1 = jnp.where(rows == TM - 1, next_row, pltpu.roll(pm, TM - 1, axis=0))
    cw = cw_ref[...]
    conv = cw[0:1] * pm1 + cw[1:2] * pm + cw[2:3] * pp1 + cb_ref[...]
    co = _silu(conv)

    ssd_ref[0] = co[:, :640].astype(BF16)
    gqk = co[:, 640:1152]
    gqk = gqk * lax.rsqrt(_group_sum64(gqk * gqk) + EPS)
    gdn_ref[0, :, :256] = (gqk[:, :256] * (HD ** -0.5)).astype(BF16)
    gdn_ref[0, :, 256:512] = gqk[:, 256:].astype(BF16)
    gdn_ref[0, :, 512:] = co[:, 1152:1408].astype(BF16)

    z1 = r[:, :640]
    z2 = r[:, 1280:1664]
    zs_ref[0, :, :640] = _silu(z1).astype(BF16)
    zs_ref[0, :, 640:] = _silu(z2).astype(BF16)

    aqk = r[:, 640:1152]
    aqk = aqk * lax.rsqrt(_group_sum64(aqk * aqk) * (1.0 / HD) + EPS) * qkw_ref[...]
    lane = _iota2((1, 512), 1)
    swapped = jnp.where(jnp.bitwise_and(lane, 31) < 16,
                        pltpu.roll(aqk, 512 - 16, axis=1), pltpu.roll(aqk, 16, axis=1))
    cos = jnp.concatenate([cos_ref[...]] * 4, axis=1)
    sin = jnp.concatenate([sin_ref[...]] * 4, axis=1)
    aqk = aqk * cos + swapped * sin
    q_ref[0] = (aqk[:, :384] * Q_PRESCALE).astype(BF16)
    ones = jnp.ones((TM, HD), F32)
    zeros = jnp.zeros((TM, HD), F32)
    for g in range(ATT_KVH):
        kv_ref[0, g] = jnp.concatenate([aqk[:, 384 + g * HD:384 + (g + 1) * HD], zeros,
                                        r[:, 1152 + g * HD:1152 + (g + 1) * HD], ones], axis=1).astype(BF16)
    small_ref[0] = r[:, 1664:1792]


def _in_projection(xa, xb, off, mod3, norm_w, w_conv, w_rest, conv_w, conv_b, qk_w, cos_t, sin_t):
    bsz = xb.shape[0]
    nt = xb.shape[1] // TM + off
    t = nt * TM
    nblk8 = xb.shape[1] // HALO
    per_tile = TM // HALO
    row = lambda b, i: (b, i, 0)
    full = lambda b, i: (0, 0)
    outs = [jax.ShapeDtypeStruct((bsz, t, w), dt)
            for w, dt in ((640, BF16), (768, BF16), (1024, BF16), (384, BF16), (128, F32))]
    outs.insert(4, jax.ShapeDtypeStruct((bsz, ATT_KVH, t, 256), BF16))
    return pl.pallas_call(
        functools.partial(_proj_kernel, nt),
        out_shape=outs,
        grid=(bsz, nt),
        in_specs=[
            pl.BlockSpec((1, TM, D_MODEL), lambda b, i: (b, 0, 0)),
            pl.BlockSpec((1, TM, D_MODEL), lambda b, i: (b, jnp.maximum(i - off, 0), 0)),
            pl.BlockSpec((1, HALO, D_MODEL), lambda b, i: (b, jnp.maximum((i - off) * per_tile - 1, 0), 0)),
            pl.BlockSpec((1, HALO, D_MODEL),
                         lambda b, i: (b, jnp.minimum((i - off + 1) * per_tile, nblk8 - 1), 0)),
            pl.BlockSpec((1, 1, 3 * D_MODEL), lambda b, i: (jnp.where(i == 0, bsz, b), 0, 0)),
            pl.BlockSpec((1, D_MODEL), full),
            pl.BlockSpec((D_MODEL, CONV_DIM), full),
            pl.BlockSpec((D_MODEL, REST_DIM), full),
            pl.BlockSpec((3, CONV_DIM), full),
            pl.BlockSpec((1, CONV_DIM), full),
            pl.BlockSpec((1, 512), full),
            pl.BlockSpec((TM, 128), lambda b, i: (i, 0)),
            pl.BlockSpec((TM, 128), lambda b, i: (i, 0)),
        ],
        out_specs=[pl.BlockSpec((1, TM, 640), row), pl.BlockSpec((1, TM, 768), row),
                   pl.BlockSpec((1, TM, 1024), row), pl.BlockSpec((1, TM, 384), row),
                   pl.BlockSpec((1, ATT_KVH, TM, 256), lambda b, i: (b, 0, i, 0)),
                   pl.BlockSpec((1, TM, 128), row)],
        compiler_params=pltpu.CompilerParams(dimension_semantics=("arbitrary", "arbitrary"),
                                             vmem_limit_bytes=VMEM_LIMIT),
        name="in_projection",
    )(xa, xb, xb, xb, mod3, norm_w, w_conv, w_rest, conv_w, conv_b, qk_w, cos_t, sin_t)


def _ssd_kernel(nc, ctx_chunks, xin_ref, small_ref, prm_ref, dskip_ref, y_ref,
                cum_s, cumt_s, dt_s, dtt_s, y_s, st_s, sin_s):
    q = SSD_CHUNK
    ri = _iota2((q, q), 0)
    ci = _iota2((q, q), 1)
    ltri = (ri >= ci).astype(BF16)
    utri = (ri <= ci).astype(BF16)
    lane = _iota2((1, 128), 1)
    fwd_lane = lane < SSD_HEADS

    a_row = -jnp.exp(prm_ref[0:1, :])
    bias_row = prm_ref[1:2, :]

    def prep(c, carry):
        base = pl.multiple_of(c * q, q)
        dt = _softplus(small_ref[0, pl.ds(base, q), :] + bias_row)
        dta = dt * a_row
        cum = jnp.where(fwd_lane, _mm_exact_lhs(ltri, dta), _mm_exact_lhs(utri, dta))
        cum_s[pl.ds(base, q), :] = cum
        dt_s[pl.ds(base, q), :] = dt
        cumt_s[pl.ds(base, q), :] = cum.T
        dtt_s[pl.ds(base, q), :] = dt.T
        return carry

    lax.fori_loop(0, nc, prep, 0)

    lower = ri >= ci
    upper = ri <= ci
    dsk = dskip_ref[...]

    def diag(c, carry):
        base = pl.multiple_of(c * q, q)
        xin = xin_ref[0, pl.ds(base, q), :]
        cum = cum_s[pl.ds(base, q), :]
        cumt = cumt_s[pl.ds(base, q), :]
        dtc = dt_s[pl.ds(base, q), :]
        dtt = dtt_s[pl.ds(base, q), :]
        for g in range(2):
            bg = xin[:, 384 + g * HD:384 + (g + 1) * HD]
            cg = xin[:, 512 + g * HD:512 + (g + 1) * HD]
            sc = _mm_nt(cg, bg)
            xds = [[], []]
            for hg in range(3):
                h = g * 3 + hg
                xh = xin[:, h * HD:(h + 1) * HD]
                w = None
                for d in range(2):
                    col = d * SSD_HEADS + h
                    ccol = cum[:, col:col + 1]
                    diff = ccol - cumt[col:col + 1, :]
                    lm = jnp.exp(jnp.where(lower if d == 0 else upper, diff, NEG))
                    wd = lm * dtt[col:col + 1, :]
                    w = wd if w is None else w + wd
                    end = cum[q - 1:q, col:col + 1] if d == 0 else cum[0:1, col:col + 1]
                    xds[d].append(xh * (jnp.exp(end - ccol) * dtc[:, col:col + 1]))
                y_s[pl.ds(base, q), h * HD:(h + 1) * HD] = _mm(w * sc, xh) + dsk[:, h * HD:(h + 1) * HD] * xh
            for d in range(2):
                xd = jnp.concatenate(xds[d], axis=1)
                st_s[c * 4 + d * 2 + g] = _mm_tn(bg, xd)
        return carry

    lax.fori_loop(0, nc, diag, 0)

    orders = (list(range(nc)),
              list(range(ctx_chunks - 1, -1, -1)) + list(range(nc - 1, ctx_chunks - 1, -1)))
    for d in range(2):
        for g in range(2):
            s = jnp.zeros((HD, 3 * HD), F32)
            for c in orders[d]:
                slot = c * 4 + d * 2 + g
                sin_s[slot] = s
                row = c * q + (q - 1 if d == 0 else 0)
                decs = []
                for hg in range(3):
                    col = d * SSD_HEADS + g * 3 + hg
                    decs.append(jnp.broadcast_to(jnp.exp(cum_s[row:row + 1, col:col + 1]), (1, HD)))
                s = s * jnp.concatenate(decs, axis=1) + st_s[slot]

    def off(c, carry):
        base = pl.multiple_of(c * q, q)
        xin = xin_ref[0, pl.ds(base, q), :]
        cum = cum_s[pl.ds(base, q), :]
        parts = []
        for g in range(2):
            cg = xin[:, 512 + g * HD:512 + (g + 1) * HD]
            acc = None
            for d in range(2):
                yo = _mm(cg, sin_s[c * 4 + d * 2 + g])
                es = []
                for hg in range(3):
                    col = d * SSD_HEADS + g * 3 + hg
                    es.append(jnp.broadcast_to(jnp.exp(cum[:, col:col + 1]), (q, HD)))
                yo = yo * jnp.concatenate(es, axis=1)
                acc = yo if acc is None else acc + yo
            parts.append(acc)
        y_ref[0, pl.ds(base, q), :] = (y_s[pl.ds(base, q), :] + jnp.concatenate(parts, axis=1)).astype(BF16)
        return carry

    lax.fori_loop(0, nc, off, 0)


def _ssd(ssd_in, small, prm, dskip):
    bsz, t, _ = ssd_in.shape
    nc = t // SSD_CHUNK
    return pl.pallas_call(
        functools.partial(_ssd_kernel, nc, CTX // SSD_CHUNK),
        out_shape=jax.ShapeDtypeStruct((bsz, t, SSD_W), BF16),
        grid=(bsz,),
        in_specs=[pl.BlockSpec((1, t, 640), lambda b: (b, 0, 0)),
                  pl.BlockSpec((1, t, 128), lambda b: (b, 0, 0)),
                  pl.BlockSpec((8, 128), lambda b: (0, 0)),
                  pl.BlockSpec((1, SSD_W), lambda b: (0, 0))],
        out_specs=pl.BlockSpec((1, t, SSD_W), lambda b: (b, 0, 0)),
        scratch_shapes=[pltpu.VMEM((t, 128), F32), pltpu.VMEM((t, 128), F32),
                        pltpu.VMEM((t, 128), F32), pltpu.VMEM((t, 128), F32),
                        pltpu.VMEM((t, SSD_W), F32),
                        pltpu.VMEM((nc * 4, HD, 3 * HD), F32),
                        pltpu.VMEM((nc * 4, HD, 3 * HD), F32)],
        compiler_params=pltpu.CompilerParams(dimension_semantics=("arbitrary",),
                                             vmem_limit_bytes=VMEM_LIMIT),
        name="ssd",
    )(ssd_in, small, prm, dskip)


def _bd4(x):
    xb = x.astype(BF16)
    blk = jnp.right_shift(_iota2((1, 4 * HD), 1), 6)
    zero = jnp.zeros_like(xb)
    return jnp.concatenate([jnp.where(blk == h, xb, zero) for h in range(GDN_HEADS)], axis=0)


def _bmm(a, b):
    return lax.dot_general(a.astype(BF16), b.astype(BF16), (((2,), (1,)), ((0,), (0,))),
                           preferred_element_type=F32)


def _bmm_nt(a, b):
    return lax.dot_general(a.astype(BF16), b.astype(BF16), (((2,), (2,)), ((0,), (0,))),
                           preferred_element_type=F32)


def _unit_tri_inverse_stages(a_strict):
    n = a_strict.shape[-1]
    ri = _iota2((1, n, n), 1)
    ci = _iota2((1, n, n), 2)

    def blk(shift):
        return jnp.right_shift(ri, shift) == jnp.right_shift(ci, shift)

    nd = jnp.where(blk(3), -a_strict, 0.0)
    p0 = (ri == ci).astype(F32) + nd
    q1 = _bmm(nd, nd)
    yield None
    t = _bmm(jnp.concatenate([p0, q1], axis=1), q1)
    yield None
    p1 = p0 + t[:, :n]
    inv = p1 + _bmm(p1, t[:, n:])
    yield None
    for shift in (4, 5, 6):
        e = jnp.where(jnp.logical_and(blk(shift), jnp.logical_not(blk(shift - 1))), a_strict, 0.0)
        ie = _bmm(inv, e)
        yield None
        inv = inv - _bmm(ie, inv)
        yield None
    yield inv


def _gdn_attn_kernel(nch, ctx_chunks, gin_ref, small_ref, prm_ref, q_ref, kv_ref, o_ref, att_ref,
                     gc_s, rows_s, mq_s, n_s, cd_s, of_s, ob_s):
    c64 = GDN_CHUNK
    w4 = GDN_W
    nsc = nch // 2
    ri128 = _iota2((128, 128), 0)
    ci128 = _iota2((128, 128), 1)
    same = jnp.right_shift(ri128, 6) == jnp.right_shift(ci128, 6)
    lbd = jnp.logical_and(same, ri128 >= ci128).astype(BF16)
    ubd = jnp.logical_and(same, ri128 <= ci128).astype(BF16)
    lane = _iota2((1, 128), 1)
    fwd_lane = lane < SM_A + GDN_HEADS
    beta_lane = jnp.logical_and(lane >= SM_B, lane < SM_B + 2 * GDN_HEADS)

    a_row = -jnp.exp(prm_ref[0:1, :])
    bias_row = prm_ref[1:2, :]

    def prep(sc, carry):
        base = pl.multiple_of(sc * 128, 128)
        sm = small_ref[0, pl.ds(base, 128), :]
        g = a_row * _softplus(sm + bias_row)
        cum = jnp.where(fwd_lane, _mm_exact_lhs(lbd, g), _mm_exact_lhs(ubd, g))
        gc_s[pl.ds(base, 128), :] = cum
        slab_t = jnp.where(beta_lane, jax.nn.sigmoid(sm), cum).T
        for half in range(2):
            lo = half * c64
            table = []
            for first in (SM_A, SM_A + GDN_HEADS, SM_B, SM_B + GDN_HEADS):
                table.append(jnp.concatenate(
                    [slab_t[first + h:first + h + 1, lo:lo + c64] for h in range(GDN_HEADS)], axis=1))
            for d, edge in ((0, lo + c64 - 1), (1, lo)):
                first = SM_A + d * GDN_HEADS
                table.append(jnp.concatenate(
                    [jnp.broadcast_to(slab_t[first + h:first + h + 1, edge:edge + 1], (1, c64))
                     for h in range(GDN_HEADS)], axis=1))
            table.append(jnp.zeros((2, w4), F32))
            rows_s[pl.ds(pl.multiple_of(sc * 16 + half * 8, 8), 8), :] = jnp.concatenate(table, axis=0)
        return carry

    lax.fori_loop(0, nsc, prep, 0)

    ri = _iota2((c64, c64), 0)
    ci = _iota2((c64, c64), 1)
    incl = (ri >= ci, ri <= ci)
    strict = (ri > ci, ri < ci)

    nb = 4

    def local(cb):
        chunks = []
        for cc in range(nb):
            c = cb * nb + cc
            base = pl.multiple_of(c * c64, c64)
            gin = gin_ref[0, pl.ds(base, c64), :]
            gc = gc_s[pl.ds(base, c64), :]
            rows = rows_s[pl.ds(pl.multiple_of(c * 8, 8), 8), :]
            qs = [gin[:, h * HD:(h + 1) * HD] for h in range(GDN_HEADS)]
            ks = [gin[:, w4 + h * HD:w4 + (h + 1) * HD] for h in range(GDN_HEADS)]
            vs = [gin[:, 2 * w4 + h * HD:2 * w4 + (h + 1) * HD] for h in range(GDN_HEADS)]
            chunks.append((c, base, gc, rows, qs, ks, vs))
        kq = _bmm_nt(jnp.stack([jnp.concatenate([ch[5][h], ch[4][h]], axis=0)
                                for ch in chunks for h in range(GDN_HEADS)]),
                     jnp.stack([ch[5][h] for ch in chunks for h in range(GDN_HEADS)]))
        yield
        mb_l, rhs_l, lhs_l, qd_l = [], [], [], []
        for ci, (c, base, gc, rows, qs, ks, vs) in enumerate(chunks):
            kts = [k.astype(F32).T for k in ks]
            for d in range(2):
                for h in range(GDN_HEADS):
                    sl = slice(h * HD, (h + 1) * HD)
                    colg = SM_A + d * GDN_HEADS + h
                    kqh = kq[ci * GDN_HEADS + h]
                    gmat = jnp.broadcast_to(gc[:, colg:colg + 1], (c64, c64))
                    gcr, br, endr = rows[d:d + 1, sl], rows[2 + d:3 + d, sl], rows[4 + d:5 + d, sl]
                    dec = jnp.exp(jnp.where(incl[d], gmat - gcr, NEG))
                    mb_l.append(jnp.where(strict[d], kqh[:c64] * dec, 0.0) * br)
                    eg = jnp.exp(gmat)
                    kdtb = kts[h] * (jnp.exp(endr - gcr) * br)
                    lhs_l.append(jnp.concatenate([kdtb, kqh[c64:] * dec * br], axis=0))
                    rhs_l.append(jnp.concatenate([vs[h], ks[h] * eg], axis=1))
                    qd_l.append(qs[h] * eg)
        z = None
        for z in _unit_tri_inverse_stages(jnp.stack(mb_l)):
            if z is None:
                yield
        zr = _bmm(z, jnp.stack(rhs_l))
        yield
        a12 = _bmm(jnp.stack(lhs_l), zr)
        yield
        for ci, (c, base, gc, rows, qs, ks, vs) in enumerate(chunks):
            for d in range(2):
                slot = c * 2 + d
                ids = [ci * 2 * GDN_HEADS + d * GDN_HEADS + h for h in range(GDN_HEADS)]
                mq = [jnp.concatenate([-a12[n][:c64, HD:], qd_l[n] - a12[n][c64:, HD:]], axis=0) for n in ids]
                mq_s[slot] = jnp.concatenate(mq, axis=1).astype(BF16)
                n_s[slot] = jnp.concatenate([a12[n][:c64, :HD] for n in ids], axis=1)
                cd_s[slot] = jnp.broadcast_to(jnp.exp(rows[4 + d:5 + d]), (8, w4))
                oc = jnp.concatenate([a12[n][c64:, :HD] for n in ids], axis=1)
                if d == 0:
                    of_s[pl.ds(base, c64), :] = oc
                else:
                    ob_s[pl.ds(base, c64), :] = oc

    t_all = q_ref.shape[1]
    key_idx = _iota2((1, t_all), 1)
    heads_per_kv = ATT_QH // ATT_KVH

    def attend(tile):
        rows = pl.ds(pl.multiple_of(tile * TM, TM), TM)
        qt = q_ref[0, rows, :]
        nkeys = jnp.where(tile == 0, CTX, t_all)
        for g in range(ATT_KVH):
            k = kv_ref[0, g, :, :HD]
            va = kv_ref[0, g, :, 2 * HD:]
            outs = []
            for jj in range(heads_per_kv):
                h = g * heads_per_kv + jj
                s = _mm_nt(qt[:, h * HD:(h + 1) * HD], k)
                yield
                s = jnp.where(key_idx < nkeys, s, NEG)
                p = jnp.exp2(s - jnp.max(s, axis=-1, keepdims=True))
                acc = _mm(p, va)
                yield
                outs.append(acc[:, :HD] / acc[:, HD:])
            outs.append(jnp.zeros((TM, HD), F32))
            att_ref[0, g, rows, :] = jnp.concatenate(outs, axis=1).astype(BF16)

    def trip(j, carry):
        streams = [local(j), attend(j)]
        while streams:
            for st in list(streams):
                try:
                    next(st)
                except StopIteration:
                    streams.remove(st)
        return carry

    lax.fori_loop(0, nch // nb, trip, 0)

    def step(t, states):
        cf = t
        cb = jnp.where(t < ctx_chunks, ctx_chunks - 1 - t, nch - 1 + ctx_chunks - t)
        new_states = []
        for d, c in ((0, cf), (1, cb)):
            slot = c * 2 + d
            s = states[d]
            r = jnp.dot(mq_s[slot], _bd4(s), preferred_element_type=F32)
            new_states.append(s * cd_s[slot][0:1] + r[:c64] + n_s[slot])
            rows = pl.ds(pl.multiple_of(c * c64, c64), c64)
            if d == 0:
                of_s[rows, :] = of_s[rows, :] + r[c64:]
            else:
                ob_s[rows, :] = ob_s[rows, :] + r[c64:]
        return tuple(new_states)

    zero = jnp.zeros((c64, w4), F32)
    lax.fori_loop(0, nch, step, (zero, zero))
    o_ref[0] = (of_s[...] + ob_s[...]).astype(BF16)


def _gdn_attention(gdn_in, small, prm, q, kv):
    bsz, t, _ = gdn_in.shape
    nch = t // GDN_CHUNK
    assert nch // 4 == t // TM
    return pl.pallas_call(
        functools.partial(_gdn_attn_kernel, nch, CTX // GDN_CHUNK),
        out_shape=[jax.ShapeDtypeStruct((bsz, t, GDN_W), BF16),
                   jax.ShapeDtypeStruct((bsz, ATT_KVH, t, 256), BF16)],
        grid=(bsz,),
        in_specs=[pl.BlockSpec((1, t, 768), lambda b: (b, 0, 0)),
                  pl.BlockSpec((1, t, 128), lambda b: (b, 0, 0)),
                  pl.BlockSpec((8, 128), lambda b: (0, 0)),
                  pl.BlockSpec((1, t, ATT_W), lambda b: (b, 0, 0)),
                  pl.BlockSpec((1, ATT_KVH, t, 256), lambda b: (b, 0, 0, 0))],
        out_specs=[pl.BlockSpec((1, t, GDN_W), lambda b: (b, 0, 0)),
                   pl.BlockSpec((1, ATT_KVH, t, 256), lambda b: (b, 0, 0, 0))],
        scratch_shapes=[pltpu.VMEM((t, 128), F32),
                        pltpu.VMEM((nch * 8, GDN_W), F32),
                        pltpu.VMEM((nch * 2, 128, GDN_W), BF16),
                        pltpu.VMEM((nch * 2, GDN_CHUNK, GDN_W), F32),
                        pltpu.VMEM((nch * 2, 8, GDN_W), F32),
                        pltpu.VMEM((t, GDN_W), F32),
                        pltpu.VMEM((t, GDN_W), F32)],
        compiler_params=pltpu.CompilerParams(dimension_semantics=("arbitrary",),
                                             vmem_limit_bytes=VMEM_LIMIT),
        name="gdn_attention",
    )(gdn_in, small, prm, q, kv)


def _out_kernel(first_tile, xa_ref, x_ref, y_ref, o_ref, a_ref, zs_ref, mod_ref, sw_ref, gw_ref, w_ref, out_ref):
    zs = zs_ref[0].astype(F32)
    t = y_ref[0].astype(F32) * zs[:, :SSD_W]
    ssd = t * lax.rsqrt(jnp.mean(t * t, axis=-1, keepdims=True) + EPS) * sw_ref[...]
    o = o_ref[0].astype(F32)
    gdn = o * lax.rsqrt(_group_sum64(o * o) * (1.0 / HD) + EPS) * gw_ref[...] * zs[:, SSD_W:SSD_W + GDN_W]
    a = a_ref[0]
    att = jnp.concatenate([a[g][:, :HD * ATT_QH // ATT_KVH] for g in range(ATT_KVH)], axis=1)
    att = att * zs[:, SSD_W + GDN_W:]
    mix = jnp.concatenate([ssd, gdn, att], axis=1).astype(BF16)
    gate = mod_ref[0][:, 2 * D_MODEL:]
    x = jnp.where(pl.program_id(1) + first_tile == 0, xa_ref[0], x_ref[0])
    out_ref[0] = x + gate * jnp.dot(mix, w_ref[...], preferred_element_type=F32)


def _out_projection(xa, xb, off, y, o, a, zs, mod3, ssd_nw, gdn_nw, w_out, first_tile):
    bsz, t, _ = y.shape
    nt = t // TM - first_tile
    row = lambda b, i: (b, i + first_tile, 0)
    full = lambda b, i: (0, 0)
    return pl.pallas_call(
        functools.partial(_out_kernel, first_tile),
        out_shape=jax.ShapeDtypeStruct((bsz, nt * TM, D_MODEL), F32),
        grid=(bsz, nt),
        in_specs=[pl.BlockSpec((1, TM, D_MODEL), lambda b, i: (b, 0, 0)),
                  pl.BlockSpec((1, TM, D_MODEL), lambda b, i: (b, jnp.maximum(i + first_tile - off, 0), 0)),
                  pl.BlockSpec((1, TM, SSD_W), row),
                  pl.BlockSpec((1, TM, GDN_W), row),
                  pl.BlockSpec((1, ATT_KVH, TM, 256), lambda b, i: (b, 0, i + first_tile, 0)),
                  pl.BlockSpec((1, TM, D_MODEL), row),
                  pl.BlockSpec((1, 1, 3 * D_MODEL),
                               lambda b, i: (jnp.where(i + first_tile == 0, bsz, b), 0, 0)),
                  pl.BlockSpec((1, SSD_W), full),
                  pl.BlockSpec((1, GDN_W), full),
                  pl.BlockSpec((D_MODEL, D_MODEL), full)],
        out_specs=pl.BlockSpec((1, TM, D_MODEL), lambda b, i: (b, i, 0)),
        compiler_params=pltpu.CompilerParams(dimension_semantics=("arbitrary", "arbitrary"),
                                             vmem_limit_bytes=VMEM_LIMIT),
        name="out_projection",
    )(xa, xb, y, o, a, zs, mod3, ssd_nw, gdn_nw, w_out)


_CONV_SEGS = [(0, CONV_DIM)]
_REST_SEGS = [(1408, 384), (1804, 256), (2076, 384), (2460, 128), (2588, 128), (2716, 384),
              (1792, 12), (2060, 8), (2068, 8)]


def _take_segments(w, segs, axis):
    return jnp.concatenate([lax.slice_in_dim(w, a, a + n, axis=axis) for a, n in segs], axis=axis)


def _prepare_params(w_in, conv_w, conv_b, ssd_A_log, ssd_dt_bias, gdn_A_log, gdn_dt_bias):
    depth = w_in.shape[0]
    w_conv = _take_segments(w_in, _CONV_SEGS, 2).astype(BF16)
    pad = jnp.zeros(w_in.shape[:2] + (REST_DIM - sum(n for _, n in _REST_SEGS),), w_in.dtype)
    w_rest = jnp.concatenate([_take_segments(w_in, _REST_SEGS, 2), pad], axis=2).astype(BF16)
    cw = _take_segments(conv_w, _CONV_SEGS, 2)
    cb = _take_segments(conv_b, _CONV_SEGS, 1)[:, None, :]

    def lanes(a, b):
        v = jnp.concatenate([a.reshape(depth, -1), b.reshape(depth, -1)], axis=1)
        return jnp.pad(v, ((0, 0), (0, 128 - v.shape[1])))

    prm = jnp.stack([lanes(ssd_A_log, gdn_A_log), lanes(ssd_dt_bias, gdn_dt_bias)], axis=1)
    prm = jnp.pad(prm, ((0, 0), (0, 6), (0, 0)))
    return w_conv, w_rest, cw, cb, prm


def _rope_tables(t):
    f32 = np.float32
    pos = np.arange(t - CTX)
    n_freq = HD // 4
    freqs = np.power(f32(ROPE_THETA), -np.arange(n_freq, dtype=f32) / f32(n_freq)).astype(f32)
    ang_r = ((pos // GRID_W).astype(f32)[:, None] * freqs).astype(f32)
    ang_c = ((pos % GRID_W).astype(f32)[:, None] * freqs).astype(f32)
    cos = np.concatenate([np.cos(ang_r)] * 2 + [np.cos(ang_c)] * 2, axis=1)
    sin = np.concatenate([-np.sin(ang_r), np.sin(ang_r), -np.sin(ang_c), np.sin(ang_c)], axis=1)
    cos = np.concatenate([np.ones((CTX, HD), f32), cos], axis=0).astype(f32)
    sin = np.concatenate([np.zeros((CTX, HD), f32), sin], axis=0).astype(f32)
    return jnp.asarray(np.tile(cos, (1, 2))), jnp.asarray(np.tile(sin, (1, 2)))


def kernel(x, c, ctx, c_ctx, norm_w, w_mod, b_mod, w_in, conv_w, conv_b, ssd_A_log, ssd_dt_bias, ssd_D,
           ssd_norm_w, gdn_A_log, gdn_dt_bias, gdn_norm_w, q_norm_w, k_norm_w, w_out):
    bsz = x.shape[0]
    t = CTX + x.shape[1]
    c_all = jnp.concatenate([c, c_ctx[None, :], jnp.zeros((7, D_MODEL), F32)], axis=0)
    cos_t, sin_t = _rope_tables(t)
    w_conv, w_rest, cw, cb, prm = _prepare_params(w_in, conv_w, conv_b, ssd_A_log, ssd_dt_bias,
                                                  gdn_A_log, gdn_dt_bias)
    qk_w = jnp.concatenate([jnp.tile(q_norm_w, (1, ATT_QH)), jnp.tile(k_norm_w, (1, ATT_KVH))], axis=1)
    dskip = jnp.repeat(ssd_D, HD, axis=1)
    gdn_nw = jnp.tile(gdn_norm_w, (1, GDN_HEADS))
    w_out16 = w_out.astype(BF16)

    xa, xb, off = ctx, x, 1
    for layer in range(DEPTH):
        first_tile = 1 if layer == DEPTH - 1 else 0
        mod3 = _modulation(c_all, w_mod[layer], b_mod[layer]).reshape(bsz + 8, 1, 3 * D_MODEL)
        ssd_in, gdn_in, zs, q, kv, small = _in_projection(
            xa, xb, off, mod3, norm_w[layer][None], w_conv[layer], w_rest[layer], cw[layer], cb[layer],
            qk_w[layer][None], cos_t, sin_t)
        y = _ssd(ssd_in, small, prm[layer], dskip[layer][None])
        o, a = _gdn_attention(gdn_in, small, prm[layer], q, kv)
        out = _out_projection(xa, xb, off, y, o, a, zs, mod3, ssd_norm_w[layer][None], gdn_nw[layer][None],
                              w_out16[layer], first_tile)
        xa, xb, off = out, out, 0
    return out
```

```python
import functools
import math

import jax
import jax.numpy as jnp
import numpy as np
from jax import lax
from jax.experimental import pallas as pl
from jax.experimental.pallas import tpu as pltpu

F32 = jnp.float32
BF16 = jnp.bfloat16

D_MODEL = 1024
CTX = 256
GRID_W = 64
EPS = 1e-6
DEPTH = 2

HD = 64
SSD_HEADS = 6
SSD_W = SSD_HEADS * HD
SSD_CHUNK = 128
GDN_HEADS = 4
GDN_W = GDN_HEADS * HD
GDN_CHUNK = 64
ATT_QH = 6
ATT_KVH = 2
ATT_W = ATT_QH * HD
ROPE_THETA = 10000.0
Q_PRESCALE = (HD ** -0.5) * math.log2(math.e)

CONV_DIM = 1408
REST_DIM = 1792
TM = 256
HALO = 8
NEG = -1e30
VMEM_LIMIT = 56 * 1024 * 1024

SM_DT = 0
SM_A = 12
SM_B = 20


def _mm(a, b):
    return jnp.dot(a.astype(BF16), b.astype(BF16), preferred_element_type=F32)


def _mm_nt(a, b):
    return lax.dot_general(a.astype(BF16), b.astype(BF16), (((1,), (1,)), ((), ())),
                           preferred_element_type=F32)


def _split3(a):
    h = a.astype(BF16)
    r = a - h.astype(F32)
    m = r.astype(BF16)
    l = (r - m.astype(F32)).astype(BF16)
    return h, m, l


def _mm_exact_rhs(a, b_bf16):
    h, m, l = _split3(a)
    d = functools.partial(jnp.dot, preferred_element_type=F32)
    return d(h, b_bf16) + (d(m, b_bf16) + d(l, b_bf16))


def _mm_exact_lhs(a_bf16, b):
    h, m, l = _split3(b)
    d = functools.partial(jnp.dot, preferred_element_type=F32)
    return d(a_bf16, h) + (d(a_bf16, m) + d(a_bf16, l))


def _silu(x):
    return x * jax.nn.sigmoid(x)


def _softplus(x):
    return jnp.maximum(x, 0.0) + jnp.log1p(jnp.exp(-jnp.abs(x)))


def _iota2(shape, dim):
    return lax.broadcasted_iota(jnp.int32, shape, dim)


def _aligned(x, m):
    return x if isinstance(x, int) else pl.multiple_of(x, m)


def _group_sum64(xx):
    r = jnp.right_shift(_iota2((128, 128), 0), 6)
    c = jnp.right_shift(_iota2((128, 128), 1), 6)
    g = (r == c).astype(BF16)
    outs = []
    for t in range(xx.shape[1] // 128):
        outs.append(_mm_exact_rhs(xx[:, t * 128:(t + 1) * 128], g))
    return outs[0] if len(outs) == 1 else jnp.concatenate(outs, axis=1)


def _mod_kernel(c_ref, w_ref, b_ref, o_ref):
    o_ref[...] = _mm(_silu(c_ref[...]), w_ref[...]) + b_ref[...]


def _modulation(c_all, w_mod, b_mod):
    n = w_mod.shape[1]
    bn = 768
    rows = c_all.shape[0]
    return pl.pallas_call(
        _mod_kernel,
        out_shape=jax.ShapeDtypeStruct((rows, n), F32),
        grid=(n // bn,),
        in_specs=[pl.BlockSpec((rows, D_MODEL), lambda j: (0, 0)),
                  pl.BlockSpec((D_MODEL, bn), lambda j: (0, j)),
                  pl.BlockSpec((1, bn), lambda j: (0, j))],
        out_specs=pl.BlockSpec((rows, bn), lambda j: (0, j)),
        compiler_params=pltpu.CompilerParams(dimension_semantics=("arbitrary",),
                                             vmem_limit_bytes=VMEM_LIMIT),
        name="modulation",
    )(c_all, w_mod, b_mod.reshape(1, n))


def _proj_kernel(nt, xa_ref, xm_ref, xp_ref, xn_ref, mod_ref, nw_ref, wc_ref, wr_ref, cw_ref, cb_ref,
                 qkw_ref, cos_ref, sin_ref,
                 ssd_ref, gdn_ref, zs_ref, q_ref, kv_ref, small_ref):
    i = pl.program_id(1)
    m = mod_ref[0]
    gain = nw_ref[...] * (1.0 + m[:, D_MODEL:2 * D_MODEL])
    shift = m[:, :D_MODEL]

    xm = jnp.where(i == 0, xa_ref[0], xm_ref[0])
    xe = jnp.concatenate([xm, xp_ref[0], xn_ref[0]], axis=0)
    ms = jnp.mean(xe * xe, axis=-1, keepdims=True)
    he = (xe * lax.rsqrt(ms + EPS) * gain + shift).astype(BF16)

    p = jnp.dot(he, wc_ref[...], preferred_element_type=F32)
    r = jnp.dot(he[:TM], wr_ref[...], preferred_element_type=F32)

    pm = p[:TM]
    seg_first = i <= 1
    seg_last = jnp.logical_or(i == 0, i == nt - 1)
    prev_row = jnp.where(seg_first, 0.0, p[TM + HALO - 1:TM + HALO])
    next_row = jnp.where(seg_last, 0.0, p[TM + HALO:TM + HALO + 1])
    rows = _iota2((TM, 1), 0)
    pm1 = jnp.where(rows == 0, prev_row, pltpu.roll(pm, 1, axis=0))
    pp1 = jnp.where(rows == TM - 1, next_row, pltpu.roll(pm, TM - 1, axis=0))
    cw = cw_ref[...]
    conv = cw[0:1] * pm1 + cw[1:2] * pm + cw[2:3] * pp1 + cb_ref[...]
    co = _silu(conv)

    ssd_ref[0] = co[:, :640].astype(BF16)
    gqk = co[:, 640:1152]
    gqk = gqk * lax.rsqrt(_group_sum64(gqk * gqk) + EPS)
    gdn_ref[0, :, :256] = (gqk[:, :256] * (HD ** -0.5)).astype(BF16)
    gdn_ref[0, :, 256:512] = gqk[:, 256:].astype(BF16)
    gdn_ref[0, :, 512:] = co[:, 1152:1408].astype(BF16)

    z1 = r[:, :640]
    z2 = r[:, 1280:1664]
    zs_ref[0, :, :640] = _silu(z1).astype(BF16)
    zs_ref[0, :, 640:] = _silu(z2).astype(BF16)

    aqk = r[:, 640:1152]
    aqk = aqk * lax.rsqrt(_group_sum64(aqk * aqk) * (1.0 / HD) + EPS) * qkw_ref[...]
    lane = _iota2((1, 512), 1)
    swapped = jnp.where(jnp.bitwise_and(lane, 31) < 16,
                        pltpu.roll(aqk, 512 - 16, axis=1), pltpu.roll(aqk, 16, axis=1))
    cos = jnp.concatenate([cos_ref[...]] * 4, axis=1)
    sin = jnp.concatenate([sin_ref[...]] * 4, axis=1)
    aqk = aqk * cos + swapped * sin
    q_ref[0] = (aqk[:, :384] * Q_PRESCALE).astype(BF16)
    ones = jnp.ones((TM, HD), F32)
    zeros = jnp.zeros((TM, HD), F32)
    for g in range(ATT_KVH):
        kv_ref[0, g] = jnp.concatenate([aqk[:, 384 + g * HD:384 + (g + 1) * HD], zeros,
                                        r[:, 1152 + g * HD:1152 + (g + 1) * HD], ones], axis=1).astype(BF16)
    small_ref[0] = r[:, 1664:1792]


def _in_projection(xa, xb, off, mod3, norm_w, w_conv, w_rest, conv_w, conv_b, qk_w, cos_t, sin_t):
    bsz = xb.shape[0]
    nt = xb.shape[1] // TM + off
    t = nt * TM
    nblk8 = xb.shape[1] // HALO
    per_tile = TM // HALO
    row = lambda b, i: (b, i, 0)
    full = lambda b, i: (0, 0)
    outs = [jax.ShapeDtypeStruct((bsz, t, w), dt)
            for w, dt in ((640, BF16), (768, BF16), (1024, BF16), (384, BF16), (128, F32))]
    outs.insert(4, jax.ShapeDtypeStruct((bsz, ATT_KVH, t, 256), BF16))
    return pl.pallas_call(
        functools.partial(_proj_kernel, nt),
        out_shape=outs,
        grid=(bsz, nt),
        in_specs=[
            pl.BlockSpec((1, TM, D_MODEL), lambda b, i: (b, 0, 0)),
            pl.BlockSpec((1, TM, D_MODEL), lambda b, i: (b, jnp.maximum(i - off, 0), 0)),
            pl.BlockSpec((1, HALO, D_MODEL), lambda b, i: (b, jnp.maximum((i - off) * per_tile - 1, 0), 0)),
            pl.BlockSpec((1, HALO, D_MODEL),
                         lambda b, i: (b, jnp.minimum((i - off + 1) * per_tile, nblk8 - 1), 0)),
            pl.BlockSpec((1, 1, 3 * D_MODEL), lambda b, i: (jnp.where(i == 0, bsz, b), 0, 0)),
            pl.BlockSpec((1, D_MODEL), full),
            pl.BlockSpec((D_MODEL, CONV_DIM), full),
            pl.BlockSpec((D_MODEL, REST_DIM), full),
            pl.BlockSpec((3, CONV_DIM), full),
            pl.BlockSpec((1, CONV_DIM), full),
            pl.BlockSpec((1, 512), full),
            pl.BlockSpec((TM, 128), lambda b, i: (i, 0)),
            pl.BlockSpec((TM, 128), lambda b, i: (i, 0)),
        ],
        out_specs=[pl.BlockSpec((1, TM, 640), row), pl.BlockSpec((1, TM, 768), row),
                   pl.BlockSpec((1, TM, 1024), row), pl.BlockSpec((1, TM, 384), row),
                   pl.BlockSpec((1, ATT_KVH, TM, 256), lambda b, i: (b, 0, i, 0)),
                   pl.BlockSpec((1, TM, 128), row)],
        compiler_params=pltpu.CompilerParams(dimension_semantics=("arbitrary", "arbitrary"),
                                             vmem_limit_bytes=VMEM_LIMIT),
        name="in_projection",
    )(xa, xb, xb, xb, mod3, norm_w, w_conv, w_rest, conv_w, conv_b, qk_w, cos_t, sin_t)


def _ssd_kernel(nc, ctx_chunks, xin_ref, small_ref, prm_ref, dskip_ref, y_ref,
                cum_s, cumt_s, dtt_s, st_s, sin_s):
    q = SSD_CHUNK
    hpg = SSD_HEADS // 2
    ri = _iota2((q, q), 0)
    ci = _iota2((q, q), 1)
    ltri = (ri >= ci).astype(BF16)
    utri = (ri <= ci).astype(BF16)
    lane = _iota2((1, 128), 1)
    fwd_lane = lane < SSD_HEADS

    a_row = -jnp.exp(prm_ref[0:1, :])
    bias_row = prm_ref[1:2, :]

    def prep(c, carry):
        base = pl.multiple_of(c * q, q)
        dt = _softplus(small_ref[0, pl.ds(base, q), :] + bias_row)
        dta = dt * a_row
        cum = jnp.where(fwd_lane, _mm_exact_lhs(ltri, dta), _mm_exact_lhs(utri, dta))
        cum_s[pl.ds(base, q), :] = cum
        cumt_s[pl.ds(base, q), :] = cum.T
        dtt_s[pl.ds(base, q), :] = dt.T
        return carry

    lax.fori_loop(0, nc, prep, 0)

    def end_of(cumt, d, col):
        return cumt[col:col + 1, q - 1:q] if d == 0 else cumt[col:col + 1, 0:1]

    def states(c, carry):
        base = pl.multiple_of(c * q, q)
        xin = xin_ref[0, pl.ds(base, q), :]
        cumt = cumt_s[pl.ds(base, q), :]
        dtt = dtt_s[pl.ds(base, q), :]
        lhs, rhs = [], []
        for g in range(2):
            bgt = xin[:, 384 + g * HD:384 + (g + 1) * HD].astype(F32).T
            for d in range(2):
                for hg in range(hpg):
                    h = g * hpg + hg
                    col = d * SSD_HEADS + h
                    w_row = jnp.exp(end_of(cumt, d, col) - cumt[col:col + 1, :]) * dtt[col:col + 1, :]
                    lhs.append(bgt * w_row)
                    rhs.append(xin[:, h * HD:(h + 1) * HD])
        st = _bmm(jnp.stack(lhs), jnp.stack(rhs))
        st_s[pl.ds(c * 12, 12)] = st
        return carry

    lax.fori_loop(0, nc, states, 0)

    orders = (list(range(nc)),
              list(range(ctx_chunks - 1, -1, -1)) + list(range(nc - 1, ctx_chunks - 1, -1)))
    for g in range(2):
        for d in range(2):
            for hg in range(hpg):
                col = d * SSD_HEADS + g * hpg + hg
                s = jnp.zeros((HD, HD), F32)
                for c in orders[d]:
                    slot = c * 12 + g * 6 + d * 3 + hg
                    sin_s[slot] = s
                    row = c * q + (q - 1 if d == 0 else 0)
                    s = s * jnp.exp(cum_s[row:row + 1, col:col + 1]) + st_s[slot]

    lower = ri >= ci
    upper = ri <= ci
    dsk = dskip_ref[...]

    def outputs(c, carry):
        base = pl.multiple_of(c * q, q)
        xin = xin_ref[0, pl.ds(base, q), :]
        cum = cum_s[pl.ds(base, q), :]
        cumt = cumt_s[pl.ds(base, q), :]
        dtt = dtt_s[pl.ds(base, q), :]
        sin = sin_s[pl.ds(c * 12, 12)]
        ws, xs, cs, es = [], [], [], []
        for g in range(2):
            bg = xin[:, 384 + g * HD:384 + (g + 1) * HD]
            cg = xin[:, 512 + g * HD:512 + (g + 1) * HD]
            sc = _mm_nt(cg, bg)
            for hg in range(hpg):
                h = g * hpg + hg
                w = None
                for d in range(2):
                    col = d * SSD_HEADS + h
                    gmat = jnp.broadcast_to(cum[:, col:col + 1], (q, q))
                    lm = jnp.exp(jnp.where(lower if d == 0 else upper, gmat - cumt[col:col + 1, :], NEG))
                    wd = lm * dtt[col:col + 1, :]
                    w = wd if w is None else w + wd
                    es.append((g * 6 + d * 3 + hg, jnp.exp(gmat[:, :HD])))
                ws.append(w * sc)
                xs.append(xin[:, h * HD:(h + 1) * HD])
            cs.append(cg)
        ydiag = _bmm(jnp.stack(ws), jnp.stack(xs))
        es.sort(key=lambda t: t[0])
        cstack = jnp.stack([cs[i // 6] for i in range(12)])
        yoff = _bmm(cstack, sin)
        outs = []
        for h in range(SSD_HEADS):
            g, hg = divmod(h, hpg)
            y = ydiag[h] + dsk[:, h * HD:(h + 1) * HD] * xs[h]
            for d in range(2):
                i = g * 6 + d * 3 + hg
                y = y + yoff[i] * es[i][1]
            outs.append(y)
        y_ref[0, pl.ds(base, q), :] = jnp.concatenate(outs, axis=1).astype(BF16)
        return carry

    lax.fori_loop(0, nc, outputs, 0)


def _ssd(ssd_in, small, prm, dskip):
    bsz, t, _ = ssd_in.shape
    nc = t // SSD_CHUNK
    return pl.pallas_call(
        functools.partial(_ssd_kernel, nc, CTX // SSD_CHUNK),
        out_shape=jax.ShapeDtypeStruct((bsz, t, SSD_W), BF16),
        grid=(bsz,),
        in_specs=[pl.BlockSpec((1, t, 640), lambda b: (b, 0, 0)),
                  pl.BlockSpec((1, t, 128), lambda b: (b, 0, 0)),
                  pl.BlockSpec((8, 128), lambda b: (0, 0)),
                  pl.BlockSpec((1, SSD_W), lambda b: (0, 0))],
        out_specs=pl.BlockSpec((1, t, SSD_W), lambda b: (b, 0, 0)),
        scratch_shapes=[pltpu.VMEM((t, 128), F32), pltpu.VMEM((t, 128), F32), pltpu.VMEM((t, 128), F32),
                        pltpu.VMEM((nc * 12, HD, HD), F32),
                        pltpu.VMEM((nc * 12, HD, HD), F32)],
        compiler_params=pltpu.CompilerParams(dimension_semantics=("arbitrary",),
                                             vmem_limit_bytes=VMEM_LIMIT),
        name="ssd",
    )(ssd_in, small, prm, dskip)


def _bd4(x):
    xb = x.astype(BF16)
    blk = jnp.right_shift(_iota2((1, 4 * HD), 1), 6)
    zero = jnp.zeros_like(xb)
    return jnp.concatenate([jnp.where(blk == h, xb, zero) for h in range(GDN_HEADS)], axis=0)


def _bmm(a, b):
    return lax.dot_general(a.astype(BF16), b.astype(BF16), (((2,), (1,)), ((0,), (0,))),
                           preferred_element_type=F32)


def _bmm_nt(a, b):
    return lax.dot_general(a.astype(BF16), b.astype(BF16), (((2,), (2,)), ((0,), (0,))),
                           preferred_element_type=F32)


def _unit_tri_inverse_stages(a_strict):
    n = a_strict.shape[-1]
    ri = _iota2((1, n, n), 1)
    ci = _iota2((1, n, n), 2)

    def blk(shift):
        return jnp.right_shift(ri, shift) == jnp.right_shift(ci, shift)

    nd = jnp.where(blk(3), -a_strict, 0.0)
    p0 = (ri == ci).astype(F32) + nd
    q1 = _bmm(nd, nd)
    yield None
    t = _bmm(jnp.concatenate([p0, q1], axis=1), q1)
    yield None
    p1 = p0 + t[:, :n]
    inv = p1 + _bmm(p1, t[:, n:])
    yield None
    for shift in (4, 5, 6):
        e = jnp.where(jnp.logical_and(blk(shift), jnp.logical_not(blk(shift - 1))), a_strict, 0.0)
        ie = _bmm(inv, e)
        yield None
        inv = inv - _bmm(ie, inv)
        yield None
    yield inv


def _gdn_attn_kernel(nch, ctx_chunks, ctx_attn, gin_ref, small_ref, prm_ref, q_ref, kv_ref, o_ref, att_ref,
                     gc_s, rows_s, mq_s, n_s, cd_s, of_s, ob_s):
    c64 = GDN_CHUNK
    w4 = GDN_W
    nsc = nch // 2
    ri128 = _iota2((128, 128), 0)
    ci128 = _iota2((128, 128), 1)
    same = jnp.right_shift(ri128, 6) == jnp.right_shift(ci128, 6)
    lbd = jnp.logical_and(same, ri128 >= ci128).astype(BF16)
    ubd = jnp.logical_and(same, ri128 <= ci128).astype(BF16)
    lane = _iota2((1, 128), 1)
    fwd_lane = lane < SM_A + GDN_HEADS
    beta_lane = jnp.logical_and(lane >= SM_B, lane < SM_B + 2 * GDN_HEADS)

    a_row = -jnp.exp(prm_ref[0:1, :])
    bias_row = prm_ref[1:2, :]

    def prep(sc, carry):
        base = pl.multiple_of(sc * 128, 128)
        sm = small_ref[0, pl.ds(base, 128), :]
        g = a_row * _softplus(sm + bias_row)
        cum = jnp.where(fwd_lane, _mm_exact_lhs(lbd, g), _mm_exact_lhs(ubd, g))
        gc_s[pl.ds(base, 128), :] = cum
        slab_t = jnp.where(beta_lane, jax.nn.sigmoid(sm), cum).T
        for half in range(2):
            lo = half * c64
            table = []
            for first in (SM_A, SM_A + GDN_HEADS, SM_B, SM_B + GDN_HEADS):
                table.append(jnp.concatenate(
                    [slab_t[first + h:first + h + 1, lo:lo + c64] for h in range(GDN_HEADS)], axis=1))
            for d, edge in ((0, lo + c64 - 1), (1, lo)):
                first = SM_A + d * GDN_HEADS
                table.append(jnp.concatenate(
                    [jnp.broadcast_to(slab_t[first + h:first + h + 1, edge:edge + 1], (1, c64))
                     for h in range(GDN_HEADS)], axis=1))
            table.append(jnp.zeros((2, w4), F32))
            rows_s[pl.ds(pl.multiple_of(sc * 16 + half * 8, 8), 8), :] = jnp.concatenate(table, axis=0)
        return carry

    lax.fori_loop(0, nsc, prep, 0)

    ri = _iota2((c64, c64), 0)
    ci = _iota2((c64, c64), 1)
    incl = (ri >= ci, ri <= ci)
    strict = (ri > ci, ri < ci)

    nb = 4

    def local(cb):
        chunks = []
        for cc in range(nb):
            c = cb * nb + cc
            base = _aligned(c * c64, c64)
            gin = gin_ref[0, pl.ds(base, c64), :]
            gc = gc_s[pl.ds(base, c64), :]
            rows = rows_s[pl.ds(_aligned(c * 8, 8), 8), :]
            qs = [gin[:, h * HD:(h + 1) * HD] for h in range(GDN_HEADS)]
            ks = [gin[:, w4 + h * HD:w4 + (h + 1) * HD] for h in range(GDN_HEADS)]
            vs = [gin[:, 2 * w4 + h * HD:2 * w4 + (h + 1) * HD] for h in range(GDN_HEADS)]
            chunks.append((c, base, gc, rows, qs, ks, vs))
        kq = _bmm_nt(jnp.stack([jnp.concatenate([ch[5][h], ch[4][h]], axis=0)
                                for ch in chunks for h in range(GDN_HEADS)]),
                     jnp.stack([ch[5][h] for ch in chunks for h in range(GDN_HEADS)]))
        yield
        mb_l, rhs_l, lhs_l, qd_l = [], [], [], []
        for ci, (c, base, gc, rows, qs, ks, vs) in enumerate(chunks):
            kts = [k.astype(F32).T for k in ks]
            for d in range(2):
                for h in range(GDN_HEADS):
                    sl = slice(h * HD, (h + 1) * HD)
                    colg = SM_A + d * GDN_HEADS + h
                    kqh = kq[ci * GDN_HEADS + h]
                    gmat = jnp.broadcast_to(gc[:, colg:colg + 1], (c64, c64))
                    gcr, br, endr = rows[d:d + 1, sl], rows[2 + d:3 + d, sl], rows[4 + d:5 + d, sl]
                    dec = jnp.exp(jnp.where(incl[d], gmat - gcr, NEG))
                    mb_l.append(jnp.where(strict[d], kqh[:c64] * dec, 0.0) * br)
                    eg = jnp.exp(gmat)
                    kdtb = kts[h] * (jnp.exp(endr - gcr) * br)
                    lhs_l.append(jnp.concatenate([kdtb, kqh[c64:] * dec * br], axis=0))
                    rhs_l.append(jnp.concatenate([vs[h], ks[h] * eg], axis=1))
                    qd_l.append(qs[h] * eg)
        z = None
        for z in _unit_tri_inverse_stages(jnp.stack(mb_l)):
            if z is None:
                yield
        zr = _bmm(z, jnp.stack(rhs_l))
        yield
        a12 = _bmm(jnp.stack(lhs_l), zr)
        yield
        for ci, (c, base, gc, rows, qs, ks, vs) in enumerate(chunks):
            for d in range(2):
                slot = c * 2 + d
                ids = [ci * 2 * GDN_HEADS + d * GDN_HEADS + h for h in range(GDN_HEADS)]
                mq = [jnp.concatenate([-a12[n][:c64, HD:], qd_l[n] - a12[n][c64:, HD:]], axis=0) for n in ids]
                mq_s[slot] = jnp.concatenate(mq, axis=1).astype(BF16)
                n_s[slot] = jnp.concatenate([a12[n][:c64, :HD] for n in ids], axis=1)
                cd_s[slot] = jnp.broadcast_to(jnp.exp(rows[4 + d:5 + d]), (8, w4))
                oc = jnp.concatenate([a12[n][c64:, :HD] for n in ids], axis=1)
                if d == 0:
                    of_s[pl.ds(base, c64), :] = oc
                else:
                    ob_s[pl.ds(base, c64), :] = oc

    t_all = q_ref.shape[1]
    heads_per_kv = ATT_QH // ATT_KVH

    def attend(tile, nkeys):
        rows = pl.ds(_aligned(tile * TM, TM), TM)
        qt = q_ref[0, rows, :]
        for g in range(ATT_KVH):
            k = kv_ref[0, g, :nkeys, :HD]
            va = kv_ref[0, g, :nkeys, 2 * HD:]
            outs = []
            for jj in range(heads_per_kv):
                h = g * heads_per_kv + jj
                s = _mm_nt(qt[:, h * HD:(h + 1) * HD], k)
                yield
                p = jnp.exp2(s - jnp.max(s, axis=-1, keepdims=True))
                acc = _mm(p, va)
                yield
                outs.append(acc[:, :HD] / acc[:, HD:])
            outs.append(jnp.zeros((TM, HD), F32))
            att_ref[0, g, rows, :] = jnp.concatenate(outs, axis=1).astype(BF16)

    def trip(j, nkeys):
        streams = [local(j)] + ([attend(j, nkeys)] if nkeys else [])
        while streams:
            for st in list(streams):
                try:
                    next(st)
                except StopIteration:
                    streams.remove(st)

    trip(0, CTX if ctx_attn else 0)
    if not ctx_attn:
        att_ref[0, :, :TM, :] = jnp.zeros((ATT_KVH, TM, 256), BF16)

    def later_trip(j, carry):
        trip(j, t_all)
        return carry

    lax.fori_loop(1, nch // nb, later_trip, 0)

    def step(t, states):
        cf = t
        cb = jnp.where(t < ctx_chunks, ctx_chunks - 1 - t, nch - 1 + ctx_chunks - t)
        new_states = []
        for d, c in ((0, cf), (1, cb)):
            slot = c * 2 + d
            s = states[d]
            r = jnp.dot(mq_s[slot], _bd4(s), preferred_element_type=F32)
            new_states.append(s * cd_s[slot][0:1] + r[:c64] + n_s[slot])
            rows = pl.ds(pl.multiple_of(c * c64, c64), c64)
            if d == 0:
                of_s[rows, :] = of_s[rows, :] + r[c64:]
            else:
                ob_s[rows, :] = ob_s[rows, :] + r[c64:]
        return tuple(new_states)

    zero = jnp.zeros((c64, w4), F32)
    lax.fori_loop(0, nch, step, (zero, zero))
    o_ref[0] = (of_s[...] + ob_s[...]).astype(BF16)


def _gdn_attention(gdn_in, small, prm, q, kv, ctx_attn):
    bsz, t, _ = gdn_in.shape
    nch = t // GDN_CHUNK
    assert nch // 4 == t // TM
    return pl.pallas_call(
        functools.partial(_gdn_attn_kernel, nch, CTX // GDN_CHUNK, ctx_attn),
        out_shape=[jax.ShapeDtypeStruct((bsz, t, GDN_W), BF16),
                   jax.ShapeDtypeStruct((bsz, ATT_KVH, t, 256), BF16)],
        grid=(bsz,),
        in_specs=[pl.BlockSpec((1, t, 768), lambda b: (b, 0, 0)),
                  pl.BlockSpec((1, t, 128), lambda b: (b, 0, 0)),
                  pl.BlockSpec((8, 128), lambda b: (0, 0)),
                  pl.BlockSpec((1, t, ATT_W), lambda b: (b, 0, 0)),
                  pl.BlockSpec((1, ATT_KVH, t, 256), lambda b: (b, 0, 0, 0))],
        out_specs=[pl.BlockSpec((1, t, GDN_W), lambda b: (b, 0, 0)),
                   pl.BlockSpec((1, ATT_KVH, t, 256), lambda b: (b, 0, 0, 0))],
        scratch_shapes=[pltpu.VMEM((t, 128), F32),
                        pltpu.VMEM((nch * 8, GDN_W), F32),
                        pltpu.VMEM((nch * 2, 128, GDN_W), BF16),
                        pltpu.VMEM((nch * 2, GDN_CHUNK, GDN_W), F32),
                        pltpu.VMEM((nch * 2, 8, GDN_W), F32),
                        pltpu.VMEM((t, GDN_W), F32),
                        pltpu.VMEM((t, GDN_W), F32)],
        compiler_params=pltpu.CompilerParams(dimension_semantics=("arbitrary",),
                                             vmem_limit_bytes=VMEM_LIMIT),
        name="gdn_attention",
    )(gdn_in, small, prm, q, kv)


def _out_kernel(first_tile, xa_ref, x_ref, y_ref, o_ref, a_ref, zs_ref, mod_ref, sw_ref, gw_ref, w_ref, out_ref):
    zs = zs_ref[0].astype(F32)
    t = y_ref[0].astype(F32) * zs[:, :SSD_W]
    ssd = t * lax.rsqrt(jnp.mean(t * t, axis=-1, keepdims=True) + EPS) * sw_ref[...]
    o = o_ref[0].astype(F32)
    gdn = o * lax.rsqrt(_group_sum64(o * o) * (1.0 / HD) + EPS) * gw_ref[...] * zs[:, SSD_W:SSD_W + GDN_W]
    a = a_ref[0]
    att = jnp.concatenate([a[g][:, :HD * ATT_QH // ATT_KVH] for g in range(ATT_KVH)], axis=1)
    att = att * zs[:, SSD_W + GDN_W:]
    mix = jnp.concatenate([ssd, gdn, att], axis=1).astype(BF16)
    gate = mod_ref[0][:, 2 * D_MODEL:]
    x = jnp.where(pl.program_id(1) + first_tile == 0, xa_ref[0], x_ref[0])
    out_ref[0] = x + gate * jnp.dot(mix, w_ref[...], preferred_element_type=F32)


def _out_projection(xa, xb, off, y, o, a, zs, mod3, ssd_nw, gdn_nw, w_out, first_tile):
    bsz, t, _ = y.shape
    nt = t // TM - first_tile
    row = lambda b, i: (b, i + first_tile, 0)
    full = lambda b, i: (0, 0)
    return pl.pallas_call(
        functools.partial(_out_kernel, first_tile),
        out_shape=jax.ShapeDtypeStruct((bsz, nt * TM, D_MODEL), F32),
        grid=(bsz, nt),
        in_specs=[pl.BlockSpec((1, TM, D_MODEL), lambda b, i: (b, 0, 0)),
                  pl.BlockSpec((1, TM, D_MODEL), lambda b, i: (b, jnp.maximum(i + first_tile - off, 0), 0)),
                  pl.BlockSpec((1, TM, SSD_W), row),
                  pl.BlockSpec((1, TM, GDN_W), row),
                  pl.BlockSpec((1, ATT_KVH, TM, 256), lambda b, i: (b, 0, i + first_tile, 0)),
                  pl.BlockSpec((1, TM, D_MODEL), row),
                  pl.BlockSpec((1, 1, 3 * D_MODEL),
                               lambda b, i: (jnp.where(i + first_tile == 0, bsz, b), 0, 0)),
                  pl.BlockSpec((1, SSD_W), full),
                  pl.BlockSpec((1, GDN_W), full),
                  pl.BlockSpec((D_MODEL, D_MODEL), full)],
        out_specs=pl.BlockSpec((1, TM, D_MODEL), lambda b, i: (b, i, 0)),
        compiler_params=pltpu.CompilerParams(dimension_semantics=("arbitrary", "arbitrary"),
                                             vmem_limit_bytes=VMEM_LIMIT),
        name="out_projection",
    )(xa, xb, y, o, a, zs, mod3, ssd_nw, gdn_nw, w_out)


_CONV_SEGS = [(0, CONV_DIM)]
_REST_SEGS = [(1408, 384), (1804, 256), (2076, 384), (2460, 128), (2588, 128), (2716, 384),
              (1792, 12), (2060, 8), (2068, 8)]


def _take_segments(w, segs, axis):
    return jnp.concatenate([lax.slice_in_dim(w, a, a + n, axis=axis) for a, n in segs], axis=axis)


def _prepare_params(w_in, conv_w, conv_b, ssd_A_log, ssd_dt_bias, gdn_A_log, gdn_dt_bias):
    depth = w_in.shape[0]
    w_conv = _take_segments(w_in, _CONV_SEGS, 2).astype(BF16)
    pad = jnp.zeros(w_in.shape[:2] + (REST_DIM - sum(n for _, n in _REST_SEGS),), w_in.dtype)
    w_rest = jnp.concatenate([_take_segments(w_in, _REST_SEGS, 2), pad], axis=2).astype(BF16)
    cw = _take_segments(conv_w, _CONV_SEGS, 2)
    cb = _take_segments(conv_b, _CONV_SEGS, 1)[:, None, :]

    def lanes(a, b):
        v = jnp.concatenate([a.reshape(depth, -1), b.reshape(depth, -1)], axis=1)
        return jnp.pad(v, ((0, 0), (0, 128 - v.shape[1])))

    prm = jnp.stack([lanes(ssd_A_log, gdn_A_log), lanes(ssd_dt_bias, gdn_dt_bias)], axis=1)
    prm = jnp.pad(prm, ((0, 0), (0, 6), (0, 0)))
    return w_conv, w_rest, cw, cb, prm


def _rope_tables(t):
    f32 = np.float32
    pos = np.arange(t - CTX)
    n_freq = HD // 4
    freqs = np.power(f32(ROPE_THETA), -np.arange(n_freq, dtype=f32) / f32(n_freq)).astype(f32)
    ang_r = ((pos // GRID_W).astype(f32)[:, None] * freqs).astype(f32)
    ang_c = ((pos % GRID_W).astype(f32)[:, None] * freqs).astype(f32)
    cos = np.concatenate([np.cos(ang_r)] * 2 + [np.cos(ang_c)] * 2, axis=1)
    sin = np.concatenate([-np.sin(ang_r), np.sin(ang_r), -np.sin(ang_c), np.sin(ang_c)], axis=1)
    cos = np.concatenate([np.ones((CTX, HD), f32), cos], axis=0).astype(f32)
    sin = np.concatenate([np.zeros((CTX, HD), f32), sin], axis=0).astype(f32)
    return jnp.asarray(np.tile(cos, (1, 2))), jnp.asarray(np.tile(sin, (1, 2)))


def kernel(x, c, ctx, c_ctx, norm_w, w_mod, b_mod, w_in, conv_w, conv_b, ssd_A_log, ssd_dt_bias, ssd_D,
           ssd_norm_w, gdn_A_log, gdn_dt_bias, gdn_norm_w, q_norm_w, k_norm_w, w_out):
    bsz = x.shape[0]
    t = CTX + x.shape[1]
    c_all = jnp.concatenate([c, c_ctx[None, :], jnp.zeros((7, D_MODEL), F32)], axis=0)
    cos_t, sin_t = _rope_tables(t)
    w_conv, w_rest, cw, cb, prm = _prepare_params(w_in, conv_w, conv_b, ssd_A_log, ssd_dt_bias,
                                                  gdn_A_log, gdn_dt_bias)
    qk_w = jnp.concatenate([jnp.tile(q_norm_w, (1, ATT_QH)), jnp.tile(k_norm_w, (1, ATT_KVH))], axis=1)
    dskip = jnp.repeat(ssd_D, HD, axis=1)
    gdn_nw = jnp.tile(gdn_norm_w, (1, GDN_HEADS))
    w_out16 = w_out.astype(BF16)

    xa, xb, off = ctx, x, 1
    for layer in range(DEPTH):
        first_tile = 1 if layer == DEPTH - 1 else 0
        mod3 = _modulation(c_all, w_mod[layer], b_mod[layer]).reshape(bsz + 8, 1, 3 * D_MODEL)
        ssd_in, gdn_in, zs, q, kv, small = _in_projection(
            xa, xb, off, mod3, norm_w[layer][None], w_conv[layer], w_rest[layer], cw[layer], cb[layer],
            qk_w[layer][None], cos_t, sin_t)
        y = _ssd(ssd_in, small, prm[layer], dskip[layer][None])
        o, a = _gdn_attention(gdn_in, small, prm[layer], q, kv, first_tile == 0)
        out = _out_projection(xa, xb, off, y, o, a, zs, mod3, ssd_norm_w[layer][None], gdn_nw[layer][None],
                              w_out16[layer], first_tile)
        xa, xb, off = out, out, 0
    return out
```

```python
import functools
import math

import jax
import jax.numpy as jnp
import numpy as np
from jax import lax
from jax.experimental import pallas as pl
from jax.experimental.pallas import tpu as pltpu

F32 = jnp.float32
BF16 = jnp.bfloat16

D_MODEL = 1024
CTX = 256
GRID_W = 64
EPS = 1e-6
DEPTH = 2

HD = 64
SSD_HEADS = 6
SSD_W = SSD_HEADS * HD
SSD_CHUNK = 128
GDN_HEADS = 4
GDN_W = GDN_HEADS * HD
GDN_CHUNK = 64
ATT_QH = 6
ATT_KVH = 2
ATT_W = ATT_QH * HD
ROPE_THETA = 10000.0
Q_PRESCALE = (HD ** -0.5) * math.log2(math.e)

CONV_DIM = 1408
REST_DIM = 1792
TM = 256
HALO = 8
NEG = -1e30
VMEM_LIMIT = 56 * 1024 * 1024

SM_DT = 0
SM_A = 12
SM_B = 20


def _mm(a, b):
    return jnp.dot(a.astype(BF16), b.astype(BF16), preferred_element_type=F32)


def _mm_nt(a, b):
    return lax.dot_general(a.astype(BF16), b.astype(BF16), (((1,), (1,)), ((), ())),
                           preferred_element_type=F32)


def _split3(a):
    h = a.astype(BF16)
    r = a - h.astype(F32)
    m = r.astype(BF16)
    l = (r - m.astype(F32)).astype(BF16)
    return h, m, l


def _mm_exact_lhs(a_bf16, b):
    h, m, l = _split3(b)
    d = functools.partial(jnp.dot, preferred_element_type=F32)
    return d(a_bf16, h) + (d(a_bf16, m) + d(a_bf16, l))


def _silu(x):
    return x * jax.nn.sigmoid(x)


def _softplus(x):
    return jnp.maximum(x, 0.0) + jnp.log1p(jnp.exp(-jnp.abs(x)))


def _iota2(shape, dim):
    return lax.broadcasted_iota(jnp.int32, shape, dim)


def _aligned(x, m):
    return x if isinstance(x, int) else pl.multiple_of(x, m)


def _group_sum64(xx):
    r = jnp.right_shift(_iota2((128, 128), 0), 6)
    c = jnp.right_shift(_iota2((128, 128), 1), 6)
    g = (r == c).astype(BF16)
    hi = xx.astype(BF16)
    lo = (xx - hi.astype(F32)).astype(BF16)
    outs = []
    for t in range(xx.shape[1] // 128):
        sl = slice(t * 128, (t + 1) * 128)
        outs.append(jnp.dot(hi[:, sl], g, preferred_element_type=F32)
                    + jnp.dot(lo[:, sl], g, preferred_element_type=F32))
    return outs[0] if len(outs) == 1 else jnp.concatenate(outs, axis=1)


def _mod_kernel(c_ref, w_ref, b_ref, o_ref):
    o_ref[...] = _mm(_silu(c_ref[...]), w_ref[0]) + b_ref[0]


def _modulation(c_all, w_mod, b_mod, layer):
    n = w_mod.shape[2]
    bn = 768
    rows = c_all.shape[0]
    return pl.pallas_call(
        _mod_kernel,
        out_shape=jax.ShapeDtypeStruct((rows, n), F32),
        grid=(n // bn,),
        in_specs=[pl.BlockSpec((rows, D_MODEL), lambda j: (0, 0)),
                  pl.BlockSpec((1, D_MODEL, bn), lambda j: (layer, 0, j)),
                  pl.BlockSpec((1, 1, bn), lambda j: (layer, 0, j))],
        out_specs=pl.BlockSpec((rows, bn), lambda j: (0, j)),
        compiler_params=pltpu.CompilerParams(dimension_semantics=("arbitrary",),
                                             vmem_limit_bytes=VMEM_LIMIT),
        name="modulation",
    )(c_all, w_mod, b_mod.reshape(b_mod.shape[0], 1, n))


def _proj_kernel(nt, xa_ref, xm_ref, xp_ref, xn_ref, mod_ref, nw_ref, wc_ref, wr_ref, cw_ref, cb_ref,
                 qkw_ref, cos_ref, sin_ref,
                 ssd_ref, gdn_ref, zs_ref, q_ref, kv_ref, small_ref):
    i = pl.program_id(1)
    m = mod_ref[0]
    gain = nw_ref[0] * (1.0 + m[:, D_MODEL:2 * D_MODEL])
    shift = m[:, :D_MODEL]

    xm = jnp.where(i == 0, xa_ref[0], xm_ref[0])
    xe = jnp.concatenate([xm, xp_ref[0], xn_ref[0]], axis=0)
    ms = jnp.mean(xe * xe, axis=-1, keepdims=True)
    he = (xe * lax.rsqrt(ms + EPS) * gain + shift).astype(BF16)

    p = jnp.dot(he, wc_ref[0], preferred_element_type=F32)
    r = jnp.dot(he[:TM], wr_ref[0], preferred_element_type=F32)

    pm = p[:TM]
    seg_first = i <= 1
    seg_last = jnp.logical_or(i == 0, i == nt - 1)
    prev_row = jnp.where(seg_first, 0.0, p[TM + HALO - 1:TM + HALO])
    next_row = jnp.where(seg_last, 0.0, p[TM + HALO:TM + HALO + 1])
    rows = _iota2((TM, 1), 0)
    pm1 = jnp.where(rows == 0, prev_row, pltpu.roll(pm, 1, axis=0))
    pp1 = jnp.where(rows == TM - 1, next_row, pltpu.roll(pm, TM - 1, axis=0))
    cw = cw_ref[0]
    conv = cw[0:1] * pm1 + cw[1:2] * pm + cw[2:3] * pp1 + cb_ref[0]
    co = _silu(conv)

    ssd_ref[0] = co[:, :640].astype(BF16)
    gqk = co[:, 640:1152]
    gqk = gqk * lax.rsqrt(_group_sum64(gqk * gqk) + EPS)
    gdn_ref[0, :, :256] = (gqk[:, :256] * (HD ** -0.5)).astype(BF16)
    gdn_ref[0, :, 256:512] = gqk[:, 256:].astype(BF16)
    gdn_ref[0, :, 512:] = co[:, 1152:1408].astype(BF16)

    z1 = r[:, :640]
    z2 = r[:, 1280:1664]
    zs_ref[0, :, :640] = _silu(z1).astype(BF16)
    zs_ref[0, :, 640:] = _silu(z2).astype(BF16)

    aqk = r[:, 640:1152]
    aqk = aqk * lax.rsqrt(_group_sum64(aqk * aqk) * (1.0 / HD) + EPS) * qkw_ref[0]
    lane = _iota2((1, 512), 1)
    swapped = jnp.where(jnp.bitwise_and(lane, 31) < 16,
                        pltpu.roll(aqk, 512 - 16, axis=1), pltpu.roll(aqk, 16, axis=1))
    cos = jnp.concatenate([cos_ref[...]] * 4, axis=1)
    sin = jnp.concatenate([sin_ref[...]] * 4, axis=1)
    aqk = aqk * cos + swapped * sin
    q_ref[0] = (aqk[:, :384] * Q_PRESCALE).astype(BF16)
    ones = jnp.ones((TM, HD), F32)
    zeros = jnp.zeros((TM, HD), F32)
    for g in range(ATT_KVH):
        kv_ref[0, g] = jnp.concatenate([aqk[:, 384 + g * HD:384 + (g + 1) * HD], zeros,
                                        r[:, 1152 + g * HD:1152 + (g + 1) * HD], ones], axis=1).astype(BF16)
    small_ref[0] = r[:, 1664:1792]


def _in_projection(xa, xb, off, layer, mod3, norm_w, w_conv, w_rest, conv_w, conv_b, qk_w, cos_t, sin_t):
    bsz = xb.shape[0]
    nt = xb.shape[1] // TM + off
    t = nt * TM
    nblk8 = xb.shape[1] // HALO
    per_tile = TM // HALO
    row = lambda b, i: (b, i, 0)
    lay = lambda b, i: (layer, 0, 0)
    outs = [jax.ShapeDtypeStruct((bsz, t, w), dt)
            for w, dt in ((640, BF16), (768, BF16), (1024, BF16), (384, BF16), (128, F32))]
    outs.insert(4, jax.ShapeDtypeStruct((bsz, ATT_KVH, t, 256), BF16))
    return pl.pallas_call(
        functools.partial(_proj_kernel, nt),
        out_shape=outs,
        grid=(bsz, nt),
        in_specs=[
            pl.BlockSpec((1, TM, D_MODEL), lambda b, i: (b, 0, 0)),
            pl.BlockSpec((1, TM, D_MODEL), lambda b, i: (b, jnp.maximum(i - off, 0), 0)),
            pl.BlockSpec((1, HALO, D_MODEL), lambda b, i: (b, jnp.maximum((i - off) * per_tile - 1, 0), 0)),
            pl.BlockSpec((1, HALO, D_MODEL),
                         lambda b, i: (b, jnp.minimum((i - off + 1) * per_tile, nblk8 - 1), 0)),
            pl.BlockSpec((1, 1, 3 * D_MODEL), lambda b, i: (jnp.where(i == 0, bsz, b), 0, 0)),
            pl.BlockSpec((1, 1, D_MODEL), lay),
            pl.BlockSpec((1, D_MODEL, CONV_DIM), lay),
            pl.BlockSpec((1, D_MODEL, REST_DIM), lay),
            pl.BlockSpec((1, 3, CONV_DIM), lay),
            pl.BlockSpec((1, 1, CONV_DIM), lay),
            pl.BlockSpec((1, 1, 512), lay),
            pl.BlockSpec((TM, 128), lambda b, i: (i, 0)),
            pl.BlockSpec((TM, 128), lambda b, i: (i, 0)),
        ],
        out_specs=[pl.BlockSpec((1, TM, 640), row), pl.BlockSpec((1, TM, 768), row),
                   pl.BlockSpec((1, TM, 1024), row), pl.BlockSpec((1, TM, 384), row),
                   pl.BlockSpec((1, ATT_KVH, TM, 256), lambda b, i: (b, 0, i, 0)),
                   pl.BlockSpec((1, TM, 128), row)],
        compiler_params=pltpu.CompilerParams(dimension_semantics=("arbitrary", "arbitrary"),
                                             vmem_limit_bytes=VMEM_LIMIT),
        name="in_projection",
    )(xa, xb, xb, xb, mod3, norm_w, w_conv, w_rest, conv_w, conv_b, qk_w, cos_t, sin_t)


def _ssd_kernel(nc, ctx_chunks, xin_ref, small_ref, prm_ref, dskip_ref, y_ref,
                cum_s, cumt_s, dtt_s, st_s, sin_s):
    q = SSD_CHUNK
    hpg = SSD_HEADS // 2
    ri = _iota2((q, q), 0)
    ci = _iota2((q, q), 1)
    ltri = (ri >= ci).astype(BF16)
    utri = (ri <= ci).astype(BF16)
    lane = _iota2((1, 128), 1)
    fwd_lane = lane < SSD_HEADS

    a_row = -jnp.exp(prm_ref[0, 0:1, :])
    bias_row = prm_ref[0, 1:2, :]

    def prep(c, carry):
        base = pl.multiple_of(c * q, q)
        dt = _softplus(small_ref[0, pl.ds(base, q), :] + bias_row)
        dta = dt * a_row
        cum = jnp.where(fwd_lane, _mm_exact_lhs(ltri, dta), _mm_exact_lhs(utri, dta))
        cum_s[pl.ds(base, q), :] = cum
        cumt_s[pl.ds(base, q), :] = cum.T
        dtt_s[pl.ds(base, q), :] = dt.T
        return carry

    lax.fori_loop(0, nc, prep, 0)

    def end_of(cumt, d, col):
        return cumt[col:col + 1, q - 1:q] if d == 0 else cumt[col:col + 1, 0:1]

    def states(c, carry):
        base = pl.multiple_of(c * q, q)
        xin = xin_ref[0, pl.ds(base, q), :]
        cumt = cumt_s[pl.ds(base, q), :]
        dtt = dtt_s[pl.ds(base, q), :]
        lhs, rhs = [], []
        for g in range(2):
            bgt = xin[:, 384 + g * HD:384 + (g + 1) * HD].astype(F32).T
            for d in range(2):
                for hg in range(hpg):
                    h = g * hpg + hg
                    col = d * SSD_HEADS + h
                    w_row = jnp.exp(end_of(cumt, d, col) - cumt[col:col + 1, :]) * dtt[col:col + 1, :]
                    lhs.append(bgt * w_row)
                    rhs.append(xin[:, h * HD:(h + 1) * HD])
        st = _bmm(jnp.stack(lhs), jnp.stack(rhs))
        st_s[pl.ds(c * 12, 12)] = st
        return carry

    lax.fori_loop(0, nc, states, 0)

    orders = (list(range(nc)),
              list(range(ctx_chunks - 1, -1, -1)) + list(range(nc - 1, ctx_chunks - 1, -1)))
    for g in range(2):
        for d in range(2):
            for hg in range(hpg):
                col = d * SSD_HEADS + g * hpg + hg
                s = jnp.zeros((HD, HD), F32)
                for c in orders[d]:
                    slot = c * 12 + g * 6 + d * 3 + hg
                    sin_s[slot] = s
                    row = c * q + (q - 1 if d == 0 else 0)
                    s = s * jnp.exp(cum_s[row:row + 1, col:col + 1]) + st_s[slot]

    lower = ri >= ci
    upper = ri <= ci
    dsk = dskip_ref[0]

    def outputs(c, carry):
        base = pl.multiple_of(c * q, q)
        xin = xin_ref[0, pl.ds(base, q), :]
        cum = cum_s[pl.ds(base, q), :]
        cumt = cumt_s[pl.ds(base, q), :]
        dtt = dtt_s[pl.ds(base, q), :]
        sin = sin_s[pl.ds(c * 12, 12)]
        ws, xs, cs, es = [], [], [], []
        for g in range(2):
            bg = xin[:, 384 + g * HD:384 + (g + 1) * HD]
            cg = xin[:, 512 + g * HD:512 + (g + 1) * HD]
            sc = _mm_nt(cg, bg)
            for hg in range(hpg):
                h = g * hpg + hg
                w = None
                for d in range(2):
                    col = d * SSD_HEADS + h
                    gmat = jnp.broadcast_to(cum[:, col:col + 1], (q, q))
                    lm = jnp.exp(jnp.where(lower if d == 0 else upper, gmat - cumt[col:col + 1, :], NEG))
                    wd = lm * dtt[col:col + 1, :]
                    w = wd if w is None else w + wd
                    es.append((g * 6 + d * 3 + hg, jnp.exp(gmat[:, :HD])))
                ws.append(w * sc)
                xs.append(xin[:, h * HD:(h + 1) * HD])
            cs.append(cg)
        ydiag = _bmm(jnp.stack(ws), jnp.stack(xs))
        es.sort(key=lambda t: t[0])
        cstack = jnp.stack([cs[i // 6] for i in range(12)])
        yoff = _bmm(cstack, sin)
        outs = []
        for h in range(SSD_HEADS):
            g, hg = divmod(h, hpg)
            y = ydiag[h] + dsk[:, h * HD:(h + 1) * HD] * xs[h]
            for d in range(2):
                i = g * 6 + d * 3 + hg
                y = y + yoff[i] * es[i][1]
            outs.append(y)
        y_ref[0, pl.ds(base, q), :] = jnp.concatenate(outs, axis=1).astype(BF16)
        return carry

    lax.fori_loop(0, nc, outputs, 0)


def _ssd(ssd_in, small, prm, dskip, layer):
    bsz, t, _ = ssd_in.shape
    nc = t // SSD_CHUNK
    return pl.pallas_call(
        functools.partial(_ssd_kernel, nc, CTX // SSD_CHUNK),
        out_shape=jax.ShapeDtypeStruct((bsz, t, SSD_W), BF16),
        grid=(bsz,),
        in_specs=[pl.BlockSpec((1, t, 640), lambda b: (b, 0, 0)),
                  pl.BlockSpec((1, t, 128), lambda b: (b, 0, 0)),
                  pl.BlockSpec((1, 8, 128), lambda b: (layer, 0, 0)),
                  pl.BlockSpec((1, 1, SSD_W), lambda b: (layer, 0, 0))],
        out_specs=pl.BlockSpec((1, t, SSD_W), lambda b: (b, 0, 0)),
        scratch_shapes=[pltpu.VMEM((t, 128), F32), pltpu.VMEM((t, 128), F32), pltpu.VMEM((t, 128), F32),
                        pltpu.VMEM((nc * 12, HD, HD), F32),
                        pltpu.VMEM((nc * 12, HD, HD), F32)],
        compiler_params=pltpu.CompilerParams(dimension_semantics=("arbitrary",),
                                             vmem_limit_bytes=VMEM_LIMIT),
        name="ssd",
    )(ssd_in, small, prm, dskip)


def _bd4(x):
    xb = x.astype(BF16)
    blk = jnp.right_shift(_iota2((1, 4 * HD), 1), 6)
    zero = jnp.zeros_like(xb)
    return jnp.concatenate([jnp.where(blk == h, xb, zero) for h in range(GDN_HEADS)], axis=0)


def _bmm(a, b):
    return lax.dot_general(a.astype(BF16), b.astype(BF16), (((2,), (1,)), ((0,), (0,))),
                           preferred_element_type=F32)


def _bmm_nt(a, b):
    return lax.dot_general(a.astype(BF16), b.astype(BF16), (((2,), (2,)), ((0,), (0,))),
                           preferred_element_type=F32)


def _unit_tri_inverse_stages(a_strict):
    n = a_strict.shape[-1]
    ri = _iota2((1, n, n), 1)
    ci = _iota2((1, n, n), 2)

    def blk(shift):
        return jnp.right_shift(ri, shift) == jnp.right_shift(ci, shift)

    nd = jnp.where(blk(3), -a_strict, 0.0)
    p0 = (ri == ci).astype(F32) + nd
    q1 = _bmm(nd, nd)
    yield None
    t = _bmm(jnp.concatenate([p0, q1], axis=1), q1)
    yield None
    p1 = p0 + t[:, :n]
    inv = p1 + _bmm(p1, t[:, n:])
    yield None
    for shift in (4, 5, 6):
        e = jnp.where(jnp.logical_and(blk(shift), jnp.logical_not(blk(shift - 1))), a_strict, 0.0)
        ie = _bmm(inv, e)
        yield None
        inv = inv - _bmm(ie, inv)
        yield None
    yield inv


def _gdn_attn_kernel(nch, ctx_chunks, ctx_attn, gin_ref, small_ref, prm_ref, q_ref, kv_ref, o_ref, att_ref,
                     gc_s, rows_s, mq_s, n_s, cd_s, of_s, ob_s):
    c64 = GDN_CHUNK
    w4 = GDN_W
    nsc = nch // 2
    ri128 = _iota2((128, 128), 0)
    ci128 = _iota2((128, 128), 1)
    same = jnp.right_shift(ri128, 6) == jnp.right_shift(ci128, 6)
    lbd = jnp.logical_and(same, ri128 >= ci128).astype(BF16)
    ubd = jnp.logical_and(same, ri128 <= ci128).astype(BF16)
    lane = _iota2((1, 128), 1)
    fwd_lane = lane < SM_A + GDN_HEADS
    beta_lane = jnp.logical_and(lane >= SM_B, lane < SM_B + 2 * GDN_HEADS)

    a_row = -jnp.exp(prm_ref[0, 0:1, :])
    bias_row = prm_ref[0, 1:2, :]

    def prep(sc, carry):
        base = pl.multiple_of(sc * 128, 128)
        sm = small_ref[0, pl.ds(base, 128), :]
        g = a_row * _softplus(sm + bias_row)
        cum = jnp.where(fwd_lane, _mm_exact_lhs(lbd, g), _mm_exact_lhs(ubd, g))
        gc_s[pl.ds(base, 128), :] = cum
        slab_t = jnp.where(beta_lane, jax.nn.sigmoid(sm), cum).T
        for half in range(2):
            lo = half * c64
            table = []
            for first in (SM_A, SM_A + GDN_HEADS, SM_B, SM_B + GDN_HEADS):
                table.append(jnp.concatenate(
                    [slab_t[first + h:first + h + 1, lo:lo + c64] for h in range(GDN_HEADS)], axis=1))
            for d, edge in ((0, lo + c64 - 1), (1, lo)):
                first = SM_A + d * GDN_HEADS
                table.append(jnp.concatenate(
                    [jnp.broadcast_to(slab_t[first + h:first + h + 1, edge:edge + 1], (1, c64))
                     for h in range(GDN_HEADS)], axis=1))
            table.append(jnp.zeros((2, w4), F32))
            rows_s[pl.ds(pl.multiple_of(sc * 16 + half * 8, 8), 8), :] = jnp.concatenate(table, axis=0)
        return carry

    lax.fori_loop(0, nsc, prep, 0)

    ri = _iota2((c64, c64), 0)
    ci = _iota2((c64, c64), 1)
    incl = (ri >= ci, ri <= ci)
    strict = (ri > ci, ri < ci)

    nb = 4

    def local(cb):
        chunks = []
        for cc in range(nb):
            c = cb * nb + cc
            base = _aligned(c * c64, c64)
            gin = gin_ref[0, pl.ds(base, c64), :]
            gc = gc_s[pl.ds(base, c64), :]
            rows = rows_s[pl.ds(_aligned(c * 8, 8), 8), :]
            qs = [gin[:, h * HD:(h + 1) * HD] for h in range(GDN_HEADS)]
            ks = [gin[:, w4 + h * HD:w4 + (h + 1) * HD] for h in range(GDN_HEADS)]
            vs = [gin[:, 2 * w4 + h * HD:2 * w4 + (h + 1) * HD] for h in range(GDN_HEADS)]
            chunks.append((c, base, gc, rows, qs, ks, vs))
        kq = _bmm_nt(jnp.stack([jnp.concatenate([ch[5][h], ch[4][h]], axis=0)
                                for ch in chunks for h in range(GDN_HEADS)]),
                     jnp.stack([ch[5][h] for ch in chunks for h in range(GDN_HEADS)]))
        yield
        mb_l, rhs_l, lhs_l, qd_l = [], [], [], []
        for ci, (c, base, gc, rows, qs, ks, vs) in enumerate(chunks):
            kts = [k.astype(F32).T for k in ks]
            for d in range(2):
                for h in range(GDN_HEADS):
                    sl = slice(h * HD, (h + 1) * HD)
                    colg = SM_A + d * GDN_HEADS + h
                    kqh = kq[ci * GDN_HEADS + h]
                    gmat = jnp.broadcast_to(gc[:, colg:colg + 1], (c64, c64))
                    gcr, br, endr = rows[d:d + 1, sl], rows[2 + d:3 + d, sl], rows[4 + d:5 + d, sl]
                    dec = jnp.exp(jnp.where(incl[d], gmat - gcr, NEG))
                    mb_l.append(jnp.where(strict[d], kqh[:c64] * dec, 0.0) * br)
                    eg = jnp.exp(gmat)
                    kdtb = kts[h] * (jnp.exp(endr - gcr) * br)
                    lhs_l.append(jnp.concatenate([kdtb, kqh[c64:] * dec * br], axis=0))
                    rhs_l.append(jnp.concatenate([vs[h], ks[h] * eg], axis=1))
                    qd_l.append(qs[h] * eg)
        z = None
        for z in _unit_tri_inverse_stages(jnp.stack(mb_l)):
            if z is None:
                yield
        zr = _bmm(z, jnp.stack(rhs_l))
        yield
        a12 = _bmm(jnp.stack(lhs_l), zr)
        yield
        for ci, (c, base, gc, rows, qs, ks, vs) in enumerate(chunks):
            for d in range(2):
                slot = c * 2 + d
                ids = [ci * 2 * GDN_HEADS + d * GDN_HEADS + h for h in range(GDN_HEADS)]
                mq = [jnp.concatenate([-a12[n][:c64, HD:], qd_l[n] - a12[n][c64:, HD:]], axis=0) for n in ids]
                mq_s[slot] = jnp.concatenate(mq, axis=1).astype(BF16)
                n_s[slot] = jnp.concatenate([a12[n][:c64, :HD] for n in ids], axis=1)
                cd_s[slot] = jnp.broadcast_to(jnp.exp(rows[4 + d:5 + d]), (8, w4))
                oc = jnp.concatenate([a12[n][c64:, :HD] for n in ids], axis=1)
                if d == 0:
                    of_s[pl.ds(base, c64), :] = oc
                else:
                    ob_s[pl.ds(base, c64), :] = oc

    t_all = q_ref.shape[1]
    heads_per_kv = ATT_QH // ATT_KVH

    def attend(tile, nkeys):
        rows = pl.ds(_aligned(tile * TM, TM), TM)
        qt = q_ref[0, rows, :]
        for g in range(ATT_KVH):
            k = kv_ref[0, g, :nkeys, :HD]
            va = kv_ref[0, g, :nkeys, 2 * HD:]
            outs = []
            for jj in range(heads_per_kv):
                h = g * heads_per_kv + jj
                s = _mm_nt(qt[:, h * HD:(h + 1) * HD], k)
                yield
                p = jnp.exp2(s - jnp.max(s, axis=-1, keepdims=True))
                acc = _mm(p, va)
                yield
                outs.append(acc[:, :HD] / acc[:, HD:])
            outs.append(jnp.zeros((TM, HD), F32))
            att_ref[0, g, rows, :] = jnp.concatenate(outs, axis=1).astype(BF16)

    def trip(j, nkeys):
        streams = [local(j)] + ([attend(j, nkeys)] if nkeys else [])
        while streams:
            for st in list(streams):
                try:
                    next(st)
                except StopIteration:
                    streams.remove(st)

    trip(0, CTX if ctx_attn else 0)
    if not ctx_attn:
        att_ref[0, :, :TM, :] = jnp.zeros((ATT_KVH, TM, 256), BF16)

    def later_trip(j, carry):
        trip(j, t_all)
        return carry

    lax.fori_loop(1, nch // nb, later_trip, 0)

    def step(t, states):
        cf = t
        cb = jnp.where(t < ctx_chunks, ctx_chunks - 1 - t, nch - 1 + ctx_chunks - t)
        new_states = []
        for d, c in ((0, cf), (1, cb)):
            slot = c * 2 + d
            s = states[d]
            r = jnp.dot(mq_s[slot], _bd4(s), preferred_element_type=F32)
            new_states.append(s * cd_s[slot][0:1] + r[:c64] + n_s[slot])
            rows = pl.ds(pl.multiple_of(c * c64, c64), c64)
            if d == 0:
                of_s[rows, :] = of_s[rows, :] + r[c64:]
            else:
                ob_s[rows, :] = ob_s[rows, :] + r[c64:]
        return tuple(new_states)

    zero = jnp.zeros((c64, w4), F32)
    lax.fori_loop(0, nch, step, (zero, zero))
    o_ref[0] = (of_s[...] + ob_s[...]).astype(BF16)


def _gdn_attention(gdn_in, small, prm, q, kv, layer, ctx_attn):
    bsz, t, _ = gdn_in.shape
    nch = t // GDN_CHUNK
    assert nch // 4 == t // TM
    return pl.pallas_call(
        functools.partial(_gdn_attn_kernel, nch, CTX // GDN_CHUNK, ctx_attn),
        out_shape=[jax.ShapeDtypeStruct((bsz, t, GDN_W), BF16),
                   jax.ShapeDtypeStruct((bsz, ATT_KVH, t, 256), BF16)],
        grid=(bsz,),
        in_specs=[pl.BlockSpec((1, t, 768), lambda b: (b, 0, 0)),
                  pl.BlockSpec((1, t, 128), lambda b: (b, 0, 0)),
                  pl.BlockSpec((1, 8, 128), lambda b: (layer, 0, 0)),
                  pl.BlockSpec((1, t, ATT_W), lambda b: (b, 0, 0)),
                  pl.BlockSpec((1, ATT_KVH, t, 256), lambda b: (b, 0, 0, 0))],
        out_specs=[pl.BlockSpec((1, t, GDN_W), lambda b: (b, 0, 0)),
                   pl.BlockSpec((1, ATT_KVH, t, 256), lambda b: (b, 0, 0, 0))],
        scratch_shapes=[pltpu.VMEM((t, 128), F32),
                        pltpu.VMEM((nch * 8, GDN_W), F32),
                        pltpu.VMEM((nch * 2, 128, GDN_W), BF16),
                        pltpu.VMEM((nch * 2, GDN_CHUNK, GDN_W), F32),
                        pltpu.VMEM((nch * 2, 8, GDN_W), F32),
                        pltpu.VMEM((t, GDN_W), F32),
                        pltpu.VMEM((t, GDN_W), F32)],
        compiler_params=pltpu.CompilerParams(dimension_semantics=("arbitrary",),
                                             vmem_limit_bytes=VMEM_LIMIT),
        name="gdn_attention",
    )(gdn_in, small, prm, q, kv)


def _out_kernel(first_tile, xa_ref, x_ref, y_ref, o_ref, a_ref, zs_ref, mod_ref, sw_ref, gw_ref, w_ref, out_ref):
    zs = zs_ref[0].astype(F32)
    t = y_ref[0].astype(F32) * zs[:, :SSD_W]
    ssd = t * lax.rsqrt(jnp.mean(t * t, axis=-1, keepdims=True) + EPS) * sw_ref[0]
    o = o_ref[0].astype(F32)
    gdn = o * lax.rsqrt(_group_sum64(o * o) * (1.0 / HD) + EPS) * gw_ref[0] * zs[:, SSD_W:SSD_W + GDN_W]
    a = a_ref[0]
    att = jnp.concatenate([a[g][:, :HD * ATT_QH // ATT_KVH] for g in range(ATT_KVH)], axis=1)
    att = att * zs[:, SSD_W + GDN_W:]
    mix = jnp.concatenate([ssd, gdn, att], axis=1).astype(BF16)
    gate = mod_ref[0][:, 2 * D_MODEL:]
    x = jnp.where(pl.program_id(1) + first_tile == 0, xa_ref[0], x_ref[0])
    out_ref[0] = x + gate * jnp.dot(mix, w_ref[0], preferred_element_type=F32)


def _out_projection(xa, xb, off, layer, y, o, a, zs, mod3, ssd_nw, gdn_nw, w_out, first_tile):
    bsz, t, _ = y.shape
    nt = t // TM - first_tile
    row = lambda b, i: (b, i + first_tile, 0)
    lay = lambda b, i: (layer, 0, 0)
    return pl.pallas_call(
        functools.partial(_out_kernel, first_tile),
        out_shape=jax.ShapeDtypeStruct((bsz, nt * TM, D_MODEL), F32),
        grid=(bsz, nt),
        in_specs=[pl.BlockSpec((1, TM, D_MODEL), lambda b, i: (b, 0, 0)),
                  pl.BlockSpec((1, TM, D_MODEL), lambda b, i: (b, jnp.maximum(i + first_tile - off, 0), 0)),
                  pl.BlockSpec((1, TM, SSD_W), row),
                  pl.BlockSpec((1, TM, GDN_W), row),
                  pl.BlockSpec((1, ATT_KVH, TM, 256), lambda b, i: (b, 0, i + first_tile, 0)),
                  pl.BlockSpec((1, TM, D_MODEL), row),
                  pl.BlockSpec((1, 1, 3 * D_MODEL),
                               lambda b, i: (jnp.where(i + first_tile == 0, bsz, b), 0, 0)),
                  pl.BlockSpec((1, 1, SSD_W), lay),
                  pl.BlockSpec((1, 1, GDN_W), lay),
                  pl.BlockSpec((1, D_MODEL, D_MODEL), lay)],
        out_specs=pl.BlockSpec((1, TM, D_MODEL), lambda b, i: (b, i, 0)),
        compiler_params=pltpu.CompilerParams(dimension_semantics=("arbitrary", "arbitrary"),
                                             vmem_limit_bytes=VMEM_LIMIT),
        name="out_projection",
    )(xa, xb, y, o, a, zs, mod3, ssd_nw, gdn_nw, w_out)


_CONV_SEGS = [(0, CONV_DIM)]
_REST_SEGS = [(1408, 384), (1804, 256), (2076, 384), (2460, 128), (2588, 128), (2716, 384),
              (1792, 12), (2060, 8), (2068, 8)]


def _take_segments(w, segs, axis):
    return jnp.concatenate([lax.slice_in_dim(w, a, a + n, axis=axis) for a, n in segs], axis=axis)


def _prepare_params(w_in, conv_w, conv_b, ssd_A_log, ssd_dt_bias, gdn_A_log, gdn_dt_bias):
    depth = w_in.shape[0]
    w_conv = _take_segments(w_in, _CONV_SEGS, 2).astype(BF16)
    pad = jnp.zeros(w_in.shape[:2] + (REST_DIM - sum(n for _, n in _REST_SEGS),), w_in.dtype)
    w_rest = jnp.concatenate([_take_segments(w_in, _REST_SEGS, 2), pad], axis=2).astype(BF16)
    cw = _take_segments(conv_w, _CONV_SEGS, 2)
    cb = _take_segments(conv_b, _CONV_SEGS, 1)[:, None, :]

    def lanes(a, b):
        v = jnp.concatenate([a.reshape(depth, -1), b.reshape(depth, -1)], axis=1)
        return jnp.pad(v, ((0, 0), (0, 128 - v.shape[1])))

    prm = jnp.stack([lanes(ssd_A_log, gdn_A_log), lanes(ssd_dt_bias, gdn_dt_bias)], axis=1)
    prm = jnp.pad(prm, ((0, 0), (0, 6), (0, 0)))
    return w_conv, w_rest, cw, cb, prm


def _rope_tables(t):
    f32 = np.float32
    pos = np.arange(t - CTX)
    n_freq = HD // 4
    freqs = np.power(f32(ROPE_THETA), -np.arange(n_freq, dtype=f32) / f32(n_freq)).astype(f32)
    ang_r = ((pos // GRID_W).astype(f32)[:, None] * freqs).astype(f32)
    ang_c = ((pos % GRID_W).astype(f32)[:, None] * freqs).astype(f32)
    cos = np.concatenate([np.cos(ang_r)] * 2 + [np.cos(ang_c)] * 2, axis=1)
    sin = np.concatenate([-np.sin(ang_r), np.sin(ang_r), -np.sin(ang_c), np.sin(ang_c)], axis=1)
    cos = np.concatenate([np.ones((CTX, HD), f32), cos], axis=0).astype(f32)
    sin = np.concatenate([np.zeros((CTX, HD), f32), sin], axis=0).astype(f32)
    return jnp.asarray(np.tile(cos, (1, 2))), jnp.asarray(np.tile(sin, (1, 2)))


def kernel(x, c, ctx, c_ctx, norm_w, w_mod, b_mod, w_in, conv_w, conv_b, ssd_A_log, ssd_dt_bias, ssd_D,
           ssd_norm_w, gdn_A_log, gdn_dt_bias, gdn_norm_w, q_norm_w, k_norm_w, w_out):
    bsz = x.shape[0]
    t = CTX + x.shape[1]
    c_all = jnp.concatenate([c, c_ctx[None, :], jnp.zeros((7, D_MODEL), F32)], axis=0)
    cos_t, sin_t = _rope_tables(t)
    w_conv, w_rest, cw, cb, prm = _prepare_params(w_in, conv_w, conv_b, ssd_A_log, ssd_dt_bias,
                                                  gdn_A_log, gdn_dt_bias)
    qk_w = jnp.concatenate([jnp.tile(q_norm_w, (1, ATT_QH)), jnp.tile(k_norm_w, (1, ATT_KVH))], axis=1)[:, None]
    dskip = jnp.repeat(ssd_D, HD, axis=1)[:, None]
    gdn_nw = jnp.tile(gdn_norm_w, (1, GDN_HEADS))[:, None]
    norm_w3, ssd_nw = norm_w[:, None], ssd_norm_w[:, None]
    w_out16 = w_out.astype(BF16)

    xa, xb, off = ctx, x, 1
    for layer in range(DEPTH):
        first_tile = 1 if layer == DEPTH - 1 else 0
        mod3 = _modulation(c_all, w_mod, b_mod, layer).reshape(bsz + 8, 1, 3 * D_MODEL)
        ssd_in, gdn_in, zs, q, kv, small = _in_projection(
            xa, xb, off, layer, mod3, norm_w3, w_conv, w_rest, cw, cb, qk_w, cos_t, sin_t)
        y = _ssd(ssd_in, small, prm, dskip, layer)
        o, a = _gdn_attention(gdn_in, small, prm, q, kv, layer, first_tile == 0)
        out = _out_projection(xa, xb, off, layer, y, o, a, zs, mod3, ssd_nw, gdn_nw, w_out16, first_tile)
        xa, xb, off = out, out, 0
    return out
```

```python
import functools
import math

import jax
import jax.numpy as jnp
import numpy as np
from jax import lax
from jax.experimental import pallas as pl
from jax.experimental.pallas import tpu as pltpu

F32 = jnp.float32
BF16 = jnp.bfloat16

D_MODEL = 1024
CTX = 256
GRID_W = 64
EPS = 1e-6
DEPTH = 2

HD = 64
SSD_HEADS = 6
SSD_W = SSD_HEADS * HD
SSD_CHUNK = 128
GDN_HEADS = 4
GDN_W = GDN_HEADS * HD
GDN_CHUNK = 64
ATT_QH = 6
ATT_KVH = 2
ATT_W = ATT_QH * HD
ROPE_THETA = 10000.0
Q_PRESCALE = (HD ** -0.5) * math.log2(math.e)

CONV_DIM = 1408
REST_DIM = 1792
TM = 256
HALO = 8
NEG = -1e30
VMEM_LIMIT = 56 * 1024 * 1024

SM_DT = 0
SM_A = 12
SM_B = 20


def _mm(a, b):
    return jnp.dot(a.astype(BF16), b.astype(BF16), preferred_element_type=F32)


def _mm_nt(a, b):
    return lax.dot_general(a.astype(BF16), b.astype(BF16), (((1,), (1,)), ((), ())),
                           preferred_element_type=F32)


def _split3(a):
    h = a.astype(BF16)
    r = a - h.astype(F32)
    m = r.astype(BF16)
    l = (r - m.astype(F32)).astype(BF16)
    return h, m, l


def _mm_exact_lhs(a_bf16, b):
    h, m, l = _split3(b)
    d = functools.partial(jnp.dot, preferred_element_type=F32)
    return d(a_bf16, h) + (d(a_bf16, m) + d(a_bf16, l))


def _silu(x):
    return x * jax.nn.sigmoid(x)


def _softplus(x):
    return jnp.maximum(x, 0.0) + jnp.log1p(jnp.exp(-jnp.abs(x)))


def _iota2(shape, dim):
    return lax.broadcasted_iota(jnp.int32, shape, dim)


def _aligned(x, m):
    return x if isinstance(x, int) else pl.multiple_of(x, m)


def _group_sum64(xx):
    r = jnp.right_shift(_iota2((128, 128), 0), 6)
    c = jnp.right_shift(_iota2((128, 128), 1), 6)
    g = (r == c).astype(BF16)
    outs = []
    for t in range(xx.shape[1] // 128):
        h, m, l = _split3(xx[:, t * 128:(t + 1) * 128])
        d = functools.partial(jnp.dot, preferred_element_type=F32)
        outs.append(d(h, g) + (d(m, g) + d(l, g)))
    return outs[0] if len(outs) == 1 else jnp.concatenate(outs, axis=1)


def _mod_kernel(c_ref, w_ref, b_ref, o_ref):
    o_ref[...] = _mm(_silu(c_ref[...]), w_ref[0]) + b_ref[0]


def _modulation(c_all, w_mod, b_mod, layer):
    n = w_mod.shape[2]
    bn = 768
    rows = c_all.shape[0]
    return pl.pallas_call(
        _mod_kernel,
        out_shape=jax.ShapeDtypeStruct((rows, n), F32),
        grid=(n // bn,),
        in_specs=[pl.BlockSpec((rows, D_MODEL), lambda j: (0, 0)),
                  pl.BlockSpec((1, D_MODEL, bn), lambda j: (layer, 0, j)),
                  pl.BlockSpec((1, 1, bn), lambda j: (layer, 0, j))],
        out_specs=pl.BlockSpec((rows, bn), lambda j: (0, j)),
        compiler_params=pltpu.CompilerParams(dimension_semantics=("arbitrary",),
                                             vmem_limit_bytes=VMEM_LIMIT),
        name="modulation",
    )(c_all, w_mod, b_mod.reshape(b_mod.shape[0], 1, n))


def _proj_kernel(nt, xa_ref, xm_ref, xp_ref, xn_ref, mod_ref, nw_ref, wc_ref, wr_ref, cw_ref, cb_ref,
                 qkw_ref, cos_ref, sin_ref,
                 ssd_ref, gdn_ref, zs_ref, q_ref, kv_ref, small_ref):
    i = pl.program_id(1)
    m = mod_ref[0]
    gain = nw_ref[0] * (1.0 + m[:, D_MODEL:2 * D_MODEL])
    shift = m[:, :D_MODEL]

    xm = jnp.where(i == 0, xa_ref[0], xm_ref[0])
    xe = jnp.concatenate([xm, xp_ref[0], xn_ref[0]], axis=0)
    ms = jnp.mean(xe * xe, axis=-1, keepdims=True)
    he = (xe * lax.rsqrt(ms + EPS) * gain + shift).astype(BF16)

    p = jnp.dot(he, wc_ref[0], preferred_element_type=F32)
    r = jnp.dot(he[:TM], wr_ref[0], preferred_element_type=F32)

    pm = p[:TM]
    seg_first = i <= 1
    seg_last = jnp.logical_or(i == 0, i == nt - 1)
    prev_row = jnp.where(seg_first, 0.0, p[TM + HALO - 1:TM + HALO])
    next_row = jnp.where(seg_last, 0.0, p[TM + HALO:TM + HALO + 1])
    rows = _iota2((TM, 1), 0)
    pm1 = jnp.where(rows == 0, prev_row, pltpu.roll(pm, 1, axis=0))
    pp1 = jnp.where(rows == TM - 1, next_row, pltpu.roll(pm, TM - 1, axis=0))
    cw = cw_ref[0]
    conv = cw[0:1] * pm1 + cw[1:2] * pm + cw[2:3] * pp1 + cb_ref[0]
    co = _silu(conv)

    ssd_ref[0] = co[:, :640].astype(BF16)
    gqk = co[:, 640:1152]
    gqk = gqk * lax.rsqrt(_group_sum64(gqk * gqk) + EPS)
    gdn_ref[0, :, :256] = (gqk[:, :256] * (HD ** -0.5)).astype(BF16)
    gdn_ref[0, :, 256:512] = gqk[:, 256:].astype(BF16)
    gdn_ref[0, :, 512:] = co[:, 1152:1408].astype(BF16)

    z1 = r[:, :640]
    z2 = r[:, 1280:1664]
    zs_ref[0, :, :640] = _silu(z1).astype(BF16)
    zs_ref[0, :, 640:] = _silu(z2).astype(BF16)

    aqk = r[:, 640:1152]
    aqk = aqk * lax.rsqrt(_group_sum64(aqk * aqk) * (1.0 / HD) + EPS) * qkw_ref[0]
    lane = _iota2((1, 512), 1)
    swapped = jnp.where(jnp.bitwise_and(lane, 31) < 16,
                        pltpu.roll(aqk, 512 - 16, axis=1), pltpu.roll(aqk, 16, axis=1))
    cos = jnp.concatenate([cos_ref[...]] * 4, axis=1)
    sin = jnp.concatenate([sin_ref[...]] * 4, axis=1)
    aqk = aqk * cos + swapped * sin
    q_ref[0] = (aqk[:, :384] * Q_PRESCALE).astype(BF16)
    ones = jnp.ones((TM, HD), F32)
    zeros = jnp.zeros((TM, HD), F32)
    for g in range(ATT_KVH):
        kv_ref[0, g] = jnp.concatenate([aqk[:, 384 + g * HD:384 + (g + 1) * HD], zeros,
                                        r[:, 1152 + g * HD:1152 + (g + 1) * HD], ones], axis=1).astype(BF16)
    small_ref[0] = r[:, 1664:1792]


def _in_projection(xa, xb, off, layer, mod3, norm_w, w_conv, w_rest, conv_w, conv_b, qk_w, cos_t, sin_t):
    bsz = xb.shape[0]
    nt = xb.shape[1] // TM + off
    t = nt * TM
    nblk8 = xb.shape[1] // HALO
    per_tile = TM // HALO
    row = lambda b, i: (b, i, 0)
    lay = lambda b, i: (layer, 0, 0)
    outs = [jax.ShapeDtypeStruct((bsz, t, w), dt)
            for w, dt in ((640, BF16), (768, BF16), (1024, BF16), (384, BF16), (128, F32))]
    outs.insert(4, jax.ShapeDtypeStruct((bsz, ATT_KVH, t, 256), BF16))
    return pl.pallas_call(
        functools.partial(_proj_kernel, nt),
        out_shape=outs,
        grid=(bsz, nt),
        in_specs=[
            pl.BlockSpec((1, TM, D_MODEL), lambda b, i: (b, 0, 0)),
            pl.BlockSpec((1, TM, D_MODEL), lambda b, i: (b, jnp.maximum(i - off, 0), 0)),
            pl.BlockSpec((1, HALO, D_MODEL), lambda b, i: (b, jnp.maximum((i - off) * per_tile - 1, 0), 0)),
            pl.BlockSpec((1, HALO, D_MODEL),
                         lambda b, i: (b, jnp.minimum((i - off + 1) * per_tile, nblk8 - 1), 0)),
            pl.BlockSpec((1, 1, 3 * D_MODEL), lambda b, i: (jnp.where(i == 0, bsz, b), 0, 0)),
            pl.BlockSpec((1, 1, D_MODEL), lay),
            pl.BlockSpec((1, D_MODEL, CONV_DIM), lay),
            pl.BlockSpec((1, D_MODEL, REST_DIM), lay),
            pl.BlockSpec((1, 3, CONV_DIM), lay),
            pl.BlockSpec((1, 1, CONV_DIM), lay),
            pl.BlockSpec((1, 1, 512), lay),
            pl.BlockSpec((TM, 128), lambda b, i: (i, 0)),
            pl.BlockSpec((TM, 128), lambda b, i: (i, 0)),
        ],
        out_specs=[pl.BlockSpec((1, TM, 640), row), pl.BlockSpec((1, TM, 768), row),
                   pl.BlockSpec((1, TM, 1024), row), pl.BlockSpec((1, TM, 384), row),
                   pl.BlockSpec((1, ATT_KVH, TM, 256), lambda b, i: (b, 0, i, 0)),
                   pl.BlockSpec((1, TM, 128), row)],
        compiler_params=pltpu.CompilerParams(dimension_semantics=("arbitrary", "arbitrary"),
                                             vmem_limit_bytes=VMEM_LIMIT),
        name="in_projection",
    )(xa, xb, xb, xb, mod3, norm_w, w_conv, w_rest, conv_w, conv_b, qk_w, cos_t, sin_t)


def _ssd_kernel(nc, ctx_chunks, xin_ref, small_ref, prm_ref, dskip_ref, y_ref,
                cum_s, cumt_s, dtt_s, st_s, sin_s):
    q = SSD_CHUNK
    hpg = SSD_HEADS // 2
    ri = _iota2((q, q), 0)
    ci = _iota2((q, q), 1)
    ltri = (ri >= ci).astype(BF16)
    utri = (ri <= ci).astype(BF16)
    lane = _iota2((1, 128), 1)
    fwd_lane = lane < SSD_HEADS

    a_row = -jnp.exp(prm_ref[0, 0:1, :])
    bias_row = prm_ref[0, 1:2, :]

    pb = 6
    tri = jnp.broadcast_to(jnp.concatenate([ltri, utri], axis=0)[None], (pb, 2 * q, q))

    def prep(cb, carry):
        base = pl.multiple_of(cb * (pb * q), q)
        dt = _softplus(small_ref[0, pl.ds(base, pb * q), :] + bias_row)
        pieces = _split3((dt * a_row).reshape(pb, q, 128))
        dot = lambda x: lax.dot_general(tri, x, (((2,), (1,)), ((0,), (0,))), preferred_element_type=F32)
        sums = dot(pieces[0]) + (dot(pieces[1]) + dot(pieces[2]))
        cum = jnp.where(fwd_lane, sums[:, :q], sums[:, q:])
        cum_s[pl.ds(base, pb * q), :] = cum.reshape(pb * q, 128)
        for i in range(pb):
            rows = pl.ds(pl.multiple_of(base + i * q, q), q)
            cumt_s[rows, :] = cum[i].T
            dtt_s[rows, :] = dt[i * q:(i + 1) * q].T
        return carry

    lax.fori_loop(0, nc // pb, prep, 0)

    def end_of(cumt, d, col):
        return cumt[col:col + 1, q - 1:q] if d == 0 else cumt[col:col + 1, 0:1]

    nb = 2

    def states(cb, carry):
        lhs, rhs = [], []
        for cc in range(nb):
            base = pl.multiple_of((cb * nb + cc) * q, q)
            xin = xin_ref[0, pl.ds(base, q), :]
            cumt = cumt_s[pl.ds(base, q), :]
            dtt = dtt_s[pl.ds(base, q), :]
            for g in range(2):
                bgt = xin[:, 384 + g * HD:384 + (g + 1) * HD].astype(F32).T
                for d in range(2):
                    for hg in range(hpg):
                        h = g * hpg + hg
                        col = d * SSD_HEADS + h
                        w_row = jnp.exp(end_of(cumt, d, col) - cumt[col:col + 1, :]) * dtt[col:col + 1, :]
                        lhs.append(bgt * w_row)
                        rhs.append(xin[:, h * HD:(h + 1) * HD])
        st = _bmm(jnp.stack(lhs), jnp.stack(rhs))
        st_s[pl.ds(cb * (12 * nb), 12 * nb)] = st
        return carry

    lax.fori_loop(0, nc // nb, states, 0)

    orders = (list(range(nc)),
              list(range(ctx_chunks - 1, -1, -1)) + list(range(nc - 1, ctx_chunks - 1, -1)))
    for g in range(2):
        for d in range(2):
            for hg in range(hpg):
                col = d * SSD_HEADS + g * hpg + hg
                s = jnp.zeros((HD, HD), F32)
                for c in orders[d]:
                    slot = c * 12 + g * 6 + d * 3 + hg
                    sin_s[slot] = s
                    row = c * q + (q - 1 if d == 0 else 0)
                    s = s * jnp.exp(cum_s[row:row + 1, col:col + 1]) + st_s[slot]

    lower = ri >= ci
    upper = ri <= ci
    dsk = dskip_ref[0]

    def outputs(cb, carry):
        sin = sin_s[pl.ds(cb * (12 * nb), 12 * nb)]
        ws, xs, cstack, es, scs = [], [], [], [], []
        per_chunk = []
        for cc in range(nb):
            base = pl.multiple_of((cb * nb + cc) * q, q)
            xin = xin_ref[0, pl.ds(base, q), :]
            per_chunk.append((base, xin, cum_s[pl.ds(base, q), :], cumt_s[pl.ds(base, q), :],
                              dtt_s[pl.ds(base, q), :]))
        sc = _bmm_nt(jnp.stack([xin[:, 512 + g * HD:512 + (g + 1) * HD] for _, xin, _, _, _ in per_chunk
                                for g in range(2)]),
                     jnp.stack([xin[:, 384 + g * HD:384 + (g + 1) * HD] for _, xin, _, _, _ in per_chunk
                                for g in range(2)]))
        for cc, (base, xin, cum, cumt, dtt) in enumerate(per_chunk):
            e_chunk = {}
            for g in range(2):
                cg = xin[:, 512 + g * HD:512 + (g + 1) * HD]
                for hg in range(hpg):
                    h = g * hpg + hg
                    w = None
                    for d in range(2):
                        col = d * SSD_HEADS + h
                        gmat = jnp.broadcast_to(cum[:, col:col + 1], (q, q))
                        lm = jnp.exp(jnp.where(lower if d == 0 else upper, gmat - cumt[col:col + 1, :], NEG))
                        wd = lm * dtt[col:col + 1, :]
                        w = wd if w is None else w + wd
                        e_chunk[g * 6 + d * 3 + hg] = jnp.exp(gmat[:, :HD])
                    ws.append(w * sc[cc * 2 + g])
                    xs.append(xin[:, h * HD:(h + 1) * HD])
            es.extend(e_chunk[i] for i in range(12))
            cstack.extend(xin[:, 512 + (i // 6) * HD:512 + (i // 6 + 1) * HD] for i in range(12))
        ydiag = _bmm(jnp.stack(ws), jnp.stack(xs))
        yoff = _bmm(jnp.stack(cstack), sin)
        for cc, (base, xin, cum, cumt, dtt) in enumerate(per_chunk):
            outs = []
            for h in range(SSD_HEADS):
                g, hg = divmod(h, hpg)
                y = ydiag[cc * SSD_HEADS + h] + dsk[:, h * HD:(h + 1) * HD] * xs[cc * SSD_HEADS + h]
                for d in range(2):
                    i = cc * 12 + g * 6 + d * 3 + hg
                    y = y + yoff[i] * es[i]
                outs.append(y)
            y_ref[0, pl.ds(base, q), :] = jnp.concatenate(outs, axis=1).astype(BF16)
        return carry

    lax.fori_loop(0, nc // nb, outputs, 0)


def _ssd(ssd_in, small, prm, dskip, layer):
    bsz, t, _ = ssd_in.shape
    nc = t // SSD_CHUNK
    return pl.pallas_call(
        functools.partial(_ssd_kernel, nc, CTX // SSD_CHUNK),
        out_shape=jax.ShapeDtypeStruct((bsz, t, SSD_W), BF16),
        grid=(bsz,),
        in_specs=[pl.BlockSpec((1, t, 640), lambda b: (b, 0, 0)),
                  pl.BlockSpec((1, t, 128), lambda b: (b, 0, 0)),
                  pl.BlockSpec((1, 8, 128), lambda b: (layer, 0, 0)),
                  pl.BlockSpec((1, 1, SSD_W), lambda b: (layer, 0, 0))],
        out_specs=pl.BlockSpec((1, t, SSD_W), lambda b: (b, 0, 0)),
        scratch_shapes=[pltpu.VMEM((t, 128), F32), pltpu.VMEM((t, 128), F32), pltpu.VMEM((t, 128), F32),
                        pltpu.VMEM((nc * 12, HD, HD), F32),
                        pltpu.VMEM((nc * 12, HD, HD), F32)],
        compiler_params=pltpu.CompilerParams(dimension_semantics=("arbitrary",),
                                             vmem_limit_bytes=VMEM_LIMIT),
        name="ssd",
    )(ssd_in, small, prm, dskip)


def _bd4(x):
    xb = x.astype(BF16)
    blk = jnp.right_shift(_iota2((1, 4 * HD), 1), 6)
    zero = jnp.zeros_like(xb)
    return jnp.concatenate([jnp.where(blk == h, xb, zero) for h in range(GDN_HEADS)], axis=0)


def _bmm(a, b):
    return lax.dot_general(a.astype(BF16), b.astype(BF16), (((2,), (1,)), ((0,), (0,))),
                           preferred_element_type=F32)


def _bmm_nt(a, b):
    return lax.dot_general(a.astype(BF16), b.astype(BF16), (((2,), (2,)), ((0,), (0,))),
                           preferred_element_type=F32)


def _unit_tri_inverse_stages(a_strict):
    n = a_strict.shape[-1]
    ri = _iota2((1, n, n), 1)
    ci = _iota2((1, n, n), 2)

    def blk(shift):
        return jnp.right_shift(ri, shift) == jnp.right_shift(ci, shift)

    nd = jnp.where(blk(3), -a_strict, 0.0)
    p0 = (ri == ci).astype(F32) + nd
    q1 = _bmm(nd, nd)
    yield None
    t = _bmm(jnp.concatenate([p0, q1], axis=1), q1)
    yield None
    p1 = p0 + t[:, :n]
    inv = p1 + _bmm(p1, t[:, n:])
    yield None
    for shift in (4, 5, 6):
        e = jnp.where(jnp.logical_and(blk(shift), jnp.logical_not(blk(shift - 1))), a_strict, 0.0)
        ie = _bmm(inv, e)
        yield None
        inv = inv - _bmm(ie, inv)
        yield None
    yield inv


def _gdn_attn_kernel(nch, ctx_chunks, ctx_attn, gin_ref, small_ref, prm_ref, q_ref, kv_ref, o_ref, att_ref,
                     gc_s, rows_s, mq_s, n_s, cd_s, of_s, ob_s):
    c64 = GDN_CHUNK
    w4 = GDN_W
    nsc = nch // 2
    ri128 = _iota2((128, 128), 0)
    ci128 = _iota2((128, 128), 1)
    same = jnp.right_shift(ri128, 6) == jnp.right_shift(ci128, 6)
    lbd = jnp.logical_and(same, ri128 >= ci128).astype(BF16)
    ubd = jnp.logical_and(same, ri128 <= ci128).astype(BF16)
    lane = _iota2((1, 128), 1)
    fwd_lane = lane < SM_A + GDN_HEADS
    beta_lane = jnp.logical_and(lane >= SM_B, lane < SM_B + 2 * GDN_HEADS)

    a_row = -jnp.exp(prm_ref[0, 0:1, :])
    bias_row = prm_ref[0, 1:2, :]

    pb = 6
    tri = jnp.broadcast_to(jnp.concatenate([lbd, ubd], axis=0)[None], (pb, 256, 128))

    def prep(sb, carry):
        base = pl.multiple_of(sb * (pb * 128), 128)
        sm = small_ref[0, pl.ds(base, pb * 128), :]
        pieces = _split3((a_row * _softplus(sm + bias_row)).reshape(pb, 128, 128))
        dot = lambda x: lax.dot_general(tri, x, (((2,), (1,)), ((0,), (0,))), preferred_element_type=F32)
        sums = dot(pieces[0]) + (dot(pieces[1]) + dot(pieces[2]))
        cum = jnp.where(fwd_lane, sums[:, :128], sums[:, 128:])
        gc_s[pl.ds(base, pb * 128), :] = cum.reshape(pb * 128, 128)
        beta = jax.nn.sigmoid(sm)
        for i in range(pb):
            slab_t = jnp.where(beta_lane, beta[i * 128:(i + 1) * 128], cum[i]).T
            for half in range(2):
                lo = half * c64
                table = []
                for first in (SM_A, SM_A + GDN_HEADS, SM_B, SM_B + GDN_HEADS):
                    table.append(jnp.concatenate(
                        [slab_t[first + h:first + h + 1, lo:lo + c64] for h in range(GDN_HEADS)], axis=1))
                for d, edge in ((0, lo + c64 - 1), (1, lo)):
                    first = SM_A + d * GDN_HEADS
                    table.append(jnp.concatenate(
                        [jnp.broadcast_to(slab_t[first + h:first + h + 1, edge:edge + 1], (1, c64))
                         for h in range(GDN_HEADS)], axis=1))
                table.append(jnp.zeros((2, w4), F32))
                rows_s[pl.ds(pl.multiple_of((sb * pb + i) * 16 + half * 8, 8), 8), :] = (
                    jnp.concatenate(table, axis=0))
        return carry

    lax.fori_loop(0, nsc // pb, prep, 0)

    ri = _iota2((c64, c64), 0)
    ci = _iota2((c64, c64), 1)
    incl = (ri >= ci, ri <= ci)
    strict = (ri > ci, ri < ci)

    nb = 4

    def local(cb):
        chunks = []
        for cc in range(nb):
            c = cb * nb + cc
            base = _aligned(c * c64, c64)
            gin = gin_ref[0, pl.ds(base, c64), :]
            gc = gc_s[pl.ds(base, c64), :]
            rows = rows_s[pl.ds(_aligned(c * 8, 8), 8), :]
            qs = [gin[:, h * HD:(h + 1) * HD] for h in range(GDN_HEADS)]
            ks = [gin[:, w4 + h * HD:w4 + (h + 1) * HD] for h in range(GDN_HEADS)]
            vs = [gin[:, 2 * w4 + h * HD:2 * w4 + (h + 1) * HD] for h in range(GDN_HEADS)]
            chunks.append((c, base, gc, rows, qs, ks, vs))
        kq = _bmm_nt(jnp.stack([jnp.concatenate([ch[5][h], ch[4][h]], axis=0)
                                for ch in chunks for h in range(GDN_HEADS)]),
                     jnp.stack([ch[5][h] for ch in chunks for h in range(GDN_HEADS)]))
        yield
        mb_l, rhs_l, lhs_l, qd_l = [], [], [], []
        for ci, (c, base, gc, rows, qs, ks, vs) in enumerate(chunks):
            kts = [k.astype(F32).T for k in ks]
            for d in range(2):
                for h in range(GDN_HEADS):
                    sl = slice(h * HD, (h + 1) * HD)
                    colg = SM_A + d * GDN_HEADS + h
                    kqh = kq[ci * GDN_HEADS + h]
                    gmat = jnp.broadcast_to(gc[:, colg:colg + 1], (c64, c64))
                    gcr, br, endr = rows[d:d + 1, sl], rows[2 + d:3 + d, sl], rows[4 + d:5 + d, sl]
                    dec = jnp.exp(jnp.where(incl[d], gmat - gcr, NEG))
                    mb_l.append(jnp.where(strict[d], kqh[:c64] * dec, 0.0) * br)
                    eg = jnp.exp(gmat)
                    kdtb = kts[h] * (jnp.exp(endr - gcr) * br)
                    lhs_l.append(jnp.concatenate([kdtb, kqh[c64:] * dec * br], axis=0))
                    rhs_l.append(jnp.concatenate([vs[h], ks[h] * eg], axis=1))
                    qd_l.append(qs[h] * eg)
        z = None
        for z in _unit_tri_inverse_stages(jnp.stack(mb_l)):
            if z is None:
                yield
        zr = _bmm(z, jnp.stack(rhs_l))
        yield
        a12 = _bmm(jnp.stack(lhs_l), zr)
        yield
        for ci, (c, base, gc, rows, qs, ks, vs) in enumerate(chunks):
            for d in range(2):
                slot = c * 2 + d
                ids = [ci * 2 * GDN_HEADS + d * GDN_HEADS + h for h in range(GDN_HEADS)]
                mq = [jnp.concatenate([-a12[n][:c64, HD:], qd_l[n] - a12[n][c64:, HD:]], axis=0) for n in ids]
                mq_s[slot] = jnp.concatenate(mq, axis=1).astype(BF16)
                n_s[slot] = jnp.concatenate([a12[n][:c64, :HD] for n in ids], axis=1)
                cd_s[slot] = jnp.broadcast_to(jnp.exp(rows[4 + d:5 + d]), (8, w4))
                oc = jnp.concatenate([a12[n][c64:, :HD] for n in ids], axis=1)
                if d == 0:
                    of_s[pl.ds(base, c64), :] = oc
                else:
                    ob_s[pl.ds(base, c64), :] = oc

    t_all = q_ref.shape[1]
    heads_per_kv = ATT_QH // ATT_KVH

    def attend(tile, nkeys):
        rows = pl.ds(_aligned(tile * TM, TM), TM)
        qt = q_ref[0, rows, :]
        for g in range(ATT_KVH):
            k = kv_ref[0, g, :nkeys, :HD]
            va = kv_ref[0, g, :nkeys, 2 * HD:]
            outs = []
            for jj in range(heads_per_kv):
                h = g * heads_per_kv + jj
                s = _mm_nt(qt[:, h * HD:(h + 1) * HD], k)
                yield
                p = jnp.exp2(s - jnp.max(s, axis=-1, keepdims=True))
                acc = _mm(p, va)
                yield
                outs.append(acc[:, :HD] / acc[:, HD:])
            outs.append(jnp.zeros((TM, HD), F32))
            att_ref[0, g, rows, :] = jnp.concatenate(outs, axis=1).astype(BF16)

    def trip(j, nkeys):
        streams = [local(j)] + ([attend(j, nkeys)] if nkeys else [])
        while streams:
            for st in list(streams):
                try:
                    next(st)
                except StopIteration:
                    streams.remove(st)

    trip(0, CTX if ctx_attn else 0)
    if not ctx_attn:
        att_ref[0, :, :TM, :] = jnp.zeros((ATT_KVH, TM, 256), BF16)

    def later_trip(j, carry):
        trip(j, t_all)
        return carry

    lax.fori_loop(1, nch // nb, later_trip, 0)

    def step(t, states):
        cf = t
        cb = jnp.where(t < ctx_chunks, ctx_chunks - 1 - t, nch - 1 + ctx_chunks - t)
        new_states = []
        for d, c in ((0, cf), (1, cb)):
            slot = c * 2 + d
            s = states[d]
            r = jnp.dot(mq_s[slot], _bd4(s), preferred_element_type=F32)
            new_states.append(s * cd_s[slot][0:1] + r[:c64] + n_s[slot])
            rows = pl.ds(pl.multiple_of(c * c64, c64), c64)
            if d == 0:
                of_s[rows, :] = of_s[rows, :] + r[c64:]
            else:
                ob_s[rows, :] = ob_s[rows, :] + r[c64:]
        return tuple(new_states)

    zero = jnp.zeros((c64, w4), F32)
    lax.fori_loop(0, nch, step, (zero, zero))
    o_ref[0] = (of_s[...] + ob_s[...]).astype(BF16)


def _gdn_attention(gdn_in, small, prm, q, kv, layer, ctx_attn):
    bsz, t, _ = gdn_in.shape
    nch = t // GDN_CHUNK
    assert nch // 4 == t // TM
    return pl.pallas_call(
        functools.partial(_gdn_attn_kernel, nch, CTX // GDN_CHUNK, ctx_attn),
        out_shape=[jax.ShapeDtypeStruct((bsz, t, GDN_W), BF16),
                   jax.ShapeDtypeStruct((bsz, ATT_KVH, t, 256), BF16)],
        grid=(bsz,),
        in_specs=[pl.BlockSpec((1, t, 768), lambda b: (b, 0, 0)),
                  pl.BlockSpec((1, t, 128), lambda b: (b, 0, 0)),
                  pl.BlockSpec((1, 8, 128), lambda b: (layer, 0, 0)),
                  pl.BlockSpec((1, t, ATT_W), lambda b: (b, 0, 0)),
                  pl.BlockSpec((1, ATT_KVH, t, 256), lambda b: (b, 0, 0, 0))],
        out_specs=[pl.BlockSpec((1, t, GDN_W), lambda b: (b, 0, 0)),
                   pl.BlockSpec((1, ATT_KVH, t, 256), lambda b: (b, 0, 0, 0))],
        scratch_shapes=[pltpu.VMEM((t, 128), F32),
                        pltpu.VMEM((nch * 8, GDN_W), F32),
                        pltpu.VMEM((nch * 2, 128, GDN_W), BF16),
                        pltpu.VMEM((nch * 2, GDN_CHUNK, GDN_W), F32),
                        pltpu.VMEM((nch * 2, 8, GDN_W), F32),
                        pltpu.VMEM((t, GDN_W), F32),
                        pltpu.VMEM((t, GDN_W), F32)],
        compiler_params=pltpu.CompilerParams(dimension_semantics=("arbitrary",),
                                             vmem_limit_bytes=VMEM_LIMIT),
        name="gdn_attention",
    )(gdn_in, small, prm, q, kv)


def _out_kernel(first_tile, xa_ref, x_ref, y_ref, o_ref, a_ref, zs_ref, mod_ref, sw_ref, gw_ref, w_ref, out_ref):
    zs = zs_ref[0].astype(F32)
    t = y_ref[0].astype(F32) * zs[:, :SSD_W]
    ssd = t * lax.rsqrt(jnp.mean(t * t, axis=-1, keepdims=True) + EPS) * sw_ref[0]
    o = o_ref[0].astype(F32)
    gdn = o * lax.rsqrt(_group_sum64(o * o) * (1.0 / HD) + EPS) * gw_ref[0] * zs[:, SSD_W:SSD_W + GDN_W]
    a = a_ref[0]
    att = jnp.concatenate([a[g][:, :HD * ATT_QH // ATT_KVH] for g in range(ATT_KVH)], axis=1)
    att = att * zs[:, SSD_W + GDN_W:]
    mix = jnp.concatenate([ssd, gdn, att], axis=1).astype(BF16)
    gate = mod_ref[0][:, 2 * D_MODEL:]
    x = jnp.where(pl.program_id(1) + first_tile == 0, xa_ref[0], x_ref[0])
    out_ref[0] = x + gate * jnp.dot(mix, w_ref[0], preferred_element_type=F32)


def _out_projection(xa, xb, off, layer, y, o, a, zs, mod3, ssd_nw, gdn_nw, w_out, first_tile):
    bsz, t, _ = y.shape
    nt = t // TM - first_tile
    row = lambda b, i: (b, i + first_tile, 0)
    lay = lambda b, i: (layer, 0, 0)
    return pl.pallas_call(
        functools.partial(_out_kernel, first_tile),
        out_shape=jax.ShapeDtypeStruct((bsz, nt * TM, D_MODEL), F32),
        grid=(bsz, nt),
        in_specs=[pl.BlockSpec((1, TM, D_MODEL), lambda b, i: (b, 0, 0)),
                  pl.BlockSpec((1, TM, D_MODEL), lambda b, i: (b, jnp.maximum(i + first_tile - off, 0), 0)),
                  pl.BlockSpec((1, TM, SSD_W), row),
                  pl.BlockSpec((1, TM, GDN_W), row),
                  pl.BlockSpec((1, ATT_KVH, TM, 256), lambda b, i: (b, 0, i + first_tile, 0)),
                  pl.BlockSpec((1, TM, D_MODEL), row),
                  pl.BlockSpec((1, 1, 3 * D_MODEL),
                               lambda b, i: (jnp.where(i + first_tile == 0, bsz, b), 0, 0)),
                  pl.BlockSpec((1, 1, SSD_W), lay),
                  pl.BlockSpec((1, 1, GDN_W), lay),
                  pl.BlockSpec((1, D_MODEL, D_MODEL), lay)],
        out_specs=pl.BlockSpec((1, TM, D_MODEL), lambda b, i: (b, i, 0)),
        compiler_params=pltpu.CompilerParams(dimension_semantics=("arbitrary", "arbitrary"),
                                             vmem_limit_bytes=VMEM_LIMIT),
        name="out_projection",
    )(xa, xb, y, o, a, zs, mod3, ssd_nw, gdn_nw, w_out)


_CONV_SEGS = [(0, CONV_DIM)]
_REST_SEGS = [(1408, 384), (1804, 256), (2076, 384), (2460, 128), (2588, 128), (2716, 384),
              (1792, 12), (2060, 8), (2068, 8)]


def _take_segments(w, segs, axis):
    return jnp.concatenate([lax.slice_in_dim(w, a, a + n, axis=axis) for a, n in segs], axis=axis)


def _prepare_params(w_in, conv_w, conv_b, ssd_A_log, ssd_dt_bias, gdn_A_log, gdn_dt_bias):
    depth = w_in.shape[0]
    w_conv = _take_segments(w_in, _CONV_SEGS, 2).astype(BF16)
    pad = jnp.zeros(w_in.shape[:2] + (REST_DIM - sum(n for _, n in _REST_SEGS),), w_in.dtype)
    w_rest = jnp.concatenate([_take_segments(w_in, _REST_SEGS, 2), pad], axis=2).astype(BF16)
    cw = _take_segments(conv_w, _CONV_SEGS, 2)
    cb = _take_segments(conv_b, _CONV_SEGS, 1)[:, None, :]

    def lanes(a, b):
        v = jnp.concatenate([a.reshape(depth, -1), b.reshape(depth, -1)], axis=1)
        return jnp.pad(v, ((0, 0), (0, 128 - v.shape[1])))

    prm = jnp.stack([lanes(ssd_A_log, gdn_A_log), lanes(ssd_dt_bias, gdn_dt_bias)], axis=1)
    prm = jnp.pad(prm, ((0, 0), (0, 6), (0, 0)))
    return w_conv, w_rest, cw, cb, prm


def _rope_tables(t):
    f32 = np.float32
    pos = np.arange(t - CTX)
    n_freq = HD // 4
    freqs = np.power(f32(ROPE_THETA), -np.arange(n_freq, dtype=f32) / f32(n_freq)).astype(f32)
    ang_r = ((pos // GRID_W).astype(f32)[:, None] * freqs).astype(f32)
    ang_c = ((pos % GRID_W).astype(f32)[:, None] * freqs).astype(f32)
    cos = np.concatenate([np.cos(ang_r)] * 2 + [np.cos(ang_c)] * 2, axis=1)
    sin = np.concatenate([-np.sin(ang_r), np.sin(ang_r), -np.sin(ang_c), np.sin(ang_c)], axis=1)
    cos = np.concatenate([np.ones((CTX, HD), f32), cos], axis=0).astype(f32)
    sin = np.concatenate([np.zeros((CTX, HD), f32), sin], axis=0).astype(f32)
    return jnp.asarray(np.tile(cos, (1, 2))), jnp.asarray(np.tile(sin, (1, 2)))


def kernel(x, c, ctx, c_ctx, norm_w, w_mod, b_mod, w_in, conv_w, conv_b, ssd_A_log, ssd_dt_bias, ssd_D,
           ssd_norm_w, gdn_A_log, gdn_dt_bias, gdn_norm_w, q_norm_w, k_norm_w, w_out):
    bsz = x.shape[0]
    t = CTX + x.shape[1]
    c_all = jnp.concatenate([c, c_ctx[None, :], jnp.zeros((7, D_MODEL), F32)], axis=0)
    cos_t, sin_t = _rope_tables(t)
    w_conv, w_rest, cw, cb, prm = _prepare_params(w_in, conv_w, conv_b, ssd_A_log, ssd_dt_bias,
                                                  gdn_A_log, gdn_dt_bias)
    qk_w = jnp.concatenate([jnp.tile(q_norm_w, (1, ATT_QH)), jnp.tile(k_norm_w, (1, ATT_KVH))], axis=1)[:, None]
    dskip = jnp.repeat(ssd_D, HD, axis=1)[:, None]
    gdn_nw = jnp.tile(gdn_norm_w, (1, GDN_HEADS))[:, None]
    norm_w3, ssd_nw = norm_w[:, None], ssd_norm_w[:, None]
    w_out16 = w_out.astype(BF16)

    xa, xb, off = ctx, x, 1
    for layer in range(DEPTH):
        first_tile = 1 if layer == DEPTH - 1 else 0
        mod3 = _modulation(c_all, w_mod, b_mod, layer).reshape(bsz + 8, 1, 3 * D_MODEL)
        ssd_in, gdn_in, zs, q, kv, small = _in_projection(
            xa, xb, off, layer, mod3, norm_w3, w_conv, w_rest, cw, cb, qk_w, cos_t, sin_t)
        y = _ssd(ssd_in, small, prm, dskip, layer)
        o, a = _gdn_attention(gdn_in, small, prm, q, kv, layer, first_tile == 0)
        out = _out_projection(xa, xb, off, layer, y, o, a, zs, mod3, ssd_nw, gdn_nw, w_out16, first_tile)
        xa, xb, off = out, out, 0
    return out
```

```python
import functools
import math

import jax
import jax.numpy as jnp
import numpy as np
from jax import lax
from jax.experimental import pallas as pl
from jax.experimental.pallas import tpu as pltpu

F32 = jnp.float32
BF16 = jnp.bfloat16

D_MODEL = 1024
CTX = 256
GRID_W = 64
EPS = 1e-6
DEPTH = 2

HD = 64
SSD_HEADS = 6
SSD_W = SSD_HEADS * HD
SSD_CHUNK = 128
GDN_HEADS = 4
GDN_W = GDN_HEADS * HD
GDN_CHUNK = 64
ATT_QH = 6
ATT_KVH = 2
ATT_W = ATT_QH * HD
ROPE_THETA = 10000.0
Q_PRESCALE = (HD ** -0.5) * math.log2(math.e)

CONV_DIM = 1408
REST_DIM = 1792
TM = 256
HALO = 8
NEG = -1e30
VMEM_LIMIT = 56 * 1024 * 1024

SM_DT = 0
SM_A = 12
SM_B = 20


def _mm(a, b):
    return jnp.dot(a.astype(BF16), b.astype(BF16), preferred_element_type=F32)


def _mm_nt(a, b):
    return lax.dot_general(a.astype(BF16), b.astype(BF16), (((1,), (1,)), ((), ())),
                           preferred_element_type=F32)


def _split3(a):
    h = a.astype(BF16)
    r = a - h.astype(F32)
    m = r.astype(BF16)
    l = (r - m.astype(F32)).astype(BF16)
    return h, m, l


def _mm_exact_lhs(a_bf16, b):
    h, m, l = _split3(b)
    d = functools.partial(jnp.dot, preferred_element_type=F32)
    return d(a_bf16, h) + (d(a_bf16, m) + d(a_bf16, l))


def _silu(x):
    return x * jax.nn.sigmoid(x)


def _softplus(x):
    return jnp.maximum(x, 0.0) + jnp.log1p(jnp.exp(-jnp.abs(x)))


def _iota2(shape, dim):
    return lax.broadcasted_iota(jnp.int32, shape, dim)


def _aligned(x, m):
    return x if isinstance(x, int) else pl.multiple_of(x, m)


def _group_sum64(xx):
    r = jnp.right_shift(_iota2((128, 128), 0), 6)
    c = jnp.right_shift(_iota2((128, 128), 1), 6)
    g = (r == c).astype(BF16)
    xb = xx.astype(BF16)
    outs = [jnp.dot(xb[:, t * 128:(t + 1) * 128], g, preferred_element_type=F32)
            for t in range(xx.shape[1] // 128)]
    return outs[0] if len(outs) == 1 else jnp.concatenate(outs, axis=1)


def _mod_kernel(c_ref, w_ref, b_ref, o_ref):
    o_ref[...] = _mm(_silu(c_ref[...]), w_ref[0]) + b_ref[0]


def _modulation(c_all, w_mod, b_mod, layer):
    n = w_mod.shape[2]
    bn = 768
    rows = c_all.shape[0]
    return pl.pallas_call(
        _mod_kernel,
        out_shape=jax.ShapeDtypeStruct((rows, n), F32),
        grid=(n // bn,),
        in_specs=[pl.BlockSpec((rows, D_MODEL), lambda j: (0, 0)),
                  pl.BlockSpec((1, D_MODEL, bn), lambda j: (layer, 0, j)),
                  pl.BlockSpec((1, 1, bn), lambda j: (layer, 0, j))],
        out_specs=pl.BlockSpec((rows, bn), lambda j: (0, j)),
        compiler_params=pltpu.CompilerParams(dimension_semantics=("arbitrary",),
                                             vmem_limit_bytes=VMEM_LIMIT),
        name="modulation",
    )(c_all, w_mod, b_mod.reshape(b_mod.shape[0], 1, n))


def _proj_kernel(nt, xa_ref, xm_ref, xp_ref, xn_ref, mod_ref, nw_ref, wc_ref, wr_ref, cw_ref, cb_ref,
                 qkw_ref, cos_ref, sin_ref,
                 ssd_ref, gdn_ref, zs_ref, q_ref, kv_ref, small_ref):
    i = pl.program_id(1)
    m = mod_ref[0]
    gain = nw_ref[0] * (1.0 + m[:, D_MODEL:2 * D_MODEL])
    shift = m[:, :D_MODEL]

    xm = jnp.where(i == 0, xa_ref[0], xm_ref[0])
    xe = jnp.concatenate([xm, xp_ref[0], xn_ref[0]], axis=0)
    ms = jnp.mean(xe * xe, axis=-1, keepdims=True)
    he = (xe * lax.rsqrt(ms + EPS) * gain + shift).astype(BF16)

    p = jnp.dot(he, wc_ref[0], preferred_element_type=F32)
    r = jnp.dot(he[:TM], wr_ref[0], preferred_element_type=F32)

    pm = p[:TM]
    seg_first = i <= 1
    seg_last = jnp.logical_or(i == 0, i == nt - 1)
    prev_row = jnp.where(seg_first, 0.0, p[TM + HALO - 1:TM + HALO])
    next_row = jnp.where(seg_last, 0.0, p[TM + HALO:TM + HALO + 1])
    rows = _iota2((TM, 1), 0)
    pm1 = jnp.where(rows == 0, prev_row, pltpu.roll(pm, 1, axis=0))
    pp1 = jnp.where(rows == TM - 1, next_row, pltpu.roll(pm, TM - 1, axis=0))
    cw = cw_ref[0]
    conv = cw[0:1] * pm1 + cw[1:2] * pm + cw[2:3] * pp1 + cb_ref[0]
    co = _silu(conv)

    ssd_ref[0] = co[:, :640].astype(BF16)
    gqk = co[:, 640:1152]
    gqk = gqk * lax.rsqrt(_group_sum64(gqk * gqk) + EPS)
    gdn_ref[0, :, :256] = (gqk[:, :256] * (HD ** -0.5)).astype(BF16)
    gdn_ref[0, :, 256:512] = gqk[:, 256:].astype(BF16)
    gdn_ref[0, :, 512:] = co[:, 1152:1408].astype(BF16)

    z1 = r[:, :640]
    z2 = r[:, 1280:1664]
    zs_ref[0, :, :640] = _silu(z1).astype(BF16)
    zs_ref[0, :, 640:] = _silu(z2).astype(BF16)

    aqk = r[:, 640:1152]
    aqk = aqk * lax.rsqrt(_group_sum64(aqk * aqk) * (1.0 / HD) + EPS) * qkw_ref[0]
    lane = _iota2((1, 512), 1)
    swapped = jnp.where(jnp.bitwise_and(lane, 31) < 16,
                        pltpu.roll(aqk, 512 - 16, axis=1), pltpu.roll(aqk, 16, axis=1))
    cos = jnp.concatenate([cos_ref[...]] * 4, axis=1)
    sin = jnp.concatenate([sin_ref[...]] * 4, axis=1)
    aqk = aqk * cos + swapped * sin
    q_ref[0] = (aqk[:, :384] * Q_PRESCALE).astype(BF16)
    ones = jnp.ones((TM, HD), F32)
    zeros = jnp.zeros((TM, HD), F32)
    for g in range(ATT_KVH):
        kv_ref[0, g] = jnp.concatenate([aqk[:, 384 + g * HD:384 + (g + 1) * HD], zeros,
                                        r[:, 1152 + g * HD:1152 + (g + 1) * HD], ones], axis=1).astype(BF16)
    small_ref[0] = r[:, 1664:1792]


def _in_projection(xa, xb, off, layer, mod3, norm_w, w_conv, w_rest, conv_w, conv_b, qk_w, cos_t, sin_t):
    bsz = xb.shape[0]
    nt = xb.shape[1] // TM + off
    t = nt * TM
    nblk8 = xb.shape[1] // HALO
    per_tile = TM // HALO
    row = lambda b, i: (b, i, 0)
    lay = lambda b, i: (layer, 0, 0)
    outs = [jax.ShapeDtypeStruct((bsz, t, w), dt)
            for w, dt in ((640, BF16), (768, BF16), (1024, BF16), (384, BF16), (128, F32))]
    outs.insert(4, jax.ShapeDtypeStruct((bsz, ATT_KVH, t, 256), BF16))
    return pl.pallas_call(
        functools.partial(_proj_kernel, nt),
        out_shape=outs,
        grid=(bsz, nt),
        in_specs=[
            pl.BlockSpec((1, TM, D_MODEL), lambda b, i: (b, 0, 0)),
            pl.BlockSpec((1, TM, D_MODEL), lambda b, i: (b, jnp.maximum(i - off, 0), 0)),
            pl.BlockSpec((1, HALO, D_MODEL), lambda b, i: (b, jnp.maximum((i - off) * per_tile - 1, 0), 0)),
            pl.BlockSpec((1, HALO, D_MODEL),
                         lambda b, i: (b, jnp.minimum((i - off + 1) * per_tile, nblk8 - 1), 0)),
            pl.BlockSpec((1, 1, 3 * D_MODEL), lambda b, i: (jnp.where(i == 0, bsz, b), 0, 0)),
            pl.BlockSpec((1, 1, D_MODEL), lay),
            pl.BlockSpec((1, D_MODEL, CONV_DIM), lay),
            pl.BlockSpec((1, D_MODEL, REST_DIM), lay),
            pl.BlockSpec((1, 3, CONV_DIM), lay),
            pl.BlockSpec((1, 1, CONV_DIM), lay),
            pl.BlockSpec((1, 1, 512), lay),
            pl.BlockSpec((TM, 128), lambda b, i: (i, 0)),
            pl.BlockSpec((TM, 128), lambda b, i: (i, 0)),
        ],
        out_specs=[pl.BlockSpec((1, TM, 640), row), pl.BlockSpec((1, TM, 768), row),
                   pl.BlockSpec((1, TM, 1024), row), pl.BlockSpec((1, TM, 384), row),
                   pl.BlockSpec((1, ATT_KVH, TM, 256), lambda b, i: (b, 0, i, 0)),
                   pl.BlockSpec((1, TM, 128), row)],
        compiler_params=pltpu.CompilerParams(dimension_semantics=("arbitrary", "arbitrary"),
                                             vmem_limit_bytes=VMEM_LIMIT),
        name="in_projection",
    )(xa, xb, xb, xb, mod3, norm_w, w_conv, w_rest, conv_w, conv_b, qk_w, cos_t, sin_t)


def _ssd_kernel(nc, ctx_chunks, xin_ref, small_ref, prm_ref, dskip_ref, y_ref,
                cum_s, cumt_s, dtt_s, st_s, sin_s):
    q = SSD_CHUNK
    hpg = SSD_HEADS // 2
    ri = _iota2((q, q), 0)
    ci = _iota2((q, q), 1)
    ltri = (ri >= ci).astype(BF16)
    utri = (ri <= ci).astype(BF16)
    lane = _iota2((1, 128), 1)
    fwd_lane = lane < SSD_HEADS

    a_row = -jnp.exp(prm_ref[0, 0:1, :])
    bias_row = prm_ref[0, 1:2, :]

    pb = 6
    tri = jnp.broadcast_to(jnp.concatenate([ltri, utri], axis=0)[None], (pb, 2 * q, q))

    def prep(cb, carry):
        base = pl.multiple_of(cb * (pb * q), q)
        dt = _softplus(small_ref[0, pl.ds(base, pb * q), :] + bias_row)
        pieces = _split3((dt * a_row).reshape(pb, q, 128))
        dot = lambda x: lax.dot_general(tri, x, (((2,), (1,)), ((0,), (0,))), preferred_element_type=F32)
        sums = dot(pieces[0]) + (dot(pieces[1]) + dot(pieces[2]))
        cum = jnp.where(fwd_lane, sums[:, :q], sums[:, q:])
        cum_s[pl.ds(base, pb * q), :] = cum.reshape(pb * q, 128)
        for i in range(pb):
            rows = pl.ds(pl.multiple_of(base + i * q, q), q)
            cumt_s[rows, :] = cum[i].T
            dtt_s[rows, :] = dt[i * q:(i + 1) * q].T
        return carry

    lax.fori_loop(0, nc // pb, prep, 0)

    def end_of(cumt, d, col):
        return cumt[col:col + 1, q - 1:q] if d == 0 else cumt[col:col + 1, 0:1]

    nb = 2

    def states(cb, carry):
        lhs, rhs = [], []
        for cc in range(nb):
            base = pl.multiple_of((cb * nb + cc) * q, q)
            xin = xin_ref[0, pl.ds(base, q), :]
            cumt = cumt_s[pl.ds(base, q), :]
            dtt = dtt_s[pl.ds(base, q), :]
            for g in range(2):
                bgt = xin[:, 384 + g * HD:384 + (g + 1) * HD].astype(F32).T
                for d in range(2):
                    for hg in range(hpg):
                        h = g * hpg + hg
                        col = d * SSD_HEADS + h
                        w_row = jnp.exp(end_of(cumt, d, col) - cumt[col:col + 1, :]) * dtt[col:col + 1, :]
                        lhs.append(bgt * w_row)
                        rhs.append(xin[:, h * HD:(h + 1) * HD])
        st = _bmm(jnp.stack(lhs), jnp.stack(rhs))
        st_s[pl.ds(cb * (12 * nb), 12 * nb)] = st
        return carry

    lax.fori_loop(0, nc // nb, states, 0)

    orders = (list(range(nc)),
              list(range(ctx_chunks - 1, -1, -1)) + list(range(nc - 1, ctx_chunks - 1, -1)))
    for g in range(2):
        for d in range(2):
            for hg in range(hpg):
                col = d * SSD_HEADS + g * hpg + hg
                s = jnp.zeros((HD, HD), F32)
                for c in orders[d]:
                    slot = c * 12 + g * 6 + d * 3 + hg
                    sin_s[slot] = s
                    row = c * q + (q - 1 if d == 0 else 0)
                    s = s * jnp.exp(cum_s[row:row + 1, col:col + 1]) + st_s[slot]

    lower = ri >= ci
    upper = ri <= ci
    dsk = dskip_ref[0]

    def outputs(cb, carry):
        sin = sin_s[pl.ds(cb * (12 * nb), 12 * nb)]
        ws, xs, cstack, es, scs = [], [], [], [], []
        per_chunk = []
        for cc in range(nb):
            base = pl.multiple_of((cb * nb + cc) * q, q)
            xin = xin_ref[0, pl.ds(base, q), :]
            per_chunk.append((base, xin, cum_s[pl.ds(base, q), :], cumt_s[pl.ds(base, q), :],
                              dtt_s[pl.ds(base, q), :]))
        sc = _bmm_nt(jnp.stack([xin[:, 512 + g * HD:512 + (g + 1) * HD] for _, xin, _, _, _ in per_chunk
                                for g in range(2)]),
                     jnp.stack([xin[:, 384 + g * HD:384 + (g + 1) * HD] for _, xin, _, _, _ in per_chunk
                                for g in range(2)]))
        for cc, (base, xin, cum, cumt, dtt) in enumerate(per_chunk):
            e_chunk = {}
            for g in range(2):
                cg = xin[:, 512 + g * HD:512 + (g + 1) * HD]
                for hg in range(hpg):
                    h = g * hpg + hg
                    w = None
                    for d in range(2):
                        col = d * SSD_HEADS + h
                        gmat = jnp.broadcast_to(cum[:, col:col + 1], (q, q))
                        lm = jnp.exp(jnp.where(lower if d == 0 else upper, gmat - cumt[col:col + 1, :], NEG))
                        wd = lm * dtt[col:col + 1, :]
                        w = wd if w is None else w + wd
                        e_chunk[g * 6 + d * 3 + hg] = jnp.exp(gmat[:, :HD])
                    ws.append(w * sc[cc * 2 + g])
                    xs.append(xin[:, h * HD:(h + 1) * HD])
            es.extend(e_chunk[i] for i in range(12))
            cstack.extend(xin[:, 512 + (i // 6) * HD:512 + (i // 6 + 1) * HD] for i in range(12))
        ydiag = _bmm(jnp.stack(ws), jnp.stack(xs))
        yoff = _bmm(jnp.stack(cstack), sin)
        for cc, (base, xin, cum, cumt, dtt) in enumerate(per_chunk):
            outs = []
            for h in range(SSD_HEADS):
                g, hg = divmod(h, hpg)
                y = ydiag[cc * SSD_HEADS + h] + dsk[:, h * HD:(h + 1) * HD] * xs[cc * SSD_HEADS + h]
                for d in range(2):
                    i = cc * 12 + g * 6 + d * 3 + hg
                    y = y + yoff[i] * es[i]
                outs.append(y)
            y_ref[0, pl.ds(base, q), :] = jnp.concatenate(outs, axis=1).astype(BF16)
        return carry

    lax.fori_loop(0, nc // nb, outputs, 0)


def _ssd(ssd_in, small, prm, dskip, layer):
    bsz, t, _ = ssd_in.shape
    nc = t // SSD_CHUNK
    return pl.pallas_call(
        functools.partial(_ssd_kernel, nc, CTX // SSD_CHUNK),
        out_shape=jax.ShapeDtypeStruct((bsz, t, SSD_W), BF16),
        grid=(bsz,),
        in_specs=[pl.BlockSpec((1, t, 640), lambda b: (b, 0, 0)),
                  pl.BlockSpec((1, t, 128), lambda b: (b, 0, 0)),
                  pl.BlockSpec((1, 8, 128), lambda b: (layer, 0, 0)),
                  pl.BlockSpec((1, 1, SSD_W), lambda b: (layer, 0, 0))],
        out_specs=pl.BlockSpec((1, t, SSD_W), lambda b: (b, 0, 0)),
        scratch_shapes=[pltpu.VMEM((t, 128), F32), pltpu.VMEM((t, 128), F32), pltpu.VMEM((t, 128), F32),
                        pltpu.VMEM((nc * 12, HD, HD), F32),
                        pltpu.VMEM((nc * 12, HD, HD), F32)],
        compiler_params=pltpu.CompilerParams(dimension_semantics=("arbitrary",),
                                             vmem_limit_bytes=VMEM_LIMIT),
        name="ssd",
    )(ssd_in, small, prm, dskip)


def _bd4(x):
    xb = x.astype(BF16)
    blk = jnp.right_shift(_iota2((1, 4 * HD), 1), 6)
    zero = jnp.zeros_like(xb)
    return jnp.concatenate([jnp.where(blk == h, xb, zero) for h in range(GDN_HEADS)], axis=0)


def _bmm(a, b):
    return lax.dot_general(a.astype(BF16), b.astype(BF16), (((2,), (1,)), ((0,), (0,))),
                           preferred_element_type=F32)


def _bmm_nt(a, b):
    return lax.dot_general(a.astype(BF16), b.astype(BF16), (((2,), (2,)), ((0,), (0,))),
                           preferred_element_type=F32)


def _unit_tri_inverse_stages(a_strict):
    n = a_strict.shape[-1]
    ri = _iota2((1, n, n), 1)
    ci = _iota2((1, n, n), 2)

    def blk(shift):
        return jnp.right_shift(ri, shift) == jnp.right_shift(ci, shift)

    nd = jnp.where(blk(3), -a_strict, 0.0)
    p0 = (ri == ci).astype(F32) + nd
    q1 = _bmm(nd, nd)
    yield None
    t = _bmm(jnp.concatenate([p0, q1], axis=1), q1)
    yield None
    p1 = p0 + t[:, :n]
    inv = p1 + _bmm(p1, t[:, n:])
    yield None
    for shift in (4, 5, 6):
        e = jnp.where(jnp.logical_and(blk(shift), jnp.logical_not(blk(shift - 1))), a_strict, 0.0)
        ie = _bmm(inv, e)
        yield None
        inv = inv - _bmm(ie, inv)
        yield None
    yield inv


def _gdn_attn_kernel(nch, ctx_chunks, ctx_attn, gin_ref, small_ref, prm_ref, q_ref, kv_ref, o_ref, att_ref,
                     gc_s, rows_s, mq_s, n_s, cd_s, of_s, ob_s):
    c64 = GDN_CHUNK
    w4 = GDN_W
    nsc = nch // 2
    ri128 = _iota2((128, 128), 0)
    ci128 = _iota2((128, 128), 1)
    same = jnp.right_shift(ri128, 6) == jnp.right_shift(ci128, 6)
    lbd = jnp.logical_and(same, ri128 >= ci128).astype(BF16)
    ubd = jnp.logical_and(same, ri128 <= ci128).astype(BF16)
    lane = _iota2((1, 128), 1)
    fwd_lane = lane < SM_A + GDN_HEADS
    beta_lane = jnp.logical_and(lane >= SM_B, lane < SM_B + 2 * GDN_HEADS)

    a_row = -jnp.exp(prm_ref[0, 0:1, :])
    bias_row = prm_ref[0, 1:2, :]

    pb = 6
    tri = jnp.broadcast_to(jnp.concatenate([lbd, ubd], axis=0)[None], (pb, 256, 128))

    def prep(sb, carry):
        base = pl.multiple_of(sb * (pb * 128), 128)
        sm = small_ref[0, pl.ds(base, pb * 128), :]
        pieces = _split3((a_row * _softplus(sm + bias_row)).reshape(pb, 128, 128))
        dot = lambda x: lax.dot_general(tri, x, (((2,), (1,)), ((0,), (0,))), preferred_element_type=F32)
        sums = dot(pieces[0]) + (dot(pieces[1]) + dot(pieces[2]))
        cum = jnp.where(fwd_lane, sums[:, :128], sums[:, 128:])
        gc_s[pl.ds(base, pb * 128), :] = cum.reshape(pb * 128, 128)
        beta = jax.nn.sigmoid(sm)
        for i in range(pb):
            slab_t = jnp.where(beta_lane, beta[i * 128:(i + 1) * 128], cum[i]).T
            for half in range(2):
                lo = half * c64
                table = []
                for first in (SM_A, SM_A + GDN_HEADS, SM_B, SM_B + GDN_HEADS):
                    table.append(jnp.concatenate(
                        [slab_t[first + h:first + h + 1, lo:lo + c64] for h in range(GDN_HEADS)], axis=1))
                for d, edge in ((0, lo + c64 - 1), (1, lo)):
                    first = SM_A + d * GDN_HEADS
                    table.append(jnp.concatenate(
                        [jnp.broadcast_to(slab_t[first + h:first + h + 1, edge:edge + 1], (1, c64))
                         for h in range(GDN_HEADS)], axis=1))
                table.append(jnp.zeros((2, w4), F32))
                rows_s[pl.ds(pl.multiple_of((sb * pb + i) * 16 + half * 8, 8), 8), :] = (
                    jnp.concatenate(table, axis=0))
        return carry

    lax.fori_loop(0, nsc // pb, prep, 0)

    ri = _iota2((c64, c64), 0)
    ci = _iota2((c64, c64), 1)
    incl = (ri >= ci, ri <= ci)
    strict = (ri > ci, ri < ci)

    nb = 4

    def local(cb):
        chunks = []
        for cc in range(nb):
            c = cb * nb + cc
            base = _aligned(c * c64, c64)
            gin = gin_ref[0, pl.ds(base, c64), :]
            gc = gc_s[pl.ds(base, c64), :]
            rows = rows_s[pl.ds(_aligned(c * 8, 8), 8), :]
            qs = [gin[:, h * HD:(h + 1) * HD] for h in range(GDN_HEADS)]
            ks = [gin[:, w4 + h * HD:w4 + (h + 1) * HD] for h in range(GDN_HEADS)]
            vs = [gin[:, 2 * w4 + h * HD:2 * w4 + (h + 1) * HD] for h in range(GDN_HEADS)]
            chunks.append((c, base, gc, rows, qs, ks, vs))
        kq = _bmm_nt(jnp.stack([jnp.concatenate([ch[5][h], ch[4][h]], axis=0)
                                for ch in chunks for h in range(GDN_HEADS)]),
                     jnp.stack([ch[5][h] for ch in chunks for h in range(GDN_HEADS)]))
        yield
        mb_l, rhs_l, lhs_l, qd_l = [], [], [], []
        for ci, (c, base, gc, rows, qs, ks, vs) in enumerate(chunks):
            kts = [k.astype(F32).T for k in ks]
            for d in range(2):
                for h in range(GDN_HEADS):
                    sl = slice(h * HD, (h + 1) * HD)
                    colg = SM_A + d * GDN_HEADS + h
                    kqh = kq[ci * GDN_HEADS + h]
                    gmat = jnp.broadcast_to(gc[:, colg:colg + 1], (c64, c64))
                    gcr, br, endr = rows[d:d + 1, sl], rows[2 + d:3 + d, sl], rows[4 + d:5 + d, sl]
                    dec = jnp.exp(jnp.where(incl[d], gmat - gcr, NEG))
                    mb_l.append(jnp.where(strict[d], kqh[:c64] * dec, 0.0) * br)
                    eg = jnp.exp(gmat)
                    kdtb = kts[h] * (jnp.exp(endr - gcr) * br)
                    lhs_l.append(jnp.concatenate([kdtb, kqh[c64:] * dec * br], axis=0))
                    rhs_l.append(jnp.concatenate([vs[h], ks[h] * eg], axis=1))
                    qd_l.append(qs[h] * eg)
        z = None
        for z in _unit_tri_inverse_stages(jnp.stack(mb_l)):
            if z is None:
                yield
        zr = _bmm(z, jnp.stack(rhs_l))
        yield
        a12 = _bmm(jnp.stack(lhs_l), zr)
        yield
        for ci, (c, base, gc, rows, qs, ks, vs) in enumerate(chunks):
            for d in range(2):
                slot = c * 2 + d
                ids = [ci * 2 * GDN_HEADS + d * GDN_HEADS + h for h in range(GDN_HEADS)]
                mq = [jnp.concatenate([-a12[n][:c64, HD:], qd_l[n] - a12[n][c64:, HD:]], axis=0) for n in ids]
                mq_s[slot] = jnp.concatenate(mq, axis=1).astype(BF16)
                n_s[slot] = jnp.concatenate([a12[n][:c64, :HD] for n in ids], axis=1)
                cd_s[slot] = jnp.broadcast_to(jnp.exp(rows[4 + d:5 + d]), (8, w4))
                oc = jnp.concatenate([a12[n][c64:, :HD] for n in ids], axis=1)
                if d == 0:
                    of_s[pl.ds(base, c64), :] = oc
                else:
                    ob_s[pl.ds(base, c64), :] = oc

    t_all = q_ref.shape[1]
    heads_per_kv = ATT_QH // ATT_KVH

    def attend(tile, nkeys):
        rows = pl.ds(_aligned(tile * TM, TM), TM)
        qt = q_ref[0, rows, :]
        for g in range(ATT_KVH):
            k = kv_ref[0, g, :nkeys, :HD]
            va = kv_ref[0, g, :nkeys, 2 * HD:]
            outs = []
            for jj in range(heads_per_kv):
                h = g * heads_per_kv + jj
                s = _mm_nt(qt[:, h * HD:(h + 1) * HD], k)
                yield
                p = jnp.exp2(s - jnp.max(s, axis=-1, keepdims=True))
                acc = _mm(p, va)
                yield
                outs.append(acc[:, :HD] / acc[:, HD:])
            outs.append(jnp.zeros((TM, HD), F32))
            att_ref[0, g, rows, :] = jnp.concatenate(outs, axis=1).astype(BF16)

    def trip(j, nkeys):
        streams = [local(j)] + ([attend(j, nkeys)] if nkeys else [])
        while streams:
            for st in list(streams):
                try:
                    next(st)
                except StopIteration:
                    streams.remove(st)

    trip(0, CTX if ctx_attn else 0)
    if not ctx_attn:
        att_ref[0, :, :TM, :] = jnp.zeros((ATT_KVH, TM, 256), BF16)

    def later_trip(j, carry):
        trip(j, t_all)
        return carry

    lax.fori_loop(1, nch // nb, later_trip, 0)

    def step(t, states):
        cf = t
        cb = jnp.where(t < ctx_chunks, ctx_chunks - 1 - t, nch - 1 + ctx_chunks - t)
        new_states = []
        for d, c in ((0, cf), (1, cb)):
            slot = c * 2 + d
            s = states[d]
            r = jnp.dot(mq_s[slot], _bd4(s), preferred_element_type=F32)
            new_states.append(s * cd_s[slot][0:1] + r[:c64] + n_s[slot])
            rows = pl.ds(pl.multiple_of(c * c64, c64), c64)
            if d == 0:
                of_s[rows, :] = of_s[rows, :] + r[c64:]
            else:
                ob_s[rows, :] = ob_s[rows, :] + r[c64:]
        return tuple(new_states)

    zero = jnp.zeros((c64, w4), F32)
    lax.fori_loop(0, nch, step, (zero, zero))
    o_ref[0] = (of_s[...] + ob_s[...]).astype(BF16)


def _gdn_attention(gdn_in, small, prm, q, kv, layer, ctx_attn):
    bsz, t, _ = gdn_in.shape
    nch = t // GDN_CHUNK
    assert nch // 4 == t // TM
    return pl.pallas_call(
        functools.partial(_gdn_attn_kernel, nch, CTX // GDN_CHUNK, ctx_attn),
        out_shape=[jax.ShapeDtypeStruct((bsz, t, GDN_W), BF16),
                   jax.ShapeDtypeStruct((bsz, ATT_KVH, t, 256), BF16)],
        grid=(bsz,),
        in_specs=[pl.BlockSpec((1, t, 768), lambda b: (b, 0, 0)),
                  pl.BlockSpec((1, t, 128), lambda b: (b, 0, 0)),
                  pl.BlockSpec((1, 8, 128), lambda b: (layer, 0, 0)),
                  pl.BlockSpec((1, t, ATT_W), lambda b: (b, 0, 0)),
                  pl.BlockSpec((1, ATT_KVH, t, 256), lambda b: (b, 0, 0, 0))],
        out_specs=[pl.BlockSpec((1, t, GDN_W), lambda b: (b, 0, 0)),
                   pl.BlockSpec((1, ATT_KVH, t, 256), lambda b: (b, 0, 0, 0))],
        scratch_shapes=[pltpu.VMEM((t, 128), F32),
                        pltpu.VMEM((nch * 8, GDN_W), F32),
                        pltpu.VMEM((nch * 2, 128, GDN_W), BF16),
                        pltpu.VMEM((nch * 2, GDN_CHUNK, GDN_W), F32),
                        pltpu.VMEM((nch * 2, 8, GDN_W), F32),
                        pltpu.VMEM((t, GDN_W), F32),
                        pltpu.VMEM((t, GDN_W), F32)],
        compiler_params=pltpu.CompilerParams(dimension_semantics=("arbitrary",),
                                             vmem_limit_bytes=VMEM_LIMIT),
        name="gdn_attention",
    )(gdn_in, small, prm, q, kv)


def _out_kernel(first_tile, xa_ref, x_ref, y_ref, o_ref, a_ref, zs_ref, mod_ref, sw_ref, gw_ref, w_ref, out_ref):
    zs = zs_ref[0].astype(F32)
    t = y_ref[0].astype(F32) * zs[:, :SSD_W]
    ssd = t * lax.rsqrt(jnp.mean(t * t, axis=-1, keepdims=True) + EPS) * sw_ref[0]
    o = o_ref[0].astype(F32)
    gdn = o * lax.rsqrt(_group_sum64(o * o) * (1.0 / HD) + EPS) * gw_ref[0] * zs[:, SSD_W:SSD_W + GDN_W]
    a = a_ref[0]
    att = jnp.concatenate([a[g][:, :HD * ATT_QH // ATT_KVH] for g in range(ATT_KVH)], axis=1)
    att = att * zs[:, SSD_W + GDN_W:]
    mix = jnp.concatenate([ssd, gdn, att], axis=1).astype(BF16)
    gate = mod_ref[0][:, 2 * D_MODEL:]
    x = jnp.where(pl.program_id(1) + first_tile == 0, xa_ref[0], x_ref[0])
    out_ref[0] = x + gate * jnp.dot(mix, w_ref[0], preferred_element_type=F32)


def _out_projection(xa, xb, off, layer, y, o, a, zs, mod3, ssd_nw, gdn_nw, w_out, first_tile):
    bsz, t, _ = y.shape
    nt = t // TM - first_tile
    row = lambda b, i: (b, i + first_tile, 0)
    lay = lambda b, i: (layer, 0, 0)
    return pl.pallas_call(
        functools.partial(_out_kernel, first_tile),
        out_shape=jax.ShapeDtypeStruct((bsz, nt * TM, D_MODEL), F32),
        grid=(bsz, nt),
        in_specs=[pl.BlockSpec((1, TM, D_MODEL), lambda b, i: (b, 0, 0)),
                  pl.BlockSpec((1, TM, D_MODEL), lambda b, i: (b, jnp.maximum(i + first_tile - off, 0), 0)),
                  pl.BlockSpec((1, TM, SSD_W), row),
                  pl.BlockSpec((1, TM, GDN_W), row),
                  pl.BlockSpec((1, ATT_KVH, TM, 256), lambda b, i: (b, 0, i + first_tile, 0)),
                  pl.BlockSpec((1, TM, D_MODEL), row),
                  pl.BlockSpec((1, 1, 3 * D_MODEL),
                               lambda b, i: (jnp.where(i + first_tile == 0, bsz, b), 0, 0)),
                  pl.BlockSpec((1, 1, SSD_W), lay),
                  pl.BlockSpec((1, 1, GDN_W), lay),
                  pl.BlockSpec((1, D_MODEL, D_MODEL), lay)],
        out_specs=pl.BlockSpec((1, TM, D_MODEL), lambda b, i: (b, i, 0)),
        compiler_params=pltpu.CompilerParams(dimension_semantics=("arbitrary", "arbitrary"),
                                             vmem_limit_bytes=VMEM_LIMIT),
        name="out_projection",
    )(xa, xb, y, o, a, zs, mod3, ssd_nw, gdn_nw, w_out)


_REST_SEGS = [(1408, 384), (1804, 256), (2076, 384), (2460, 128), (2588, 128), (2716, 384),
              (1792, 12), (2060, 8), (2068, 8)]


def _relayout_kernel(w_ref, wc_ref, wr_ref):
    w = w_ref[0]
    wc_ref[0] = w[:, :CONV_DIM].astype(BF16)
    pad = jnp.zeros((w.shape[0], REST_DIM - sum(n for _, n in _REST_SEGS)), F32)
    wr_ref[0] = jnp.concatenate([w[:, a:a + n] for a, n in _REST_SEGS] + [pad], axis=1).astype(BF16)


def _relayout_w_in(w_in):
    depth, d, n = w_in.shape
    rows = 128
    return pl.pallas_call(
        _relayout_kernel,
        out_shape=[jax.ShapeDtypeStruct((depth, d, CONV_DIM), BF16),
                   jax.ShapeDtypeStruct((depth, d, REST_DIM), BF16)],
        grid=(depth, d // rows),
        in_specs=[pl.BlockSpec((1, rows, n), lambda l, i: (l, i, 0))],
        out_specs=[pl.BlockSpec((1, rows, CONV_DIM), lambda l, i: (l, i, 0)),
                   pl.BlockSpec((1, rows, REST_DIM), lambda l, i: (l, i, 0))],
        compiler_params=pltpu.CompilerParams(dimension_semantics=("arbitrary", "arbitrary"),
                                             vmem_limit_bytes=VMEM_LIMIT),
        name="relayout_w_in",
    )(w_in)


def _prepare_params(w_in, conv_w, conv_b, ssd_A_log, ssd_dt_bias, gdn_A_log, gdn_dt_bias):
    depth = w_in.shape[0]
    w_conv, w_rest = _relayout_w_in(w_in)
    cw = conv_w
    cb = conv_b[:, None, :]

    def lanes(a, b):
        v = jnp.concatenate([a.reshape(depth, -1), b.reshape(depth, -1)], axis=1)
        return jnp.pad(v, ((0, 0), (0, 128 - v.shape[1])))

    prm = jnp.stack([lanes(ssd_A_log, gdn_A_log), lanes(ssd_dt_bias, gdn_dt_bias)], axis=1)
    prm = jnp.pad(prm, ((0, 0), (0, 6), (0, 0)))
    return w_conv, w_rest, cw, cb, prm


def _rope_tables(t):
    f32 = np.float32
    pos = np.arange(t - CTX)
    n_freq = HD // 4
    freqs = np.power(f32(ROPE_THETA), -np.arange(n_freq, dtype=f32) / f32(n_freq)).astype(f32)
    ang_r = ((pos // GRID_W).astype(f32)[:, None] * freqs).astype(f32)
    ang_c = ((pos % GRID_W).astype(f32)[:, None] * freqs).astype(f32)
    cos = np.concatenate([np.cos(ang_r)] * 2 + [np.cos(ang_c)] * 2, axis=1)
    sin = np.concatenate([-np.sin(ang_r), np.sin(ang_r), -np.sin(ang_c), np.sin(ang_c)], axis=1)
    cos = np.concatenate([np.ones((CTX, HD), f32), cos], axis=0).astype(f32)
    sin = np.concatenate([np.zeros((CTX, HD), f32), sin], axis=0).astype(f32)
    return jnp.asarray(np.tile(cos, (1, 2))), jnp.asarray(np.tile(sin, (1, 2)))


def kernel(x, c, ctx, c_ctx, norm_w, w_mod, b_mod, w_in, conv_w, conv_b, ssd_A_log, ssd_dt_bias, ssd_D,
           ssd_norm_w, gdn_A_log, gdn_dt_bias, gdn_norm_w, q_norm_w, k_norm_w, w_out):
    bsz = x.shape[0]
    t = CTX + x.shape[1]
    c_all = jnp.concatenate([c, c_ctx[None, :], jnp.zeros((7, D_MODEL), F32)], axis=0)
    cos_t, sin_t = _rope_tables(t)
    w_conv, w_rest, cw, cb, prm = _prepare_params(w_in, conv_w, conv_b, ssd_A_log, ssd_dt_bias,
                                                  gdn_A_log, gdn_dt_bias)
    qk_w = jnp.concatenate([jnp.tile(q_norm_w, (1, ATT_QH)), jnp.tile(k_norm_w, (1, ATT_KVH))], axis=1)[:, None]
    dskip = jnp.repeat(ssd_D, HD, axis=1)[:, None]
    gdn_nw = jnp.tile(gdn_norm_w, (1, GDN_HEADS))[:, None]
    norm_w3, ssd_nw = norm_w[:, None], ssd_norm_w[:, None]
    w_out16 = w_out.astype(BF16)

    xa, xb, off = ctx, x, 1
    for layer in range(DEPTH):
        first_tile = 1 if layer == DEPTH - 1 else 0
        mod3 = _modulation(c_all, w_mod, b_mod, layer).reshape(bsz + 8, 1, 3 * D_MODEL)
        ssd_in, gdn_in, zs, q, kv, small = _in_projection(
            xa, xb, off, layer, mod3, norm_w3, w_conv, w_rest, cw, cb, qk_w, cos_t, sin_t)
        y = _ssd(ssd_in, small, prm, dskip, layer)
        o, a = _gdn_attention(gdn_in, small, prm, q, kv, layer, first_tile == 0)
        out = _out_projection(xa, xb, off, layer, y, o, a, zs, mod3, ssd_nw, gdn_nw, w_out16, first_tile)
        xa, xb, off = out, out, 0
    return out
```

```python
import functools
import math

import jax
import jax.numpy as jnp
import numpy as np
from jax import lax
from jax.experimental import pallas as pl
from jax.experimental.pallas import tpu as pltpu

F32 = jnp.float32
BF16 = jnp.bfloat16

D_MODEL = 1024
CTX = 256
GRID_W = 64
EPS = 1e-6
DEPTH = 2

HD = 64
SSD_HEADS = 6
SSD_W = SSD_HEADS * HD
SSD_CHUNK = 128
GDN_HEADS = 4
GDN_W = GDN_HEADS * HD
GDN_CHUNK = 64
ATT_QH = 6
ATT_KVH = 2
ATT_W = ATT_QH * HD
ROPE_THETA = 10000.0
Q_PRESCALE = (HD ** -0.5) * math.log2(math.e)

CONV_DIM = 1408
REST_DIM = 1792
TM = 256
HALO = 8
NEG = -1e30
VMEM_LIMIT = 56 * 1024 * 1024

SM_DT = 0
SM_A = 12
SM_B = 20


def _mm(a, b):
    return jnp.dot(a.astype(BF16), b.astype(BF16), preferred_element_type=F32)


def _mm_nt(a, b):
    return lax.dot_general(a.astype(BF16), b.astype(BF16), (((1,), (1,)), ((), ())),
                           preferred_element_type=F32)


def _split3(a):
    h = a.astype(BF16)
    r = a - h.astype(F32)
    m = r.astype(BF16)
    l = (r - m.astype(F32)).astype(BF16)
    return h, m, l


def _mm_exact_lhs(a_bf16, b):
    h, m, l = _split3(b)
    d = functools.partial(jnp.dot, preferred_element_type=F32)
    return d(a_bf16, h) + (d(a_bf16, m) + d(a_bf16, l))


def _silu(x):
    return x * jax.nn.sigmoid(x)


def _softplus(x):
    return jnp.maximum(x, 0.0) + jnp.log1p(jnp.exp(-jnp.abs(x)))


def _iota2(shape, dim):
    return lax.broadcasted_iota(jnp.int32, shape, dim)


def _aligned(x, m):
    return x if isinstance(x, int) else pl.multiple_of(x, m)


def _group_sum64(xx):
    r = jnp.right_shift(_iota2((128, 128), 0), 6)
    c = jnp.right_shift(_iota2((128, 128), 1), 6)
    g = (r == c).astype(BF16)
    xb = xx.astype(BF16)
    outs = [jnp.dot(xb[:, t * 128:(t + 1) * 128], g, preferred_element_type=F32)
            for t in range(xx.shape[1] // 128)]
    return outs[0] if len(outs) == 1 else jnp.concatenate(outs, axis=1)


def _mod_kernel(c_ref, w_ref, b_ref, o_ref):
    o_ref[...] = _mm(_silu(c_ref[...]), w_ref[0]) + b_ref[0]


def _modulation(c_all, w_mod, b_mod, layer):
    n = w_mod.shape[2]
    bn = 768
    rows = c_all.shape[0]
    return pl.pallas_call(
        _mod_kernel,
        out_shape=jax.ShapeDtypeStruct((rows, n), F32),
        grid=(n // bn,),
        in_specs=[pl.BlockSpec((rows, D_MODEL), lambda j: (0, 0)),
                  pl.BlockSpec((1, D_MODEL, bn), lambda j: (layer, 0, j)),
                  pl.BlockSpec((1, 1, bn), lambda j: (layer, 0, j))],
        out_specs=pl.BlockSpec((rows, bn), lambda j: (0, j)),
        compiler_params=pltpu.CompilerParams(dimension_semantics=("arbitrary",),
                                             vmem_limit_bytes=VMEM_LIMIT),
        name="modulation",
    )(c_all, w_mod, b_mod.reshape(b_mod.shape[0], 1, n))


def _proj_kernel(nt, xa_ref, xm_ref, xp_ref, xn_ref, mod_ref, nw_ref, wc_ref, wr_ref, cw_ref, cb_ref,
                 qkw_ref, cos_ref, sin_ref,
                 ssd_ref, gdn_ref, zs_ref, q_ref, kv_ref, small_ref):
    i = pl.program_id(1)
    m = mod_ref[0]
    gain = nw_ref[0] * (1.0 + m[:, D_MODEL:2 * D_MODEL])
    shift = m[:, :D_MODEL]

    xm = jnp.where(i == 0, xa_ref[0], xm_ref[0])
    xe = jnp.concatenate([xm, xp_ref[0], xn_ref[0]], axis=0)
    ms = jnp.mean(xe * xe, axis=-1, keepdims=True)
    he = (xe * lax.rsqrt(ms + EPS) * gain + shift).astype(BF16)

    p = jnp.dot(he, wc_ref[0], preferred_element_type=F32)
    r = jnp.dot(he[:TM], wr_ref[0], preferred_element_type=F32)

    pm = p[:TM]
    seg_first = i <= 1
    seg_last = jnp.logical_or(i == 0, i == nt - 1)
    prev_row = jnp.where(seg_first, 0.0, p[TM + HALO - 1:TM + HALO])
    next_row = jnp.where(seg_last, 0.0, p[TM + HALO:TM + HALO + 1])
    rows = _iota2((TM, 1), 0)
    pm1 = jnp.where(rows == 0, prev_row, pltpu.roll(pm, 1, axis=0))
    pp1 = jnp.where(rows == TM - 1, next_row, pltpu.roll(pm, TM - 1, axis=0))
    cw = cw_ref[0]
    conv = cw[0:1] * pm1 + cw[1:2] * pm + cw[2:3] * pp1 + cb_ref[0]
    co = _silu(conv)

    ssd_ref[0] = co[:, :640].astype(BF16)
    gqk = co[:, 640:1152]
    gqk = gqk * lax.rsqrt(_group_sum64(gqk * gqk) + EPS)
    gdn_ref[0, :, :256] = (gqk[:, :256] * (HD ** -0.5)).astype(BF16)
    gdn_ref[0, :, 256:512] = gqk[:, 256:].astype(BF16)
    gdn_ref[0, :, 512:] = co[:, 1152:1408].astype(BF16)

    z1 = r[:, :640]
    z2 = r[:, 1280:1664]
    zs_ref[0, :, :640] = _silu(z1).astype(BF16)
    zs_ref[0, :, 640:] = _silu(z2).astype(BF16)

    aqk = r[:, 640:1152]
    aqk = aqk * lax.rsqrt(_group_sum64(aqk * aqk) * (1.0 / HD) + EPS) * qkw_ref[0]
    lane = _iota2((1, 512), 1)
    swapped = jnp.where(jnp.bitwise_and(lane, 31) < 16,
                        pltpu.roll(aqk, 512 - 16, axis=1), pltpu.roll(aqk, 16, axis=1))
    cos = jnp.concatenate([cos_ref[...]] * 4, axis=1)
    sin = jnp.concatenate([sin_ref[...]] * 4, axis=1)
    aqk = aqk * cos + swapped * sin
    q_ref[0] = (aqk[:, :384] * Q_PRESCALE).astype(BF16)
    ones = jnp.ones((TM, HD), F32)
    zeros = jnp.zeros((TM, HD), F32)
    for g in range(ATT_KVH):
        kv_ref[0, g] = jnp.concatenate([aqk[:, 384 + g * HD:384 + (g + 1) * HD], zeros,
                                        r[:, 1152 + g * HD:1152 + (g + 1) * HD], ones], axis=1).astype(BF16)
    small_ref[0] = r[:, 1664:1792]


def _in_projection(xa, xb, off, layer, mod3, norm_w, w_conv, w_rest, conv_w, conv_b, qk_w, cos_t, sin_t):
    bsz = xb.shape[0]
    nt = xb.shape[1] // TM + off
    t = nt * TM
    nblk8 = xb.shape[1] // HALO
    per_tile = TM // HALO
    row = lambda b, i: (b, i, 0)
    lay = lambda b, i: (layer, 0, 0)
    outs = [jax.ShapeDtypeStruct((bsz, t, w), dt)
            for w, dt in ((640, BF16), (768, BF16), (1024, BF16), (384, BF16), (128, F32))]
    outs.insert(4, jax.ShapeDtypeStruct((bsz, ATT_KVH, t, 256), BF16))
    return pl.pallas_call(
        functools.partial(_proj_kernel, nt),
        out_shape=outs,
        grid=(bsz, nt),
        in_specs=[
            pl.BlockSpec((1, TM, D_MODEL), lambda b, i: (b, 0, 0)),
            pl.BlockSpec((1, TM, D_MODEL), lambda b, i: (b, jnp.maximum(i - off, 0), 0)),
            pl.BlockSpec((1, HALO, D_MODEL), lambda b, i: (b, jnp.maximum((i - off) * per_tile - 1, 0), 0)),
            pl.BlockSpec((1, HALO, D_MODEL),
                         lambda b, i: (b, jnp.minimum((i - off + 1) * per_tile, nblk8 - 1), 0)),
            pl.BlockSpec((1, 1, 3 * D_MODEL), lambda b, i: (jnp.where(i == 0, bsz, b), 0, 0)),
            pl.BlockSpec((1, 1, D_MODEL), lay),
            pl.BlockSpec((1, D_MODEL, CONV_DIM), lay),
            pl.BlockSpec((1, D_MODEL, REST_DIM), lay),
            pl.BlockSpec((1, 3, CONV_DIM), lay),
            pl.BlockSpec((1, 1, CONV_DIM), lay),
            pl.BlockSpec((1, 1, 512), lay),
            pl.BlockSpec((TM, 128), lambda b, i: (i, 0)),
            pl.BlockSpec((TM, 128), lambda b, i: (i, 0)),
        ],
        out_specs=[pl.BlockSpec((1, TM, 640), row), pl.BlockSpec((1, TM, 768), row),
                   pl.BlockSpec((1, TM, 1024), row), pl.BlockSpec((1, TM, 384), row),
                   pl.BlockSpec((1, ATT_KVH, TM, 256), lambda b, i: (b, 0, i, 0)),
                   pl.BlockSpec((1, TM, 128), row)],
        compiler_params=pltpu.CompilerParams(dimension_semantics=("arbitrary", "arbitrary"),
                                             vmem_limit_bytes=VMEM_LIMIT),
        name="in_projection",
    )(xa, xb, xb, xb, mod3, norm_w, w_conv, w_rest, conv_w, conv_b, qk_w, cos_t, sin_t)


def _ssd_kernel(nc, ctx_chunks, xin_ref, small_ref, prm_ref, dskip_ref, y_ref,
                cum_s, cumt_s, dtt_s, st_s, sin_s):
    q = SSD_CHUNK
    hpg = SSD_HEADS // 2
    ri = _iota2((q, q), 0)
    ci = _iota2((q, q), 1)
    ltri = (ri >= ci).astype(BF16)
    utri = (ri <= ci).astype(BF16)
    lane = _iota2((1, 128), 1)
    fwd_lane = lane < SSD_HEADS

    a_row = -jnp.exp(prm_ref[0, 0:1, :])
    bias_row = prm_ref[0, 1:2, :]

    pb = 6
    tri = jnp.broadcast_to(jnp.concatenate([ltri, utri], axis=0)[None], (pb, 2 * q, q))

    def prep(cb, carry):
        base = pl.multiple_of(cb * (pb * q), q)
        dt = _softplus(small_ref[0, pl.ds(base, pb * q), :] + bias_row)
        pieces = _split3((dt * a_row).reshape(pb, q, 128))
        dot = lambda x: lax.dot_general(tri, x, (((2,), (1,)), ((0,), (0,))), preferred_element_type=F32)
        sums = dot(pieces[0]) + (dot(pieces[1]) + dot(pieces[2]))
        cum = jnp.where(fwd_lane, sums[:, :q], sums[:, q:])
        cum_s[pl.ds(base, pb * q), :] = cum.reshape(pb * q, 128)
        for i in range(pb):
            rows = pl.ds(pl.multiple_of(base + i * q, q), q)
            cumt_s[rows, :] = cum[i].T
            dtt_s[rows, :] = dt[i * q:(i + 1) * q].T
        return carry

    lax.fori_loop(0, nc // pb, prep, 0)

    def end_of(cumt, d, col):
        return cumt[col:col + 1, q - 1:q] if d == 0 else cumt[col:col + 1, 0:1]

    nb = 2

    def states(cb, carry):
        lhs, rhs = [], []
        for cc in range(nb):
            base = pl.multiple_of((cb * nb + cc) * q, q)
            xin = xin_ref[0, pl.ds(base, q), :]
            cumt = cumt_s[pl.ds(base, q), :]
            dtt = dtt_s[pl.ds(base, q), :]
            for g in range(2):
                bgt = xin[:, 384 + g * HD:384 + (g + 1) * HD].astype(F32).T
                for d in range(2):
                    for hg in range(hpg):
                        h = g * hpg + hg
                        col = d * SSD_HEADS + h
                        w_row = jnp.exp(end_of(cumt, d, col) - cumt[col:col + 1, :]) * dtt[col:col + 1, :]
                        lhs.append(bgt * w_row)
                        rhs.append(xin[:, h * HD:(h + 1) * HD])
        st = _bmm(jnp.stack(lhs), jnp.stack(rhs))
        st_s[pl.ds(cb * (12 * nb), 12 * nb)] = st
        return carry

    lax.fori_loop(0, nc // nb, states, 0)

    orders = (list(range(nc)),
              list(range(ctx_chunks - 1, -1, -1)) + list(range(nc - 1, ctx_chunks - 1, -1)))
    for g in range(2):
        for d in range(2):
            for hg in range(hpg):
                col = d * SSD_HEADS + g * hpg + hg
                s = jnp.zeros((HD, HD), F32)
                for c in orders[d]:
                    slot = c * 12 + g * 6 + d * 3 + hg
                    sin_s[slot] = s
                    row = c * q + (q - 1 if d == 0 else 0)
                    s = s * jnp.exp(cum_s[row:row + 1, col:col + 1]) + st_s[slot]

    lower = ri >= ci
    upper = ri <= ci
    dsk = dskip_ref[0]

    def outputs(cb, carry):
        sin = sin_s[pl.ds(cb * (12 * nb), 12 * nb)]
        ws, xs, cstack, es, scs = [], [], [], [], []
        per_chunk = []
        for cc in range(nb):
            base = pl.multiple_of((cb * nb + cc) * q, q)
            xin = xin_ref[0, pl.ds(base, q), :]
            per_chunk.append((base, xin, cum_s[pl.ds(base, q), :], cumt_s[pl.ds(base, q), :],
                              dtt_s[pl.ds(base, q), :]))
        sc = _bmm_nt(jnp.stack([xin[:, 512 + g * HD:512 + (g + 1) * HD] for _, xin, _, _, _ in per_chunk
                                for g in range(2)]),
                     jnp.stack([xin[:, 384 + g * HD:384 + (g + 1) * HD] for _, xin, _, _, _ in per_chunk
                                for g in range(2)]))
        for cc, (base, xin, cum, cumt, dtt) in enumerate(per_chunk):
            e_chunk = {}
            for g in range(2):
                cg = xin[:, 512 + g * HD:512 + (g + 1) * HD]
                for hg in range(hpg):
                    h = g * hpg + hg
                    w = None
                    for d in range(2):
                        col = d * SSD_HEADS + h
                        gmat = jnp.broadcast_to(cum[:, col:col + 1], (q, q))
                        lm = jnp.exp(jnp.where(lower if d == 0 else upper, gmat - cumt[col:col + 1, :], NEG))
                        wd = lm * dtt[col:col + 1, :]
                        w = wd if w is None else w + wd
                        e_chunk[g * 6 + d * 3 + hg] = jnp.exp(gmat[:, :HD])
                    ws.append(w * sc[cc * 2 + g])
                    xs.append(xin[:, h * HD:(h + 1) * HD])
            es.extend(e_chunk[i] for i in range(12))
            cstack.extend(xin[:, 512 + (i // 6) * HD:512 + (i // 6 + 1) * HD] for i in range(12))
        ydiag = _bmm(jnp.stack(ws), jnp.stack(xs))
        yoff = _bmm(jnp.stack(cstack), sin)
        for cc, (base, xin, cum, cumt, dtt) in enumerate(per_chunk):
            outs = []
            for h in range(SSD_HEADS):
                g, hg = divmod(h, hpg)
                y = ydiag[cc * SSD_HEADS + h] + dsk[:, h * HD:(h + 1) * HD] * xs[cc * SSD_HEADS + h]
                for d in range(2):
                    i = cc * 12 + g * 6 + d * 3 + hg
                    y = y + yoff[i] * es[i]
                outs.append(y)
            y_ref[0, pl.ds(base, q), :] = jnp.concatenate(outs, axis=1).astype(BF16)
        return carry

    lax.fori_loop(0, nc // nb, outputs, 0)


def _ssd(ssd_in, small, prm, dskip, layer):
    bsz, t, _ = ssd_in.shape
    nc = t // SSD_CHUNK
    return pl.pallas_call(
        functools.partial(_ssd_kernel, nc, CTX // SSD_CHUNK),
        out_shape=jax.ShapeDtypeStruct((bsz, t, SSD_W), BF16),
        grid=(bsz,),
        in_specs=[pl.BlockSpec((1, t, 640), lambda b: (b, 0, 0)),
                  pl.BlockSpec((1, t, 128), lambda b: (b, 0, 0)),
                  pl.BlockSpec((1, 8, 128), lambda b: (layer, 0, 0)),
                  pl.BlockSpec((1, 1, SSD_W), lambda b: (layer, 0, 0))],
        out_specs=pl.BlockSpec((1, t, SSD_W), lambda b: (b, 0, 0)),
        scratch_shapes=[pltpu.VMEM((t, 128), F32), pltpu.VMEM((t, 128), F32), pltpu.VMEM((t, 128), F32),
                        pltpu.VMEM((nc * 12, HD, HD), F32),
                        pltpu.VMEM((nc * 12, HD, HD), F32)],
        compiler_params=pltpu.CompilerParams(dimension_semantics=("arbitrary",),
                                             vmem_limit_bytes=VMEM_LIMIT),
        name="ssd",
    )(ssd_in, small, prm, dskip)


def _bd4(x):
    xb = x.astype(BF16)
    blk = jnp.right_shift(_iota2((1, 4 * HD), 1), 6)
    zero = jnp.zeros_like(xb)
    return jnp.concatenate([jnp.where(blk == h, xb, zero) for h in range(GDN_HEADS)], axis=0)


def _bmm(a, b):
    return lax.dot_general(a.astype(BF16), b.astype(BF16), (((2,), (1,)), ((0,), (0,))),
                           preferred_element_type=F32)


def _bmm_nt(a, b):
    return lax.dot_general(a.astype(BF16), b.astype(BF16), (((2,), (2,)), ((0,), (0,))),
                           preferred_element_type=F32)


def _unit_tri_inverse_stages(a_strict):
    n = a_strict.shape[-1]
    ri = _iota2((1, n, n), 1)
    ci = _iota2((1, n, n), 2)

    def blk(shift):
        return jnp.right_shift(ri, shift) == jnp.right_shift(ci, shift)

    nd = jnp.where(blk(3), -a_strict, 0.0)
    p0 = (ri == ci).astype(F32) + nd
    q1 = _bmm(nd, nd)
    yield None
    t = _bmm(jnp.concatenate([p0, q1], axis=1), q1)
    yield None
    p1 = p0 + t[:, :n]
    inv = p1 + _bmm(p1, t[:, n:])
    yield None
    for shift in (4, 5, 6):
        e = jnp.where(jnp.logical_and(blk(shift), jnp.logical_not(blk(shift - 1))), a_strict, 0.0)
        ie = _bmm(inv, e)
        yield None
        inv = inv - _bmm(ie, inv)
        yield None
    yield inv


def _gdn_attn_kernel(nch, ctx_chunks, ctx_attn, gin_ref, small_ref, prm_ref, q_ref, kv_ref, o_ref, att_ref,
                     gc_s, rows_s, mq_s, n_s, cd_s, of_s, ob_s):
    c64 = GDN_CHUNK
    w4 = GDN_W
    nsc = nch // 2
    ri128 = _iota2((128, 128), 0)
    ci128 = _iota2((128, 128), 1)
    same = jnp.right_shift(ri128, 6) == jnp.right_shift(ci128, 6)
    lbd = jnp.logical_and(same, ri128 >= ci128).astype(BF16)
    ubd = jnp.logical_and(same, ri128 <= ci128).astype(BF16)
    lane = _iota2((1, 128), 1)
    fwd_lane = lane < SM_A + GDN_HEADS
    beta_lane = jnp.logical_and(lane >= SM_B, lane < SM_B + 2 * GDN_HEADS)

    a_row = -jnp.exp(prm_ref[0, 0:1, :])
    bias_row = prm_ref[0, 1:2, :]

    pb = 6
    tri = jnp.broadcast_to(jnp.concatenate([lbd, ubd], axis=0)[None], (pb, 256, 128))

    def prep(sb, carry):
        base = pl.multiple_of(sb * (pb * 128), 128)
        sm = small_ref[0, pl.ds(base, pb * 128), :]
        pieces = _split3((a_row * _softplus(sm + bias_row)).reshape(pb, 128, 128))
        dot = lambda x: lax.dot_general(tri, x, (((2,), (1,)), ((0,), (0,))), preferred_element_type=F32)
        sums = dot(pieces[0]) + (dot(pieces[1]) + dot(pieces[2]))
        cum = jnp.where(fwd_lane, sums[:, :128], sums[:, 128:])
        gc_s[pl.ds(base, pb * 128), :] = cum.reshape(pb * 128, 128)
        beta = jax.nn.sigmoid(sm)
        for i in range(pb):
            slab_t = jnp.where(beta_lane, beta[i * 128:(i + 1) * 128], cum[i]).T
            for half in range(2):
                lo = half * c64
                table = []
                for first in (SM_A, SM_A + GDN_HEADS, SM_B, SM_B + GDN_HEADS):
                    table.append(jnp.concatenate(
                        [slab_t[first + h:first + h + 1, lo:lo + c64] for h in range(GDN_HEADS)], axis=1))
                for d, edge in ((0, lo + c64 - 1), (1, lo)):
                    first = SM_A + d * GDN_HEADS
                    table.append(jnp.concatenate(
                        [jnp.broadcast_to(slab_t[first + h:first + h + 1, edge:edge + 1], (1, c64))
                         for h in range(GDN_HEADS)], axis=1))
                table.append(jnp.zeros((2, w4), F32))
                rows_s[pl.ds(pl.multiple_of((sb * pb + i) * 16 + half * 8, 8), 8), :] = (
                    jnp.concatenate(table, axis=0))
        return carry

    lax.fori_loop(0, nsc // pb, prep, 0)

    ri = _iota2((c64, c64), 0)
    ci = _iota2((c64, c64), 1)
    incl = (ri >= ci, ri <= ci)
    strict = (ri > ci, ri < ci)

    nb = 4

    def local(cb):
        chunks = []
        for cc in range(nb):
            c = cb * nb + cc
            base = _aligned(c * c64, c64)
            gin = gin_ref[0, pl.ds(base, c64), :]
            gc = gc_s[pl.ds(base, c64), :]
            rows = rows_s[pl.ds(_aligned(c * 8, 8), 8), :]
            qs = [gin[:, h * HD:(h + 1) * HD] for h in range(GDN_HEADS)]
            ks = [gin[:, w4 + h * HD:w4 + (h + 1) * HD] for h in range(GDN_HEADS)]
            vs = [gin[:, 2 * w4 + h * HD:2 * w4 + (h + 1) * HD] for h in range(GDN_HEADS)]
            chunks.append((c, base, gc, rows, qs, ks, vs))
        kq = _bmm_nt(jnp.stack([jnp.concatenate([ch[5][h], ch[4][h]], axis=0)
                                for ch in chunks for h in range(GDN_HEADS)]),
                     jnp.stack([ch[5][h] for ch in chunks for h in range(GDN_HEADS)]))
        yield
        mb_l, rhs_l, lhs_l, qd_l = [], [], [], []
        for ci, (c, base, gc, rows, qs, ks, vs) in enumerate(chunks):
            kts = [k.astype(F32).T for k in ks]
            for d in range(2):
                for h in range(GDN_HEADS):
                    sl = slice(h * HD, (h + 1) * HD)
                    colg = SM_A + d * GDN_HEADS + h
                    kqh = kq[ci * GDN_HEADS + h]
                    gmat = jnp.broadcast_to(gc[:, colg:colg + 1], (c64, c64))
                    gcr, br, endr = rows[d:d + 1, sl], rows[2 + d:3 + d, sl], rows[4 + d:5 + d, sl]
                    dec = jnp.exp(jnp.where(incl[d], gmat - gcr, NEG))
                    mb_l.append(jnp.where(strict[d], kqh[:c64] * dec, 0.0) * br)
                    eg = jnp.exp(gmat)
                    kdtb = kts[h] * (jnp.exp(endr - gcr) * br)
                    lhs_l.append(jnp.concatenate([kdtb, kqh[c64:] * dec * br], axis=0))
                    rhs_l.append(jnp.concatenate([vs[h], ks[h] * eg], axis=1))
                    qd_l.append(qs[h] * eg)
        z = None
        for z in _unit_tri_inverse_stages(jnp.stack(mb_l)):
            if z is None:
                yield
        zr = _bmm(z, jnp.stack(rhs_l))
        yield
        a12 = _bmm(jnp.stack(lhs_l), zr)
        yield
        for ci, (c, base, gc, rows, qs, ks, vs) in enumerate(chunks):
            for d in range(2):
                slot = c * 2 + d
                ids = [ci * 2 * GDN_HEADS + d * GDN_HEADS + h for h in range(GDN_HEADS)]
                mq = [jnp.concatenate([-a12[n][:c64, HD:], qd_l[n] - a12[n][c64:, HD:]], axis=0) for n in ids]
                mq_s[slot] = jnp.concatenate(mq, axis=1).astype(BF16)
                n_s[slot] = jnp.concatenate([a12[n][:c64, :HD] for n in ids], axis=1)
                cd_s[slot] = jnp.broadcast_to(jnp.exp(rows[4 + d:5 + d]), (8, w4))
                oc = jnp.concatenate([a12[n][c64:, :HD] for n in ids], axis=1)
                if d == 0:
                    of_s[pl.ds(base, c64), :] = oc
                else:
                    ob_s[pl.ds(base, c64), :] = oc

    t_all = q_ref.shape[1]
    heads_per_kv = ATT_QH // ATT_KVH

    def attend(tile, nkeys):
        rows = pl.ds(_aligned(tile * TM, TM), TM)
        qt = q_ref[0, rows, :]
        outs = []

        def finish(h, s):
            g = h // heads_per_kv
            p = jnp.exp2(s - jnp.max(s, axis=-1, keepdims=True))
            acc = _mm(p, kv_ref[0, g, :nkeys, 2 * HD:])
            outs.append(acc[:, :HD] / acc[:, HD:])
            if h % heads_per_kv == heads_per_kv - 1:
                tile_out = outs[-heads_per_kv:] + [jnp.zeros((TM, HD), F32)]
                att_ref[0, g, rows, :] = jnp.concatenate(tile_out, axis=1).astype(BF16)

        s_prev = None
        for h in range(ATT_QH):
            k = kv_ref[0, h // heads_per_kv, :nkeys, :HD]
            s = _mm_nt(qt[:, h * HD:(h + 1) * HD], k)
            yield
            if s_prev is not None:
                finish(h - 1, s_prev)
                yield
            s_prev = s
        finish(ATT_QH - 1, s_prev)
        yield

    def trip(j, nkeys):
        streams = [local(j)] + ([attend(j, nkeys)] if nkeys else [])
        while streams:
            for st in list(streams):
                try:
                    next(st)
                except StopIteration:
                    streams.remove(st)

    trip(0, CTX if ctx_attn else 0)
    if not ctx_attn:
        att_ref[0, :, :TM, :] = jnp.zeros((ATT_KVH, TM, 256), BF16)

    def later_trip(j, carry):
        trip(j, t_all)
        return carry

    lax.fori_loop(1, nch // nb, later_trip, 0)

    def step(t, states):
        cf = t
        cb = jnp.where(t < ctx_chunks, ctx_chunks - 1 - t, nch - 1 + ctx_chunks - t)
        new_states = []
        for d, c in ((0, cf), (1, cb)):
            slot = c * 2 + d
            s = states[d]
            r = jnp.dot(mq_s[slot], _bd4(s), preferred_element_type=F32)
            new_states.append(s * cd_s[slot][0:1] + r[:c64] + n_s[slot])
            rows = pl.ds(pl.multiple_of(c * c64, c64), c64)
            if d == 0:
                of_s[rows, :] = of_s[rows, :] + r[c64:]
            else:
                ob_s[rows, :] = ob_s[rows, :] + r[c64:]
        return tuple(new_states)

    zero = jnp.zeros((c64, w4), F32)
    lax.fori_loop(0, nch, step, (zero, zero))
    o_ref[0] = (of_s[...] + ob_s[...]).astype(BF16)


def _gdn_attention(gdn_in, small, prm, q, kv, layer, ctx_attn):
    bsz, t, _ = gdn_in.shape
    nch = t // GDN_CHUNK
    assert nch // 4 == t // TM
    return pl.pallas_call(
        functools.partial(_gdn_attn_kernel, nch, CTX // GDN_CHUNK, ctx_attn),
        out_shape=[jax.ShapeDtypeStruct((bsz, t, GDN_W), BF16),
                   jax.ShapeDtypeStruct((bsz, ATT_KVH, t, 256), BF16)],
        grid=(bsz,),
        in_specs=[pl.BlockSpec((1, t, 768), lambda b: (b, 0, 0)),
                  pl.BlockSpec((1, t, 128), lambda b: (b, 0, 0)),
                  pl.BlockSpec((1, 8, 128), lambda b: (layer, 0, 0)),
                  pl.BlockSpec((1, t, ATT_W), lambda b: (b, 0, 0)),
                  pl.BlockSpec((1, ATT_KVH, t, 256), lambda b: (b, 0, 0, 0))],
        out_specs=[pl.BlockSpec((1, t, GDN_W), lambda b: (b, 0, 0), pipeline_mode=pl.Buffered(1)),
                   pl.BlockSpec((1, ATT_KVH, t, 256), lambda b: (b, 0, 0, 0), pipeline_mode=pl.Buffered(1))],
        scratch_shapes=[pltpu.VMEM((t, 128), F32),
                        pltpu.VMEM((nch * 8, GDN_W), F32),
                        pltpu.VMEM((nch * 2, 128, GDN_W), BF16),
                        pltpu.VMEM((nch * 2, GDN_CHUNK, GDN_W), F32),
                        pltpu.VMEM((nch * 2, 8, GDN_W), F32),
                        pltpu.VMEM((t, GDN_W), F32),
                        pltpu.VMEM((t, GDN_W), F32)],
        compiler_params=pltpu.CompilerParams(dimension_semantics=("arbitrary",),
                                             vmem_limit_bytes=VMEM_LIMIT),
        name="gdn_attention",
    )(gdn_in, small, prm, q, kv)


def _out_kernel(first_tile, xa_ref, x_ref, y_ref, o_ref, a_ref, zs_ref, mod_ref, sw_ref, gw_ref, w_ref, out_ref):
    zs = zs_ref[0].astype(F32)
    t = y_ref[0].astype(F32) * zs[:, :SSD_W]
    ssd = t * lax.rsqrt(jnp.mean(t * t, axis=-1, keepdims=True) + EPS) * sw_ref[0]
    o = o_ref[0].astype(F32)
    gdn = o * lax.rsqrt(_group_sum64(o * o) * (1.0 / HD) + EPS) * gw_ref[0] * zs[:, SSD_W:SSD_W + GDN_W]
    a = a_ref[0]
    att = jnp.concatenate([a[g][:, :HD * ATT_QH // ATT_KVH] for g in range(ATT_KVH)], axis=1)
    att = att * zs[:, SSD_W + GDN_W:]
    mix = jnp.concatenate([ssd, gdn, att], axis=1).astype(BF16)
    gate = mod_ref[0][:, 2 * D_MODEL:]
    x = jnp.where(pl.program_id(1) + first_tile == 0, xa_ref[0], x_ref[0])
    out_ref[0] = x + gate * jnp.dot(mix, w_ref[0], preferred_element_type=F32)


def _out_projection(xa, xb, off, layer, y, o, a, zs, mod3, ssd_nw, gdn_nw, w_out, first_tile):
    bsz, t, _ = y.shape
    nt = t // TM - first_tile
    row = lambda b, i: (b, i + first_tile, 0)
    lay = lambda b, i: (layer, 0, 0)
    return pl.pallas_call(
        functools.partial(_out_kernel, first_tile),
        out_shape=jax.ShapeDtypeStruct((bsz, nt * TM, D_MODEL), F32),
        grid=(bsz, nt),
        in_specs=[pl.BlockSpec((1, TM, D_MODEL), lambda b, i: (b, 0, 0)),
                  pl.BlockSpec((1, TM, D_MODEL), lambda b, i: (b, jnp.maximum(i + first_tile - off, 0), 0)),
                  pl.BlockSpec((1, TM, SSD_W), row),
                  pl.BlockSpec((1, TM, GDN_W), row),
                  pl.BlockSpec((1, ATT_KVH, TM, 256), lambda b, i: (b, 0, i + first_tile, 0)),
                  pl.BlockSpec((1, TM, D_MODEL), row),
                  pl.BlockSpec((1, 1, 3 * D_MODEL),
                               lambda b, i: (jnp.where(i + first_tile == 0, bsz, b), 0, 0)),
                  pl.BlockSpec((1, 1, SSD_W), lay),
                  pl.BlockSpec((1, 1, GDN_W), lay),
                  pl.BlockSpec((1, D_MODEL, D_MODEL), lay)],
        out_specs=pl.BlockSpec((1, TM, D_MODEL), lambda b, i: (b, i, 0)),
        compiler_params=pltpu.CompilerParams(dimension_semantics=("arbitrary", "arbitrary"),
                                             vmem_limit_bytes=VMEM_LIMIT),
        name="out_projection",
    )(xa, xb, y, o, a, zs, mod3, ssd_nw, gdn_nw, w_out)


_REST_SEGS = [(1408, 384), (1804, 256), (2076, 384), (2460, 128), (2588, 128), (2716, 384),
              (1792, 12), (2060, 8), (2068, 8)]


def _prepare_params(w_in, conv_w, conv_b, ssd_A_log, ssd_dt_bias, gdn_A_log, gdn_dt_bias):
    depth = w_in.shape[0]
    w_conv = w_in[:, :, :CONV_DIM].astype(BF16)
    pad = jnp.zeros(w_in.shape[:2] + (REST_DIM - sum(n for _, n in _REST_SEGS),), w_in.dtype)
    w_rest = jnp.concatenate([lax.slice_in_dim(w_in, a, a + n, axis=2) for a, n in _REST_SEGS] + [pad],
                             axis=2).astype(BF16)
    cw = conv_w
    cb = conv_b[:, None, :]

    def lanes(a, b):
        v = jnp.concatenate([a.reshape(depth, -1), b.reshape(depth, -1)], axis=1)
        return jnp.pad(v, ((0, 0), (0, 128 - v.shape[1])))

    prm = jnp.stack([lanes(ssd_A_log, gdn_A_log), lanes(ssd_dt_bias, gdn_dt_bias)], axis=1)
    prm = jnp.pad(prm, ((0, 0), (0, 6), (0, 0)))
    return w_conv, w_rest, cw, cb, prm


def _rope_tables(t):
    f32 = np.float32
    pos = np.arange(t - CTX)
    n_freq = HD // 4
    freqs = np.power(f32(ROPE_THETA), -np.arange(n_freq, dtype=f32) / f32(n_freq)).astype(f32)
    ang_r = ((pos // GRID_W).astype(f32)[:, None] * freqs).astype(f32)
    ang_c = ((pos % GRID_W).astype(f32)[:, None] * freqs).astype(f32)
    cos = np.concatenate([np.cos(ang_r)] * 2 + [np.cos(ang_c)] * 2, axis=1)
    sin = np.concatenate([-np.sin(ang_r), np.sin(ang_r), -np.sin(ang_c), np.sin(ang_c)], axis=1)
    cos = np.concatenate([np.ones((CTX, HD), f32), cos], axis=0).astype(f32)
    sin = np.concatenate([np.zeros((CTX, HD), f32), sin], axis=0).astype(f32)
    return jnp.asarray(np.tile(cos, (1, 2))), jnp.asarray(np.tile(sin, (1, 2)))


def kernel(x, c, ctx, c_ctx, norm_w, w_mod, b_mod, w_in, conv_w, conv_b, ssd_A_log, ssd_dt_bias, ssd_D,
           ssd_norm_w, gdn_A_log, gdn_dt_bias, gdn_norm_w, q_norm_w, k_norm_w, w_out):
    bsz = x.shape[0]
    t = CTX + x.shape[1]
    c_all = jnp.concatenate([c, c_ctx[None, :], jnp.zeros((7, D_MODEL), F32)], axis=0)
    cos_t, sin_t = _rope_tables(t)
    w_conv, w_rest, cw, cb, prm = _prepare_params(w_in, conv_w, conv_b, ssd_A_log, ssd_dt_bias,
                                                  gdn_A_log, gdn_dt_bias)
    qk_w = jnp.concatenate([jnp.tile(q_norm_w, (1, ATT_QH)), jnp.tile(k_norm_w, (1, ATT_KVH))], axis=1)[:, None]
    dskip = jnp.repeat(ssd_D, HD, axis=1)[:, None]
    gdn_nw = jnp.tile(gdn_norm_w, (1, GDN_HEADS))[:, None]
    norm_w3, ssd_nw = norm_w[:, None], ssd_norm_w[:, None]
    w_out16 = w_out.astype(BF16)

    xa, xb, off = ctx, x, 1
    for layer in range(DEPTH):
        first_tile = 1 if layer == DEPTH - 1 else 0
        mod3 = _modulation(c_all, w_mod, b_mod, layer).reshape(bsz + 8, 1, 3 * D_MODEL)
        ssd_in, gdn_in, zs, q, kv, small = _in_projection(
            xa, xb, off, layer, mod3, norm_w3, w_conv, w_rest, cw, cb, qk_w, cos_t, sin_t)
        y = _ssd(ssd_in, small, prm, dskip, layer)
        o, a = _gdn_attention(gdn_in, small, prm, q, kv, layer, first_tile == 0)
        out = _out_projection(xa, xb, off, layer, y, o, a, zs, mod3, ssd_nw, gdn_nw, w_out16, first_tile)
        xa, xb, off = out, out, 0
    return out
```

```python
import functools
import math

import jax
import jax.numpy as jnp
import numpy as np
from jax import lax
from jax.experimental import pallas as pl
from jax.experimental.pallas import tpu as pltpu

F32 = jnp.float32
BF16 = jnp.bfloat16

D_MODEL = 1024
CTX = 256
GRID_W = 64
EPS = 1e-6
DEPTH = 2

HD = 64
SSD_HEADS = 6
SSD_W = SSD_HEADS * HD
SSD_CHUNK = 128
GDN_HEADS = 4
GDN_W = GDN_HEADS * HD
GDN_CHUNK = 64
ATT_QH = 6
ATT_KVH = 2
ATT_W = ATT_QH * HD
ROPE_THETA = 10000.0
Q_PRESCALE = (HD ** -0.5) * math.log2(math.e)

CONV_DIM = 1408
REST_DIM = 1792
TM = 256
HALO = 8
NEG = -1e30
VMEM_LIMIT = 56 * 1024 * 1024

SM_DT = 0
SM_A = 12
SM_B = 20


def _mm(a, b):
    return jnp.dot(a.astype(BF16), b.astype(BF16), preferred_element_type=F32)


def _mm_nt(a, b):
    return lax.dot_general(a.astype(BF16), b.astype(BF16), (((1,), (1,)), ((), ())),
                           preferred_element_type=F32)


def _split3(a):
    h = a.astype(BF16)
    r = a - h.astype(F32)
    m = r.astype(BF16)
    l = (r - m.astype(F32)).astype(BF16)
    return h, m, l


def _mm_exact_lhs(a_bf16, b):
    h, m, l = _split3(b)
    d = functools.partial(jnp.dot, preferred_element_type=F32)
    return d(a_bf16, h) + (d(a_bf16, m) + d(a_bf16, l))


def _silu(x):
    return x * jax.nn.sigmoid(x)


def _softplus(x):
    return jnp.maximum(x, 0.0) + jnp.log1p(jnp.exp(-jnp.abs(x)))


def _iota2(shape, dim):
    return lax.broadcasted_iota(jnp.int32, shape, dim)


def _aligned(x, m):
    return x if isinstance(x, int) else pl.multiple_of(x, m)


def _group_sum64(xx):
    r = jnp.right_shift(_iota2((128, 128), 0), 6)
    c = jnp.right_shift(_iota2((128, 128), 1), 6)
    g = (r == c).astype(BF16)
    xb = xx.astype(BF16)
    outs = [jnp.dot(xb[:, t * 128:(t + 1) * 128], g, preferred_element_type=F32)
            for t in range(xx.shape[1] // 128)]
    return outs[0] if len(outs) == 1 else jnp.concatenate(outs, axis=1)


def _mod_kernel(c_ref, w_ref, b_ref, o_ref):
    o_ref[...] = _mm(_silu(c_ref[...]), w_ref[0]) + b_ref[0]


def _modulation(c_all, w_mod, b_mod, layer):
    n = w_mod.shape[2]
    bn = 768
    rows = c_all.shape[0]
    return pl.pallas_call(
        _mod_kernel,
        out_shape=jax.ShapeDtypeStruct((rows, n), F32),
        grid=(n // bn,),
        in_specs=[pl.BlockSpec((rows, D_MODEL), lambda j: (0, 0)),
                  pl.BlockSpec((1, D_MODEL, bn), lambda j: (layer, 0, j)),
                  pl.BlockSpec((1, 1, bn), lambda j: (layer, 0, j))],
        out_specs=pl.BlockSpec((rows, bn), lambda j: (0, j)),
        compiler_params=pltpu.CompilerParams(dimension_semantics=("arbitrary",),
                                             vmem_limit_bytes=VMEM_LIMIT),
        name="modulation",
    )(c_all, w_mod, b_mod.reshape(b_mod.shape[0], 1, n))


BPS = 2


def _proj_kernel(nt, xa_ref, xm_ref, xp_ref, xn_ref, mod0_ref, mod1_ref, nw_ref, wc_ref, wr_ref, cw_ref, cb_ref,
                 qkw_ref, cos_ref, sin_ref,
                 ssd_ref, gdn_ref, zs_ref, q_ref, kv_ref, small_ref):
    i = pl.program_id(1)
    mods = (mod0_ref, mod1_ref)

    def normed(s):
        m = mods[s][0]
        gain = nw_ref[0] * (1.0 + m[:, D_MODEL:2 * D_MODEL])
        shift = m[:, :D_MODEL]
        xm = jnp.where(i == 0, xa_ref[s], xm_ref[s])
        xe = jnp.concatenate([xm, xp_ref[s], xn_ref[s]], axis=0)
        ms = jnp.mean(xe * xe, axis=-1, keepdims=True)
        return (xe * lax.rsqrt(ms + EPS) * gain + shift).astype(BF16)

    def project(he):
        p = jnp.dot(he, wc_ref[0], preferred_element_type=F32)
        aqk = jnp.dot(he[:TM], wr_ref[0, :, :512], preferred_element_type=F32)
        r = jnp.dot(he[:TM], wr_ref[0, :, 512:], preferred_element_type=F32)
        return p, aqk, r

    def finish(s, p, aqk, r):
        pm = p[:TM]
        seg_first = i <= 1
        seg_last = jnp.logical_or(i == 0, i == nt - 1)
        prev_row = jnp.where(seg_first, 0.0, p[TM + HALO - 1:TM + HALO])
        next_row = jnp.where(seg_last, 0.0, p[TM + HALO:TM + HALO + 1])
        rows = _iota2((TM, 1), 0)
        pm1 = jnp.where(rows == 0, prev_row, pltpu.roll(pm, 1, axis=0))
        pp1 = jnp.where(rows == TM - 1, next_row, pltpu.roll(pm, TM - 1, axis=0))
        cw = cw_ref[0]
        conv = cw[0:1] * pm1 + cw[1:2] * pm + cw[2:3] * pp1 + cb_ref[0]
        co = _silu(conv)

        ssd_ref[s] = co[:, :640].astype(BF16)
        gqk = co[:, 640:1152]
        gqk = gqk * lax.rsqrt(_group_sum64(gqk * gqk) + EPS)
        gdn_ref[s, :, :256] = (gqk[:, :256] * (HD ** -0.5)).astype(BF16)
        gdn_ref[s, :, 256:512] = gqk[:, 256:].astype(BF16)
        gdn_ref[s, :, 512:] = co[:, 1152:1408].astype(BF16)

        zs_ref[s, :, :640] = _silu(r[:, :640]).astype(BF16)
        zs_ref[s, :, 640:] = _silu(r[:, 768:1152]).astype(BF16)

        aqk = aqk * lax.rsqrt(_group_sum64(aqk * aqk) * (1.0 / HD) + EPS) * qkw_ref[0]
        lane = _iota2((1, 512), 1)
        swapped = jnp.where(jnp.bitwise_and(lane, 31) < 16,
                            pltpu.roll(aqk, 512 - 16, axis=1), pltpu.roll(aqk, 16, axis=1))
        cos = jnp.concatenate([cos_ref[...]] * 4, axis=1)
        sin = jnp.concatenate([sin_ref[...]] * 4, axis=1)
        aqk = aqk * cos + swapped * sin
        q_ref[s] = (aqk[:, :384] * Q_PRESCALE).astype(BF16)
        ones = jnp.ones((TM, HD), F32)
        zeros = jnp.zeros((TM, HD), F32)
        for g in range(ATT_KVH):
            kv_ref[s, g] = jnp.concatenate([aqk[:, 384 + g * HD:384 + (g + 1) * HD], zeros,
                                            r[:, 640 + g * HD:640 + (g + 1) * HD], ones], axis=1).astype(BF16)
        small_ref[s] = r[:, 1152:1280]

    hes = [normed(s) for s in range(BPS)]
    prs = [project(he) for he in hes]
    for s in range(BPS):
        finish(s, *prs[s])


def _in_projection(xa, xb, off, layer, mod3, norm_w, w_conv, w_rest, conv_w, conv_b, qk_w, cos_t, sin_t):
    bsz = xb.shape[0]
    assert bsz % BPS == 0 and BPS == 2
    nt = xb.shape[1] // TM + off
    t = nt * TM
    nblk8 = xb.shape[1] // HALO
    per_tile = TM // HALO
    row = lambda b, i: (b, i, 0)
    lay = lambda b, i: (layer, 0, 0)
    outs = [jax.ShapeDtypeStruct((bsz, t, w), dt)
            for w, dt in ((640, BF16), (768, BF16), (1024, BF16), (384, BF16), (128, F32))]
    outs.insert(4, jax.ShapeDtypeStruct((bsz, ATT_KVH, t, 256), BF16))

    def mod_spec(s):
        return pl.BlockSpec((1, 1, 3 * D_MODEL), lambda b, i: (jnp.where(i == 0, bsz, b * BPS + s), 0, 0))

    return pl.pallas_call(
        functools.partial(_proj_kernel, nt),
        out_shape=outs,
        grid=(bsz // BPS, nt),
        in_specs=[
            pl.BlockSpec((BPS, TM, D_MODEL), lambda b, i: (b, 0, 0)),
            pl.BlockSpec((BPS, TM, D_MODEL), lambda b, i: (b, jnp.maximum(i - off, 0), 0)),
            pl.BlockSpec((BPS, HALO, D_MODEL), lambda b, i: (b, jnp.maximum((i - off) * per_tile - 1, 0), 0)),
            pl.BlockSpec((BPS, HALO, D_MODEL),
                         lambda b, i: (b, jnp.minimum((i - off + 1) * per_tile, nblk8 - 1), 0)),
            mod_spec(0), mod_spec(1),
            pl.BlockSpec((1, 1, D_MODEL), lay),
            pl.BlockSpec((1, D_MODEL, CONV_DIM), lay),
            pl.BlockSpec((1, D_MODEL, REST_DIM), lay),
            pl.BlockSpec((1, 3, CONV_DIM), lay),
            pl.BlockSpec((1, 1, CONV_DIM), lay),
            pl.BlockSpec((1, 1, 512), lay),
            pl.BlockSpec((TM, 128), lambda b, i: (i, 0)),
            pl.BlockSpec((TM, 128), lambda b, i: (i, 0)),
        ],
        out_specs=[pl.BlockSpec((BPS, TM, 640), row), pl.BlockSpec((BPS, TM, 768), row),
                   pl.BlockSpec((BPS, TM, 1024), row), pl.BlockSpec((BPS, TM, 384), row),
                   pl.BlockSpec((BPS, ATT_KVH, TM, 256), lambda b, i: (b, 0, i, 0)),
                   pl.BlockSpec((BPS, TM, 128), row)],
        compiler_params=pltpu.CompilerParams(dimension_semantics=("arbitrary", "arbitrary"),
                                             vmem_limit_bytes=VMEM_LIMIT),
        name="in_projection",
    )(xa, xb, xb, xb, mod3, mod3, norm_w, w_conv, w_rest, conv_w, conv_b, qk_w, cos_t, sin_t)


def _ssd_kernel(nc, ctx_chunks, xin_ref, small_ref, prm_ref, dskip_ref, y_ref,
                cum_s, cumt_s, dtt_s, st_s, sin_s):
    q = SSD_CHUNK
    hpg = SSD_HEADS // 2
    ri = _iota2((q, q), 0)
    ci = _iota2((q, q), 1)
    ltri = (ri >= ci).astype(BF16)
    utri = (ri <= ci).astype(BF16)
    lane = _iota2((1, 128), 1)
    fwd_lane = lane < SSD_HEADS

    a_row = -jnp.exp(prm_ref[0, 0:1, :])
    bias_row = prm_ref[0, 1:2, :]

    pb = 6
    tri = jnp.broadcast_to(jnp.concatenate([ltri, utri], axis=0)[None], (pb, 2 * q, q))

    def prep(cb, carry):
        base = pl.multiple_of(cb * (pb * q), q)
        dt = _softplus(small_ref[0, pl.ds(base, pb * q), :] + bias_row)
        pieces = _split3((dt * a_row).reshape(pb, q, 128))
        dot = lambda x: lax.dot_general(tri, x, (((2,), (1,)), ((0,), (0,))), preferred_element_type=F32)
        sums = dot(pieces[0]) + (dot(pieces[1]) + dot(pieces[2]))
        cum = jnp.where(fwd_lane, sums[:, :q], sums[:, q:])
        cum_s[pl.ds(base, pb * q), :] = cum.reshape(pb * q, 128)
        for i in range(pb):
            rows = pl.ds(pl.multiple_of(base + i * q, q), q)
            cumt_s[rows, :] = cum[i].T
            dtt_s[rows, :] = dt[i * q:(i + 1) * q].T
        return carry

    lax.fori_loop(0, nc // pb, prep, 0)

    def end_of(cumt, d, col):
        return cumt[col:col + 1, q - 1:q] if d == 0 else cumt[col:col + 1, 0:1]

    nb = 2

    def states(cb, carry):
        lhs, rhs = [], []
        for cc in range(nb):
            base = pl.multiple_of((cb * nb + cc) * q, q)
            xin = xin_ref[0, pl.ds(base, q), :]
            cumt = cumt_s[pl.ds(base, q), :]
            dtt = dtt_s[pl.ds(base, q), :]
            for g in range(2):
                bgt = xin[:, 384 + g * HD:384 + (g + 1) * HD].astype(F32).T
                for d in range(2):
                    for hg in range(hpg):
                        h = g * hpg + hg
                        col = d * SSD_HEADS + h
                        w_row = jnp.exp(end_of(cumt, d, col) - cumt[col:col + 1, :]) * dtt[col:col + 1, :]
                        lhs.append(bgt * w_row)
                        rhs.append(xin[:, h * HD:(h + 1) * HD])
        st = _bmm(jnp.stack(lhs), jnp.stack(rhs))
        st_s[pl.ds(cb * (12 * nb), 12 * nb)] = st
        return carry

    lax.fori_loop(0, nc // nb, states, 0)

    orders = (list(range(nc)),
              list(range(ctx_chunks - 1, -1, -1)) + list(range(nc - 1, ctx_chunks - 1, -1)))
    for g in range(2):
        for d in range(2):
            for hg in range(hpg):
                col = d * SSD_HEADS + g * hpg + hg
                s = jnp.zeros((HD, HD), F32)
                for c in orders[d]:
                    slot = c * 12 + g * 6 + d * 3 + hg
                    sin_s[slot] = s
                    row = c * q + (q - 1 if d == 0 else 0)
                    s = s * jnp.exp(cum_s[row:row + 1, col:col + 1]) + st_s[slot]

    lower = ri >= ci
    upper = ri <= ci
    dsk = dskip_ref[0]

    def outputs(cb, carry):
        sin = sin_s[pl.ds(cb * (12 * nb), 12 * nb)]
        ws, xs, cstack, es, scs = [], [], [], [], []
        per_chunk = []
        for cc in range(nb):
            base = pl.multiple_of((cb * nb + cc) * q, q)
            xin = xin_ref[0, pl.ds(base, q), :]
            per_chunk.append((base, xin, cum_s[pl.ds(base, q), :], cumt_s[pl.ds(base, q), :],
                              dtt_s[pl.ds(base, q), :]))
        sc = _bmm_nt(jnp.stack([xin[:, 512 + g * HD:512 + (g + 1) * HD] for _, xin, _, _, _ in per_chunk
                                for g in range(2)]),
                     jnp.stack([xin[:, 384 + g * HD:384 + (g + 1) * HD] for _, xin, _, _, _ in per_chunk
                                for g in range(2)]))
        for cc, (base, xin, cum, cumt, dtt) in enumerate(per_chunk):
            e_chunk = {}
            for g in range(2):
                cg = xin[:, 512 + g * HD:512 + (g + 1) * HD]
                for hg in range(hpg):
                    h = g * hpg + hg
                    w = None
                    for d in range(2):
                        col = d * SSD_HEADS + h
                        gmat = jnp.broadcast_to(cum[:, col:col + 1], (q, q))
                        lm = jnp.exp(jnp.where(lower if d == 0 else upper, gmat - cumt[col:col + 1, :], NEG))
                        wd = lm * dtt[col:col + 1, :]
                        w = wd if w is None else w + wd
                        e_chunk[g * 6 + d * 3 + hg] = jnp.exp(gmat[:, :HD])
                    ws.append(w * sc[cc * 2 + g])
                    xs.append(xin[:, h * HD:(h + 1) * HD])
            es.extend(e_chunk[i] for i in range(12))
            cstack.extend(xin[:, 512 + (i // 6) * HD:512 + (i // 6 + 1) * HD] for i in range(12))
        ydiag = _bmm(jnp.stack(ws), jnp.stack(xs))
        yoff = _bmm(jnp.stack(cstack), sin)
        for cc, (base, xin, cum, cumt, dtt) in enumerate(per_chunk):
            outs = []
            for h in range(SSD_HEADS):
                g, hg = divmod(h, hpg)
                y = ydiag[cc * SSD_HEADS + h] + dsk[:, h * HD:(h + 1) * HD] * xs[cc * SSD_HEADS + h]
                for d in range(2):
                    i = cc * 12 + g * 6 + d * 3 + hg
                    y = y + yoff[i] * es[i]
                outs.append(y)
            y_ref[0, pl.ds(base, q), :] = jnp.concatenate(outs, axis=1).astype(BF16)
        return carry

    lax.fori_loop(0, nc // nb, outputs, 0)


def _ssd(ssd_in, small, prm, dskip, layer):
    bsz, t, _ = ssd_in.shape
    nc = t // SSD_CHUNK
    return pl.pallas_call(
        functools.partial(_ssd_kernel, nc, CTX // SSD_CHUNK),
        out_shape=jax.ShapeDtypeStruct((bsz, t, SSD_W), BF16),
        grid=(bsz,),
        in_specs=[pl.BlockSpec((1, t, 640), lambda b: (b, 0, 0)),
                  pl.BlockSpec((1, t, 128), lambda b: (b, 0, 0)),
                  pl.BlockSpec((1, 8, 128), lambda b: (layer, 0, 0)),
                  pl.BlockSpec((1, 1, SSD_W), lambda b: (layer, 0, 0))],
        out_specs=pl.BlockSpec((1, t, SSD_W), lambda b: (b, 0, 0)),
        scratch_shapes=[pltpu.VMEM((t, 128), F32), pltpu.VMEM((t, 128), F32), pltpu.VMEM((t, 128), F32),
                        pltpu.VMEM((nc * 12, HD, HD), F32),
                        pltpu.VMEM((nc * 12, HD, HD), F32)],
        compiler_params=pltpu.CompilerParams(dimension_semantics=("arbitrary",),
                                             vmem_limit_bytes=VMEM_LIMIT),
        name="ssd",
    )(ssd_in, small, prm, dskip)


def _bd4(x):
    xb = x.astype(BF16)
    blk = jnp.right_shift(_iota2((1, 4 * HD), 1), 6)
    zero = jnp.zeros_like(xb)
    return jnp.concatenate([jnp.where(blk == h, xb, zero) for h in range(GDN_HEADS)], axis=0)


def _bmm(a, b):
    return lax.dot_general(a.astype(BF16), b.astype(BF16), (((2,), (1,)), ((0,), (0,))),
                           preferred_element_type=F32)


def _bmm_nt(a, b):
    return lax.dot_general(a.astype(BF16), b.astype(BF16), (((2,), (2,)), ((0,), (0,))),
                           preferred_element_type=F32)


def _unit_tri_inverse_stages(a_strict):
    n = a_strict.shape[-1]
    ri = _iota2((1, n, n), 1)
    ci = _iota2((1, n, n), 2)

    def blk(shift):
        return jnp.right_shift(ri, shift) == jnp.right_shift(ci, shift)

    nd = jnp.where(blk(3), -a_strict, 0.0)
    p0 = (ri == ci).astype(F32) + nd
    q1 = _bmm(nd, nd)
    yield None
    t = _bmm(jnp.concatenate([p0, q1], axis=1), q1)
    yield None
    p1 = p0 + t[:, :n]
    inv = p1 + _bmm(p1, t[:, n:])
    yield None
    for shift in (4, 5, 6):
        e = jnp.where(jnp.logical_and(blk(shift), jnp.logical_not(blk(shift - 1))), a_strict, 0.0)
        ie = _bmm(inv, e)
        yield None
        inv = inv - _bmm(ie, inv)
        yield None
    yield inv


def _gdn_attn_kernel(nch, ctx_chunks, ctx_attn, gin_ref, small_ref, prm_ref, q_ref, kv_ref, o_ref, att_ref,
                     gc_s, rows_s, mq_s, n_s, cd_s, of_s, ob_s):
    c64 = GDN_CHUNK
    w4 = GDN_W
    nsc = nch // 2
    ri128 = _iota2((128, 128), 0)
    ci128 = _iota2((128, 128), 1)
    same = jnp.right_shift(ri128, 6) == jnp.right_shift(ci128, 6)
    lbd = jnp.logical_and(same, ri128 >= ci128).astype(BF16)
    ubd = jnp.logical_and(same, ri128 <= ci128).astype(BF16)
    lane = _iota2((1, 128), 1)
    fwd_lane = lane < SM_A + GDN_HEADS
    beta_lane = jnp.logical_and(lane >= SM_B, lane < SM_B + 2 * GDN_HEADS)

    a_row = -jnp.exp(prm_ref[0, 0:1, :])
    bias_row = prm_ref[0, 1:2, :]

    pb = 6
    tri = jnp.broadcast_to(jnp.concatenate([lbd, ubd], axis=0)[None], (pb, 256, 128))

    def prep(sb, carry):
        base = pl.multiple_of(sb * (pb * 128), 128)
        sm = small_ref[0, pl.ds(base, pb * 128), :]
        pieces = _split3((a_row * _softplus(sm + bias_row)).reshape(pb, 128, 128))
        dot = lambda x: lax.dot_general(tri, x, (((2,), (1,)), ((0,), (0,))), preferred_element_type=F32)
        sums = dot(pieces[0]) + (dot(pieces[1]) + dot(pieces[2]))
        cum = jnp.where(fwd_lane, sums[:, :128], sums[:, 128:])
        gc_s[pl.ds(base, pb * 128), :] = cum.reshape(pb * 128, 128)
        beta = jax.nn.sigmoid(sm)
        for i in range(pb):
            slab_t = jnp.where(beta_lane, beta[i * 128:(i + 1) * 128], cum[i]).T
            for half in range(2):
                lo = half * c64
                table = []
                for first in (SM_A, SM_A + GDN_HEADS, SM_B, SM_B + GDN_HEADS):
                    table.append(jnp.concatenate(
                        [slab_t[first + h:first + h + 1, lo:lo + c64] for h in range(GDN_HEADS)], axis=1))
                for d, edge in ((0, lo + c64 - 1), (1, lo)):
                    first = SM_A + d * GDN_HEADS
                    table.append(jnp.concatenate(
                        [jnp.broadcast_to(slab_t[first + h:first + h + 1, edge:edge + 1], (1, c64))
                         for h in range(GDN_HEADS)], axis=1))
                table.append(jnp.zeros((2, w4), F32))
                rows_s[pl.ds(pl.multiple_of((sb * pb + i) * 16 + half * 8, 8), 8), :] = (
                    jnp.concatenate(table, axis=0))
        return carry

    lax.fori_loop(0, nsc // pb, prep, 0)

    ri = _iota2((c64, c64), 0)
    ci = _iota2((c64, c64), 1)
    incl = (ri >= ci, ri <= ci)
    strict = (ri > ci, ri < ci)

    nb = 4

    def local(cb):
        chunks = []
        for cc in range(nb):
            c = cb * nb + cc
            base = _aligned(c * c64, c64)
            gin = gin_ref[0, pl.ds(base, c64), :]
            gc = gc_s[pl.ds(base, c64), :]
            rows = rows_s[pl.ds(_aligned(c * 8, 8), 8), :]
            qs = [gin[:, h * HD:(h + 1) * HD] for h in range(GDN_HEADS)]
            ks = [gin[:, w4 + h * HD:w4 + (h + 1) * HD] for h in range(GDN_HEADS)]
            vs = [gin[:, 2 * w4 + h * HD:2 * w4 + (h + 1) * HD] for h in range(GDN_HEADS)]
            chunks.append((c, base, gc, rows, qs, ks, vs))
        kq = _bmm_nt(jnp.stack([jnp.concatenate([ch[5][h], ch[4][h]], axis=0)
                                for ch in chunks for h in range(GDN_HEADS)]),
                     jnp.stack([ch[5][h] for ch in chunks for h in range(GDN_HEADS)]))
        yield
        mb_l, rhs_l, lhs_l, qd_l = [], [], [], []
        for ci, (c, base, gc, rows, qs, ks, vs) in enumerate(chunks):
            kts = [k.astype(F32).T for k in ks]
            for d in range(2):
                for h in range(GDN_HEADS):
                    sl = slice(h * HD, (h + 1) * HD)
                    colg = SM_A + d * GDN_HEADS + h
                    kqh = kq[ci * GDN_HEADS + h]
                    gmat = jnp.broadcast_to(gc[:, colg:colg + 1], (c64, c64))
                    gcr, br, endr = rows[d:d + 1, sl], rows[2 + d:3 + d, sl], rows[4 + d:5 + d, sl]
                    dec = jnp.exp(jnp.where(incl[d], gmat - gcr, NEG))
                    mb_l.append(jnp.where(strict[d], kqh[:c64] * dec, 0.0) * br)
                    eg = jnp.exp(gmat)
                    kdtb = kts[h] * (jnp.exp(endr - gcr) * br)
                    lhs_l.append(jnp.concatenate([kdtb, kqh[c64:] * dec * br], axis=0))
                    rhs_l.append(jnp.concatenate([vs[h], ks[h] * eg], axis=1))
                    qd_l.append(qs[h] * eg)
        z = None
        for z in _unit_tri_inverse_stages(jnp.stack(mb_l)):
            if z is None:
                yield
        zr = _bmm(z, jnp.stack(rhs_l))
        yield
        a12 = _bmm(jnp.stack(lhs_l), zr)
        yield
        for ci, (c, base, gc, rows, qs, ks, vs) in enumerate(chunks):
            for d in range(2):
                slot = c * 2 + d
                ids = [ci * 2 * GDN_HEADS + d * GDN_HEADS + h for h in range(GDN_HEADS)]
                mq = [jnp.concatenate([-a12[n][:c64, HD:], qd_l[n] - a12[n][c64:, HD:]], axis=0) for n in ids]
                mq_s[slot] = jnp.concatenate(mq, axis=1).astype(BF16)
                n_s[slot] = jnp.concatenate([a12[n][:c64, :HD] for n in ids], axis=1)
                cd_s[slot] = jnp.broadcast_to(jnp.exp(rows[4 + d:5 + d]), (8, w4))
                oc = jnp.concatenate([a12[n][c64:, :HD] for n in ids], axis=1)
                if d == 0:
                    of_s[pl.ds(base, c64), :] = oc
                else:
                    ob_s[pl.ds(base, c64), :] = oc

    t_all = q_ref.shape[1]
    heads_per_kv = ATT_QH // ATT_KVH

    def attend(tile, nkeys):
        rows = pl.ds(_aligned(tile * TM, TM), TM)
        qt = q_ref[0, rows, :]
        outs = []

        def finish(h, s):
            g = h // heads_per_kv
            p = jnp.exp2(s - jnp.max(s, axis=-1, keepdims=True))
            acc = _mm(p, kv_ref[0, g, :nkeys, 2 * HD:])
            outs.append(acc[:, :HD] / acc[:, HD:])
            if h % heads_per_kv == heads_per_kv - 1:
                tile_out = outs[-heads_per_kv:] + [jnp.zeros((TM, HD), F32)]
                att_ref[0, g, rows, :] = jnp.concatenate(tile_out, axis=1).astype(BF16)

        s_prev = None
        for h in range(ATT_QH):
            k = kv_ref[0, h // heads_per_kv, :nkeys, :HD]
            s = _mm_nt(qt[:, h * HD:(h + 1) * HD], k)
            yield
            if s_prev is not None:
                finish(h - 1, s_prev)
                yield
            s_prev = s
        finish(ATT_QH - 1, s_prev)
        yield

    def trip(j, nkeys):
        streams = [local(j)] + ([attend(j, nkeys)] if nkeys else [])
        while streams:
            for st in list(streams):
                try:
                    next(st)
                except StopIteration:
                    streams.remove(st)

    trip(0, CTX if ctx_attn else 0)
    if not ctx_attn:
        att_ref[0, :, :TM, :] = jnp.zeros((ATT_KVH, TM, 256), BF16)

    def later_trip(j, carry):
        trip(j, t_all)
        return carry

    lax.fori_loop(1, nch // nb, later_trip, 0)

    def step(t, states):
        cf = t
        cb = jnp.where(t < ctx_chunks, ctx_chunks - 1 - t, nch - 1 + ctx_chunks - t)
        new_states = []
        for d, c in ((0, cf), (1, cb)):
            slot = c * 2 + d
            s = states[d]
            r = jnp.dot(mq_s[slot], _bd4(s), preferred_element_type=F32)
            new_states.append(s * cd_s[slot][0:1] + r[:c64] + n_s[slot])
            rows = pl.ds(pl.multiple_of(c * c64, c64), c64)
            if d == 0:
                of_s[rows, :] = of_s[rows, :] + r[c64:]
            else:
                ob_s[rows, :] = ob_s[rows, :] + r[c64:]
        return tuple(new_states)

    zero = jnp.zeros((c64, w4), F32)
    lax.fori_loop(0, nch, step, (zero, zero))
    o_ref[0] = (of_s[...] + ob_s[...]).astype(BF16)


def _gdn_attention(gdn_in, small, prm, q, kv, layer, ctx_attn):
    bsz, t, _ = gdn_in.shape
    nch = t // GDN_CHUNK
    assert nch // 4 == t // TM
    return pl.pallas_call(
        functools.partial(_gdn_attn_kernel, nch, CTX // GDN_CHUNK, ctx_attn),
        out_shape=[jax.ShapeDtypeStruct((bsz, t, GDN_W), BF16),
                   jax.ShapeDtypeStruct((bsz, ATT_KVH, t, 256), BF16)],
        grid=(bsz,),
        in_specs=[pl.BlockSpec((1, t, 768), lambda b: (b, 0, 0)),
                  pl.BlockSpec((1, t, 128), lambda b: (b, 0, 0)),
                  pl.BlockSpec((1, 8, 128), lambda b: (layer, 0, 0)),
                  pl.BlockSpec((1, t, ATT_W), lambda b: (b, 0, 0)),
                  pl.BlockSpec((1, ATT_KVH, t, 256), lambda b: (b, 0, 0, 0))],
        out_specs=[pl.BlockSpec((1, t, GDN_W), lambda b: (b, 0, 0), pipeline_mode=pl.Buffered(1)),
                   pl.BlockSpec((1, ATT_KVH, t, 256), lambda b: (b, 0, 0, 0), pipeline_mode=pl.Buffered(1))],
        scratch_shapes=[pltpu.VMEM((t, 128), F32),
                        pltpu.VMEM((nch * 8, GDN_W), F32),
                        pltpu.VMEM((nch * 2, 128, GDN_W), BF16),
                        pltpu.VMEM((nch * 2, GDN_CHUNK, GDN_W), F32),
                        pltpu.VMEM((nch * 2, 8, GDN_W), F32),
                        pltpu.VMEM((t, GDN_W), F32),
                        pltpu.VMEM((t, GDN_W), F32)],
        compiler_params=pltpu.CompilerParams(dimension_semantics=("arbitrary",),
                                             vmem_limit_bytes=VMEM_LIMIT),
        name="gdn_attention",
    )(gdn_in, small, prm, q, kv)


def _out_kernel(first_tile, xa_ref, x_ref, y_ref, o_ref, a_ref, zs_ref, mod0_ref, mod1_ref, sw_ref, gw_ref, w_ref,
                out_ref):
    mods = (mod0_ref, mod1_ref)

    def mixed(s):
        zs = zs_ref[s].astype(F32)
        t = y_ref[s].astype(F32) * zs[:, :SSD_W]
        ssd = t * lax.rsqrt(jnp.mean(t * t, axis=-1, keepdims=True) + EPS) * sw_ref[0]
        o = o_ref[s].astype(F32)
        gdn = o * lax.rsqrt(_group_sum64(o * o) * (1.0 / HD) + EPS) * gw_ref[0] * zs[:, SSD_W:SSD_W + GDN_W]
        a = a_ref[s]
        att = jnp.concatenate([a[g][:, :HD * ATT_QH // ATT_KVH] for g in range(ATT_KVH)], axis=1)
        att = att * zs[:, SSD_W + GDN_W:]
        return jnp.concatenate([ssd, gdn, att], axis=1).astype(BF16)

    mixes = [mixed(s) for s in range(BPS)]
    projs = [jnp.dot(mix, w_ref[0], preferred_element_type=F32) for mix in mixes]
    for s in range(BPS):
        gate = mods[s][0][:, 2 * D_MODEL:]
        x = jnp.where(pl.program_id(1) + first_tile == 0, xa_ref[s], x_ref[s])
        out_ref[s] = x + gate * projs[s]


def _out_projection(xa, xb, off, layer, y, o, a, zs, mod3, ssd_nw, gdn_nw, w_out, first_tile):
    bsz, t, _ = y.shape
    nt = t // TM - first_tile
    row = lambda b, i: (b, i + first_tile, 0)
    lay = lambda b, i: (layer, 0, 0)

    def mod_spec(s):
        return pl.BlockSpec((1, 1, 3 * D_MODEL),
                            lambda b, i: (jnp.where(i + first_tile == 0, bsz, b * BPS + s), 0, 0))

    return pl.pallas_call(
        functools.partial(_out_kernel, first_tile),
        out_shape=jax.ShapeDtypeStruct((bsz, nt * TM, D_MODEL), F32),
        grid=(bsz // BPS, nt),
        in_specs=[pl.BlockSpec((BPS, TM, D_MODEL), lambda b, i: (b, 0, 0)),
                  pl.BlockSpec((BPS, TM, D_MODEL), lambda b, i: (b, jnp.maximum(i + first_tile - off, 0), 0)),
                  pl.BlockSpec((BPS, TM, SSD_W), row),
                  pl.BlockSpec((BPS, TM, GDN_W), row),
                  pl.BlockSpec((BPS, ATT_KVH, TM, 256), lambda b, i: (b, 0, i + first_tile, 0)),
                  pl.BlockSpec((BPS, TM, D_MODEL), row),
                  mod_spec(0), mod_spec(1),
                  pl.BlockSpec((1, 1, SSD_W), lay),
                  pl.BlockSpec((1, 1, GDN_W), lay),
                  pl.BlockSpec((1, D_MODEL, D_MODEL), lay)],
        out_specs=pl.BlockSpec((BPS, TM, D_MODEL), lambda b, i: (b, i, 0)),
        compiler_params=pltpu.CompilerParams(dimension_semantics=("arbitrary", "arbitrary"),
                                             vmem_limit_bytes=VMEM_LIMIT),
        name="out_projection",
    )(xa, xb, y, o, a, zs, mod3, mod3, ssd_nw, gdn_nw, w_out)


_REST_SEGS = [(2076, 384), (2460, 128), (1408, 384), (1804, 256), (2588, 128), (2716, 384),
              (1792, 12), (2060, 8), (2068, 8)]


def _prepare_params(w_in, conv_w, conv_b, ssd_A_log, ssd_dt_bias, gdn_A_log, gdn_dt_bias):
    depth = w_in.shape[0]
    w_conv = w_in[:, :, :CONV_DIM].astype(BF16)
    pad = jnp.zeros(w_in.shape[:2] + (REST_DIM - sum(n for _, n in _REST_SEGS),), w_in.dtype)
    w_rest = jnp.concatenate([lax.slice_in_dim(w_in, a, a + n, axis=2) for a, n in _REST_SEGS] + [pad],
                             axis=2).astype(BF16)
    cw = conv_w
    cb = conv_b[:, None, :]

    def lanes(a, b):
        v = jnp.concatenate([a.reshape(depth, -1), b.reshape(depth, -1)], axis=1)
        return jnp.pad(v, ((0, 0), (0, 128 - v.shape[1])))

    prm = jnp.stack([lanes(ssd_A_log, gdn_A_log), lanes(ssd_dt_bias, gdn_dt_bias)], axis=1)
    prm = jnp.pad(prm, ((0, 0), (0, 6), (0, 0)))
    return w_conv, w_rest, cw, cb, prm


def _rope_tables(t):
    f32 = np.float32
    pos = np.arange(t - CTX)
    n_freq = HD // 4
    freqs = np.power(f32(ROPE_THETA), -np.arange(n_freq, dtype=f32) / f32(n_freq)).astype(f32)
    ang_r = ((pos // GRID_W).astype(f32)[:, None] * freqs).astype(f32)
    ang_c = ((pos % GRID_W).astype(f32)[:, None] * freqs).astype(f32)
    cos = np.concatenate([np.cos(ang_r)] * 2 + [np.cos(ang_c)] * 2, axis=1)
    sin = np.concatenate([-np.sin(ang_r), np.sin(ang_r), -np.sin(ang_c), np.sin(ang_c)], axis=1)
    cos = np.concatenate([np.ones((CTX, HD), f32), cos], axis=0).astype(f32)
    sin = np.concatenate([np.zeros((CTX, HD), f32), sin], axis=0).astype(f32)
    return jnp.asarray(np.tile(cos, (1, 2))), jnp.asarray(np.tile(sin, (1, 2)))


def kernel(x, c, ctx, c_ctx, norm_w, w_mod, b_mod, w_in, conv_w, conv_b, ssd_A_log, ssd_dt_bias, ssd_D,
           ssd_norm_w, gdn_A_log, gdn_dt_bias, gdn_norm_w, q_norm_w, k_norm_w, w_out):
    bsz = x.shape[0]
    t = CTX + x.shape[1]
    c_all = jnp.concatenate([c, c_ctx[None, :], jnp.zeros((7, D_MODEL), F32)], axis=0)
    cos_t, sin_t = _rope_tables(t)
    w_conv, w_rest, cw, cb, prm = _prepare_params(w_in, conv_w, conv_b, ssd_A_log, ssd_dt_bias,
                                                  gdn_A_log, gdn_dt_bias)
    qk_w = jnp.concatenate([jnp.tile(q_norm_w, (1, ATT_QH)), jnp.tile(k_norm_w, (1, ATT_KVH))], axis=1)[:, None]
    dskip = jnp.repeat(ssd_D, HD, axis=1)[:, None]
    gdn_nw = jnp.tile(gdn_norm_w, (1, GDN_HEADS))[:, None]
    norm_w3, ssd_nw = norm_w[:, None], ssd_norm_w[:, None]
    w_out16 = w_out.astype(BF16)

    xa, xb, off = ctx, x, 1
    for layer in range(DEPTH):
        first_tile = 1 if layer == DEPTH - 1 else 0
        mod3 = _modulation(c_all, w_mod, b_mod, layer).reshape(bsz + 8, 1, 3 * D_MODEL)
        ssd_in, gdn_in, zs, q, kv, small = _in_projection(
            xa, xb, off, layer, mod3, norm_w3, w_conv, w_rest, cw, cb, qk_w, cos_t, sin_t)
        y = _ssd(ssd_in, small, prm, dskip, layer)
        o, a = _gdn_attention(gdn_in, small, prm, q, kv, layer, first_tile == 0)
        out = _out_projection(xa, xb, off, layer, y, o, a, zs, mod3, ssd_nw, gdn_nw, w_out16, first_tile)
        xa, xb, off = out, out, 0
    return out
```

```python
import functools
import math

import jax
import jax.numpy as jnp
import numpy as np
from jax import lax
from jax.experimental import pallas as pl
from jax.experimental.pallas import tpu as pltpu

F32 = jnp.float32
BF16 = jnp.bfloat16

D_MODEL = 1024
CTX = 256
GRID_W = 64
EPS = 1e-6
DEPTH = 2

HD = 64
SSD_HEADS = 6
SSD_W = SSD_HEADS * HD
SSD_CHUNK = 128
GDN_HEADS = 4
GDN_W = GDN_HEADS * HD
GDN_CHUNK = 64
ATT_QH = 6
ATT_KVH = 2
ATT_W = ATT_QH * HD
ROPE_THETA = 10000.0
Q_PRESCALE = (HD ** -0.5) * math.log2(math.e)

CONV_DIM = 1408
REST_DIM = 1792
TM = 256
HALO = 8
NEG = -1e30
VMEM_LIMIT = 56 * 1024 * 1024

SM_DT = 0
SM_A = 12
SM_B = 20


def _mm(a, b):
    return jnp.dot(a.astype(BF16), b.astype(BF16), preferred_element_type=F32)


def _mm_nt(a, b):
    return lax.dot_general(a.astype(BF16), b.astype(BF16), (((1,), (1,)), ((), ())),
                           preferred_element_type=F32)


def _split3(a):
    h = a.astype(BF16)
    r = a - h.astype(F32)
    m = r.astype(BF16)
    l = (r - m.astype(F32)).astype(BF16)
    return h, m, l


def _mm_exact_lhs(a_bf16, b):
    h, m, l = _split3(b)
    d = functools.partial(jnp.dot, preferred_element_type=F32)
    return d(a_bf16, h) + (d(a_bf16, m) + d(a_bf16, l))


def _silu(x):
    return x * jax.nn.sigmoid(x)


def _softplus(x):
    return jnp.maximum(x, 0.0) + jnp.log1p(jnp.exp(-jnp.abs(x)))


def _iota2(shape, dim):
    return lax.broadcasted_iota(jnp.int32, shape, dim)


def _aligned(x, m):
    return x if isinstance(x, int) else pl.multiple_of(x, m)


def _group_sum64(xx):
    r = jnp.right_shift(_iota2((128, 128), 0), 6)
    c = jnp.right_shift(_iota2((128, 128), 1), 6)
    g = (r == c).astype(BF16)
    xb = xx.astype(BF16)
    outs = [jnp.dot(xb[:, t * 128:(t + 1) * 128], g, preferred_element_type=F32)
            for t in range(xx.shape[1] // 128)]
    return outs[0] if len(outs) == 1 else jnp.concatenate(outs, axis=1)


def _mod_kernel(c_ref, w_ref, b_ref, o_ref):
    o_ref[...] = _mm(_silu(c_ref[...]), w_ref[0]) + b_ref[0]


def _modulation(c_all, w_mod, b_mod, layer):
    n = w_mod.shape[2]
    bn = 768
    rows = c_all.shape[0]
    return pl.pallas_call(
        _mod_kernel,
        out_shape=jax.ShapeDtypeStruct((rows, n), F32),
        grid=(n // bn,),
        in_specs=[pl.BlockSpec((rows, D_MODEL), lambda j: (0, 0)),
                  pl.BlockSpec((1, D_MODEL, bn), lambda j: (layer, 0, j)),
                  pl.BlockSpec((1, 1, bn), lambda j: (layer, 0, j))],
        out_specs=pl.BlockSpec((rows, bn), lambda j: (0, j)),
        compiler_params=pltpu.CompilerParams(dimension_semantics=("arbitrary",),
                                             vmem_limit_bytes=VMEM_LIMIT),
        name="modulation",
    )(c_all, w_mod, b_mod.reshape(b_mod.shape[0], 1, n))


BPS = 2


def _proj_kernel(nt, xa_ref, xm_ref, xp_ref, xn_ref, mod0_ref, mod1_ref, nw_ref, wc_ref, wr_ref, cw_ref, cb_ref,
                 qkw_ref, cos_ref, sin_ref,
                 ssd_ref, gdn_ref, zs_ref, q_ref, kv_ref, small_ref):
    i = pl.program_id(1)
    mods = (mod0_ref, mod1_ref)

    def normed(s):
        m = mods[s][0]
        gain = nw_ref[0] * (1.0 + m[:, D_MODEL:2 * D_MODEL])
        shift = m[:, :D_MODEL]
        xm = jnp.where(i == 0, xa_ref[s], xm_ref[s])
        xe = jnp.concatenate([xm, xp_ref[s], xn_ref[s]], axis=0)
        ms = jnp.mean(xe * xe, axis=-1, keepdims=True)
        return (xe * lax.rsqrt(ms + EPS) * gain + shift).astype(BF16)

    def project(he):
        p = jnp.dot(he, wc_ref[0], preferred_element_type=F32)
        aqk = jnp.dot(he[:TM], wr_ref[0, :, :512], preferred_element_type=F32)
        r = jnp.dot(he[:TM], wr_ref[0, :, 512:], preferred_element_type=F32)
        return p, aqk, r

    def finish(s, p, aqk, r):
        pm = p[:TM]
        seg_first = i <= 1
        seg_last = jnp.logical_or(i == 0, i == nt - 1)
        prev_row = jnp.where(seg_first, 0.0, p[TM + HALO - 1:TM + HALO])
        next_row = jnp.where(seg_last, 0.0, p[TM + HALO:TM + HALO + 1])
        rows = _iota2((TM, 1), 0)
        pm1 = jnp.where(rows == 0, prev_row, pltpu.roll(pm, 1, axis=0))
        pp1 = jnp.where(rows == TM - 1, next_row, pltpu.roll(pm, TM - 1, axis=0))
        cw = cw_ref[0]
        conv = cw[0:1] * pm1 + cw[1:2] * pm + cw[2:3] * pp1 + cb_ref[0]
        co = _silu(conv)

        ssd_ref[s] = co[:, :640].astype(BF16)
        gqk = co[:, 640:1152]
        gqk = gqk * lax.rsqrt(_group_sum64(gqk * gqk) + EPS)
        gdn_ref[s, :, :256] = (gqk[:, :256] * (HD ** -0.5)).astype(BF16)
        gdn_ref[s, :, 256:512] = gqk[:, 256:].astype(BF16)
        gdn_ref[s, :, 512:] = co[:, 1152:1408].astype(BF16)

        zs_ref[s, :, :640] = _silu(r[:, :640]).astype(BF16)
        zs_ref[s, :, 640:] = _silu(r[:, 768:1152]).astype(BF16)

        aqk = aqk * lax.rsqrt(_group_sum64(aqk * aqk) * (1.0 / HD) + EPS) * qkw_ref[0]
        lane = _iota2((1, 512), 1)
        swapped = jnp.where(jnp.bitwise_and(lane, 31) < 16,
                            pltpu.roll(aqk, 512 - 16, axis=1), pltpu.roll(aqk, 16, axis=1))
        cos = jnp.concatenate([cos_ref[...]] * 4, axis=1)
        sin = jnp.concatenate([sin_ref[...]] * 4, axis=1)
        aqk = aqk * cos + swapped * sin
        q_ref[s] = (aqk[:, :384] * Q_PRESCALE).astype(BF16)
        ones = jnp.ones((TM, HD), F32)
        zeros = jnp.zeros((TM, HD), F32)
        for g in range(ATT_KVH):
            kv_ref[s, g] = jnp.concatenate([aqk[:, 384 + g * HD:384 + (g + 1) * HD], zeros,
                                            r[:, 640 + g * HD:640 + (g + 1) * HD], ones], axis=1).astype(BF16)
        small_ref[s] = r[:, 1152:1280]

    hes = [normed(s) for s in range(BPS)]
    prs = [project(he) for he in hes]
    for s in range(BPS):
        finish(s, *prs[s])


def _in_projection(xa, xb, off, layer, mod3, norm_w, w_conv, w_rest, conv_w, conv_b, qk_w, cos_t, sin_t):
    bsz = xb.shape[0]
    assert bsz % BPS == 0 and BPS == 2
    nt = xb.shape[1] // TM + off
    t = nt * TM
    nblk8 = xb.shape[1] // HALO
    per_tile = TM // HALO
    row = lambda b, i: (b, i, 0)
    lay = lambda b, i: (layer, 0, 0)
    outs = [jax.ShapeDtypeStruct((bsz, t, w), dt)
            for w, dt in ((640, BF16), (768, BF16), (1024, BF16), (384, BF16), (128, F32))]
    outs.insert(4, jax.ShapeDtypeStruct((bsz, ATT_KVH, t, 256), BF16))

    def mod_spec(s):
        return pl.BlockSpec((1, 1, 3 * D_MODEL), lambda b, i: (jnp.where(i == 0, bsz, b * BPS + s), 0, 0))

    return pl.pallas_call(
        functools.partial(_proj_kernel, nt),
        out_shape=outs,
        grid=(bsz // BPS, nt),
        in_specs=[
            pl.BlockSpec((BPS, TM, D_MODEL), lambda b, i: (b, 0, 0)),
            pl.BlockSpec((BPS, TM, D_MODEL), lambda b, i: (b, jnp.maximum(i - off, 0), 0)),
            pl.BlockSpec((BPS, HALO, D_MODEL), lambda b, i: (b, jnp.maximum((i - off) * per_tile - 1, 0), 0)),
            pl.BlockSpec((BPS, HALO, D_MODEL),
                         lambda b, i: (b, jnp.minimum((i - off + 1) * per_tile, nblk8 - 1), 0)),
            mod_spec(0), mod_spec(1),
            pl.BlockSpec((1, 1, D_MODEL), lay),
            pl.BlockSpec((1, D_MODEL, CONV_DIM), lay),
            pl.BlockSpec((1, D_MODEL, REST_DIM), lay),
            pl.BlockSpec((1, 3, CONV_DIM), lay),
            pl.BlockSpec((1, 1, CONV_DIM), lay),
            pl.BlockSpec((1, 1, 512), lay),
            pl.BlockSpec((TM, 128), lambda b, i: (i, 0)),
            pl.BlockSpec((TM, 128), lambda b, i: (i, 0)),
        ],
        out_specs=[pl.BlockSpec((BPS, TM, 640), row), pl.BlockSpec((BPS, TM, 768), row),
                   pl.BlockSpec((BPS, TM, 1024), row), pl.BlockSpec((BPS, TM, 384), row),
                   pl.BlockSpec((BPS, ATT_KVH, TM, 256), lambda b, i: (b, 0, i, 0)),
                   pl.BlockSpec((BPS, TM, 128), row)],
        compiler_params=pltpu.CompilerParams(dimension_semantics=("arbitrary", "arbitrary"),
                                             vmem_limit_bytes=VMEM_LIMIT),
        name="in_projection",
    )(xa, xb, xb, xb, mod3, mod3, norm_w, w_conv, w_rest, conv_w, conv_b, qk_w, cos_t, sin_t)


def _ssd_kernel(nc, ctx_chunks, ctx_out, xin_ref, small_ref, prm_ref, dskip_ref, y_ref,
                cum_s, cumt_s, dtt_s, st_s, sin_s):
    q = SSD_CHUNK
    hpg = SSD_HEADS // 2
    ri = _iota2((q, q), 0)
    ci = _iota2((q, q), 1)
    ltri = (ri >= ci).astype(BF16)
    utri = (ri <= ci).astype(BF16)
    lane = _iota2((1, 128), 1)
    fwd_lane = lane < SSD_HEADS

    a_row = -jnp.exp(prm_ref[0, 0:1, :])
    bias_row = prm_ref[0, 1:2, :]

    pb = 6
    tri = jnp.broadcast_to(jnp.concatenate([ltri, utri], axis=0)[None], (pb, 2 * q, q))

    def prep(cb, carry):
        base = pl.multiple_of(cb * (pb * q), q)
        dt = _softplus(small_ref[0, pl.ds(base, pb * q), :] + bias_row)
        pieces = _split3((dt * a_row).reshape(pb, q, 128))
        dot = lambda x: lax.dot_general(tri, x, (((2,), (1,)), ((0,), (0,))), preferred_element_type=F32)
        sums = dot(pieces[0]) + (dot(pieces[1]) + dot(pieces[2]))
        cum = jnp.where(fwd_lane, sums[:, :q], sums[:, q:])
        cum_s[pl.ds(base, pb * q), :] = cum.reshape(pb * q, 128)
        for i in range(pb):
            rows = pl.ds(pl.multiple_of(base + i * q, q), q)
            cumt_s[rows, :] = cum[i].T
            dtt_s[rows, :] = dt[i * q:(i + 1) * q].T
        return carry

    lax.fori_loop(0, nc // pb, prep, 0)

    def end_of(cumt, d, col):
        return cumt[col:col + 1, q - 1:q] if d == 0 else cumt[col:col + 1, 0:1]

    nb = 2

    def states(cb, carry):
        lhs, rhs = [], []
        for cc in range(nb):
            base = pl.multiple_of((cb * nb + cc) * q, q)
            xin = xin_ref[0, pl.ds(base, q), :]
            cumt = cumt_s[pl.ds(base, q), :]
            dtt = dtt_s[pl.ds(base, q), :]
            for g in range(2):
                bgt = xin[:, 384 + g * HD:384 + (g + 1) * HD].astype(F32).T
                for d in range(2):
                    for hg in range(hpg):
                        h = g * hpg + hg
                        col = d * SSD_HEADS + h
                        w_row = jnp.exp(end_of(cumt, d, col) - cumt[col:col + 1, :]) * dtt[col:col + 1, :]
                        lhs.append(bgt * w_row)
                        rhs.append(xin[:, h * HD:(h + 1) * HD])
        st = _bmm(jnp.stack(lhs), jnp.stack(rhs))
        st_s[pl.ds(cb * (12 * nb), 12 * nb)] = st
        return carry

    lax.fori_loop(0, nc // nb, states, 0)

    orders = (list(range(nc)),
              list(range(ctx_chunks - 1, -1, -1)) + list(range(nc - 1, ctx_chunks - 1, -1)))
    for g in range(2):
        for d in range(2):
            for hg in range(hpg):
                col = d * SSD_HEADS + g * hpg + hg
                s = jnp.zeros((HD, HD), F32)
                for c in orders[d]:
                    slot = c * 12 + g * 6 + d * 3 + hg
                    sin_s[slot] = s
                    row = c * q + (q - 1 if d == 0 else 0)
                    s = s * jnp.exp(cum_s[row:row + 1, col:col + 1]) + st_s[slot]

    lower = ri >= ci
    upper = ri <= ci
    dsk = dskip_ref[0]

    def outputs(cb, carry):
        sin = sin_s[pl.ds(cb * (12 * nb), 12 * nb)]
        ws, xs, cstack, es, scs = [], [], [], [], []
        per_chunk = []
        for cc in range(nb):
            base = pl.multiple_of((cb * nb + cc) * q, q)
            xin = xin_ref[0, pl.ds(base, q), :]
            per_chunk.append((base, xin, cum_s[pl.ds(base, q), :], cumt_s[pl.ds(base, q), :],
                              dtt_s[pl.ds(base, q), :]))
        sc = _bmm_nt(jnp.stack([xin[:, 512 + g * HD:512 + (g + 1) * HD] for _, xin, _, _, _ in per_chunk
                                for g in range(2)]),
                     jnp.stack([xin[:, 384 + g * HD:384 + (g + 1) * HD] for _, xin, _, _, _ in per_chunk
                                for g in range(2)]))
        for cc, (base, xin, cum, cumt, dtt) in enumerate(per_chunk):
            e_chunk = {}
            for g in range(2):
                cg = xin[:, 512 + g * HD:512 + (g + 1) * HD]
                for hg in range(hpg):
                    h = g * hpg + hg
                    w = None
                    for d in range(2):
                        col = d * SSD_HEADS + h
                        gmat = jnp.broadcast_to(cum[:, col:col + 1], (q, q))
                        lm = jnp.exp(jnp.where(lower if d == 0 else upper, gmat - cumt[col:col + 1, :], NEG))
                        wd = lm * dtt[col:col + 1, :]
                        w = wd if w is None else w + wd
                        e_chunk[g * 6 + d * 3 + hg] = jnp.exp(gmat[:, :HD])
                    ws.append(w * sc[cc * 2 + g])
                    xs.append(xin[:, h * HD:(h + 1) * HD])
            es.extend(e_chunk[i] for i in range(12))
            cstack.extend(xin[:, 512 + (i // 6) * HD:512 + (i // 6 + 1) * HD] for i in range(12))
        ydiag = _bmm(jnp.stack(ws), jnp.stack(xs))
        yoff = _bmm(jnp.stack(cstack), sin)
        for cc, (base, xin, cum, cumt, dtt) in enumerate(per_chunk):
            outs = []
            for h in range(SSD_HEADS):
                g, hg = divmod(h, hpg)
                y = ydiag[cc * SSD_HEADS + h] + dsk[:, h * HD:(h + 1) * HD] * xs[cc * SSD_HEADS + h]
                for d in range(2):
                    i = cc * 12 + g * 6 + d * 3 + hg
                    y = y + yoff[i] * es[i]
                outs.append(y)
            y_ref[0, pl.ds(base, q), :] = jnp.concatenate(outs, axis=1).astype(BF16)
        return carry

    if ctx_out:
        first_trip = 0
    else:
        assert ctx_chunks % nb == 0
        first_trip = ctx_chunks // nb
        y_ref[0, :ctx_chunks * q, :] = jnp.zeros((ctx_chunks * q, SSD_W), BF16)
    lax.fori_loop(first_trip, nc // nb, outputs, 0)


def _ssd(ssd_in, small, prm, dskip, layer, ctx_out):
    bsz, t, _ = ssd_in.shape
    nc = t // SSD_CHUNK
    return pl.pallas_call(
        functools.partial(_ssd_kernel, nc, CTX // SSD_CHUNK, ctx_out),
        out_shape=jax.ShapeDtypeStruct((bsz, t, SSD_W), BF16),
        grid=(bsz,),
        in_specs=[pl.BlockSpec((1, t, 640), lambda b: (b, 0, 0)),
                  pl.BlockSpec((1, t, 128), lambda b: (b, 0, 0)),
                  pl.BlockSpec((1, 8, 128), lambda b: (layer, 0, 0)),
                  pl.BlockSpec((1, 1, SSD_W), lambda b: (layer, 0, 0))],
        out_specs=pl.BlockSpec((1, t, SSD_W), lambda b: (b, 0, 0)),
        scratch_shapes=[pltpu.VMEM((t, 128), F32), pltpu.VMEM((t, 128), F32), pltpu.VMEM((t, 128), F32),
                        pltpu.VMEM((nc * 12, HD, HD), F32),
                        pltpu.VMEM((nc * 12, HD, HD), F32)],
        compiler_params=pltpu.CompilerParams(dimension_semantics=("arbitrary",),
                                             vmem_limit_bytes=VMEM_LIMIT),
        name="ssd",
    )(ssd_in, small, prm, dskip)


def _bd4(x):
    xb = x.astype(BF16)
    blk = jnp.right_shift(_iota2((1, 4 * HD), 1), 6)
    zero = jnp.zeros_like(xb)
    return jnp.concatenate([jnp.where(blk == h, xb, zero) for h in range(GDN_HEADS)], axis=0)


def _bmm(a, b):
    return lax.dot_general(a.astype(BF16), b.astype(BF16), (((2,), (1,)), ((0,), (0,))),
                           preferred_element_type=F32)


def _bmm_nt(a, b):
    return lax.dot_general(a.astype(BF16), b.astype(BF16), (((2,), (2,)), ((0,), (0,))),
                           preferred_element_type=F32)


def _unit_tri_inverse_stages(a_strict):
    n = a_strict.shape[-1]
    ri = _iota2((1, n, n), 1)
    ci = _iota2((1, n, n), 2)

    def blk(shift):
        return jnp.right_shift(ri, shift) == jnp.right_shift(ci, shift)

    nd = jnp.where(blk(3), -a_strict, 0.0)
    p0 = (ri == ci).astype(F32) + nd
    q1 = _bmm(nd, nd)
    yield None
    t = _bmm(jnp.concatenate([p0, q1], axis=1), q1)
    yield None
    p1 = p0 + t[:, :n]
    inv = p1 + _bmm(p1, t[:, n:])
    yield None
    for shift in (4, 5, 6):
        e = jnp.where(jnp.logical_and(blk(shift), jnp.logical_not(blk(shift - 1))), a_strict, 0.0)
        ie = _bmm(inv, e)
        yield None
        inv = inv - _bmm(ie, inv)
        yield None
    yield inv


def _gdn_attn_kernel(nch, ctx_chunks, ctx_attn, gin_ref, small_ref, prm_ref, q_ref, kv_ref, o_ref, att_ref,
                     gc_s, rows_s, mq_s, n_s, cd_s, of_s, ob_s):
    c64 = GDN_CHUNK
    w4 = GDN_W
    nsc = nch // 2
    ri128 = _iota2((128, 128), 0)
    ci128 = _iota2((128, 128), 1)
    same = jnp.right_shift(ri128, 6) == jnp.right_shift(ci128, 6)
    lbd = jnp.logical_and(same, ri128 >= ci128).astype(BF16)
    ubd = jnp.logical_and(same, ri128 <= ci128).astype(BF16)
    lane = _iota2((1, 128), 1)
    fwd_lane = lane < SM_A + GDN_HEADS
    beta_lane = jnp.logical_and(lane >= SM_B, lane < SM_B + 2 * GDN_HEADS)

    a_row = -jnp.exp(prm_ref[0, 0:1, :])
    bias_row = prm_ref[0, 1:2, :]

    pb = 6
    tri = jnp.broadcast_to(jnp.concatenate([lbd, ubd], axis=0)[None], (pb, 256, 128))

    def prep(sb, carry):
        base = pl.multiple_of(sb * (pb * 128), 128)
        sm = small_ref[0, pl.ds(base, pb * 128), :]
        pieces = _split3((a_row * _softplus(sm + bias_row)).reshape(pb, 128, 128))
        dot = lambda x: lax.dot_general(tri, x, (((2,), (1,)), ((0,), (0,))), preferred_element_type=F32)
        sums = dot(pieces[0]) + (dot(pieces[1]) + dot(pieces[2]))
        cum = jnp.where(fwd_lane, sums[:, :128], sums[:, 128:])
        gc_s[pl.ds(base, pb * 128), :] = cum.reshape(pb * 128, 128)
        beta = jax.nn.sigmoid(sm)
        for i in range(pb):
            slab_t = jnp.where(beta_lane, beta[i * 128:(i + 1) * 128], cum[i]).T
            for half in range(2):
                lo = half * c64
                table = []
                for first in (SM_A, SM_A + GDN_HEADS, SM_B, SM_B + GDN_HEADS):
                    table.append(jnp.concatenate(
                        [slab_t[first + h:first + h + 1, lo:lo + c64] for h in range(GDN_HEADS)], axis=1))
                for d, edge in ((0, lo + c64 - 1), (1, lo)):
                    first = SM_A + d * GDN_HEADS
                    table.append(jnp.concatenate(
                        [jnp.broadcast_to(slab_t[first + h:first + h + 1, edge:edge + 1], (1, c64))
                         for h in range(GDN_HEADS)], axis=1))
                table.append(jnp.zeros((2, w4), F32))
                rows_s[pl.ds(pl.multiple_of((sb * pb + i) * 16 + half * 8, 8), 8), :] = (
                    jnp.concatenate(table, axis=0))
        return carry

    lax.fori_loop(0, nsc // pb, prep, 0)

    ri = _iota2((c64, c64), 0)
    ci = _iota2((c64, c64), 1)
    incl = (ri >= ci, ri <= ci)
    strict = (ri > ci, ri < ci)

    nb = 4

    def local(cb):
        chunks = []
        for cc in range(nb):
            c = cb * nb + cc
            base = _aligned(c * c64, c64)
            gin = gin_ref[0, pl.ds(base, c64), :]
            gc = gc_s[pl.ds(base, c64), :]
            rows = rows_s[pl.ds(_aligned(c * 8, 8), 8), :]
            qs = [gin[:, h * HD:(h + 1) * HD] for h in range(GDN_HEADS)]
            ks = [gin[:, w4 + h * HD:w4 + (h + 1) * HD] for h in range(GDN_HEADS)]
            vs = [gin[:, 2 * w4 + h * HD:2 * w4 + (h + 1) * HD] for h in range(GDN_HEADS)]
            chunks.append((c, base, gc, rows, qs, ks, vs))
        kq = _bmm_nt(jnp.stack([jnp.concatenate([ch[5][h], ch[4][h]], axis=0)
                                for ch in chunks for h in range(GDN_HEADS)]),
                     jnp.stack([ch[5][h] for ch in chunks for h in range(GDN_HEADS)]))
        yield
        mb_l, rhs_l, lhs_l, qd_l = [], [], [], []
        for ci, (c, base, gc, rows, qs, ks, vs) in enumerate(chunks):
            kts = [k.astype(F32).T for k in ks]
            for d in range(2):
                for h in range(GDN_HEADS):
                    sl = slice(h * HD, (h + 1) * HD)
                    colg = SM_A + d * GDN_HEADS + h
                    kqh = kq[ci * GDN_HEADS + h]
                    gmat = jnp.broadcast_to(gc[:, colg:colg + 1], (c64, c64))
                    gcr, br, endr = rows[d:d + 1, sl], rows[2 + d:3 + d, sl], rows[4 + d:5 + d, sl]
                    dec = jnp.exp(jnp.where(incl[d], gmat - gcr, NEG))
                    mb_l.append(jnp.where(strict[d], kqh[:c64] * dec, 0.0) * br)
                    eg = jnp.exp(gmat)
                    kdtb = kts[h] * (jnp.exp(endr - gcr) * br)
                    lhs_l.append(jnp.concatenate([kdtb, kqh[c64:] * dec * br], axis=0))
                    rhs_l.append(jnp.concatenate([vs[h], ks[h] * eg], axis=1))
                    qd_l.append(qs[h] * eg)
        z = None
        for z in _unit_tri_inverse_stages(jnp.stack(mb_l)):
            if z is None:
                yield
        zr = _bmm(z, jnp.stack(rhs_l))
        yield
        a12 = _bmm(jnp.stack(lhs_l), zr)
        yield
        for ci, (c, base, gc, rows, qs, ks, vs) in enumerate(chunks):
            for d in range(2):
                slot = c * 2 + d
                ids = [ci * 2 * GDN_HEADS + d * GDN_HEADS + h for h in range(GDN_HEADS)]
                mq = [jnp.concatenate([-a12[n][:c64, HD:], qd_l[n] - a12[n][c64:, HD:]], axis=0) for n in ids]
                mq_s[slot] = jnp.concatenate(mq, axis=1).astype(BF16)
                n_s[slot] = jnp.concatenate([a12[n][:c64, :HD] for n in ids], axis=1)
                cd_s[slot] = jnp.broadcast_to(jnp.exp(rows[4 + d:5 + d]), (8, w4))
                oc = jnp.concatenate([a12[n][c64:, :HD] for n in ids], axis=1)
                if d == 0:
                    of_s[pl.ds(base, c64), :] = oc
                else:
                    ob_s[pl.ds(base, c64), :] = oc

    t_all = q_ref.shape[1]
    heads_per_kv = ATT_QH // ATT_KVH

    def attend(tile, nkeys):
        rows = pl.ds(_aligned(tile * TM, TM), TM)
        qt = q_ref[0, rows, :]
        outs = []

        def finish(h, s):
            g = h // heads_per_kv
            p = jnp.exp2(s - jnp.max(s, axis=-1, keepdims=True))
            acc = _mm(p, kv_ref[0, g, :nkeys, 2 * HD:])
            outs.append(acc[:, :HD] / acc[:, HD:])
            if h % heads_per_kv == heads_per_kv - 1:
                tile_out = outs[-heads_per_kv:] + [jnp.zeros((TM, HD), F32)]
                att_ref[0, g, rows, :] = jnp.concatenate(tile_out, axis=1).astype(BF16)

        s_prev = None
        for h in range(ATT_QH):
            k = kv_ref[0, h // heads_per_kv, :nkeys, :HD]
            s = _mm_nt(qt[:, h * HD:(h + 1) * HD], k)
            yield
            if s_prev is not None:
                finish(h - 1, s_prev)
                yield
            s_prev = s
        finish(ATT_QH - 1, s_prev)
        yield

    def trip(j, nkeys):
        streams = ([attend(j, nkeys)] if nkeys else []) + [local(j)]
        while streams:
            for st in list(streams):
                try:
                    next(st)
                except StopIteration:
                    streams.remove(st)

    trip(0, CTX if ctx_attn else 0)
    if not ctx_attn:
        att_ref[0, :, :TM, :] = jnp.zeros((ATT_KVH, TM, 256), BF16)

    def later_trip(j, carry):
        trip(j, t_all)
        return carry

    lax.fori_loop(1, nch // nb, later_trip, 0)

    def step(t, states):
        cf = t
        cb = jnp.where(t < ctx_chunks, ctx_chunks - 1 - t, nch - 1 + ctx_chunks - t)
        new_states = []
        for d, c in ((0, cf), (1, cb)):
            slot = c * 2 + d
            s = states[d]
            r = jnp.dot(mq_s[slot], _bd4(s), preferred_element_type=F32)
            new_states.append(s * cd_s[slot][0:1] + r[:c64] + n_s[slot])
            rows = pl.ds(pl.multiple_of(c * c64, c64), c64)
            if d == 0:
                of_s[rows, :] = of_s[rows, :] + r[c64:]
            else:
                ob_s[rows, :] = ob_s[rows, :] + r[c64:]
        return tuple(new_states)

    zero = jnp.zeros((c64, w4), F32)
    lax.fori_loop(0, nch, step, (zero, zero))
    o_ref[0] = (of_s[...] + ob_s[...]).astype(BF16)


def _gdn_attention(gdn_in, small, prm, q, kv, layer, ctx_attn):
    bsz, t, _ = gdn_in.shape
    nch = t // GDN_CHUNK
    assert nch // 4 == t // TM
    return pl.pallas_call(
        functools.partial(_gdn_attn_kernel, nch, CTX // GDN_CHUNK, ctx_attn),
        out_shape=[jax.ShapeDtypeStruct((bsz, t, GDN_W), BF16),
                   jax.ShapeDtypeStruct((bsz, ATT_KVH, t, 256), BF16)],
        grid=(bsz,),
        in_specs=[pl.BlockSpec((1, t, 768), lambda b: (b, 0, 0)),
                  pl.BlockSpec((1, t, 128), lambda b: (b, 0, 0)),
                  pl.BlockSpec((1, 8, 128), lambda b: (layer, 0, 0)),
                  pl.BlockSpec((1, t, ATT_W), lambda b: (b, 0, 0)),
                  pl.BlockSpec((1, ATT_KVH, t, 256), lambda b: (b, 0, 0, 0))],
        out_specs=[pl.BlockSpec((1, t, GDN_W), lambda b: (b, 0, 0), pipeline_mode=pl.Buffered(1)),
                   pl.BlockSpec((1, ATT_KVH, t, 256), lambda b: (b, 0, 0, 0), pipeline_mode=pl.Buffered(1))],
        scratch_shapes=[pltpu.VMEM((t, 128), F32),
                        pltpu.VMEM((nch * 8, GDN_W), F32),
                        pltpu.VMEM((nch * 2, 128, GDN_W), BF16),
                        pltpu.VMEM((nch * 2, GDN_CHUNK, GDN_W), F32),
                        pltpu.VMEM((nch * 2, 8, GDN_W), F32),
                        pltpu.VMEM((t, GDN_W), F32),
                        pltpu.VMEM((t, GDN_W), F32)],
        compiler_params=pltpu.CompilerParams(dimension_semantics=("arbitrary",),
                                             vmem_limit_bytes=VMEM_LIMIT),
        name="gdn_attention",
    )(gdn_in, small, prm, q, kv)


def _out_kernel(first_tile, xa_ref, x_ref, y_ref, o_ref, a_ref, zs_ref, mod0_ref, mod1_ref, sw_ref, gw_ref, w_ref,
                out_ref):
    mods = (mod0_ref, mod1_ref)

    def mixed(s):
        zs = zs_ref[s].astype(F32)
        t = y_ref[s].astype(F32) * zs[:, :SSD_W]
        ssd = t * lax.rsqrt(jnp.mean(t * t, axis=-1, keepdims=True) + EPS) * sw_ref[0]
        o = o_ref[s].astype(F32)
        gdn = o * lax.rsqrt(_group_sum64(o * o) * (1.0 / HD) + EPS) * gw_ref[0] * zs[:, SSD_W:SSD_W + GDN_W]
        a = a_ref[s]
        att = jnp.concatenate([a[g][:, :HD * ATT_QH // ATT_KVH] for g in range(ATT_KVH)], axis=1)
        att = att * zs[:, SSD_W + GDN_W:]
        return jnp.concatenate([ssd, gdn, att], axis=1).astype(BF16)

    mixes = [mixed(s) for s in range(BPS)]
    projs = [jnp.dot(mix, w_ref[0], preferred_element_type=F32) for mix in mixes]
    for s in range(BPS):
        gate = mods[s][0][:, 2 * D_MODEL:]
        x = jnp.where(pl.program_id(1) + first_tile == 0, xa_ref[s], x_ref[s])
        out_ref[s] = x + gate * projs[s]


def _out_projection(xa, xb, off, layer, y, o, a, zs, mod3, ssd_nw, gdn_nw, w_out, first_tile):
    bsz, t, _ = y.shape
    nt = t // TM - first_tile
    row = lambda b, i: (b, i + first_tile, 0)
    lay = lambda b, i: (layer, 0, 0)

    def mod_spec(s):
        return pl.BlockSpec((1, 1, 3 * D_MODEL),
                            lambda b, i: (jnp.where(i + first_tile == 0, bsz, b * BPS + s), 0, 0))

    return pl.pallas_call(
        functools.partial(_out_kernel, first_tile),
        out_shape=jax.ShapeDtypeStruct((bsz, nt * TM, D_MODEL), F32),
        grid=(bsz // BPS, nt),
        in_specs=[pl.BlockSpec((BPS, TM, D_MODEL), lambda b, i: (b, 0, 0)),
                  pl.BlockSpec((BPS, TM, D_MODEL), lambda b, i: (b, jnp.maximum(i + first_tile - off, 0), 0)),
                  pl.BlockSpec((BPS, TM, SSD_W), row),
                  pl.BlockSpec((BPS, TM, GDN_W), row),
                  pl.BlockSpec((BPS, ATT_KVH, TM, 256), lambda b, i: (b, 0, i + first_tile, 0)),
                  pl.BlockSpec((BPS, TM, D_MODEL), row),
                  mod_spec(0), mod_spec(1),
                  pl.BlockSpec((1, 1, SSD_W), lay),
                  pl.BlockSpec((1, 1, GDN_W), lay),
                  pl.BlockSpec((1, D_MODEL, D_MODEL), lay)],
        out_specs=pl.BlockSpec((BPS, TM, D_MODEL), lambda b, i: (b, i, 0)),
        compiler_params=pltpu.CompilerParams(dimension_semantics=("arbitrary", "arbitrary"),
                                             vmem_limit_bytes=VMEM_LIMIT),
        name="out_projection",
    )(xa, xb, y, o, a, zs, mod3, mod3, ssd_nw, gdn_nw, w_out)


_REST_SEGS = [(2076, 384), (2460, 128), (1408, 384), (1804, 256), (2588, 128), (2716, 384),
              (1792, 12), (2060, 8), (2068, 8)]


def _prepare_params(w_in, conv_w, conv_b, ssd_A_log, ssd_dt_bias, gdn_A_log, gdn_dt_bias):
    depth = w_in.shape[0]
    w_conv = w_in[:, :, :CONV_DIM].astype(BF16)
    pad = jnp.zeros(w_in.shape[:2] + (REST_DIM - sum(n for _, n in _REST_SEGS),), w_in.dtype)
    w_rest = jnp.concatenate([lax.slice_in_dim(w_in, a, a + n, axis=2) for a, n in _REST_SEGS] + [pad],
                             axis=2).astype(BF16)
    cw = conv_w
    cb = conv_b[:, None, :]

    def lanes(a, b):
        v = jnp.concatenate([a.reshape(depth, -1), b.reshape(depth, -1)], axis=1)
        return jnp.pad(v, ((0, 0), (0, 128 - v.shape[1])))

    prm = jnp.stack([lanes(ssd_A_log, gdn_A_log), lanes(ssd_dt_bias, gdn_dt_bias)], axis=1)
    prm = jnp.pad(prm, ((0, 0), (0, 6), (0, 0)))
    return w_conv, w_rest, cw, cb, prm


def _rope_tables(t):
    f32 = np.float32
    pos = np.arange(t - CTX)
    n_freq = HD // 4
    freqs = np.power(f32(ROPE_THETA), -np.arange(n_freq, dtype=f32) / f32(n_freq)).astype(f32)
    ang_r = ((pos // GRID_W).astype(f32)[:, None] * freqs).astype(f32)
    ang_c = ((pos % GRID_W).astype(f32)[:, None] * freqs).astype(f32)
    cos = np.concatenate([np.cos(ang_r)] * 2 + [np.cos(ang_c)] * 2, axis=1)
    sin = np.concatenate([-np.sin(ang_r), np.sin(ang_r), -np.sin(ang_c), np.sin(ang_c)], axis=1)
    cos = np.concatenate([np.ones((CTX, HD), f32), cos], axis=0).astype(f32)
    sin = np.concatenate([np.zeros((CTX, HD), f32), sin], axis=0).astype(f32)
    return jnp.asarray(np.tile(cos, (1, 2))), jnp.asarray(np.tile(sin, (1, 2)))


def kernel(x, c, ctx, c_ctx, norm_w, w_mod, b_mod, w_in, conv_w, conv_b, ssd_A_log, ssd_dt_bias, ssd_D,
           ssd_norm_w, gdn_A_log, gdn_dt_bias, gdn_norm_w, q_norm_w, k_norm_w, w_out):
    bsz = x.shape[0]
    t = CTX + x.shape[1]
    c_all = jnp.concatenate([c, c_ctx[None, :], jnp.zeros((7, D_MODEL), F32)], axis=0)
    cos_t, sin_t = _rope_tables(t)
    w_conv, w_rest, cw, cb, prm = _prepare_params(w_in, conv_w, conv_b, ssd_A_log, ssd_dt_bias,
                                                  gdn_A_log, gdn_dt_bias)
    qk_w = jnp.concatenate([jnp.tile(q_norm_w, (1, ATT_QH)), jnp.tile(k_norm_w, (1, ATT_KVH))], axis=1)[:, None]
    dskip = jnp.repeat(ssd_D, HD, axis=1)[:, None]
    gdn_nw = jnp.tile(gdn_norm_w, (1, GDN_HEADS))[:, None]
    norm_w3, ssd_nw = norm_w[:, None], ssd_norm_w[:, None]
    w_out16 = w_out.astype(BF16)

    xa, xb, off = ctx, x, 1
    for layer in range(DEPTH):
        first_tile = 1 if layer == DEPTH - 1 else 0
        mod3 = _modulation(c_all, w_mod, b_mod, layer).reshape(bsz + 8, 1, 3 * D_MODEL)
        ssd_in, gdn_in, zs, q, kv, small = _in_projection(
            xa, xb, off, layer, mod3, norm_w3, w_conv, w_rest, cw, cb, qk_w, cos_t, sin_t)
        y = _ssd(ssd_in, small, prm, dskip, layer, first_tile == 0)
        o, a = _gdn_attention(gdn_in, small, prm, q, kv, layer, first_tile == 0)
        out = _out_projection(xa, xb, off, layer, y, o, a, zs, mod3, ssd_nw, gdn_nw, w_out16, first_tile)
        xa, xb, off = out, out, 0
    return out
```

```python
import functools
import math

import jax
import jax.numpy as jnp
import numpy as np
from jax import lax
from jax.experimental import pallas as pl
from jax.experimental.pallas import tpu as pltpu

F32 = jnp.float32
BF16 = jnp.bfloat16

D_MODEL = 1024
CTX = 256
GRID_W = 64
EPS = 1e-6
DEPTH = 2

HD = 64
SSD_HEADS = 6
SSD_W = SSD_HEADS * HD
SSD_CHUNK = 128
GDN_HEADS = 4
GDN_W = GDN_HEADS * HD
GDN_CHUNK = 64
ATT_QH = 6
ATT_KVH = 2
ATT_W = ATT_QH * HD
ROPE_THETA = 10000.0
Q_PRESCALE = (HD ** -0.5) * math.log2(math.e)

CONV_DIM = 1408
REST_DIM = 1792
TM = 256
HALO = 8
NEG = -1e30
VMEM_LIMIT = 56 * 1024 * 1024

SM_DT = 0
SM_A = 12
SM_B = 20


def _mm(a, b):
    return jnp.dot(a.astype(BF16), b.astype(BF16), preferred_element_type=F32)


def _mm_nt(a, b):
    return lax.dot_general(a.astype(BF16), b.astype(BF16), (((1,), (1,)), ((), ())),
                           preferred_element_type=F32)


def _split3(a):
    h = a.astype(BF16)
    r = a - h.astype(F32)
    m = r.astype(BF16)
    l = (r - m.astype(F32)).astype(BF16)
    return h, m, l


def _silu(x):
    return x * jax.nn.sigmoid(x)


def _softplus(x):
    return jnp.maximum(x, 0.0) + jnp.log1p(jnp.exp(-jnp.abs(x)))


def _iota2(shape, dim):
    return lax.broadcasted_iota(jnp.int32, shape, dim)


def _aligned(x, m):
    return x if isinstance(x, int) else pl.multiple_of(x, m)


def _group_sum64(xx):
    r = jnp.right_shift(_iota2((128, 128), 0), 6)
    c = jnp.right_shift(_iota2((128, 128), 1), 6)
    g = (r == c).astype(BF16)
    xb = xx.astype(BF16)
    outs = [jnp.dot(xb[:, t * 128:(t + 1) * 128], g, preferred_element_type=F32)
            for t in range(xx.shape[1] // 128)]
    return outs[0] if len(outs) == 1 else jnp.concatenate(outs, axis=1)


def _mod_kernel(c_ref, w_ref, b_ref, o_ref):
    o_ref[...] = _mm(_silu(c_ref[...]), w_ref[0]) + b_ref[0]


def _modulation(c_all, w_mod, b_mod, layer):
    n = w_mod.shape[2]
    bn = 768
    rows = c_all.shape[0]
    return pl.pallas_call(
        _mod_kernel,
        out_shape=jax.ShapeDtypeStruct((rows, n), F32),
        grid=(n // bn,),
        in_specs=[pl.BlockSpec((rows, D_MODEL), lambda j: (0, 0)),
                  pl.BlockSpec((1, D_MODEL, bn), lambda j: (layer, 0, j)),
                  pl.BlockSpec((1, 1, bn), lambda j: (layer, 0, j))],
        out_specs=pl.BlockSpec((rows, bn), lambda j: (0, j)),
        compiler_params=pltpu.CompilerParams(dimension_semantics=("arbitrary",),
                                             vmem_limit_bytes=VMEM_LIMIT),
        name="modulation",
    )(c_all, w_mod, b_mod.reshape(b_mod.shape[0], 1, n))


BPS = 4


def _proj_kernel(nt, xa_ref, xm_ref, xp_ref, xn_ref, *refs):
    i = pl.program_id(1)
    mods = refs[:BPS]
    (nw_ref, wc_ref, wr_ref, cw_ref, cb_ref, qkw_ref, cos_ref, sin_ref,
     ssd_ref, gdn_ref, zs_ref, q_ref, kv_ref, small_ref) = refs[BPS:]

    def normed(s):
        m = mods[s][0]
        gain = nw_ref[0] * (1.0 + m[:, D_MODEL:2 * D_MODEL])
        shift = m[:, :D_MODEL]
        xm = jnp.where(i == 0, xa_ref[s], xm_ref[s])
        xe = jnp.concatenate([xm, xp_ref[s], xn_ref[s]], axis=0)
        ms = jnp.mean(xe * xe, axis=-1, keepdims=True)
        return (xe * lax.rsqrt(ms + EPS) * gain + shift).astype(BF16)

    def project(he):
        p = jnp.dot(he, wc_ref[0], preferred_element_type=F32)
        aqk = jnp.dot(he[:TM], wr_ref[0, :, :512], preferred_element_type=F32)
        r = jnp.dot(he[:TM], wr_ref[0, :, 512:], preferred_element_type=F32)
        return p, aqk, r

    def finish(s, p, aqk, r):
        pm = p[:TM]
        seg_first = i <= 1
        seg_last = jnp.logical_or(i == 0, i == nt - 1)
        prev_row = jnp.where(seg_first, 0.0, p[TM + HALO - 1:TM + HALO])
        next_row = jnp.where(seg_last, 0.0, p[TM + HALO:TM + HALO + 1])
        rows = _iota2((TM, 1), 0)
        pm1 = jnp.where(rows == 0, prev_row, pltpu.roll(pm, 1, axis=0))
        pp1 = jnp.where(rows == TM - 1, next_row, pltpu.roll(pm, TM - 1, axis=0))
        cw = cw_ref[0]
        conv = cw[0:1] * pm1 + cw[1:2] * pm + cw[2:3] * pp1 + cb_ref[0]
        co = _silu(conv)

        ssd_ref[s] = co[:, :640].astype(BF16)
        gqk = co[:, 640:1152]
        gqk = gqk * lax.rsqrt(_group_sum64(gqk * gqk) + EPS)
        gdn_ref[s, :, :256] = (gqk[:, :256] * (HD ** -0.5)).astype(BF16)
        gdn_ref[s, :, 256:512] = gqk[:, 256:].astype(BF16)
        gdn_ref[s, :, 512:] = co[:, 1152:1408].astype(BF16)

        zs_ref[s, :, :640] = _silu(r[:, :640]).astype(BF16)
        zs_ref[s, :, 640:] = _silu(r[:, 768:1152]).astype(BF16)

        aqk = aqk * lax.rsqrt(_group_sum64(aqk * aqk) * (1.0 / HD) + EPS) * qkw_ref[0]
        lane = _iota2((1, 512), 1)
        swapped = jnp.where(jnp.bitwise_and(lane, 31) < 16,
                            pltpu.roll(aqk, 512 - 16, axis=1), pltpu.roll(aqk, 16, axis=1))
        cos = jnp.concatenate([cos_ref[...]] * 4, axis=1)
        sin = jnp.concatenate([sin_ref[...]] * 4, axis=1)
        aqk = aqk * cos + swapped * sin
        q_ref[s] = (aqk[:, :384] * Q_PRESCALE).astype(BF16)
        ones = jnp.ones((TM, HD), F32)
        zeros = jnp.zeros((TM, HD), F32)
        for g in range(ATT_KVH):
            kv_ref[s, g] = jnp.concatenate([aqk[:, 384 + g * HD:384 + (g + 1) * HD], zeros,
                                            r[:, 640 + g * HD:640 + (g + 1) * HD], ones], axis=1).astype(BF16)
        small_ref[s] = r[:, 1152:1280]

    hes = [normed(s) for s in range(BPS)]
    prs = [project(he) for he in hes]
    for s in range(BPS):
        finish(s, *prs[s])


def _in_projection(xa, xb, off, layer, mod3, norm_w, w_conv, w_rest, conv_w, conv_b, qk_w, cos_t, sin_t):
    bsz = xb.shape[0]
    assert bsz % BPS == 0
    nt = xb.shape[1] // TM + off
    t = nt * TM
    nblk8 = xb.shape[1] // HALO
    per_tile = TM // HALO
    row = lambda b, i: (b, i, 0)
    lay = lambda b, i: (layer, 0, 0)
    outs = [jax.ShapeDtypeStruct((bsz, t, w), dt)
            for w, dt in ((640, BF16), (768, BF16), (1024, BF16), (384, BF16), (128, F32))]
    outs.insert(4, jax.ShapeDtypeStruct((bsz, ATT_KVH, t, 256), BF16))

    def mod_spec(s):
        return pl.BlockSpec((1, 1, 3 * D_MODEL), lambda b, i: (jnp.where(i == 0, bsz, b * BPS + s), 0, 0))

    return pl.pallas_call(
        functools.partial(_proj_kernel, nt),
        out_shape=outs,
        grid=(bsz // BPS, nt),
        in_specs=[
            pl.BlockSpec((BPS, TM, D_MODEL), lambda b, i: (b, 0, 0)),
            pl.BlockSpec((BPS, TM, D_MODEL), lambda b, i: (b, jnp.maximum(i - off, 0), 0)),
            pl.BlockSpec((BPS, HALO, D_MODEL), lambda b, i: (b, jnp.maximum((i - off) * per_tile - 1, 0), 0)),
            pl.BlockSpec((BPS, HALO, D_MODEL),
                         lambda b, i: (b, jnp.minimum((i - off + 1) * per_tile, nblk8 - 1), 0)),
            *[mod_spec(s) for s in range(BPS)],
            pl.BlockSpec((1, 1, D_MODEL), lay),
            pl.BlockSpec((1, D_MODEL, CONV_DIM), lay),
            pl.BlockSpec((1, D_MODEL, REST_DIM), lay),
            pl.BlockSpec((1, 3, CONV_DIM), lay),
            pl.BlockSpec((1, 1, CONV_DIM), lay),
            pl.BlockSpec((1, 1, 512), lay),
            pl.BlockSpec((TM, 128), lambda b, i: (i, 0)),
            pl.BlockSpec((TM, 128), lambda b, i: (i, 0)),
        ],
        out_specs=[pl.BlockSpec((BPS, TM, 640), row), pl.BlockSpec((BPS, TM, 768), row),
                   pl.BlockSpec((BPS, TM, 1024), row), pl.BlockSpec((BPS, TM, 384), row),
                   pl.BlockSpec((BPS, ATT_KVH, TM, 256), lambda b, i: (b, 0, i, 0)),
                   pl.BlockSpec((BPS, TM, 128), row)],
        compiler_params=pltpu.CompilerParams(dimension_semantics=("arbitrary", "arbitrary"),
                                             vmem_limit_bytes=VMEM_LIMIT),
        name="in_projection",
    )(xa, xb, xb, xb, *([mod3] * BPS), norm_w, w_conv, w_rest, conv_w, conv_b, qk_w, cos_t, sin_t)


def _ssd_kernel(nc, ctx_chunks, ctx_out, xin_ref, small_ref, prm_ref, dskip_ref, y_ref,
                cum_s, cumt_s, dtt_s, st_s, sin_s):
    q = SSD_CHUNK
    hpg = SSD_HEADS // 2
    ri = _iota2((q, q), 0)
    ci = _iota2((q, q), 1)
    ltri = (ri >= ci).astype(BF16)
    utri = (ri <= ci).astype(BF16)
    lane = _iota2((1, 128), 1)
    fwd_lane = lane < SSD_HEADS

    a_row = -jnp.exp(prm_ref[0, 0:1, :])
    bias_row = prm_ref[0, 1:2, :]

    pb = 6
    tri = jnp.broadcast_to(jnp.concatenate([ltri, utri], axis=0)[None], (pb, 2 * q, q))

    def prep(cb, carry):
        base = pl.multiple_of(cb * (pb * q), q)
        dt = _softplus(small_ref[0, pl.ds(base, pb * q), :] + bias_row)
        pieces = _split3((dt * a_row).reshape(pb, q, 128))
        dot = lambda x: lax.dot_general(tri, x, (((2,), (1,)), ((0,), (0,))), preferred_element_type=F32)
        sums = dot(pieces[0]) + (dot(pieces[1]) + dot(pieces[2]))
        cum = jnp.where(fwd_lane, sums[:, :q], sums[:, q:])
        cum_s[pl.ds(base, pb * q), :] = cum.reshape(pb * q, 128)
        for i in range(pb):
            rows = pl.ds(pl.multiple_of(base + i * q, q), q)
            cumt_s[rows, :] = cum[i].T
            dtt_s[rows, :] = dt[i * q:(i + 1) * q].T
        return carry

    lax.fori_loop(0, nc // pb, prep, 0)

    def end_of(cumt, d, col):
        return cumt[col:col + 1, q - 1:q] if d == 0 else cumt[col:col + 1, 0:1]

    nb = 2

    def states(cb, carry):
        lhs, rhs = [], []
        for cc in range(nb):
            base = pl.multiple_of((cb * nb + cc) * q, q)
            xin = xin_ref[0, pl.ds(base, q), :]
            cumt = cumt_s[pl.ds(base, q), :]
            dtt = dtt_s[pl.ds(base, q), :]
            for g in range(2):
                bgt = xin[:, 384 + g * HD:384 + (g + 1) * HD].astype(F32).T
                for d in range(2):
                    for hg in range(hpg):
                        h = g * hpg + hg
                        col = d * SSD_HEADS + h
                        w_row = jnp.exp(end_of(cumt, d, col) - cumt[col:col + 1, :]) * dtt[col:col + 1, :]
                        lhs.append(bgt * w_row)
                        rhs.append(xin[:, h * HD:(h + 1) * HD])
        st = _bmm(jnp.stack(lhs), jnp.stack(rhs))
        st_s[pl.ds(cb * (12 * nb), 12 * nb)] = st
        return carry

    lax.fori_loop(0, nc // nb, states, 0)

    orders = (list(range(nc)),
              list(range(ctx_chunks - 1, -1, -1)) + list(range(nc - 1, ctx_chunks - 1, -1)))
    for g in range(2):
        for d in range(2):
            for hg in range(hpg):
                col = d * SSD_HEADS + g * hpg + hg
                s = jnp.zeros((HD, HD), F32)
                for c in orders[d]:
                    slot = c * 12 + g * 6 + d * 3 + hg
                    sin_s[slot] = s
                    row = c * q + (q - 1 if d == 0 else 0)
                    s = s * jnp.exp(cum_s[row:row + 1, col:col + 1]) + st_s[slot]

    lower = ri >= ci
    upper = ri <= ci
    dsk = dskip_ref[0]

    def outputs(cb, carry):
        sin = sin_s[pl.ds(cb * (12 * nb), 12 * nb)]
        ws, xs, cstack, es, scs = [], [], [], [], []
        per_chunk = []
        for cc in range(nb):
            base = pl.multiple_of((cb * nb + cc) * q, q)
            xin = xin_ref[0, pl.ds(base, q), :]
            per_chunk.append((base, xin, cum_s[pl.ds(base, q), :], cumt_s[pl.ds(base, q), :],
                              dtt_s[pl.ds(base, q), :]))
        sc = _bmm_nt(jnp.stack([xin[:, 512 + g * HD:512 + (g + 1) * HD] for _, xin, _, _, _ in per_chunk
                                for g in range(2)]),
                     jnp.stack([xin[:, 384 + g * HD:384 + (g + 1) * HD] for _, xin, _, _, _ in per_chunk
                                for g in range(2)]))
        for cc, (base, xin, cum, cumt, dtt) in enumerate(per_chunk):
            e_chunk = {}
            for g in range(2):
                cg = xin[:, 512 + g * HD:512 + (g + 1) * HD]
                for hg in range(hpg):
                    h = g * hpg + hg
                    w = None
                    for d in range(2):
                        col = d * SSD_HEADS + h
                        gmat = jnp.broadcast_to(cum[:, col:col + 1], (q, q))
                        lm = jnp.exp(jnp.where(lower if d == 0 else upper, gmat - cumt[col:col + 1, :], NEG))
                        wd = lm * dtt[col:col + 1, :]
                        w = wd if w is None else w + wd
                        e_chunk[g * 6 + d * 3 + hg] = jnp.exp(gmat[:, :HD])
                    ws.append(w * sc[cc * 2 + g])
                    xs.append(xin[:, h * HD:(h + 1) * HD])
            es.extend(e_chunk[i] for i in range(12))
            cstack.extend(xin[:, 512 + (i // 6) * HD:512 + (i // 6 + 1) * HD] for i in range(12))
        ydiag = _bmm(jnp.stack(ws), jnp.stack(xs))
        yoff = _bmm(jnp.stack(cstack), sin)
        for cc, (base, xin, cum, cumt, dtt) in enumerate(per_chunk):
            outs = []
            for h in range(SSD_HEADS):
                g, hg = divmod(h, hpg)
                y = ydiag[cc * SSD_HEADS + h] + dsk[:, h * HD:(h + 1) * HD] * xs[cc * SSD_HEADS + h]
                for d in range(2):
                    i = cc * 12 + g * 6 + d * 3 + hg
                    y = y + yoff[i] * es[i]
                outs.append(y)
            y_ref[0, pl.ds(base, q), :] = jnp.concatenate(outs, axis=1).astype(BF16)
        return carry

    if ctx_out:
        first_trip = 0
    else:
        assert ctx_chunks % nb == 0
        first_trip = ctx_chunks // nb
        y_ref[0, :ctx_chunks * q, :] = jnp.zeros((ctx_chunks * q, SSD_W), BF16)
    lax.fori_loop(first_trip, nc // nb, outputs, 0)


def _ssd(ssd_in, small, prm, dskip, layer, ctx_out):
    bsz, t, _ = ssd_in.shape
    nc = t // SSD_CHUNK
    return pl.pallas_call(
        functools.partial(_ssd_kernel, nc, CTX // SSD_CHUNK, ctx_out),
        out_shape=jax.ShapeDtypeStruct((bsz, t, SSD_W), BF16),
        grid=(bsz,),
        in_specs=[pl.BlockSpec((1, t, 640), lambda b: (b, 0, 0)),
                  pl.BlockSpec((1, t, 128), lambda b: (b, 0, 0)),
                  pl.BlockSpec((1, 8, 128), lambda b: (layer, 0, 0)),
                  pl.BlockSpec((1, 1, SSD_W), lambda b: (layer, 0, 0))],
        out_specs=pl.BlockSpec((1, t, SSD_W), lambda b: (b, 0, 0)),
        scratch_shapes=[pltpu.VMEM((t, 128), F32), pltpu.VMEM((t, 128), F32), pltpu.VMEM((t, 128), F32),
                        pltpu.VMEM((nc * 12, HD, HD), F32),
                        pltpu.VMEM((nc * 12, HD, HD), F32)],
        compiler_params=pltpu.CompilerParams(dimension_semantics=("arbitrary",),
                                             vmem_limit_bytes=VMEM_LIMIT),
        name="ssd",
    )(ssd_in, small, prm, dskip)


def _bd4(x):
    xb = x.astype(BF16)
    blk = jnp.right_shift(_iota2((1, 4 * HD), 1), 6)
    zero = jnp.zeros_like(xb)
    return jnp.concatenate([jnp.where(blk == h, xb, zero) for h in range(GDN_HEADS)], axis=0)


def _bmm(a, b):
    return lax.dot_general(a.astype(BF16), b.astype(BF16), (((2,), (1,)), ((0,), (0,))),
                           preferred_element_type=F32)


def _bmm_nt(a, b):
    return lax.dot_general(a.astype(BF16), b.astype(BF16), (((2,), (2,)), ((0,), (0,))),
                           preferred_element_type=F32)


def _unit_tri_inverse_stages(a_strict):
    n = a_strict.shape[-1]
    ri = _iota2((1, n, n), 1)
    ci = _iota2((1, n, n), 2)

    def blk(shift):
        return jnp.right_shift(ri, shift) == jnp.right_shift(ci, shift)

    nd = jnp.where(blk(3), -a_strict, 0.0)
    p0 = (ri == ci).astype(F32) + nd
    q1 = _bmm(nd, nd)
    yield None
    t = _bmm(jnp.concatenate([p0, q1], axis=1), q1)
    yield None
    p1 = p0 + t[:, :n]
    inv = p1 + _bmm(p1, t[:, n:])
    yield None
    for shift in (4, 5, 6):
        e = jnp.where(jnp.logical_and(blk(shift), jnp.logical_not(blk(shift - 1))), a_strict, 0.0)
        ie = _bmm(inv, e)
        yield None
        inv = inv - _bmm(ie, inv)
        yield None
    yield inv


def _gdn_attn_kernel(nch, ctx_chunks, ctx_attn, gin_ref, small_ref, prm_ref, q_ref, kv_ref, o_ref, att_ref,
                     gc_s, rows_s, mq_s, n_s, cd_s, of_s, ob_s):
    c64 = GDN_CHUNK
    w4 = GDN_W
    nsc = nch // 2
    ri128 = _iota2((128, 128), 0)
    ci128 = _iota2((128, 128), 1)
    same = jnp.right_shift(ri128, 6) == jnp.right_shift(ci128, 6)
    lbd = jnp.logical_and(same, ri128 >= ci128).astype(BF16)
    ubd = jnp.logical_and(same, ri128 <= ci128).astype(BF16)
    lane = _iota2((1, 128), 1)
    fwd_lane = lane < SM_A + GDN_HEADS
    beta_lane = jnp.logical_and(lane >= SM_B, lane < SM_B + 2 * GDN_HEADS)

    a_row = -jnp.exp(prm_ref[0, 0:1, :])
    bias_row = prm_ref[0, 1:2, :]

    pb = 6
    tri = jnp.broadcast_to(jnp.concatenate([lbd, ubd], axis=0)[None], (pb, 256, 128))

    def prep(sb, carry):
        base = pl.multiple_of(sb * (pb * 128), 128)
        sm = small_ref[0, pl.ds(base, pb * 128), :]
        pieces = _split3((a_row * _softplus(sm + bias_row)).reshape(pb, 128, 128))
        dot = lambda x: lax.dot_general(tri, x, (((2,), (1,)), ((0,), (0,))), preferred_element_type=F32)
        sums = dot(pieces[0]) + (dot(pieces[1]) + dot(pieces[2]))
        cum = jnp.where(fwd_lane, sums[:, :128], sums[:, 128:])
        gc_s[pl.ds(base, pb * 128), :] = cum.reshape(pb * 128, 128)
        beta = jax.nn.sigmoid(sm)
        for i in range(pb):
            slab_t = jnp.where(beta_lane, beta[i * 128:(i + 1) * 128], cum[i]).T
            for half in range(2):
                lo = half * c64
                table = []
                for first in (SM_A, SM_A + GDN_HEADS, SM_B, SM_B + GDN_HEADS):
                    table.append(jnp.concatenate(
                        [slab_t[first + h:first + h + 1, lo:lo + c64] for h in range(GDN_HEADS)], axis=1))
                for d, edge in ((0, lo + c64 - 1), (1, lo)):
                    first = SM_A + d * GDN_HEADS
                    table.append(jnp.concatenate(
                        [jnp.broadcast_to(slab_t[first + h:first + h + 1, edge:edge + 1], (1, c64))
                         for h in range(GDN_HEADS)], axis=1))
                table.append(jnp.zeros((2, w4), F32))
                rows_s[pl.ds(pl.multiple_of((sb * pb + i) * 16 + half * 8, 8), 8), :] = (
                    jnp.concatenate(table, axis=0))
        return carry

    lax.fori_loop(0, nsc // pb, prep, 0)

    ri = _iota2((c64, c64), 0)
    ci = _iota2((c64, c64), 1)
    incl = (ri >= ci, ri <= ci)
    strict = (ri > ci, ri < ci)

    nb = 4

    def local(cb):
        chunks = []
        for cc in range(nb):
            c = cb * nb + cc
            base = _aligned(c * c64, c64)
            gin = gin_ref[0, pl.ds(base, c64), :]
            gc = gc_s[pl.ds(base, c64), :]
            rows = rows_s[pl.ds(_aligned(c * 8, 8), 8), :]
            qs = [gin[:, h * HD:(h + 1) * HD] for h in range(GDN_HEADS)]
            ks = [gin[:, w4 + h * HD:w4 + (h + 1) * HD] for h in range(GDN_HEADS)]
            vs = [gin[:, 2 * w4 + h * HD:2 * w4 + (h + 1) * HD] for h in range(GDN_HEADS)]
            chunks.append((c, base, gc, rows, qs, ks, vs))
        kq = _bmm_nt(jnp.stack([jnp.concatenate([ch[5][h], ch[4][h]], axis=0)
                                for ch in chunks for h in range(GDN_HEADS)]),
                     jnp.stack([ch[5][h] for ch in chunks for h in range(GDN_HEADS)]))
        yield
        mb_l, rhs_l, lhs_l, qd_l = [], [], [], []
        for ci, (c, base, gc, rows, qs, ks, vs) in enumerate(chunks):
            kts = [k.astype(F32).T for k in ks]
            for d in range(2):
                for h in range(GDN_HEADS):
                    sl = slice(h * HD, (h + 1) * HD)
                    colg = SM_A + d * GDN_HEADS + h
                    kqh = kq[ci * GDN_HEADS + h]
                    gmat = jnp.broadcast_to(gc[:, colg:colg + 1], (c64, c64))
                    gcr, br, endr = rows[d:d + 1, sl], rows[2 + d:3 + d, sl], rows[4 + d:5 + d, sl]
                    dec = jnp.exp(jnp.where(incl[d], gmat - gcr, NEG))
                    mb_l.append(jnp.where(strict[d], kqh[:c64] * dec, 0.0) * br)
                    eg = jnp.exp(gmat)
                    kdtb = kts[h] * (jnp.exp(endr - gcr) * br)
                    lhs_l.append(jnp.concatenate([kdtb, kqh[c64:] * dec * br], axis=0))
                    rhs_l.append(jnp.concatenate([vs[h], ks[h] * eg], axis=1))
                    qd_l.append(qs[h] * eg)
        z = None
        for z in _unit_tri_inverse_stages(jnp.stack(mb_l)):
            if z is None:
                yield
        zr = _bmm(z, jnp.stack(rhs_l))
        yield
        a12 = _bmm(jnp.stack(lhs_l), zr)
        yield
        for ci, (c, base, gc, rows, qs, ks, vs) in enumerate(chunks):
            for d in range(2):
                slot = c * 2 + d
                ids = [ci * 2 * GDN_HEADS + d * GDN_HEADS + h for h in range(GDN_HEADS)]
                mq = [jnp.concatenate([-a12[n][:c64, HD:], qd_l[n] - a12[n][c64:, HD:]], axis=0) for n in ids]
                mq_s[slot] = jnp.concatenate(mq, axis=1).astype(BF16)
                n_s[slot] = jnp.concatenate([a12[n][:c64, :HD] for n in ids], axis=1)
                cd_s[slot] = jnp.broadcast_to(jnp.exp(rows[4 + d:5 + d]), (8, w4))
                oc = jnp.concatenate([a12[n][c64:, :HD] for n in ids], axis=1)
                if d == 0:
                    of_s[pl.ds(base, c64), :] = oc
                else:
                    ob_s[pl.ds(base, c64), :] = oc

    t_all = q_ref.shape[1]
    heads_per_kv = ATT_QH // ATT_KVH

    def attend(tile, nkeys):
        rows = pl.ds(_aligned(tile * TM, TM), TM)
        qt = q_ref[0, rows, :]
        outs = []

        def finish(h, s):
            g = h // heads_per_kv
            p = jnp.exp2(s - jnp.max(s, axis=-1, keepdims=True))
            acc = _mm(p, kv_ref[0, g, :nkeys, 2 * HD:])
            outs.append(acc[:, :HD] / acc[:, HD:])
            if h % heads_per_kv == heads_per_kv - 1:
                tile_out = outs[-heads_per_kv:] + [jnp.zeros((TM, HD), F32)]
                att_ref[0, g, rows, :] = jnp.concatenate(tile_out, axis=1).astype(BF16)

        s_prev = None
        for h in range(ATT_QH):
            k = kv_ref[0, h // heads_per_kv, :nkeys, :HD]
            s = _mm_nt(qt[:, h * HD:(h + 1) * HD], k)
            yield
            if s_prev is not None:
                finish(h - 1, s_prev)
                yield
            s_prev = s
        finish(ATT_QH - 1, s_prev)
        yield

    def trip(j, nkeys):
        streams = ([attend(j, nkeys)] if nkeys else []) + [local(j)]
        while streams:
            for st in list(streams):
                try:
                    next(st)
                except StopIteration:
                    streams.remove(st)

    trip(0, CTX if ctx_attn else 0)
    if not ctx_attn:
        att_ref[0, :, :TM, :] = jnp.zeros((ATT_KVH, TM, 256), BF16)

    def later_trip(j, carry):
        trip(j, t_all)
        return carry

    lax.fori_loop(1, nch // nb, later_trip, 0)

    def step(t, states):
        cf = t
        cb = jnp.where(t < ctx_chunks, ctx_chunks - 1 - t, nch - 1 + ctx_chunks - t)
        new_states = []
        for d, c in ((0, cf), (1, cb)):
            slot = c * 2 + d
            s = states[d]
            r = jnp.dot(mq_s[slot], _bd4(s), preferred_element_type=F32)
            new_states.append(s * cd_s[slot][0:1] + r[:c64] + n_s[slot])
            rows = pl.ds(pl.multiple_of(c * c64, c64), c64)
            if d == 0:
                of_s[rows, :] = of_s[rows, :] + r[c64:]
            else:
                ob_s[rows, :] = ob_s[rows, :] + r[c64:]
        return tuple(new_states)

    zero = jnp.zeros((c64, w4), F32)
    lax.fori_loop(0, nch, step, (zero, zero))
    o_ref[0] = (of_s[...] + ob_s[...]).astype(BF16)


def _gdn_attention(gdn_in, small, prm, q, kv, layer, ctx_attn):
    bsz, t, _ = gdn_in.shape
    nch = t // GDN_CHUNK
    assert nch // 4 == t // TM
    return pl.pallas_call(
        functools.partial(_gdn_attn_kernel, nch, CTX // GDN_CHUNK, ctx_attn),
        out_shape=[jax.ShapeDtypeStruct((bsz, t, GDN_W), BF16),
                   jax.ShapeDtypeStruct((bsz, ATT_KVH, t, 256), BF16)],
        grid=(bsz,),
        in_specs=[pl.BlockSpec((1, t, 768), lambda b: (b, 0, 0)),
                  pl.BlockSpec((1, t, 128), lambda b: (b, 0, 0)),
                  pl.BlockSpec((1, 8, 128), lambda b: (layer, 0, 0)),
                  pl.BlockSpec((1, t, ATT_W), lambda b: (b, 0, 0)),
                  pl.BlockSpec((1, ATT_KVH, t, 256), lambda b: (b, 0, 0, 0))],
        out_specs=[pl.BlockSpec((1, t, GDN_W), lambda b: (b, 0, 0), pipeline_mode=pl.Buffered(1)),
                   pl.BlockSpec((1, ATT_KVH, t, 256), lambda b: (b, 0, 0, 0), pipeline_mode=pl.Buffered(1))],
        scratch_shapes=[pltpu.VMEM((t, 128), F32),
                        pltpu.VMEM((nch * 8, GDN_W), F32),
                        pltpu.VMEM((nch * 2, 128, GDN_W), BF16),
                        pltpu.VMEM((nch * 2, GDN_CHUNK, GDN_W), F32),
                        pltpu.VMEM((nch * 2, 8, GDN_W), F32),
                        pltpu.VMEM((t, GDN_W), F32),
                        pltpu.VMEM((t, GDN_W), F32)],
        compiler_params=pltpu.CompilerParams(dimension_semantics=("arbitrary",),
                                             vmem_limit_bytes=VMEM_LIMIT),
        name="gdn_attention",
    )(gdn_in, small, prm, q, kv)


def _out_kernel(first_tile, xa_ref, x_ref, y_ref, o_ref, a_ref, zs_ref, *refs):
    mods = refs[:BPS]
    sw_ref, gw_ref, w_ref, out_ref = refs[BPS:]

    def mixed(s):
        zs = zs_ref[s].astype(F32)
        t = y_ref[s].astype(F32) * zs[:, :SSD_W]
        ssd = t * lax.rsqrt(jnp.mean(t * t, axis=-1, keepdims=True) + EPS) * sw_ref[0]
        o = o_ref[s].astype(F32)
        gdn = o * lax.rsqrt(_group_sum64(o * o) * (1.0 / HD) + EPS) * gw_ref[0] * zs[:, SSD_W:SSD_W + GDN_W]
        a = a_ref[s]
        att = jnp.concatenate([a[g][:, :HD * ATT_QH // ATT_KVH] for g in range(ATT_KVH)], axis=1)
        att = att * zs[:, SSD_W + GDN_W:]
        return jnp.concatenate([ssd, gdn, att], axis=1).astype(BF16)

    mixes = [mixed(s) for s in range(BPS)]
    projs = [jnp.dot(mix, w_ref[0], preferred_element_type=F32) for mix in mixes]
    for s in range(BPS):
        gate = mods[s][0][:, 2 * D_MODEL:]
        x = jnp.where(pl.program_id(1) + first_tile == 0, xa_ref[s], x_ref[s])
        out_ref[s] = x + gate * projs[s]


def _out_projection(xa, xb, off, layer, y, o, a, zs, mod3, ssd_nw, gdn_nw, w_out, first_tile):
    bsz, t, _ = y.shape
    nt = t // TM - first_tile
    row = lambda b, i: (b, i + first_tile, 0)
    lay = lambda b, i: (layer, 0, 0)

    def mod_spec(s):
        return pl.BlockSpec((1, 1, 3 * D_MODEL),
                            lambda b, i: (jnp.where(i + first_tile == 0, bsz, b * BPS + s), 0, 0))

    return pl.pallas_call(
        functools.partial(_out_kernel, first_tile),
        out_shape=jax.ShapeDtypeStruct((bsz, nt * TM, D_MODEL), F32),
        grid=(bsz // BPS, nt),
        in_specs=[pl.BlockSpec((BPS, TM, D_MODEL), lambda b, i: (b, 0, 0)),
                  pl.BlockSpec((BPS, TM, D_MODEL), lambda b, i: (b, jnp.maximum(i + first_tile - off, 0), 0)),
                  pl.BlockSpec((BPS, TM, SSD_W), row),
                  pl.BlockSpec((BPS, TM, GDN_W), row),
                  pl.BlockSpec((BPS, ATT_KVH, TM, 256), lambda b, i: (b, 0, i + first_tile, 0)),
                  pl.BlockSpec((BPS, TM, D_MODEL), row),
                  *[mod_spec(s) for s in range(BPS)],
                  pl.BlockSpec((1, 1, SSD_W), lay),
                  pl.BlockSpec((1, 1, GDN_W), lay),
                  pl.BlockSpec((1, D_MODEL, D_MODEL), lay)],
        out_specs=pl.BlockSpec((BPS, TM, D_MODEL), lambda b, i: (b, i, 0)),
        compiler_params=pltpu.CompilerParams(dimension_semantics=("arbitrary", "arbitrary"),
                                             vmem_limit_bytes=VMEM_LIMIT),
        name="out_projection",
    )(xa, xb, y, o, a, zs, *([mod3] * BPS), ssd_nw, gdn_nw, w_out)


_REST_SEGS = [(2076, 384), (2460, 128), (1408, 384), (1804, 256), (2588, 128), (2716, 384),
              (1792, 12), (2060, 8), (2068, 8)]


def _prepare_params(w_in, conv_w, conv_b, ssd_A_log, ssd_dt_bias, gdn_A_log, gdn_dt_bias):
    depth = w_in.shape[0]
    w_conv = w_in[:, :, :CONV_DIM].astype(BF16)
    pad = jnp.zeros(w_in.shape[:2] + (REST_DIM - sum(n for _, n in _REST_SEGS),), w_in.dtype)
    w_rest = jnp.concatenate([lax.slice_in_dim(w_in, a, a + n, axis=2) for a, n in _REST_SEGS] + [pad],
                             axis=2).astype(BF16)
    cw = conv_w
    cb = conv_b[:, None, :]

    def lanes(a, b):
        v = jnp.concatenate([a.reshape(depth, -1), b.reshape(depth, -1)], axis=1)
        return jnp.pad(v, ((0, 0), (0, 128 - v.shape[1])))

    prm = jnp.stack([lanes(ssd_A_log, gdn_A_log), lanes(ssd_dt_bias, gdn_dt_bias)], axis=1)
    prm = jnp.pad(prm, ((0, 0), (0, 6), (0, 0)))
    return w_conv, w_rest, cw, cb, prm


def _rope_tables(t):
    f32 = np.float32
    pos = np.arange(t - CTX)
    n_freq = HD // 4
    freqs = np.power(f32(ROPE_THETA), -np.arange(n_freq, dtype=f32) / f32(n_freq)).astype(f32)
    ang_r = ((pos // GRID_W).astype(f32)[:, None] * freqs).astype(f32)
    ang_c = ((pos % GRID_W).astype(f32)[:, None] * freqs).astype(f32)
    cos = np.concatenate([np.cos(ang_r)] * 2 + [np.cos(ang_c)] * 2, axis=1)
    sin = np.concatenate([-np.sin(ang_r), np.sin(ang_r), -np.sin(ang_c), np.sin(ang_c)], axis=1)
    cos = np.concatenate([np.ones((CTX, HD), f32), cos], axis=0).astype(f32)
    sin = np.concatenate([np.zeros((CTX, HD), f32), sin], axis=0).astype(f32)
    return jnp.asarray(np.tile(cos, (1, 2))), jnp.asarray(np.tile(sin, (1, 2)))


def kernel(x, c, ctx, c_ctx, norm_w, w_mod, b_mod, w_in, conv_w, conv_b, ssd_A_log, ssd_dt_bias, ssd_D,
           ssd_norm_w, gdn_A_log, gdn_dt_bias, gdn_norm_w, q_norm_w, k_norm_w, w_out):
    bsz = x.shape[0]
    t = CTX + x.shape[1]
    c_all = jnp.concatenate([c, c_ctx[None, :], jnp.zeros((7, D_MODEL), F32)], axis=0)
    cos_t, sin_t = _rope_tables(t)
    w_conv, w_rest, cw, cb, prm = _prepare_params(w_in, conv_w, conv_b, ssd_A_log, ssd_dt_bias,
                                                  gdn_A_log, gdn_dt_bias)
    qk_w = jnp.concatenate([jnp.tile(q_norm_w, (1, ATT_QH)), jnp.tile(k_norm_w, (1, ATT_KVH))], axis=1)[:, None]
    dskip = jnp.repeat(ssd_D, HD, axis=1)[:, None]
    gdn_nw = jnp.tile(gdn_norm_w, (1, GDN_HEADS))[:, None]
    norm_w3, ssd_nw = norm_w[:, None], ssd_norm_w[:, None]
    w_out16 = w_out.astype(BF16)

    xa, xb, off = ctx, x, 1
    for layer in range(DEPTH):
        first_tile = 1 if layer == DEPTH - 1 else 0
        mod3 = _modulation(c_all, w_mod, b_mod, layer).reshape(bsz + 8, 1, 3 * D_MODEL)
        ssd_in, gdn_in, zs, q, kv, small = _in_projection(
            xa, xb, off, layer, mod3, norm_w3, w_conv, w_rest, cw, cb, qk_w, cos_t, sin_t)
        y = _ssd(ssd_in, small, prm, dskip, layer, first_tile == 0)
        o, a = _gdn_attention(gdn_in, small, prm, q, kv, layer, first_tile == 0)
        out = _out_projection(xa, xb, off, layer, y, o, a, zs, mod3, ssd_nw, gdn_nw, w_out16, first_tile)
        xa, xb, off = out, out, 0
    return out
```

```python
import functools
import math

import jax
import jax.numpy as jnp
import numpy as np
from jax import lax
from jax.experimental import pallas as pl
from jax.experimental.pallas import tpu as pltpu

F32 = jnp.float32
BF16 = jnp.bfloat16

D_MODEL = 1024
CTX = 256
GRID_W = 64
EPS = 1e-6
DEPTH = 2

HD = 64
SSD_HEADS = 6
SSD_W = SSD_HEADS * HD
SSD_CHUNK = 128
GDN_HEADS = 4
GDN_W = GDN_HEADS * HD
GDN_CHUNK = 64
ATT_QH = 6
ATT_KVH = 2
ATT_W = ATT_QH * HD
ROPE_THETA = 10000.0
Q_PRESCALE = (HD ** -0.5) * math.log2(math.e)

CONV_DIM = 1408
REST_DIM = 1792
TM = 256
HALO = 8
NEG = -1e30
VMEM_LIMIT = 56 * 1024 * 1024

SM_DT = 0
SM_A = 12
SM_B = 20


def _mm(a, b):
    return jnp.dot(a.astype(BF16), b.astype(BF16), preferred_element_type=F32)


def _mm_nt(a, b):
    return lax.dot_general(a.astype(BF16), b.astype(BF16), (((1,), (1,)), ((), ())),
                           preferred_element_type=F32)


def _split3(a):
    h = a.astype(BF16)
    r = a - h.astype(F32)
    m = r.astype(BF16)
    l = (r - m.astype(F32)).astype(BF16)
    return h, m, l


def _silu(x):
    return x * jax.nn.sigmoid(x)


def _softplus(x):
    return jnp.maximum(x, 0.0) + jnp.log1p(jnp.exp(-jnp.abs(x)))


def _iota2(shape, dim):
    return lax.broadcasted_iota(jnp.int32, shape, dim)


def _aligned(x, m):
    return x if isinstance(x, int) else pl.multiple_of(x, m)


def _group_sum64(xx):
    r = jnp.right_shift(_iota2((128, 128), 0), 6)
    c = jnp.right_shift(_iota2((128, 128), 1), 6)
    g = (r == c).astype(BF16)
    xb = xx.astype(BF16)
    outs = [jnp.dot(xb[:, t * 128:(t + 1) * 128], g, preferred_element_type=F32)
            for t in range(xx.shape[1] // 128)]
    return outs[0] if len(outs) == 1 else jnp.concatenate(outs, axis=1)


def _mod_kernel(c_ref, w_ref, b_ref, o_ref):
    o_ref[...] = _mm(_silu(c_ref[...]), w_ref[0]) + b_ref[0]


def _modulation(c_all, w_mod, b_mod, layer):
    n = w_mod.shape[2]
    bn = 768
    rows = c_all.shape[0]
    return pl.pallas_call(
        _mod_kernel,
        out_shape=jax.ShapeDtypeStruct((rows, n), F32),
        grid=(n // bn,),
        in_specs=[pl.BlockSpec((rows, D_MODEL), lambda j: (0, 0)),
                  pl.BlockSpec((1, D_MODEL, bn), lambda j: (layer, 0, j)),
                  pl.BlockSpec((1, 1, bn), lambda j: (layer, 0, j))],
        out_specs=pl.BlockSpec((rows, bn), lambda j: (0, j)),
        compiler_params=pltpu.CompilerParams(dimension_semantics=("arbitrary",),
                                             vmem_limit_bytes=VMEM_LIMIT),
        name="modulation",
    )(c_all, w_mod, b_mod.reshape(b_mod.shape[0], 1, n))


BPS = 4


def _proj_kernel(nt, xa_ref, xm_ref, xp_ref, xn_ref, *refs):
    i = pl.program_id(1)
    mods = refs[:BPS]
    (nw_ref, wc_ref, wr_ref, cw_ref, cb_ref, qkw_ref, cos_ref, sin_ref,
     ssd_ref, gdn_ref, zs_ref, q_ref, kv_ref, small_ref) = refs[BPS:]

    def normed(s):
        m = mods[s][0]
        gain = nw_ref[0] * (1.0 + m[:, D_MODEL:2 * D_MODEL])
        shift = m[:, :D_MODEL]
        xm = jnp.where(i == 0, xa_ref[s], xm_ref[s])
        xe = jnp.concatenate([xm, xp_ref[s], xn_ref[s]], axis=0)
        ms = jnp.mean(xe * xe, axis=-1, keepdims=True)
        return (xe * lax.rsqrt(ms + EPS) * gain + shift).astype(BF16)

    def project(he):
        p = jnp.dot(he, wc_ref[0], preferred_element_type=F32)
        aqk = jnp.dot(he[:TM], wr_ref[0, :, :512], preferred_element_type=F32)
        r = jnp.dot(he[:TM], wr_ref[0, :, 512:], preferred_element_type=F32)
        return p, aqk, r

    def finish(s, p, aqk, r):
        pm = p[:TM]
        seg_first = i <= 1
        seg_last = jnp.logical_or(i == 0, i == nt - 1)
        prev_row = jnp.where(seg_first, 0.0, p[TM + HALO - 1:TM + HALO])
        next_row = jnp.where(seg_last, 0.0, p[TM + HALO:TM + HALO + 1])
        rows = _iota2((TM, 1), 0)
        pm1 = jnp.where(rows == 0, prev_row, pltpu.roll(pm, 1, axis=0))
        pp1 = jnp.where(rows == TM - 1, next_row, pltpu.roll(pm, TM - 1, axis=0))
        cw = cw_ref[0]
        conv = cw[0:1] * pm1 + cw[1:2] * pm + cw[2:3] * pp1 + cb_ref[0]
        co = _silu(conv)

        ssd_ref[s] = co[:, :640].astype(BF16)
        gqk = co[:, 640:1152]
        gqk = gqk * lax.rsqrt(_group_sum64(gqk * gqk) + EPS)
        gdn_ref[s, :, :256] = (gqk[:, :256] * (HD ** -0.5)).astype(BF16)
        gdn_ref[s, :, 256:512] = gqk[:, 256:].astype(BF16)
        gdn_ref[s, :, 512:] = co[:, 1152:1408].astype(BF16)

        zs_ref[s, :, :640] = _silu(r[:, :640]).astype(BF16)
        zs_ref[s, :, 640:] = _silu(r[:, 768:1152]).astype(BF16)

        aqk = aqk * lax.rsqrt(_group_sum64(aqk * aqk) * (1.0 / HD) + EPS) * qkw_ref[0]
        lane = _iota2((1, 512), 1)
        swapped = jnp.where(jnp.bitwise_and(lane, 31) < 16,
                            pltpu.roll(aqk, 512 - 16, axis=1), pltpu.roll(aqk, 16, axis=1))
        cos = jnp.concatenate([cos_ref[...]] * 4, axis=1)
        sin = jnp.concatenate([sin_ref[...]] * 4, axis=1)
        aqk = aqk * cos + swapped * sin
        q_ref[s] = (aqk[:, :384] * Q_PRESCALE).astype(BF16)
        ones = jnp.ones((TM, HD), F32)
        zeros = jnp.zeros((TM, HD), F32)
        for g in range(ATT_KVH):
            kv_ref[s, g] = jnp.concatenate([aqk[:, 384 + g * HD:384 + (g + 1) * HD], zeros,
                                            r[:, 640 + g * HD:640 + (g + 1) * HD], ones], axis=1).astype(BF16)
        small_ref[s] = r[:, 1152:1280]

    hes = [normed(s) for s in range(BPS)]
    prs = [project(he) for he in hes]
    for s in range(BPS):
        finish(s, *prs[s])


def _in_projection(xa, xb, off, layer, mod3, norm_w, w_conv, w_rest, conv_w, conv_b, qk_w, cos_t, sin_t):
    bsz = xb.shape[0]
    assert bsz % BPS == 0
    nt = xb.shape[1] // TM + off
    t = nt * TM
    nblk8 = xb.shape[1] // HALO
    per_tile = TM // HALO
    row = lambda b, i: (b, i, 0)
    lay = lambda b, i: (layer, 0, 0)
    outs = [jax.ShapeDtypeStruct((bsz, t, w), dt)
            for w, dt in ((640, BF16), (768, BF16), (1024, BF16), (384, BF16), (128, F32))]
    outs.insert(4, jax.ShapeDtypeStruct((bsz, ATT_KVH, t, 256), BF16))

    def mod_spec(s):
        return pl.BlockSpec((1, 1, 3 * D_MODEL), lambda b, i: (jnp.where(i == 0, bsz, b * BPS + s), 0, 0))

    return pl.pallas_call(
        functools.partial(_proj_kernel, nt),
        out_shape=outs,
        grid=(bsz // BPS, nt),
        in_specs=[
            pl.BlockSpec((BPS, TM, D_MODEL), lambda b, i: (b, 0, 0)),
            pl.BlockSpec((BPS, TM, D_MODEL), lambda b, i: (b, jnp.maximum(i - off, 0), 0)),
            pl.BlockSpec((BPS, HALO, D_MODEL), lambda b, i: (b, jnp.maximum((i - off) * per_tile - 1, 0), 0)),
            pl.BlockSpec((BPS, HALO, D_MODEL),
                         lambda b, i: (b, jnp.minimum((i - off + 1) * per_tile, nblk8 - 1), 0)),
            *[mod_spec(s) for s in range(BPS)],
            pl.BlockSpec((1, 1, D_MODEL), lay),
            pl.BlockSpec((1, D_MODEL, CONV_DIM), lay),
            pl.BlockSpec((1, D_MODEL, REST_DIM), lay),
            pl.BlockSpec((1, 3, CONV_DIM), lay),
            pl.BlockSpec((1, 1, CONV_DIM), lay),
            pl.BlockSpec((1, 1, 512), lay),
            pl.BlockSpec((TM, 128), lambda b, i: (i, 0)),
            pl.BlockSpec((TM, 128), lambda b, i: (i, 0)),
        ],
        out_specs=[pl.BlockSpec((BPS, TM, 640), row), pl.BlockSpec((BPS, TM, 768), row),
                   pl.BlockSpec((BPS, TM, 1024), row), pl.BlockSpec((BPS, TM, 384), row),
                   pl.BlockSpec((BPS, ATT_KVH, TM, 256), lambda b, i: (b, 0, i, 0)),
                   pl.BlockSpec((BPS, TM, 128), row)],
        compiler_params=pltpu.CompilerParams(dimension_semantics=("arbitrary", "arbitrary"),
                                             vmem_limit_bytes=VMEM_LIMIT),
        name="in_projection",
    )(xa, xb, xb, xb, *([mod3] * BPS), norm_w, w_conv, w_rest, conv_w, conv_b, qk_w, cos_t, sin_t)


def _ssd_kernel(nc, ctx_chunks, ctx_out, xin_ref, small_ref, prm_ref, dskip_ref, y_ref,
                cum_s, cumt_s, dtt_s, st_s, sin_s):
    q = SSD_CHUNK
    hpg = SSD_HEADS // 2
    ri = _iota2((q, q), 0)
    ci = _iota2((q, q), 1)
    ltri = (ri >= ci).astype(BF16)
    utri = (ri <= ci).astype(BF16)
    lane = _iota2((1, 128), 1)
    fwd_lane = lane < SSD_HEADS

    a_row = -jnp.exp(prm_ref[0, 0:1, :])
    bias_row = prm_ref[0, 1:2, :]

    pb = 6
    tri = jnp.broadcast_to(jnp.concatenate([ltri, utri], axis=0)[None], (pb, 2 * q, q))

    def prep(cb, carry):
        base = pl.multiple_of(cb * (pb * q), q)
        dt = _softplus(small_ref[0, pl.ds(base, pb * q), :] + bias_row)
        pieces = _split3((dt * a_row).reshape(pb, q, 128))
        dot = lambda x: lax.dot_general(tri, x, (((2,), (1,)), ((0,), (0,))), preferred_element_type=F32)
        sums = dot(pieces[0]) + (dot(pieces[1]) + dot(pieces[2]))
        cum = jnp.where(fwd_lane, sums[:, :q], sums[:, q:])
        cum_s[pl.ds(base, pb * q), :] = cum.reshape(pb * q, 128)
        for i in range(pb):
            rows = pl.ds(pl.multiple_of(base + i * q, q), q)
            cumt_s[rows, :] = cum[i].T
            dtt_s[rows, :] = dt[i * q:(i + 1) * q].T
        return carry

    lax.fori_loop(0, nc // pb, prep, 0)

    def end_of(cumt, d, col):
        return cumt[col:col + 1, q - 1:q] if d == 0 else cumt[col:col + 1, 0:1]

    nb = 2

    def states(cb, carry):
        lhs, rhs = [], []
        for cc in range(nb):
            base = pl.multiple_of((cb * nb + cc) * q, q)
            xin = xin_ref[0, pl.ds(base, q), :]
            cumt = cumt_s[pl.ds(base, q), :]
            dtt = dtt_s[pl.ds(base, q), :]
            for g in range(2):
                bgt = xin[:, 384 + g * HD:384 + (g + 1) * HD].astype(F32).T
                for d in range(2):
                    for hg in range(hpg):
                        h = g * hpg + hg
                        col = d * SSD_HEADS + h
                        w_row = jnp.exp(end_of(cumt, d, col) - cumt[col:col + 1, :]) * dtt[col:col + 1, :]
                        lhs.append(bgt * w_row)
                        rhs.append(xin[:, h * HD:(h + 1) * HD])
        st = _bmm(jnp.stack(lhs), jnp.stack(rhs))
        st_s[pl.ds(cb * (12 * nb), 12 * nb)] = st
        return carry

    lax.fori_loop(0, nc // nb, states, 0)

    orders = (list(range(nc)),
              list(range(ctx_chunks - 1, -1, -1)) + list(range(nc - 1, ctx_chunks - 1, -1)))
    for g in range(2):
        for d in range(2):
            for hg in range(hpg):
                col = d * SSD_HEADS + g * hpg + hg
                s = jnp.zeros((HD, HD), F32)
                for c in orders[d]:
                    slot = c * 12 + g * 6 + d * 3 + hg
                    sin_s[slot] = s
                    row = c * q + (q - 1 if d == 0 else 0)
                    s = s * jnp.exp(cum_s[row:row + 1, col:col + 1]) + st_s[slot]

    lower = ri >= ci
    upper = ri <= ci
    dsk = dskip_ref[0]

    def outputs(cb, carry):
        sin = sin_s[pl.ds(cb * (12 * nb), 12 * nb)]
        ws, xs, cstack, es, scs = [], [], [], [], []
        per_chunk = []
        for cc in range(nb):
            base = pl.multiple_of((cb * nb + cc) * q, q)
            xin = xin_ref[0, pl.ds(base, q), :]
            per_chunk.append((base, xin, cum_s[pl.ds(base, q), :], cumt_s[pl.ds(base, q), :],
                              dtt_s[pl.ds(base, q), :]))
        sc = _bmm_nt(jnp.stack([xin[:, 512 + g * HD:512 + (g + 1) * HD] for _, xin, _, _, _ in per_chunk
                                for g in range(2)]),
                     jnp.stack([xin[:, 384 + g * HD:384 + (g + 1) * HD] for _, xin, _, _, _ in per_chunk
                                for g in range(2)]))
        for cc, (base, xin, cum, cumt, dtt) in enumerate(per_chunk):
            e_chunk = {}
            for g in range(2):
                cg = xin[:, 512 + g * HD:512 + (g + 1) * HD]
                for hg in range(hpg):
                    h = g * hpg + hg
                    w = None
                    for d in range(2):
                        col = d * SSD_HEADS + h
                        gmat = jnp.broadcast_to(cum[:, col:col + 1], (q, q))
                        lm = jnp.exp(jnp.where(lower if d == 0 else upper, gmat - cumt[col:col + 1, :], NEG))
                        wd = lm * dtt[col:col + 1, :]
                        w = wd if w is None else w + wd
                        e_chunk[g * 6 + d * 3 + hg] = jnp.exp(gmat[:, :HD])
                    ws.append(w * sc[cc * 2 + g])
                    xs.append(xin[:, h * HD:(h + 1) * HD])
            es.extend(e_chunk[i] for i in range(12))
            cstack.extend(xin[:, 512 + (i // 6) * HD:512 + (i // 6 + 1) * HD] for i in range(12))
        ydiag = _bmm(jnp.stack(ws), jnp.stack(xs))
        yoff = _bmm(jnp.stack(cstack), sin)
        for cc, (base, xin, cum, cumt, dtt) in enumerate(per_chunk):
            outs = []
            for h in range(SSD_HEADS):
                g, hg = divmod(h, hpg)
                y = ydiag[cc * SSD_HEADS + h] + dsk[:, h * HD:(h + 1) * HD] * xs[cc * SSD_HEADS + h]
                for d in range(2):
                    i = cc * 12 + g * 6 + d * 3 + hg
                    y = y + yoff[i] * es[i]
                outs.append(y)
            y_ref[0, pl.ds(base, q), :] = jnp.concatenate(outs, axis=1).astype(BF16)
        return carry

    if ctx_out:
        first_trip = 0
    else:
        assert ctx_chunks % nb == 0
        first_trip = ctx_chunks // nb
        y_ref[0, :ctx_chunks * q, :] = jnp.zeros((ctx_chunks * q, SSD_W), BF16)
    lax.fori_loop(first_trip, nc // nb, outputs, 0)


def _ssd(ssd_in, small, prm, dskip, layer, ctx_out):
    bsz, t, _ = ssd_in.shape
    nc = t // SSD_CHUNK
    return pl.pallas_call(
        functools.partial(_ssd_kernel, nc, CTX // SSD_CHUNK, ctx_out),
        out_shape=jax.ShapeDtypeStruct((bsz, t, SSD_W), BF16),
        grid=(bsz,),
        in_specs=[pl.BlockSpec((1, t, 640), lambda b: (b, 0, 0)),
                  pl.BlockSpec((1, t, 128), lambda b: (b, 0, 0)),
                  pl.BlockSpec((1, 8, 128), lambda b: (layer, 0, 0)),
                  pl.BlockSpec((1, 1, SSD_W), lambda b: (layer, 0, 0))],
        out_specs=pl.BlockSpec((1, t, SSD_W), lambda b: (b, 0, 0)),
        scratch_shapes=[pltpu.VMEM((t, 128), F32), pltpu.VMEM((t, 128), F32), pltpu.VMEM((t, 128), F32),
                        pltpu.VMEM((nc * 12, HD, HD), F32),
                        pltpu.VMEM((nc * 12, HD, HD), F32)],
        compiler_params=pltpu.CompilerParams(dimension_semantics=("arbitrary",),
                                             vmem_limit_bytes=VMEM_LIMIT),
        name="ssd",
    )(ssd_in, small, prm, dskip)


def _bd4(x):
    xb = x.astype(BF16)
    blk = jnp.right_shift(_iota2((1, 4 * HD), 1), 6)
    zero = jnp.zeros_like(xb)
    return jnp.concatenate([jnp.where(blk == h, xb, zero) for h in range(GDN_HEADS)], axis=0)


def _bmm(a, b):
    return lax.dot_general(a.astype(BF16), b.astype(BF16), (((2,), (1,)), ((0,), (0,))),
                           preferred_element_type=F32)


def _bmm_nt(a, b):
    return lax.dot_general(a.astype(BF16), b.astype(BF16), (((2,), (2,)), ((0,), (0,))),
                           preferred_element_type=F32)


def _unit_tri_inverse_stages(a_strict):
    n = a_strict.shape[-1]
    ri = _iota2((1, n, n), 1)
    ci = _iota2((1, n, n), 2)

    def blk(shift):
        return jnp.right_shift(ri, shift) == jnp.right_shift(ci, shift)

    nd = jnp.where(blk(3), -a_strict, 0.0)
    p0 = (ri == ci).astype(F32) + nd
    q1 = _bmm(nd, nd)
    yield None
    t = _bmm(jnp.concatenate([p0, q1], axis=1), q1)
    yield None
    p1 = p0 + t[:, :n]
    inv = p1 + _bmm(p1, t[:, n:])
    yield None
    for shift in (4, 5, 6):
        e = jnp.where(jnp.logical_and(blk(shift), jnp.logical_not(blk(shift - 1))), a_strict, 0.0)
        ie = _bmm(inv, e)
        yield None
        inv = inv - _bmm(ie, inv)
        yield None
    yield inv


def _gdn_attn_kernel(nch, ctx_chunks, ctx_attn, gin_ref, small_ref, prm_ref, q_ref, kv_ref, o_ref, att_ref,
                     gc_s, rows_s, mq_s, n_s, cd_s, of_s, ob_s):
    c64 = GDN_CHUNK
    w4 = GDN_W
    nsc = nch // 2
    ri128 = _iota2((128, 128), 0)
    ci128 = _iota2((128, 128), 1)
    same = jnp.right_shift(ri128, 6) == jnp.right_shift(ci128, 6)
    lbd = jnp.logical_and(same, ri128 >= ci128).astype(BF16)
    ubd = jnp.logical_and(same, ri128 <= ci128).astype(BF16)
    lane = _iota2((1, 128), 1)
    fwd_lane = lane < SM_A + GDN_HEADS
    beta_lane = jnp.logical_and(lane >= SM_B, lane < SM_B + 2 * GDN_HEADS)

    a_row = -jnp.exp(prm_ref[0, 0:1, :])
    bias_row = prm_ref[0, 1:2, :]

    pb = 6
    tri = jnp.broadcast_to(jnp.concatenate([lbd, ubd], axis=0)[None], (pb, 256, 128))

    def prep(sb, carry):
        base = pl.multiple_of(sb * (pb * 128), 128)
        sm = small_ref[0, pl.ds(base, pb * 128), :]
        pieces = _split3((a_row * _softplus(sm + bias_row)).reshape(pb, 128, 128))
        dot = lambda x: lax.dot_general(tri, x, (((2,), (1,)), ((0,), (0,))), preferred_element_type=F32)
        sums = dot(pieces[0]) + (dot(pieces[1]) + dot(pieces[2]))
        cum = jnp.where(fwd_lane, sums[:, :128], sums[:, 128:])
        gc_s[pl.ds(base, pb * 128), :] = cum.reshape(pb * 128, 128)
        beta = jax.nn.sigmoid(sm)
        for i in range(pb):
            slab_t = jnp.where(beta_lane, beta[i * 128:(i + 1) * 128], cum[i]).T
            for half in range(2):
                lo = half * c64
                table = []
                for first in (SM_A, SM_A + GDN_HEADS, SM_B, SM_B + GDN_HEADS):
                    table.append(jnp.concatenate(
                        [slab_t[first + h:first + h + 1, lo:lo + c64] for h in range(GDN_HEADS)], axis=1))
                for d, edge in ((0, lo + c64 - 1), (1, lo)):
                    first = SM_A + d * GDN_HEADS
                    table.append(jnp.concatenate(
                        [jnp.broadcast_to(slab_t[first + h:first + h + 1, edge:edge + 1], (1, c64))
                         for h in range(GDN_HEADS)], axis=1))
                table.append(jnp.zeros((2, w4), F32))
                rows_s[pl.ds(pl.multiple_of((sb * pb + i) * 16 + half * 8, 8), 8), :] = (
                    jnp.concatenate(table, axis=0))
        return carry

    lax.fori_loop(0, nsc // pb, prep, 0)

    ri = _iota2((c64, c64), 0)
    ci = _iota2((c64, c64), 1)
    incl = (ri >= ci, ri <= ci)
    strict = (ri > ci, ri < ci)

    nb = 4

    def local(cb):
        chunks = []
        for cc in range(nb):
            c = cb * nb + cc
            base = _aligned(c * c64, c64)
            gin = gin_ref[0, pl.ds(base, c64), :]
            gc = gc_s[pl.ds(base, c64), :]
            rows = rows_s[pl.ds(_aligned(c * 8, 8), 8), :]
            qs = [gin[:, h * HD:(h + 1) * HD] for h in range(GDN_HEADS)]
            ks = [gin[:, w4 + h * HD:w4 + (h + 1) * HD] for h in range(GDN_HEADS)]
            vs = [gin[:, 2 * w4 + h * HD:2 * w4 + (h + 1) * HD] for h in range(GDN_HEADS)]
            chunks.append((c, base, gc, rows, qs, ks, vs))
        kq = _bmm_nt(jnp.stack([jnp.concatenate([ch[5][h], ch[4][h]], axis=0)
                                for ch in chunks for h in range(GDN_HEADS)]),
                     jnp.stack([ch[5][h] for ch in chunks for h in range(GDN_HEADS)]))
        yield
        mb_l, rhs_l, lhs_l, qd_l = [], [], [], []
        for ci, (c, base, gc, rows, qs, ks, vs) in enumerate(chunks):
            kts = [k.astype(F32).T for k in ks]
            for d in range(2):
                for h in range(GDN_HEADS):
                    sl = slice(h * HD, (h + 1) * HD)
                    colg = SM_A + d * GDN_HEADS + h
                    kqh = kq[ci * GDN_HEADS + h]
                    gmat = jnp.broadcast_to(gc[:, colg:colg + 1], (c64, c64))
                    gcr, br, endr = rows[d:d + 1, sl], rows[2 + d:3 + d, sl], rows[4 + d:5 + d, sl]
                    dec = jnp.exp(jnp.where(incl[d], gmat - gcr, NEG))
                    mb_l.append(jnp.where(strict[d], kqh[:c64] * dec, 0.0) * br)
                    eg = jnp.exp(gmat)
                    kdtb = kts[h] * (jnp.exp(endr - gcr) * br)
                    lhs_l.append(jnp.concatenate([kdtb, kqh[c64:] * dec * br], axis=0))
                    rhs_l.append(jnp.concatenate([vs[h], ks[h] * eg], axis=1))
                    qd_l.append(qs[h] * eg)
        z = None
        for z in _unit_tri_inverse_stages(jnp.stack(mb_l)):
            if z is None:
                yield
        zr = _bmm(z, jnp.stack(rhs_l))
        yield
        a12 = _bmm(jnp.stack(lhs_l), zr)
        yield
        for ci, (c, base, gc, rows, qs, ks, vs) in enumerate(chunks):
            for d in range(2):
                slot = c * 2 + d
                ids = [ci * 2 * GDN_HEADS + d * GDN_HEADS + h for h in range(GDN_HEADS)]
                mq = [jnp.concatenate([-a12[n][:c64, HD:], qd_l[n] - a12[n][c64:, HD:]], axis=0) for n in ids]
                mq_s[slot] = jnp.concatenate(mq, axis=1).astype(BF16)
                n_s[slot] = jnp.concatenate([a12[n][:c64, :HD] for n in ids], axis=1)
                cd_s[slot] = jnp.broadcast_to(jnp.exp(rows[4 + d:5 + d]), (8, w4))
                oc = jnp.concatenate([a12[n][c64:, :HD] for n in ids], axis=1)
                if d == 0:
                    of_s[pl.ds(base, c64), :] = oc
                else:
                    ob_s[pl.ds(base, c64), :] = oc

    t_all = q_ref.shape[1]
    heads_per_kv = ATT_QH // ATT_KVH

    def attend(tile, nkeys):
        rows = pl.ds(_aligned(tile * TM, TM), TM)
        qt = q_ref[0, rows, :]
        outs = []

        def finish(h, s):
            g = h // heads_per_kv
            p = jnp.exp2(s - jnp.max(s, axis=-1, keepdims=True))
            acc = _mm(p, kv_ref[0, g, :nkeys, 2 * HD:])
            outs.append(acc[:, :HD] / acc[:, HD:])
            if h % heads_per_kv == heads_per_kv - 1:
                tile_out = outs[-heads_per_kv:] + [jnp.zeros((TM, HD), F32)]
                att_ref[0, g, rows, :] = jnp.concatenate(tile_out, axis=1).astype(BF16)

        s_prev = None
        for h in range(ATT_QH):
            k = kv_ref[0, h // heads_per_kv, :nkeys, :HD]
            s = _mm_nt(qt[:, h * HD:(h + 1) * HD], k)
            yield
            if s_prev is not None:
                finish(h - 1, s_prev)
                yield
            s_prev = s
        finish(ATT_QH - 1, s_prev)
        yield

    def trip(j, nkeys):
        streams = ([attend(j, nkeys)] if nkeys else []) + [local(j)]
        while streams:
            for st in list(streams):
                try:
                    next(st)
                except StopIteration:
                    streams.remove(st)

    trip(0, CTX if ctx_attn else 0)
    if not ctx_attn:
        att_ref[0, :, :TM, :] = jnp.zeros((ATT_KVH, TM, 256), BF16)

    def later_trip(j, carry):
        trip(j, t_all)
        return carry

    lax.fori_loop(1, nch // nb, later_trip, 0)

    def step(t, states):
        cf = t
        cb = jnp.where(t < ctx_chunks, ctx_chunks - 1 - t, nch - 1 + ctx_chunks - t)
        new_states = []
        for d, c in ((0, cf), (1, cb)):
            slot = c * 2 + d
            s = states[d]
            r = jnp.dot(mq_s[slot], _bd4(s), preferred_element_type=F32)
            new_states.append(s * cd_s[slot][0:1] + r[:c64] + n_s[slot])
            rows = pl.ds(pl.multiple_of(c * c64, c64), c64)
            if d == 0:
                of_s[rows, :] = of_s[rows, :] + r[c64:]
            else:
                ob_s[rows, :] = ob_s[rows, :] + r[c64:]
        return tuple(new_states)

    zero = jnp.zeros((c64, w4), F32)
    lax.fori_loop(0, nch, step, (zero, zero), unroll=4)
    o_ref[0] = (of_s[...] + ob_s[...]).astype(BF16)


def _gdn_attention(gdn_in, small, prm, q, kv, layer, ctx_attn):
    bsz, t, _ = gdn_in.shape
    nch = t // GDN_CHUNK
    assert nch // 4 == t // TM
    return pl.pallas_call(
        functools.partial(_gdn_attn_kernel, nch, CTX // GDN_CHUNK, ctx_attn),
        out_shape=[jax.ShapeDtypeStruct((bsz, t, GDN_W), BF16),
                   jax.ShapeDtypeStruct((bsz, ATT_KVH, t, 256), BF16)],
        grid=(bsz,),
        in_specs=[pl.BlockSpec((1, t, 768), lambda b: (b, 0, 0)),
                  pl.BlockSpec((1, t, 128), lambda b: (b, 0, 0)),
                  pl.BlockSpec((1, 8, 128), lambda b: (layer, 0, 0)),
                  pl.BlockSpec((1, t, ATT_W), lambda b: (b, 0, 0)),
                  pl.BlockSpec((1, ATT_KVH, t, 256), lambda b: (b, 0, 0, 0))],
        out_specs=[pl.BlockSpec((1, t, GDN_W), lambda b: (b, 0, 0), pipeline_mode=pl.Buffered(1)),
                   pl.BlockSpec((1, ATT_KVH, t, 256), lambda b: (b, 0, 0, 0), pipeline_mode=pl.Buffered(1))],
        scratch_shapes=[pltpu.VMEM((t, 128), F32),
                        pltpu.VMEM((nch * 8, GDN_W), F32),
                        pltpu.VMEM((nch * 2, 128, GDN_W), BF16),
                        pltpu.VMEM((nch * 2, GDN_CHUNK, GDN_W), F32),
                        pltpu.VMEM((nch * 2, 8, GDN_W), F32),
                        pltpu.VMEM((t, GDN_W), F32),
                        pltpu.VMEM((t, GDN_W), F32)],
        compiler_params=pltpu.CompilerParams(dimension_semantics=("arbitrary",),
                                             vmem_limit_bytes=VMEM_LIMIT),
        name="gdn_attention",
    )(gdn_in, small, prm, q, kv)


def _out_kernel(first_tile, xa_ref, x_ref, y_ref, o_ref, a_ref, zs_ref, *refs):
    mods = refs[:BPS]
    sw_ref, gw_ref, w_ref, out_ref = refs[BPS:]

    def mixed(s):
        zs = zs_ref[s].astype(F32)
        t = y_ref[s].astype(F32) * zs[:, :SSD_W]
        ssd = t * lax.rsqrt(jnp.mean(t * t, axis=-1, keepdims=True) + EPS) * sw_ref[0]
        o = o_ref[s].astype(F32)
        gdn = o * lax.rsqrt(_group_sum64(o * o) * (1.0 / HD) + EPS) * gw_ref[0] * zs[:, SSD_W:SSD_W + GDN_W]
        a = a_ref[s]
        att = jnp.concatenate([a[g][:, :HD * ATT_QH // ATT_KVH] for g in range(ATT_KVH)], axis=1)
        att = att * zs[:, SSD_W + GDN_W:]
        return jnp.concatenate([ssd, gdn, att], axis=1).astype(BF16)

    mixes = [mixed(s) for s in range(BPS)]
    projs = [jnp.dot(mix, w_ref[0], preferred_element_type=F32) for mix in mixes]
    for s in range(BPS):
        gate = mods[s][0][:, 2 * D_MODEL:]
        x = jnp.where(pl.program_id(1) + first_tile == 0, xa_ref[s], x_ref[s])
        out_ref[s] = x + gate * projs[s]


def _out_projection(xa, xb, off, layer, y, o, a, zs, mod3, ssd_nw, gdn_nw, w_out, first_tile):
    bsz, t, _ = y.shape
    nt = t // TM - first_tile
    row = lambda b, i: (b, i + first_tile, 0)
    lay = lambda b, i: (layer, 0, 0)

    def mod_spec(s):
        return pl.BlockSpec((1, 1, 3 * D_MODEL),
                            lambda b, i: (jnp.where(i + first_tile == 0, bsz, b * BPS + s), 0, 0))

    return pl.pallas_call(
        functools.partial(_out_kernel, first_tile),
        out_shape=jax.ShapeDtypeStruct((bsz, nt * TM, D_MODEL), F32),
        grid=(bsz // BPS, nt),
        in_specs=[pl.BlockSpec((BPS, TM, D_MODEL), lambda b, i: (b, 0, 0)),
                  pl.BlockSpec((BPS, TM, D_MODEL), lambda b, i: (b, jnp.maximum(i + first_tile - off, 0), 0)),
                  pl.BlockSpec((BPS, TM, SSD_W), row),
                  pl.BlockSpec((BPS, TM, GDN_W), row),
                  pl.BlockSpec((BPS, ATT_KVH, TM, 256), lambda b, i: (b, 0, i + first_tile, 0)),
                  pl.BlockSpec((BPS, TM, D_MODEL), row),
                  *[mod_spec(s) for s in range(BPS)],
                  pl.BlockSpec((1, 1, SSD_W), lay),
                  pl.BlockSpec((1, 1, GDN_W), lay),
                  pl.BlockSpec((1, D_MODEL, D_MODEL), lay)],
        out_specs=pl.BlockSpec((BPS, TM, D_MODEL), lambda b, i: (b, i, 0)),
        compiler_params=pltpu.CompilerParams(dimension_semantics=("arbitrary", "arbitrary"),
                                             vmem_limit_bytes=VMEM_LIMIT),
        name="out_projection",
    )(xa, xb, y, o, a, zs, *([mod3] * BPS), ssd_nw, gdn_nw, w_out)


_REST_SEGS = [(2076, 384), (2460, 128), (1408, 384), (1804, 256), (2588, 128), (2716, 384),
              (1792, 12), (2060, 8), (2068, 8)]


def _prepare_params(w_in, conv_w, conv_b, ssd_A_log, ssd_dt_bias, gdn_A_log, gdn_dt_bias):
    depth = w_in.shape[0]
    w_conv = w_in[:, :, :CONV_DIM].astype(BF16)
    pad = jnp.zeros(w_in.shape[:2] + (REST_DIM - sum(n for _, n in _REST_SEGS),), w_in.dtype)
    w_rest = jnp.concatenate([lax.slice_in_dim(w_in, a, a + n, axis=2) for a, n in _REST_SEGS] + [pad],
                             axis=2).astype(BF16)
    cw = conv_w
    cb = conv_b[:, None, :]

    def lanes(a, b):
        v = jnp.concatenate([a.reshape(depth, -1), b.reshape(depth, -1)], axis=1)
        return jnp.pad(v, ((0, 0), (0, 128 - v.shape[1])))

    prm = jnp.stack([lanes(ssd_A_log, gdn_A_log), lanes(ssd_dt_bias, gdn_dt_bias)], axis=1)
    prm = jnp.pad(prm, ((0, 0), (0, 6), (0, 0)))
    return w_conv, w_rest, cw, cb, prm


def _rope_tables(t):
    f32 = np.float32
    pos = np.arange(t - CTX)
    n_freq = HD // 4
    freqs = np.power(f32(ROPE_THETA), -np.arange(n_freq, dtype=f32) / f32(n_freq)).astype(f32)
    ang_r = ((pos // GRID_W).astype(f32)[:, None] * freqs).astype(f32)
    ang_c = ((pos % GRID_W).astype(f32)[:, None] * freqs).astype(f32)
    cos = np.concatenate([np.cos(ang_r)] * 2 + [np.cos(ang_c)] * 2, axis=1)
    sin = np.concatenate([-np.sin(ang_r), np.sin(ang_r), -np.sin(ang_c), np.sin(ang_c)], axis=1)
    cos = np.concatenate([np.ones((CTX, HD), f32), cos], axis=0).astype(f32)
    sin = np.concatenate([np.zeros((CTX, HD), f32), sin], axis=0).astype(f32)
    return jnp.asarray(np.tile(cos, (1, 2))), jnp.asarray(np.tile(sin, (1, 2)))


def kernel(x, c, ctx, c_ctx, norm_w, w_mod, b_mod, w_in, conv_w, conv_b, ssd_A_log, ssd_dt_bias, ssd_D,
           ssd_norm_w, gdn_A_log, gdn_dt_bias, gdn_norm_w, q_norm_w, k_norm_w, w_out):
    bsz = x.shape[0]
    t = CTX + x.shape[1]
    c_all = jnp.concatenate([c, c_ctx[None, :], jnp.zeros((7, D_MODEL), F32)], axis=0)
    cos_t, sin_t = _rope_tables(t)
    w_conv, w_rest, cw, cb, prm = _prepare_params(w_in, conv_w, conv_b, ssd_A_log, ssd_dt_bias,
                                                  gdn_A_log, gdn_dt_bias)
    qk_w = jnp.concatenate([jnp.tile(q_norm_w, (1, ATT_QH)), jnp.tile(k_norm_w, (1, ATT_KVH))], axis=1)[:, None]
    dskip = jnp.repeat(ssd_D, HD, axis=1)[:, None]
    gdn_nw = jnp.tile(gdn_norm_w, (1, GDN_HEADS))[:, None]
    norm_w3, ssd_nw = norm_w[:, None], ssd_norm_w[:, None]
    w_out16 = w_out.astype(BF16)

    xa, xb, off = ctx, x, 1
    for layer in range(DEPTH):
        first_tile = 1 if layer == DEPTH - 1 else 0
        mod3 = _modulation(c_all, w_mod, b_mod, layer).reshape(bsz + 8, 1, 3 * D_MODEL)
        ssd_in, gdn_in, zs, q, kv, small = _in_projection(
            xa, xb, off, layer, mod3, norm_w3, w_conv, w_rest, cw, cb, qk_w, cos_t, sin_t)
        y = _ssd(ssd_in, small, prm, dskip, layer, first_tile == 0)
        o, a = _gdn_attention(gdn_in, small, prm, q, kv, layer, first_tile == 0)
        out = _out_projection(xa, xb, off, layer, y, o, a, zs, mod3, ssd_nw, gdn_nw, w_out16, first_tile)
        xa, xb, off = out, out, 0
    return out
```

```python
import functools
import math

import jax
import jax.numpy as jnp
import numpy as np
from jax import lax
from jax.experimental import pallas as pl
from jax.experimental.pallas import tpu as pltpu

F32 = jnp.float32
BF16 = jnp.bfloat16

D_MODEL = 1024
CTX = 256
GRID_W = 64
EPS = 1e-6
DEPTH = 2

HD = 64
SSD_HEADS = 6
SSD_W = SSD_HEADS * HD
SSD_CHUNK = 128
GDN_HEADS = 4
GDN_W = GDN_HEADS * HD
GDN_CHUNK = 64
ATT_QH = 6
ATT_KVH = 2
ATT_W = ATT_QH * HD
ROPE_THETA = 10000.0
Q_PRESCALE = (HD ** -0.5) * math.log2(math.e)

CONV_DIM = 1408
REST_DIM = 1792
TM = 256
HALO = 8
NEG = -1e30
VMEM_LIMIT = 56 * 1024 * 1024

SM_DT = 0
SM_A = 12
SM_B = 20


def _mm(a, b):
    return jnp.dot(a.astype(BF16), b.astype(BF16), preferred_element_type=F32)


def _mm_nt(a, b):
    return lax.dot_general(a.astype(BF16), b.astype(BF16), (((1,), (1,)), ((), ())),
                           preferred_element_type=F32)


def _split3(a):
    h = a.astype(BF16)
    r = a - h.astype(F32)
    m = r.astype(BF16)
    l = (r - m.astype(F32)).astype(BF16)
    return h, m, l


def _silu(x):
    return x * jax.nn.sigmoid(x)


def _softplus(x):
    return jnp.maximum(x, 0.0) + jnp.log1p(jnp.exp(-jnp.abs(x)))


def _iota2(shape, dim):
    return lax.broadcasted_iota(jnp.int32, shape, dim)


def _aligned(x, m):
    return x if isinstance(x, int) else pl.multiple_of(x, m)


def _group_sum64(xx):
    r = jnp.right_shift(_iota2((128, 128), 0), 6)
    c = jnp.right_shift(_iota2((128, 128), 1), 6)
    g = (r == c).astype(BF16)
    xb = xx.astype(BF16)
    outs = [jnp.dot(xb[:, t * 128:(t + 1) * 128], g, preferred_element_type=F32)
            for t in range(xx.shape[1] // 128)]
    return outs[0] if len(outs) == 1 else jnp.concatenate(outs, axis=1)


def _mod_kernel(c_ref, w_ref, b_ref, o_ref):
    o_ref[...] = _mm(_silu(c_ref[...]), w_ref[0]) + b_ref[0]


def _modulation(c_all, w_mod, b_mod, layer):
    n = w_mod.shape[2]
    bn = 768
    rows = c_all.shape[0]
    return pl.pallas_call(
        _mod_kernel,
        out_shape=jax.ShapeDtypeStruct((rows, n), F32),
        grid=(n // bn,),
        in_specs=[pl.BlockSpec((rows, D_MODEL), lambda j: (0, 0)),
                  pl.BlockSpec((1, D_MODEL, bn), lambda j: (layer, 0, j)),
                  pl.BlockSpec((1, 1, bn), lambda j: (layer, 0, j))],
        out_specs=pl.BlockSpec((rows, bn), lambda j: (0, j)),
        compiler_params=pltpu.CompilerParams(dimension_semantics=("arbitrary",),
                                             vmem_limit_bytes=VMEM_LIMIT),
        name="modulation",
    )(c_all, w_mod, b_mod.reshape(b_mod.shape[0], 1, n))


BPS = 4


def _proj_kernel(nt, xa_ref, xm_ref, xp_ref, xn_ref, *refs):
    i = pl.program_id(1)
    mods = refs[:BPS]
    (nw_ref, wc_ref, wr_ref, cw_ref, cb_ref, qkw_ref, cos_ref, sin_ref,
     ssd_ref, gdn_ref, zs_ref, q_ref, kv_ref, small_ref) = refs[BPS:]

    def normed(s):
        m = mods[s][0]
        gain = nw_ref[0] * (1.0 + m[:, D_MODEL:2 * D_MODEL])
        shift = m[:, :D_MODEL]
        xm = jnp.where(i == 0, xa_ref[s], xm_ref[s])
        xe = jnp.concatenate([xm, xp_ref[s], xn_ref[s]], axis=0)
        ms = jnp.mean(xe * xe, axis=-1, keepdims=True)
        return (xe * lax.rsqrt(ms + EPS) * gain + shift).astype(BF16)

    def project(he):
        p = jnp.dot(he, wc_ref[0], preferred_element_type=F32)
        aqk = jnp.dot(he[:TM], wr_ref[0, :, :512], preferred_element_type=F32)
        r = jnp.dot(he[:TM], wr_ref[0, :, 512:], preferred_element_type=F32)
        return p, aqk, r

    def finish(s, p, aqk, r):
        pm = p[:TM]
        seg_first = i <= 1
        seg_last = jnp.logical_or(i == 0, i == nt - 1)
        prev_row = jnp.where(seg_first, 0.0, p[TM + HALO - 1:TM + HALO])
        next_row = jnp.where(seg_last, 0.0, p[TM + HALO:TM + HALO + 1])
        rows = _iota2((TM, 1), 0)
        pm1 = jnp.where(rows == 0, prev_row, pltpu.roll(pm, 1, axis=0))
        pp1 = jnp.where(rows == TM - 1, next_row, pltpu.roll(pm, TM - 1, axis=0))
        cw = cw_ref[0]
        conv = cw[0:1] * pm1 + cw[1:2] * pm + cw[2:3] * pp1 + cb_ref[0]
        co = _silu(conv)

        ssd_ref[s] = co[:, :640].astype(BF16)
        gqk = co[:, 640:1152]
        gqk = gqk * lax.rsqrt(_group_sum64(gqk * gqk) + EPS)
        gdn_ref[s, :, :256] = (gqk[:, :256] * (HD ** -0.5)).astype(BF16)
        gdn_ref[s, :, 256:512] = gqk[:, 256:].astype(BF16)
        gdn_ref[s, :, 512:] = co[:, 1152:1408].astype(BF16)

        zs_ref[s, :, :640] = _silu(r[:, :640]).astype(BF16)
        zs_ref[s, :, 640:] = _silu(r[:, 768:1152]).astype(BF16)

        aqk = aqk * lax.rsqrt(_group_sum64(aqk * aqk) * (1.0 / HD) + EPS) * qkw_ref[0]
        lane = _iota2((1, 512), 1)
        swapped = jnp.where(jnp.bitwise_and(lane, 31) < 16,
                            pltpu.roll(aqk, 512 - 16, axis=1), pltpu.roll(aqk, 16, axis=1))
        cos = jnp.concatenate([cos_ref[...]] * 4, axis=1)
        sin = jnp.concatenate([sin_ref[...]] * 4, axis=1)
        aqk = aqk * cos + swapped * sin
        q_ref[s] = (aqk[:, :384] * Q_PRESCALE).astype(BF16)
        ones = jnp.ones((TM, HD), F32)
        zeros = jnp.zeros((TM, HD), F32)
        for g in range(ATT_KVH):
            kv_ref[s, g] = jnp.concatenate([aqk[:, 384 + g * HD:384 + (g + 1) * HD], zeros,
                                            r[:, 640 + g * HD:640 + (g + 1) * HD], ones], axis=1).astype(BF16)
        small_ref[s] = r[:, 1152:1280]

    hes = [normed(s) for s in range(BPS)]
    prs = [project(he) for he in hes]
    for s in range(BPS):
        finish(s, *prs[s])


def _in_projection(xa, xb, off, layer, mod3, norm_w, w_conv, w_rest, conv_w, conv_b, qk_w, cos_t, sin_t):
    bsz = xb.shape[0]
    assert bsz % BPS == 0
    nt = xb.shape[1] // TM + off
    t = nt * TM
    nblk8 = xb.shape[1] // HALO
    per_tile = TM // HALO
    row = lambda b, i: (b, i, 0)
    lay = lambda b, i: (layer, 0, 0)
    outs = [jax.ShapeDtypeStruct((bsz, t, w), dt)
            for w, dt in ((640, BF16), (768, BF16), (1024, BF16), (384, BF16), (128, F32))]
    outs.insert(4, jax.ShapeDtypeStruct((bsz, ATT_KVH, t, 256), BF16))

    def mod_spec(s):
        return pl.BlockSpec((1, 1, 3 * D_MODEL), lambda b, i: (jnp.where(i == 0, bsz, b * BPS + s), 0, 0))

    return pl.pallas_call(
        functools.partial(_proj_kernel, nt),
        out_shape=outs,
        grid=(bsz // BPS, nt),
        in_specs=[
            pl.BlockSpec((BPS, TM, D_MODEL), lambda b, i: (b, 0, 0)),
            pl.BlockSpec((BPS, TM, D_MODEL), lambda b, i: (b, jnp.maximum(i - off, 0), 0)),
            pl.BlockSpec((BPS, HALO, D_MODEL), lambda b, i: (b, jnp.maximum((i - off) * per_tile - 1, 0), 0)),
            pl.BlockSpec((BPS, HALO, D_MODEL),
                         lambda b, i: (b, jnp.minimum((i - off + 1) * per_tile, nblk8 - 1), 0)),
            *[mod_spec(s) for s in range(BPS)],
            pl.BlockSpec((1, 1, D_MODEL), lay),
            pl.BlockSpec((1, D_MODEL, CONV_DIM), lay),
            pl.BlockSpec((1, D_MODEL, REST_DIM), lay),
            pl.BlockSpec((1, 3, CONV_DIM), lay),
            pl.BlockSpec((1, 1, CONV_DIM), lay),
            pl.BlockSpec((1, 1, 512), lay),
            pl.BlockSpec((TM, 128), lambda b, i: (i, 0)),
            pl.BlockSpec((TM, 128), lambda b, i: (i, 0)),
        ],
        out_specs=[pl.BlockSpec((BPS, TM, 640), row), pl.BlockSpec((BPS, TM, 768), row),
                   pl.BlockSpec((BPS, TM, 1024), row), pl.BlockSpec((BPS, TM, 384), row),
                   pl.BlockSpec((BPS, ATT_KVH, TM, 256), lambda b, i: (b, 0, i, 0)),
                   pl.BlockSpec((BPS, TM, 128), row)],
        compiler_params=pltpu.CompilerParams(dimension_semantics=("arbitrary", "arbitrary"),
                                             vmem_limit_bytes=VMEM_LIMIT),
        name="in_projection",
    )(xa, xb, xb, xb, *([mod3] * BPS), norm_w, w_conv, w_rest, conv_w, conv_b, qk_w, cos_t, sin_t)


def _ssd_kernel(nc, ctx_chunks, ctx_out, xin_ref, small_ref, prm_ref, dskip_ref, y_ref,
                cum_s, cumt_s, dtt_s, st_s, sin_s):
    q = SSD_CHUNK
    hpg = SSD_HEADS // 2
    ri = _iota2((q, q), 0)
    ci = _iota2((q, q), 1)
    ltri = (ri >= ci).astype(BF16)
    utri = (ri <= ci).astype(BF16)
    lane = _iota2((1, 128), 1)
    fwd_lane = lane < SSD_HEADS

    a_row = -jnp.exp(prm_ref[0, 0:1, :])
    bias_row = prm_ref[0, 1:2, :]

    pb = 6
    tri = jnp.broadcast_to(jnp.concatenate([ltri, utri], axis=0)[None], (pb, 2 * q, q))

    def prep(cb, carry):
        base = pl.multiple_of(cb * (pb * q), q)
        dt = _softplus(small_ref[0, pl.ds(base, pb * q), :] + bias_row)
        pieces = _split3((dt * a_row).reshape(pb, q, 128))
        dot = lambda x: lax.dot_general(tri, x, (((2,), (1,)), ((0,), (0,))), preferred_element_type=F32)
        sums = dot(pieces[0]) + (dot(pieces[1]) + dot(pieces[2]))
        cum = jnp.where(fwd_lane, sums[:, :q], sums[:, q:])
        cum_s[pl.ds(base, pb * q), :] = cum.reshape(pb * q, 128)
        for i in range(pb):
            rows = pl.ds(pl.multiple_of(base + i * q, q), q)
            cumt_s[rows, :] = cum[i].T
            dtt_s[rows, :] = dt[i * q:(i + 1) * q].T
        return carry

    lax.fori_loop(0, nc // pb, prep, 0)

    def end_of(cumt, d, col):
        return cumt[col:col + 1, q - 1:q] if d == 0 else cumt[col:col + 1, 0:1]

    nb = 2

    def states(cb, carry):
        lhs, rhs = [], []
        for cc in range(nb):
            base = pl.multiple_of((cb * nb + cc) * q, q)
            xin = xin_ref[0, pl.ds(base, q), :]
            cumt = cumt_s[pl.ds(base, q), :]
            dtt = dtt_s[pl.ds(base, q), :]
            for g in range(2):
                bgt = xin[:, 384 + g * HD:384 + (g + 1) * HD].astype(F32).T
                for d in range(2):
                    for hg in range(hpg):
                        h = g * hpg + hg
                        col = d * SSD_HEADS + h
                        w_row = jnp.exp(end_of(cumt, d, col) - cumt[col:col + 1, :]) * dtt[col:col + 1, :]
                        lhs.append(bgt * w_row)
                        rhs.append(xin[:, h * HD:(h + 1) * HD])
        st = _bmm(jnp.stack(lhs), jnp.stack(rhs))
        st_s[pl.ds(cb * (12 * nb), 12 * nb)] = st
        return carry

    lax.fori_loop(0, nc // nb, states, 0, unroll=3)

    orders = (list(range(nc)),
              list(range(ctx_chunks - 1, -1, -1)) + list(range(nc - 1, ctx_chunks - 1, -1)))
    for g in range(2):
        for d in range(2):
            for hg in range(hpg):
                col = d * SSD_HEADS + g * hpg + hg
                s = jnp.zeros((HD, HD), F32)
                for c in orders[d]:
                    slot = c * 12 + g * 6 + d * 3 + hg
                    sin_s[slot] = s
                    row = c * q + (q - 1 if d == 0 else 0)
                    s = s * jnp.exp(cum_s[row:row + 1, col:col + 1]) + st_s[slot]

    lower = ri >= ci
    upper = ri <= ci
    dsk = dskip_ref[0]

    def outputs(cb, carry):
        sin = sin_s[pl.ds(cb * (12 * nb), 12 * nb)]
        ws, xs, cstack, es, scs = [], [], [], [], []
        per_chunk = []
        for cc in range(nb):
            base = pl.multiple_of((cb * nb + cc) * q, q)
            xin = xin_ref[0, pl.ds(base, q), :]
            per_chunk.append((base, xin, cum_s[pl.ds(base, q), :], cumt_s[pl.ds(base, q), :],
                              dtt_s[pl.ds(base, q), :]))
        sc = _bmm_nt(jnp.stack([xin[:, 512 + g * HD:512 + (g + 1) * HD] for _, xin, _, _, _ in per_chunk
                                for g in range(2)]),
                     jnp.stack([xin[:, 384 + g * HD:384 + (g + 1) * HD] for _, xin, _, _, _ in per_chunk
                                for g in range(2)]))
        for cc, (base, xin, cum, cumt, dtt) in enumerate(per_chunk):
            e_chunk = {}
            for g in range(2):
                cg = xin[:, 512 + g * HD:512 + (g + 1) * HD]
                for hg in range(hpg):
                    h = g * hpg + hg
                    w = None
                    for d in range(2):
                        col = d * SSD_HEADS + h
                        gmat = jnp.broadcast_to(cum[:, col:col + 1], (q, q))
                        lm = jnp.exp(jnp.where(lower if d == 0 else upper, gmat - cumt[col:col + 1, :], NEG))
                        wd = lm * dtt[col:col + 1, :]
                        w = wd if w is None else w + wd
                        e_chunk[g * 6 + d * 3 + hg] = jnp.exp(gmat[:, :HD])
                    ws.append(w * sc[cc * 2 + g])
                    xs.append(xin[:, h * HD:(h + 1) * HD])
            es.extend(e_chunk[i] for i in range(12))
            cstack.extend(xin[:, 512 + (i // 6) * HD:512 + (i // 6 + 1) * HD] for i in range(12))
        ydiag = _bmm(jnp.stack(ws), jnp.stack(xs))
        yoff = _bmm(jnp.stack(cstack), sin)
        for cc, (base, xin, cum, cumt, dtt) in enumerate(per_chunk):
            outs = []
            for h in range(SSD_HEADS):
                g, hg = divmod(h, hpg)
                y = ydiag[cc * SSD_HEADS + h] + dsk[:, h * HD:(h + 1) * HD] * xs[cc * SSD_HEADS + h]
                for d in range(2):
                    i = cc * 12 + g * 6 + d * 3 + hg
                    y = y + yoff[i] * es[i]
                outs.append(y)
            y_ref[0, pl.ds(base, q), :] = jnp.concatenate(outs, axis=1).astype(BF16)
        return carry

    if ctx_out:
        first_trip = 0
    else:
        assert ctx_chunks % nb == 0
        first_trip = ctx_chunks // nb
        y_ref[0, :ctx_chunks * q, :] = jnp.zeros((ctx_chunks * q, SSD_W), BF16)
    lax.fori_loop(first_trip, nc // nb, outputs, 0, unroll=3 if ctx_out else 2)


def _ssd(ssd_in, small, prm, dskip, layer, ctx_out):
    bsz, t, _ = ssd_in.shape
    nc = t // SSD_CHUNK
    return pl.pallas_call(
        functools.partial(_ssd_kernel, nc, CTX // SSD_CHUNK, ctx_out),
        out_shape=jax.ShapeDtypeStruct((bsz, t, SSD_W), BF16),
        grid=(bsz,),
        in_specs=[pl.BlockSpec((1, t, 640), lambda b: (b, 0, 0)),
                  pl.BlockSpec((1, t, 128), lambda b: (b, 0, 0)),
                  pl.BlockSpec((1, 8, 128), lambda b: (layer, 0, 0)),
                  pl.BlockSpec((1, 1, SSD_W), lambda b: (layer, 0, 0))],
        out_specs=pl.BlockSpec((1, t, SSD_W), lambda b: (b, 0, 0)),
        scratch_shapes=[pltpu.VMEM((t, 128), F32), pltpu.VMEM((t, 128), F32), pltpu.VMEM((t, 128), F32),
                        pltpu.VMEM((nc * 12, HD, HD), F32),
                        pltpu.VMEM((nc * 12, HD, HD), F32)],
        compiler_params=pltpu.CompilerParams(dimension_semantics=("arbitrary",),
                                             vmem_limit_bytes=VMEM_LIMIT),
        name="ssd",
    )(ssd_in, small, prm, dskip)


def _bd4(x):
    xb = x.astype(BF16)
    blk = jnp.right_shift(_iota2((1, 4 * HD), 1), 6)
    zero = jnp.zeros_like(xb)
    return jnp.concatenate([jnp.where(blk == h, xb, zero) for h in range(GDN_HEADS)], axis=0)


def _bmm(a, b):
    return lax.dot_general(a.astype(BF16), b.astype(BF16), (((2,), (1,)), ((0,), (0,))),
                           preferred_element_type=F32)


def _bmm_nt(a, b):
    return lax.dot_general(a.astype(BF16), b.astype(BF16), (((2,), (2,)), ((0,), (0,))),
                           preferred_element_type=F32)


def _unit_tri_inverse_stages(a_strict):
    n = a_strict.shape[-1]
    ri = _iota2((1, n, n), 1)
    ci = _iota2((1, n, n), 2)

    def blk(shift):
        return jnp.right_shift(ri, shift) == jnp.right_shift(ci, shift)

    nd = jnp.where(blk(3), -a_strict, 0.0)
    p0 = (ri == ci).astype(F32) + nd
    q1 = _bmm(nd, nd)
    yield None
    t = _bmm(jnp.concatenate([p0, q1], axis=1), q1)
    yield None
    p1 = p0 + t[:, :n]
    inv = p1 + _bmm(p1, t[:, n:])
    yield None
    for shift in (4, 5, 6):
        e = jnp.where(jnp.logical_and(blk(shift), jnp.logical_not(blk(shift - 1))), a_strict, 0.0)
        ie = _bmm(inv, e)
        yield None
        inv = inv - _bmm(ie, inv)
        yield None
    yield inv


def _gdn_attn_kernel(nch, ctx_chunks, ctx_attn, gin_ref, small_ref, prm_ref, q_ref, kv_ref, o_ref, att_ref,
                     gc_s, rows_s, mq_s, n_s, cd_s, of_s, ob_s):
    c64 = GDN_CHUNK
    w4 = GDN_W
    nsc = nch // 2
    ri128 = _iota2((128, 128), 0)
    ci128 = _iota2((128, 128), 1)
    same = jnp.right_shift(ri128, 6) == jnp.right_shift(ci128, 6)
    lbd = jnp.logical_and(same, ri128 >= ci128).astype(BF16)
    ubd = jnp.logical_and(same, ri128 <= ci128).astype(BF16)
    lane = _iota2((1, 128), 1)
    fwd_lane = lane < SM_A + GDN_HEADS
    beta_lane = jnp.logical_and(lane >= SM_B, lane < SM_B + 2 * GDN_HEADS)

    a_row = -jnp.exp(prm_ref[0, 0:1, :])
    bias_row = prm_ref[0, 1:2, :]

    pb = 6
    tri = jnp.broadcast_to(jnp.concatenate([lbd, ubd], axis=0)[None], (pb, 256, 128))

    def prep(sb, carry):
        base = pl.multiple_of(sb * (pb * 128), 128)
        sm = small_ref[0, pl.ds(base, pb * 128), :]
        pieces = _split3((a_row * _softplus(sm + bias_row)).reshape(pb, 128, 128))
        dot = lambda x: lax.dot_general(tri, x, (((2,), (1,)), ((0,), (0,))), preferred_element_type=F32)
        sums = dot(pieces[0]) + (dot(pieces[1]) + dot(pieces[2]))
        cum = jnp.where(fwd_lane, sums[:, :128], sums[:, 128:])
        gc_s[pl.ds(base, pb * 128), :] = cum.reshape(pb * 128, 128)
        beta = jax.nn.sigmoid(sm)
        for i in range(pb):
            slab_t = jnp.where(beta_lane, beta[i * 128:(i + 1) * 128], cum[i]).T
            for half in range(2):
                lo = half * c64
                table = []
                for first in (SM_A, SM_A + GDN_HEADS, SM_B, SM_B + GDN_HEADS):
                    table.append(jnp.concatenate(
                        [slab_t[first + h:first + h + 1, lo:lo + c64] for h in range(GDN_HEADS)], axis=1))
                for d, edge in ((0, lo + c64 - 1), (1, lo)):
                    first = SM_A + d * GDN_HEADS
                    table.append(jnp.concatenate(
                        [jnp.broadcast_to(slab_t[first + h:first + h + 1, edge:edge + 1], (1, c64))
                         for h in range(GDN_HEADS)], axis=1))
                table.append(jnp.zeros((2, w4), F32))
                rows_s[pl.ds(pl.multiple_of((sb * pb + i) * 16 + half * 8, 8), 8), :] = (
                    jnp.concatenate(table, axis=0))
        return carry

    lax.fori_loop(0, nsc // pb, prep, 0)

    ri = _iota2((c64, c64), 0)
    ci = _iota2((c64, c64), 1)
    incl = (ri >= ci, ri <= ci)
    strict = (ri > ci, ri < ci)

    nb = 4

    def local(cb):
        chunks = []
        for cc in range(nb):
            c = cb * nb + cc
            base = _aligned(c * c64, c64)
            gin = gin_ref[0, pl.ds(base, c64), :]
            gc = gc_s[pl.ds(base, c64), :]
            rows = rows_s[pl.ds(_aligned(c * 8, 8), 8), :]
            qs = [gin[:, h * HD:(h + 1) * HD] for h in range(GDN_HEADS)]
            ks = [gin[:, w4 + h * HD:w4 + (h + 1) * HD] for h in range(GDN_HEADS)]
            vs = [gin[:, 2 * w4 + h * HD:2 * w4 + (h + 1) * HD] for h in range(GDN_HEADS)]
            chunks.append((c, base, gc, rows, qs, ks, vs))
        kq = _bmm_nt(jnp.stack([jnp.concatenate([ch[5][h], ch[4][h]], axis=0)
                                for ch in chunks for h in range(GDN_HEADS)]),
                     jnp.stack([ch[5][h] for ch in chunks for h in range(GDN_HEADS)]))
        yield
        mb_l, rhs_l, lhs_l, qd_l = [], [], [], []
        for ci, (c, base, gc, rows, qs, ks, vs) in enumerate(chunks):
            kts = [k.astype(F32).T for k in ks]
            for d in range(2):
                for h in range(GDN_HEADS):
                    sl = slice(h * HD, (h + 1) * HD)
                    colg = SM_A + d * GDN_HEADS + h
                    kqh = kq[ci * GDN_HEADS + h]
                    gmat = jnp.broadcast_to(gc[:, colg:colg + 1], (c64, c64))
                    gcr, br, endr = rows[d:d + 1, sl], rows[2 + d:3 + d, sl], rows[4 + d:5 + d, sl]
                    dec = jnp.exp(jnp.where(incl[d], gmat - gcr, NEG))
                    mb_l.append(jnp.where(strict[d], kqh[:c64] * dec, 0.0) * br)
                    eg = jnp.exp(gmat)
                    kdtb = kts[h] * (jnp.exp(endr - gcr) * br)
                    lhs_l.append(jnp.concatenate([kdtb, kqh[c64:] * dec * br], axis=0))
                    rhs_l.append(jnp.concatenate([vs[h], ks[h] * eg], axis=1))
                    qd_l.append(qs[h] * eg)
        z = None
        for z in _unit_tri_inverse_stages(jnp.stack(mb_l)):
            if z is None:
                yield
        zr = _bmm(z, jnp.stack(rhs_l))
        yield
        a12 = _bmm(jnp.stack(lhs_l), zr)
        yield
        for ci, (c, base, gc, rows, qs, ks, vs) in enumerate(chunks):
            for d in range(2):
                slot = c * 2 + d
                ids = [ci * 2 * GDN_HEADS + d * GDN_HEADS + h for h in range(GDN_HEADS)]
                mq = [jnp.concatenate([-a12[n][:c64, HD:], qd_l[n] - a12[n][c64:, HD:]], axis=0) for n in ids]
                mq_s[slot] = jnp.concatenate(mq, axis=1).astype(BF16)
                n_s[slot] = jnp.concatenate([a12[n][:c64, :HD] for n in ids], axis=1)
                cd_s[slot] = jnp.broadcast_to(jnp.exp(rows[4 + d:5 + d]), (8, w4))
                oc = jnp.concatenate([a12[n][c64:, :HD] for n in ids], axis=1)
                if d == 0:
                    of_s[pl.ds(base, c64), :] = oc
                else:
                    ob_s[pl.ds(base, c64), :] = oc

    t_all = q_ref.shape[1]
    heads_per_kv = ATT_QH // ATT_KVH

    def attend(tile, nkeys):
        rows = pl.ds(_aligned(tile * TM, TM), TM)
        qt = q_ref[0, rows, :]
        outs = []

        def finish(h, s):
            g = h // heads_per_kv
            p = jnp.exp2(s - jnp.max(s, axis=-1, keepdims=True))
            acc = _mm(p, kv_ref[0, g, :nkeys, 2 * HD:])
            outs.append(acc[:, :HD] / acc[:, HD:])
            if h % heads_per_kv == heads_per_kv - 1:
                tile_out = outs[-heads_per_kv:] + [jnp.zeros((TM, HD), F32)]
                att_ref[0, g, rows, :] = jnp.concatenate(tile_out, axis=1).astype(BF16)

        s_prev = None
        for h in range(ATT_QH):
            k = kv_ref[0, h // heads_per_kv, :nkeys, :HD]
            s = _mm_nt(qt[:, h * HD:(h + 1) * HD], k)
            yield
            if s_prev is not None:
                finish(h - 1, s_prev)
                yield
            s_prev = s
        finish(ATT_QH - 1, s_prev)
        yield

    def trip(j, nkeys):
        streams = ([attend(j, nkeys)] if nkeys else []) + [local(j)]
        while streams:
            for st in list(streams):
                try:
                    next(st)
                except StopIteration:
                    streams.remove(st)

    trip(0, CTX if ctx_attn else 0)
    if not ctx_attn:
        att_ref[0, :, :TM, :] = jnp.zeros((ATT_KVH, TM, 256), BF16)

    def later_trip(j, carry):
        trip(j, t_all)
        return carry

    lax.fori_loop(1, nch // nb, later_trip, 0)

    def step(t, states):
        cf = t
        cb = jnp.where(t < ctx_chunks, ctx_chunks - 1 - t, nch - 1 + ctx_chunks - t)
        new_states = []
        for d, c in ((0, cf), (1, cb)):
            slot = c * 2 + d
            s = states[d]
            r = jnp.dot(mq_s[slot], _bd4(s), preferred_element_type=F32)
            new_states.append(s * cd_s[slot][0:1] + r[:c64] + n_s[slot])
            rows = pl.ds(pl.multiple_of(c * c64, c64), c64)
            if d == 0:
                of_s[rows, :] = of_s[rows, :] + r[c64:]
            else:
                ob_s[rows, :] = ob_s[rows, :] + r[c64:]
        return tuple(new_states)

    zero = jnp.zeros((c64, w4), F32)
    lax.fori_loop(0, nch, step, (zero, zero), unroll=4)
    o_ref[0] = (of_s[...] + ob_s[...]).astype(BF16)


def _gdn_attention(gdn_in, small, prm, q, kv, layer, ctx_attn):
    bsz, t, _ = gdn_in.shape
    nch = t // GDN_CHUNK
    assert nch // 4 == t // TM
    return pl.pallas_call(
        functools.partial(_gdn_attn_kernel, nch, CTX // GDN_CHUNK, ctx_attn),
        out_shape=[jax.ShapeDtypeStruct((bsz, t, GDN_W), BF16),
                   jax.ShapeDtypeStruct((bsz, ATT_KVH, t, 256), BF16)],
        grid=(bsz,),
        in_specs=[pl.BlockSpec((1, t, 768), lambda b: (b, 0, 0)),
                  pl.BlockSpec((1, t, 128), lambda b: (b, 0, 0)),
                  pl.BlockSpec((1, 8, 128), lambda b: (layer, 0, 0)),
                  pl.BlockSpec((1, t, ATT_W), lambda b: (b, 0, 0)),
                  pl.BlockSpec((1, ATT_KVH, t, 256), lambda b: (b, 0, 0, 0))],
        out_specs=[pl.BlockSpec((1, t, GDN_W), lambda b: (b, 0, 0), pipeline_mode=pl.Buffered(1)),
                   pl.BlockSpec((1, ATT_KVH, t, 256), lambda b: (b, 0, 0, 0), pipeline_mode=pl.Buffered(1))],
        scratch_shapes=[pltpu.VMEM((t, 128), F32),
                        pltpu.VMEM((nch * 8, GDN_W), F32),
                        pltpu.VMEM((nch * 2, 128, GDN_W), BF16),
                        pltpu.VMEM((nch * 2, GDN_CHUNK, GDN_W), F32),
                        pltpu.VMEM((nch * 2, 8, GDN_W), F32),
                        pltpu.VMEM((t, GDN_W), F32),
                        pltpu.VMEM((t, GDN_W), F32)],
        compiler_params=pltpu.CompilerParams(dimension_semantics=("arbitrary",),
                                             vmem_limit_bytes=VMEM_LIMIT),
        name="gdn_attention",
    )(gdn_in, small, prm, q, kv)


def _out_kernel(first_tile, xa_ref, x_ref, y_ref, o_ref, a_ref, zs_ref, *refs):
    mods = refs[:BPS]
    sw_ref, gw_ref, w_ref, out_ref = refs[BPS:]

    def mixed(s):
        zs = zs_ref[s].astype(F32)
        t = y_ref[s].astype(F32) * zs[:, :SSD_W]
        ssd = t * lax.rsqrt(jnp.mean(t * t, axis=-1, keepdims=True) + EPS) * sw_ref[0]
        o = o_ref[s].astype(F32)
        gdn = o * lax.rsqrt(_group_sum64(o * o) * (1.0 / HD) + EPS) * gw_ref[0] * zs[:, SSD_W:SSD_W + GDN_W]
        a = a_ref[s]
        att = jnp.concatenate([a[g][:, :HD * ATT_QH // ATT_KVH] for g in range(ATT_KVH)], axis=1)
        att = att * zs[:, SSD_W + GDN_W:]
        return jnp.concatenate([ssd, gdn, att], axis=1).astype(BF16)

    mixes = [mixed(s) for s in range(BPS)]
    projs = [jnp.dot(mix, w_ref[0], preferred_element_type=F32) for mix in mixes]
    for s in range(BPS):
        gate = mods[s][0][:, 2 * D_MODEL:]
        x = jnp.where(pl.program_id(1) + first_tile == 0, xa_ref[s], x_ref[s])
        out_ref[s] = x + gate * projs[s]


def _out_projection(xa, xb, off, layer, y, o, a, zs, mod3, ssd_nw, gdn_nw, w_out, first_tile):
    bsz, t, _ = y.shape
    nt = t // TM - first_tile
    row = lambda b, i: (b, i + first_tile, 0)
    lay = lambda b, i: (layer, 0, 0)

    def mod_spec(s):
        return pl.BlockSpec((1, 1, 3 * D_MODEL),
                            lambda b, i: (jnp.where(i + first_tile == 0, bsz, b * BPS + s), 0, 0))

    return pl.pallas_call(
        functools.partial(_out_kernel, first_tile),
        out_shape=jax.ShapeDtypeStruct((bsz, nt * TM, D_MODEL), F32),
        grid=(bsz // BPS, nt),
        in_specs=[pl.BlockSpec((BPS, TM, D_MODEL), lambda b, i: (b, 0, 0)),
                  pl.BlockSpec((BPS, TM, D_MODEL), lambda b, i: (b, jnp.maximum(i + first_tile - off, 0), 0)),
                  pl.BlockSpec((BPS, TM, SSD_W), row),
                  pl.BlockSpec((BPS, TM, GDN_W), row),
                  pl.BlockSpec((BPS, ATT_KVH, TM, 256), lambda b, i: (b, 0, i + first_tile, 0)),
                  pl.BlockSpec((BPS, TM, D_MODEL), row),
                  *[mod_spec(s) for s in range(BPS)],
                  pl.BlockSpec((1, 1, SSD_W), lay),
                  pl.BlockSpec((1, 1, GDN_W), lay),
                  pl.BlockSpec((1, D_MODEL, D_MODEL), lay)],
        out_specs=pl.BlockSpec((BPS, TM, D_MODEL), lambda b, i: (b, i, 0)),
        compiler_params=pltpu.CompilerParams(dimension_semantics=("arbitrary", "arbitrary"),
                                             vmem_limit_bytes=VMEM_LIMIT),
        name="out_projection",
    )(xa, xb, y, o, a, zs, *([mod3] * BPS), ssd_nw, gdn_nw, w_out)


_REST_SEGS = [(2076, 384), (2460, 128), (1408, 384), (1804, 256), (2588, 128), (2716, 384),
              (1792, 12), (2060, 8), (2068, 8)]


def _prepare_params(w_in, conv_w, conv_b, ssd_A_log, ssd_dt_bias, gdn_A_log, gdn_dt_bias):
    depth = w_in.shape[0]
    w_conv = w_in[:, :, :CONV_DIM].astype(BF16)
    pad = jnp.zeros(w_in.shape[:2] + (REST_DIM - sum(n for _, n in _REST_SEGS),), w_in.dtype)
    w_rest = jnp.concatenate([lax.slice_in_dim(w_in, a, a + n, axis=2) for a, n in _REST_SEGS] + [pad],
                             axis=2).astype(BF16)
    cw = conv_w
    cb = conv_b[:, None, :]

    def lanes(a, b):
        v = jnp.concatenate([a.reshape(depth, -1), b.reshape(depth, -1)], axis=1)
        return jnp.pad(v, ((0, 0), (0, 128 - v.shape[1])))

    prm = jnp.stack([lanes(ssd_A_log, gdn_A_log), lanes(ssd_dt_bias, gdn_dt_bias)], axis=1)
    prm = jnp.pad(prm, ((0, 0), (0, 6), (0, 0)))
    return w_conv, w_rest, cw, cb, prm


def _rope_tables(t):
    f32 = np.float32
    pos = np.arange(t - CTX)
    n_freq = HD // 4
    freqs = np.power(f32(ROPE_THETA), -np.arange(n_freq, dtype=f32) / f32(n_freq)).astype(f32)
    ang_r = ((pos // GRID_W).astype(f32)[:, None] * freqs).astype(f32)
    ang_c = ((pos % GRID_W).astype(f32)[:, None] * freqs).astype(f32)
    cos = np.concatenate([np.cos(ang_r)] * 2 + [np.cos(ang_c)] * 2, axis=1)
    sin = np.concatenate([-np.sin(ang_r), np.sin(ang_r), -np.sin(ang_c), np.sin(ang_c)], axis=1)
    cos = np.concatenate([np.ones((CTX, HD), f32), cos], axis=0).astype(f32)
    sin = np.concatenate([np.zeros((CTX, HD), f32), sin], axis=0).astype(f32)
    return jnp.asarray(np.tile(cos, (1, 2))), jnp.asarray(np.tile(sin, (1, 2)))


def kernel(x, c, ctx, c_ctx, norm_w, w_mod, b_mod, w_in, conv_w, conv_b, ssd_A_log, ssd_dt_bias, ssd_D,
           ssd_norm_w, gdn_A_log, gdn_dt_bias, gdn_norm_w, q_norm_w, k_norm_w, w_out):
    bsz = x.shape[0]
    t = CTX + x.shape[1]
    c_all = jnp.concatenate([c, c_ctx[None, :], jnp.zeros((7, D_MODEL), F32)], axis=0)
    cos_t, sin_t = _rope_tables(t)
    w_conv, w_rest, cw, cb, prm = _prepare_params(w_in, conv_w, conv_b, ssd_A_log, ssd_dt_bias,
                                                  gdn_A_log, gdn_dt_bias)
    qk_w = jnp.concatenate([jnp.tile(q_norm_w, (1, ATT_QH)), jnp.tile(k_norm_w, (1, ATT_KVH))], axis=1)[:, None]
    dskip = jnp.repeat(ssd_D, HD, axis=1)[:, None]
    gdn_nw = jnp.tile(gdn_norm_w, (1, GDN_HEADS))[:, None]
    norm_w3, ssd_nw = norm_w[:, None], ssd_norm_w[:, None]
    w_out16 = w_out.astype(BF16)

    xa, xb, off = ctx, x, 1
    for layer in range(DEPTH):
        first_tile = 1 if layer == DEPTH - 1 else 0
        mod3 = _modulation(c_all, w_mod, b_mod, layer).reshape(bsz + 8, 1, 3 * D_MODEL)
        ssd_in, gdn_in, zs, q, kv, small = _in_projection(
            xa, xb, off, layer, mod3, norm_w3, w_conv, w_rest, cw, cb, qk_w, cos_t, sin_t)
        y = _ssd(ssd_in, small, prm, dskip, layer, first_tile == 0)
        o, a = _gdn_attention(gdn_in, small, prm, q, kv, layer, first_tile == 0)
        out = _out_projection(xa, xb, off, layer, y, o, a, zs, mod3, ssd_nw, gdn_nw, w_out16, first_tile)
        xa, xb, off = out, out, 0
    return out
```
